```python
import math
import jax, jax.numpy as jnp
from jax import lax
import numpy as np

D_MODEL = 2048
BATCH = 8
SEQ = 4096
DEPTH = 4

CONV_CH = D_MODEL
CONV_WIDTH = 31
CONV_GROUPS = 16
SGU_CH = D_MODEL
SGU_GROUPS = 8
SGU_GROUP_CH = SGU_CH // SGU_GROUPS
CHUNK = 128
N_BRANCH = 2
FFN_HIDDEN = int(math.ceil(8 * D_MODEL / 3 / 256) * 256)
IN_WIDTH = 2 * CONV_CH + 2 * SGU_CH + N_BRANCH * D_MODEL
EPS = 1e-6

kernel_name = "hybrid_conformer_conv_gmlp_sandwich_block"


def rms_norm(x, g):
    xf = x.astype(jnp.float32)
    y = xf * lax.rsqrt(jnp.mean(xf * xf, axis=-1, keepdims=True) + EPS)
    return (y * g.astype(jnp.float32)).astype(x.dtype)


def layer_norm(x, g, b):
    xf = x.astype(jnp.float32)
    mu = jnp.mean(xf, axis=-1, keepdims=True)
    xc = xf - mu
    var = jnp.mean(xc * xc, axis=-1, keepdims=True)
    y = xc * lax.rsqrt(var + EPS)
    return (y * g.astype(jnp.float32) + b.astype(jnp.float32)).astype(x.dtype)


def conformer_conv_branch(a_in, a_gate, conv_w, conv_b, ln_g, ln_b, w_out):
    h = a_in * jax.nn.sigmoid(a_gate)
    rhs = conv_w[:, None, :].astype(h.dtype)
    h = lax.conv_general_dilated(
        h, rhs, window_strides=(1,), padding=[(CONV_WIDTH - 1, 0)],
        dimension_numbers=("NWC", "WIO", "NWC"),
        feature_group_count=CONV_CH)
    h = h + conv_b
    h = layer_norm(h, ln_g, ln_b)
    h = jax.nn.silu(h)
    return h @ w_out


def chunked_sgu_branch(b_in, ln_g, ln_b, w_spatial, b_spatial, w_out):
    bsz, seq, _ = b_in.shape
    z = jax.nn.gelu(b_in, approximate=False)
    u, v = jnp.split(z, 2, axis=-1)
    v = layer_norm(v, ln_g, ln_b)
    n_chunks = seq // CHUNK
    v = v.reshape(bsz, n_chunks, CHUNK, SGU_GROUPS, SGU_GROUP_CH)
    causal = jnp.tril(jnp.ones((CHUNK, CHUNK), dtype=bool))
    w_masked = jnp.where(causal[None], w_spatial, 0).astype(v.dtype)
    mixed = jnp.einsum("gts,bnsgc->bntgc", w_masked, v)
    mixed = mixed + jnp.transpose(b_spatial)[None, None, :, :, None].astype(v.dtype)
    mixed = mixed.reshape(bsz, seq, SGU_CH)
    return (u * mixed) @ w_out


def swiglu_ffn(h, w_gate_up, w_down):
    gu = h @ w_gate_up
    g, up = jnp.split(gu, 2, axis=-1)
    return (jax.nn.silu(g) * up) @ w_down


def _fwd_setup_inputs(seed: int = 0) -> dict:
    key = jax.random.key(seed)
    ks = jax.random.split(key, 20)
    L = DEPTH
    f32 = jnp.float32

    def nrm(k, shape, scale):
        return jax.random.normal(k, shape, f32) * scale

    def gain(k, shape):
        return 1.0 + 0.05 * jax.random.normal(k, shape, f32)

    x = jax.random.normal(ks[0], (BATCH, SEQ, D_MODEL), f32)
    return {
        "x": x,
        "norm_mix_pre": gain(ks[1], (L, D_MODEL)),
        "norm_mix_post": gain(ks[2], (L, D_MODEL)),
        "norm_ffn_pre": gain(ks[3], (L, D_MODEL)),
        "norm_ffn_post": gain(ks[4], (L, D_MODEL)),
        "w_in": nrm(ks[5], (L, D_MODEL, IN_WIDTH), D_MODEL ** -0.5),
        "b_gate": nrm(ks[6], (L, N_BRANCH, D_MODEL), 0.01),
        "conv_w": nrm(ks[7], (L, CONV_WIDTH, CONV_CH), CONV_WIDTH ** -0.5),
        "conv_b": nrm(ks[8], (L, CONV_CH), 0.02),
        "conv_ln_g": gain(ks[9], (L, CONV_CH)),
        "conv_ln_b": nrm(ks[10], (L, CONV_CH), 0.02),
        "w_a_out": nrm(ks[11], (L, CONV_CH, D_MODEL), CONV_CH ** -0.5),
        "sgu_ln_g": gain(ks[12], (L, SGU_CH)),
        "sgu_ln_b": nrm(ks[13], (L, SGU_CH), 0.02),
        "w_spatial": nrm(ks[14], (L, SGU_GROUPS, CHUNK, CHUNK), CHUNK ** -0.5),
        "b_spatial": gain(ks[15], (L, SGU_GROUPS, CHUNK)),
        "w_b_out": nrm(ks[16], (L, SGU_CH, D_MODEL), SGU_CH ** -0.5),
        "w_o": nrm(ks[17], (L, D_MODEL, D_MODEL), D_MODEL ** -0.5),
        "w_gate_up": nrm(ks[18], (L, D_MODEL, 2 * FFN_HIDDEN), D_MODEL ** -0.5),
        "w_down": nrm(ks[19], (L, FFN_HIDDEN, D_MODEL), FFN_HIDDEN ** -0.5),
    }


def _fwd_reference(x, norm_mix_pre, norm_mix_post, norm_ffn_pre, norm_ffn_post,
              w_in, b_gate, conv_w, conv_b, conv_ln_g, conv_ln_b, w_a_out,
              sgu_ln_g, sgu_ln_b, w_spatial, b_spatial, w_b_out, w_o,
              w_gate_up, w_down):
    bsz, seq, _ = x.shape
    split_pts = [CONV_CH, 2 * CONV_CH, 2 * CONV_CH + 2 * SGU_CH]
    for i in range(DEPTH):
        h = rms_norm(x, norm_mix_pre[i])
        proj = h @ w_in[i]
        a_in, a_gate, b_in, gate_logits = jnp.split(proj, split_pts, axis=-1)
        y_a = conformer_conv_branch(a_in, a_gate, conv_w[i], conv_b[i],
                                    conv_ln_g[i], conv_ln_b[i], w_a_out[i])
        y_b = chunked_sgu_branch(b_in, sgu_ln_g[i], sgu_ln_b[i], w_spatial[i],
                                 b_spatial[i], w_b_out[i])
        gates = jax.nn.sigmoid(gate_logits.reshape(bsz, seq, N_BRANCH, D_MODEL) + b_gate[i])
        merged = gates[:, :, 0, :] * y_a + gates[:, :, 1, :] * y_b
        x = x + rms_norm(merged @ w_o[i], norm_mix_post[i])
        h = rms_norm(x, norm_ffn_pre[i])
        x = x + rms_norm(swiglu_ffn(h, w_gate_up[i], w_down[i]), norm_ffn_post[i])
    return x


import jax as _jax
import jax.numpy as _jnp

TWIN_FORMAT = 'train_step'
FWD_PARAMS = ['x', 'norm_mix_pre', 'norm_mix_post', 'norm_ffn_pre', 'norm_ffn_post', 'w_in', 'b_gate', 'conv_w', 'conv_b', 'conv_ln_g', 'conv_ln_b', 'w_a_out', 'sgu_ln_g', 'sgu_ln_b', 'w_spatial', 'b_spatial', 'w_b_out', 'w_o', 'w_gate_up', 'w_down']
TWIN_WEIGHTS = ['norm_mix_pre', 'norm_mix_post', 'norm_ffn_pre', 'norm_ffn_post', 'w_in', 'b_gate', 'conv_w', 'conv_b', 'conv_ln_g', 'conv_ln_b', 'w_a_out', 'sgu_ln_g', 'sgu_ln_b', 'w_spatial', 'b_spatial', 'w_b_out', 'w_o', 'w_gate_up', 'w_down']
TWIN_DIFF_INPUT = 'x'
TWIN_INPUTS = ['x', 'norm_mix_pre', 'norm_mix_post', 'norm_ffn_pre', 'norm_ffn_post', 'w_in', 'b_gate', 'conv_w', 'conv_b', 'conv_ln_g', 'conv_ln_b', 'w_a_out', 'sgu_ln_g', 'sgu_ln_b', 'w_spatial', 'b_spatial', 'w_b_out', 'w_o', 'w_gate_up', 'w_down', 'loss_target', 'm_norm_mix_pre', 'm_norm_mix_post', 'm_norm_ffn_pre', 'm_norm_ffn_post', 'm_w_in', 'm_b_gate', 'm_conv_w', 'm_conv_b', 'm_conv_ln_g', 'm_conv_ln_b', 'm_w_a_out', 'm_sgu_ln_g', 'm_sgu_ln_b', 'm_w_spatial', 'm_b_spatial', 'm_w_b_out', 'm_w_o', 'm_w_gate_up', 'm_w_down', 'v_norm_mix_pre', 'v_norm_mix_post', 'v_norm_ffn_pre', 'v_norm_ffn_post', 'v_w_in', 'v_b_gate', 'v_conv_w', 'v_conv_b', 'v_conv_ln_g', 'v_conv_ln_b', 'v_w_a_out', 'v_sgu_ln_g', 'v_sgu_ln_b', 'v_w_spatial', 'v_b_spatial', 'v_w_b_out', 'v_w_o', 'v_w_gate_up', 'v_w_down']
TWIN_OUTPUTS = ['loss', 'grad_x', 'grad_norm_mix_pre', 'grad_norm_mix_post', 'grad_norm_ffn_pre', 'grad_norm_ffn_post', 'grad_w_in', 'grad_b_gate', 'grad_conv_w', 'grad_conv_b', 'grad_conv_ln_g', 'grad_conv_ln_b', 'grad_w_a_out', 'grad_sgu_ln_g', 'grad_sgu_ln_b', 'grad_w_spatial', 'grad_b_spatial', 'grad_w_b_out', 'grad_w_o', 'grad_w_gate_up', 'grad_w_down', 'delta_norm_mix_pre', 'delta_norm_mix_post', 'delta_norm_ffn_pre', 'delta_norm_ffn_post', 'delta_w_in', 'delta_b_gate', 'delta_conv_w', 'delta_conv_b', 'delta_conv_ln_g', 'delta_conv_ln_b', 'delta_w_a_out', 'delta_sgu_ln_g', 'delta_sgu_ln_b', 'delta_w_spatial', 'delta_b_spatial', 'delta_w_b_out', 'delta_w_o', 'delta_w_gate_up', 'delta_w_down', 'new_m_norm_mix_pre', 'new_m_norm_mix_post', 'new_m_norm_ffn_pre', 'new_m_norm_ffn_post', 'new_m_w_in', 'new_m_b_gate', 'new_m_conv_w', 'new_m_conv_b', 'new_m_conv_ln_g', 'new_m_conv_ln_b', 'new_m_w_a_out', 'new_m_sgu_ln_g', 'new_m_sgu_ln_b', 'new_m_w_spatial', 'new_m_b_spatial', 'new_m_w_b_out', 'new_m_w_o', 'new_m_w_gate_up', 'new_m_w_down', 'new_v_norm_mix_pre', 'new_v_norm_mix_post', 'new_v_norm_ffn_pre', 'new_v_norm_ffn_post', 'new_v_w_in', 'new_v_b_gate', 'new_v_conv_w', 'new_v_conv_b', 'new_v_conv_ln_g', 'new_v_conv_ln_b', 'new_v_w_a_out', 'new_v_sgu_ln_g', 'new_v_sgu_ln_b', 'new_v_w_spatial', 'new_v_b_spatial', 'new_v_w_b_out', 'new_v_w_o', 'new_v_w_gate_up', 'new_v_w_down']
TWIN_LEAF_KINDS = {'loss': 'loss', 'grad_x': 'grad_x', 'grad_norm_mix_pre': 'grad_w', 'grad_norm_mix_post': 'grad_w', 'grad_norm_ffn_pre': 'grad_w', 'grad_norm_ffn_post': 'grad_w', 'grad_w_in': 'grad_w', 'grad_b_gate': 'grad_w', 'grad_conv_w': 'grad_w', 'grad_conv_b': 'grad_w', 'grad_conv_ln_g': 'grad_w', 'grad_conv_ln_b': 'grad_w', 'grad_w_a_out': 'grad_w', 'grad_sgu_ln_g': 'grad_w', 'grad_sgu_ln_b': 'grad_w', 'grad_w_spatial': 'grad_w', 'grad_b_spatial': 'grad_w', 'grad_w_b_out': 'grad_w', 'grad_w_o': 'grad_w', 'grad_w_gate_up': 'grad_w', 'grad_w_down': 'grad_w', 'delta_norm_mix_pre': 'delta_w', 'delta_norm_mix_post': 'delta_w', 'delta_norm_ffn_pre': 'delta_w', 'delta_norm_ffn_post': 'delta_w', 'delta_w_in': 'delta_w', 'delta_b_gate': 'delta_w', 'delta_conv_w': 'delta_w', 'delta_conv_b': 'delta_w', 'delta_conv_ln_g': 'delta_w', 'delta_conv_ln_b': 'delta_w', 'delta_w_a_out': 'delta_w', 'delta_sgu_ln_g': 'delta_w', 'delta_sgu_ln_b': 'delta_w', 'delta_w_spatial': 'delta_w', 'delta_b_spatial': 'delta_w', 'delta_w_b_out': 'delta_w', 'delta_w_o': 'delta_w', 'delta_w_gate_up': 'delta_w', 'delta_w_down': 'delta_w', 'new_m_norm_mix_pre': 'new_m', 'new_m_norm_mix_post': 'new_m', 'new_m_norm_ffn_pre': 'new_m', 'new_m_norm_ffn_post': 'new_m', 'new_m_w_in': 'new_m', 'new_m_b_gate': 'new_m', 'new_m_conv_w': 'new_m', 'new_m_conv_b': 'new_m', 'new_m_conv_ln_g': 'new_m', 'new_m_conv_ln_b': 'new_m', 'new_m_w_a_out': 'new_m', 'new_m_sgu_ln_g': 'new_m', 'new_m_sgu_ln_b': 'new_m', 'new_m_w_spatial': 'new_m', 'new_m_b_spatial': 'new_m', 'new_m_w_b_out': 'new_m', 'new_m_w_o': 'new_m', 'new_m_w_gate_up': 'new_m', 'new_m_w_down': 'new_m', 'new_v_norm_mix_pre': 'new_v', 'new_v_norm_mix_post': 'new_v', 'new_v_norm_ffn_pre': 'new_v', 'new_v_norm_ffn_post': 'new_v', 'new_v_w_in': 'new_v', 'new_v_b_gate': 'new_v', 'new_v_conv_w': 'new_v', 'new_v_conv_b': 'new_v', 'new_v_conv_ln_g': 'new_v', 'new_v_conv_ln_b': 'new_v', 'new_v_w_a_out': 'new_v', 'new_v_sgu_ln_g': 'new_v', 'new_v_sgu_ln_b': 'new_v', 'new_v_w_spatial': 'new_v', 'new_v_b_spatial': 'new_v', 'new_v_w_b_out': 'new_v', 'new_v_w_o': 'new_v', 'new_v_w_gate_up': 'new_v', 'new_v_w_down': 'new_v'}


def _forward(args):
    return _fwd_reference(*[args[k] for k in FWD_PARAMS])


def _output_shape():
    def fwd():
        inp = _fwd_setup_inputs(0)
        return _fwd_reference(*[inp[k] for k in FWD_PARAMS])
    out = _jax.eval_shape(fwd)
    return out.shape, out.dtype

N_MICROBATCH = 1
ADAM_LR = 0.001
ADAM_B1 = 0.9
ADAM_B2 = 0.999
ADAM_EPS = 1e-08
ADAM_WD = 0.01
ADAM_STEP = 10
PER_EXAMPLE_BATCH_AXIS = {'x': 0, 'loss_target': 0}
SHARED_INPUTS = []
_WEIGHT_DTYPES = {'norm_mix_pre': _jnp.float32, 'norm_mix_post': _jnp.float32, 'norm_ffn_pre': _jnp.float32, 'norm_ffn_post': _jnp.float32, 'w_in': _jnp.float32, 'b_gate': _jnp.float32, 'conv_w': _jnp.float32, 'conv_b': _jnp.float32, 'conv_ln_g': _jnp.float32, 'conv_ln_b': _jnp.float32, 'w_a_out': _jnp.float32, 'sgu_ln_g': _jnp.float32, 'sgu_ln_b': _jnp.float32, 'w_spatial': _jnp.float32, 'b_spatial': _jnp.float32, 'w_b_out': _jnp.float32, 'w_o': _jnp.float32, 'w_gate_up': _jnp.float32, 'w_down': _jnp.float32}
MOMENT_SCALE = {'norm_mix_pre': 1.330926e+00, 'norm_mix_post': 1.674786e+01, 'norm_ffn_pre': 1.351806e+00, 'norm_ffn_post': 1.605006e+01, 'w_in': 5.492899e-01, 'b_gate': 1.126971e+00, 'conv_w': 9.930741e-01, 'conv_b': 9.467306e+00, 'conv_ln_g': 3.806216e+00, 'conv_ln_b': 5.450111e+00, 'w_a_out': 2.344971e+00, 'sgu_ln_g': 2.686896e-01, 'sgu_ln_b': 2.873398e-01, 'w_spatial': 3.798573e-01, 'b_spatial': 6.160801e-01, 'w_b_out': 3.080055e+00, 'w_o': 3.929709e+00, 'w_gate_up': 5.811341e-01, 'w_down': 1.186048e+00}


def _to_microbatches(a, axis):
    t = _jnp.moveaxis(a, axis, 0)
    t = t.reshape((N_MICROBATCH, t.shape[0] // N_MICROBATCH) + t.shape[1:])
    return _jnp.moveaxis(t, 1, axis + 1)


def setup_inputs(seed: int = 0) -> dict:
    inp = _fwd_setup_inputs(seed)
    key = _jax.random.fold_in(_jax.random.key(seed), 7919)
    shape, _ = _output_shape()
    out = dict(inp)
    out["loss_target"] = _jax.random.normal(_jax.random.fold_in(key, 0), shape, _jnp.float32)
    for i, name in enumerate(TWIN_WEIGHTS):
        w = inp[name].astype(_jnp.float32)
        if MOMENT_SCALE is None:
            s = _jnp.sqrt(_jnp.mean(_jnp.square(w)) + 1e-30)
        else:
            s = MOMENT_SCALE[name]
        km, kv = _jax.random.split(_jax.random.fold_in(key, i + 1))
        out[name] = w
        out["m_" + name] = s * _jax.random.normal(km, w.shape, _jnp.float32)
        out["v_" + name] = (s * s) * _jax.random.uniform(kv, w.shape, _jnp.float32, 0.5, 1.5)
    if N_MICROBATCH > 1:
        for name, axis in PER_EXAMPLE_BATCH_AXIS.items():
            out[name] = _to_microbatches(out[name], axis)
    return {'x': out['x'], 'norm_mix_pre': out['norm_mix_pre'], 'norm_mix_post': out['norm_mix_post'], 'norm_ffn_pre': out['norm_ffn_pre'], 'norm_ffn_post': out['norm_ffn_post'], 'w_in': out['w_in'], 'b_gate': out['b_gate'], 'conv_w': out['conv_w'], 'conv_b': out['conv_b'], 'conv_ln_g': out['conv_ln_g'], 'conv_ln_b': out['conv_ln_b'], 'w_a_out': out['w_a_out'], 'sgu_ln_g': out['sgu_ln_g'], 'sgu_ln_b': out['sgu_ln_b'], 'w_spatial': out['w_spatial'], 'b_spatial': out['b_spatial'], 'w_b_out': out['w_b_out'], 'w_o': out['w_o'], 'w_gate_up': out['w_gate_up'], 'w_down': out['w_down'], 'loss_target': out['loss_target'], 'm_norm_mix_pre': out['m_norm_mix_pre'], 'm_norm_mix_post': out['m_norm_mix_post'], 'm_norm_ffn_pre': out['m_norm_ffn_pre'], 'm_norm_ffn_post': out['m_norm_ffn_post'], 'm_w_in': out['m_w_in'], 'm_b_gate': out['m_b_gate'], 'm_conv_w': out['m_conv_w'], 'm_conv_b': out['m_conv_b'], 'm_conv_ln_g': out['m_conv_ln_g'], 'm_conv_ln_b': out['m_conv_ln_b'], 'm_w_a_out': out['m_w_a_out'], 'm_sgu_ln_g': out['m_sgu_ln_g'], 'm_sgu_ln_b': out['m_sgu_ln_b'], 'm_w_spatial': out['m_w_spatial'], 'm_b_spatial': out['m_b_spatial'], 'm_w_b_out': out['m_w_b_out'], 'm_w_o': out['m_w_o'], 'm_w_gate_up': out['m_w_gate_up'], 'm_w_down': out['m_w_down'], 'v_norm_mix_pre': out['v_norm_mix_pre'], 'v_norm_mix_post': out['v_norm_mix_post'], 'v_norm_ffn_pre': out['v_norm_ffn_pre'], 'v_norm_ffn_post': out['v_norm_ffn_post'], 'v_w_in': out['v_w_in'], 'v_b_gate': out['v_b_gate'], 'v_conv_w': out['v_conv_w'], 'v_conv_b': out['v_conv_b'], 'v_conv_ln_g': out['v_conv_ln_g'], 'v_conv_ln_b': out['v_conv_ln_b'], 'v_w_a_out': out['v_w_a_out'], 'v_sgu_ln_g': out['v_sgu_ln_g'], 'v_sgu_ln_b': out['v_sgu_ln_b'], 'v_w_spatial': out['v_w_spatial'], 'v_b_spatial': out['v_b_spatial'], 'v_w_b_out': out['v_w_b_out'], 'v_w_o': out['v_w_o'], 'v_w_gate_up': out['v_w_gate_up'], 'v_w_down': out['v_w_down']}


def _loss(weights, diff, rest, loss_target):
    with _jax.named_scope("forward"):
        args = {**rest, TWIN_DIFF_INPUT: diff, **{k: w.astype(_WEIGHT_DTYPES[k]) for k, w in weights.items()}}
        y = _forward(args)
    with _jax.named_scope("loss_head"):
        err = _jnp.square(y.astype(_jnp.float32) - loss_target)
        return 0.5 * _jnp.sum(_jnp.mean(err, axis=-1)) if err.ndim else 0.5 * err


def _adamw(w, g, m, v):
    m = ADAM_B1 * m + (1.0 - ADAM_B1) * g
    v = ADAM_B2 * v + (1.0 - ADAM_B2) * _jnp.square(g)
    m_hat = m / (1.0 - ADAM_B1 ** ADAM_STEP)
    v_hat = v / (1.0 - ADAM_B2 ** ADAM_STEP)
    delta = -ADAM_LR * (m_hat / (_jnp.sqrt(v_hat) + ADAM_EPS) + ADAM_WD * w)
    return delta, m, v


def reference(x, norm_mix_pre, norm_mix_post, norm_ffn_pre, norm_ffn_post, w_in, b_gate, conv_w, conv_b, conv_ln_g, conv_ln_b, w_a_out, sgu_ln_g, sgu_ln_b, w_spatial, b_spatial, w_b_out, w_o, w_gate_up, w_down, loss_target, m_norm_mix_pre, m_norm_mix_post, m_norm_ffn_pre, m_norm_ffn_post, m_w_in, m_b_gate, m_conv_w, m_conv_b, m_conv_ln_g, m_conv_ln_b, m_w_a_out, m_sgu_ln_g, m_sgu_ln_b, m_w_spatial, m_b_spatial, m_w_b_out, m_w_o, m_w_gate_up, m_w_down, v_norm_mix_pre, v_norm_mix_post, v_norm_ffn_pre, v_norm_ffn_post, v_w_in, v_b_gate, v_conv_w, v_conv_b, v_conv_ln_g, v_conv_ln_b, v_w_a_out, v_sgu_ln_g, v_sgu_ln_b, v_w_spatial, v_b_spatial, v_w_b_out, v_w_o, v_w_gate_up, v_w_down):
    given = dict(x=x, norm_mix_pre=norm_mix_pre, norm_mix_post=norm_mix_post, norm_ffn_pre=norm_ffn_pre, norm_ffn_post=norm_ffn_post, w_in=w_in, b_gate=b_gate, conv_w=conv_w, conv_b=conv_b, conv_ln_g=conv_ln_g, conv_ln_b=conv_ln_b, w_a_out=w_a_out, sgu_ln_g=sgu_ln_g, sgu_ln_b=sgu_ln_b, w_spatial=w_spatial, b_spatial=b_spatial, w_b_out=w_b_out, w_o=w_o, w_gate_up=w_gate_up, w_down=w_down, loss_target=loss_target, m_norm_mix_pre=m_norm_mix_pre, m_norm_mix_post=m_norm_mix_post, m_norm_ffn_pre=m_norm_ffn_pre, m_norm_ffn_post=m_norm_ffn_post, m_w_in=m_w_in, m_b_gate=m_b_gate, m_conv_w=m_conv_w, m_conv_b=m_conv_b, m_conv_ln_g=m_conv_ln_g, m_conv_ln_b=m_conv_ln_b, m_w_a_out=m_w_a_out, m_sgu_ln_g=m_sgu_ln_g, m_sgu_ln_b=m_sgu_ln_b, m_w_spatial=m_w_spatial, m_b_spatial=m_b_spatial, m_w_b_out=m_w_b_out, m_w_o=m_w_o, m_w_gate_up=m_w_gate_up, m_w_down=m_w_down, v_norm_mix_pre=v_norm_mix_pre, v_norm_mix_post=v_norm_mix_post, v_norm_ffn_pre=v_norm_ffn_pre, v_norm_ffn_post=v_norm_ffn_post, v_w_in=v_w_in, v_b_gate=v_b_gate, v_conv_w=v_conv_w, v_conv_b=v_conv_b, v_conv_ln_g=v_conv_ln_g, v_conv_ln_b=v_conv_ln_b, v_w_a_out=v_w_a_out, v_sgu_ln_g=v_sgu_ln_g, v_sgu_ln_b=v_sgu_ln_b, v_w_spatial=v_w_spatial, v_b_spatial=v_b_spatial, v_w_b_out=v_w_b_out, v_w_o=v_w_o, v_w_gate_up=v_w_gate_up, v_w_down=v_w_down)
    weights = {n: given[n] for n in TWIN_WEIGHTS}
    shared = {n: given[n] for n in SHARED_INPUTS}
    per_example = {n: given[n] for n in ['x']}
    grad_fn = _jax.value_and_grad(_loss, argnums=(0, 1))

    def one_microbatch(ex, loss_target):
        ex = dict(ex)
        diff = ex.pop(TWIN_DIFF_INPUT)
        return grad_fn(weights, diff, {**shared, **ex}, loss_target)

    if N_MICROBATCH == 1:
        loss, (grad_w, grad_x) = one_microbatch(per_example, given["loss_target"])
    else:
        def body(carry, xs):
            loss_sum, grad_sum = carry
            l_k, (gw_k, gx_k) = one_microbatch(xs[0], xs[1])
            with _jax.named_scope("update"):
                return (loss_sum + l_k, _jax.tree.map(_jnp.add, grad_sum, gw_k)), gx_k

        init = (_jnp.zeros((), _jnp.float32), _jax.tree.map(_jnp.zeros_like, weights))
        (loss, grad_w), grad_x = _jax.lax.scan(body, init, (per_example, given["loss_target"]))
    with _jax.named_scope("update"):
        delta_w, new_m, new_v = {}, {}, {}
        for n in TWIN_WEIGHTS:
            delta_w[n], new_m[n], new_v[n] = _adamw(weights[n], grad_w[n], given["m_" + n], given["v_" + n])
    return (loss, grad_x, *[grad_w[n] for n in TWIN_WEIGHTS], *[delta_w[n] for n in TWIN_WEIGHTS],
            *[new_m[n] for n in TWIN_WEIGHTS], *[new_v[n] for n in TWIN_WEIGHTS])
```

```python
import functools
import math

import jax
import jax.numpy as jnp
from jax import lax
from jax.experimental import pallas as pl
from jax.experimental.pallas import tpu as pltpu

F32 = jnp.float32
BF16 = jnp.bfloat16

DEPTH = 4
N_DEV = 8
EPS = 1e-6
CONV_WIDTH = 31
CONV_PAD = 32
CHUNK = 128
SGU_GROUPS = 8

ADAM_LR = 0.001
ADAM_B1 = 0.9
ADAM_B2 = 0.999
ADAM_EPS = 1e-08
ADAM_WD = 0.01
ADAM_STEP = 10

VMEM_BYTES_V7X = 64 * 1024 * 1024
VMEM_COMPILER_SLACK = 12 * 1024 * 1024
MESH_AXES = ("x", "y", "c")
ANY = pl.BlockSpec(memory_space=pl.ANY)


def _nbytes(shape, dtype):
    return math.prod(shape) * jnp.dtype(dtype).itemsize


def _params(block_bytes, ngrid, single_bytes=0):
    limit = min(2 * block_bytes + single_bytes + VMEM_COMPILER_SLACK, VMEM_BYTES_V7X - 4 * 1024 * 1024)
    return pltpu.CompilerParams(dimension_semantics=("arbitrary",) * ngrid, vmem_limit_bytes=int(limit))


def _mm_body(dims, nk, kaxis):
    def body(a_ref, b_ref, o_ref, *acc):
        def prod():
            return lax.dot_general(a_ref[...], b_ref[...], (dims, ((), ())), preferred_element_type=F32)

        if nk == 1:
            o_ref[...] = prod().astype(o_ref.dtype)
            return
        acc_ref, = acc
        k = pl.program_id(kaxis)

        @pl.when(k == 0)
        def _():
            acc_ref[...] = prod()

        @pl.when(k > 0)
        def _():
            acc_ref[...] += prod()

        @pl.when(k == nk - 1)
        def _():
            o_ref[...] = acc_ref[...].astype(o_ref.dtype)

    return body


def _mm_call(name, a, b, dims, grid, a_spec, b_spec, o_spec, out_shape, out_dtype, nk, kaxis, acc_shape):
    blocks = (_nbytes([d for d in a_spec.block_shape if d], a.dtype) + _nbytes([d for d in b_spec.block_shape if d], b.dtype)
              + _nbytes([d for d in o_spec.block_shape if d], out_dtype))
    scratch = [pltpu.VMEM(acc_shape, F32)] if nk > 1 else []
    acc_bytes = _nbytes(acc_shape, F32) * (2 if nk > 1 else 1)
    return pl.pallas_call(
        _mm_body(dims, nk, kaxis), name=name, grid=grid, in_specs=[a_spec, b_spec], out_specs=o_spec,
        out_shape=jax.ShapeDtypeStruct(out_shape, out_dtype), scratch_shapes=scratch,
        compiler_params=_params(blocks, len(grid), acc_bytes),
    )(a, b)


def mm_nn(name, a, b, out_dtype, tm, tn=None, tk=None):
    m, k = a.shape
    tm = min(tm, m)
    if b.ndim == 3:
        nblk, _, nb = b.shape
        return _mm_call(name, a, b, ((1,), (0,)), (nblk, m // tm),
                        pl.BlockSpec((tm, k), lambda j, i: (i, 0)), pl.BlockSpec((None, k, nb), lambda j, i: (j, 0, 0)),
                        pl.BlockSpec((tm, nb), lambda j, i: (i, j)), (m, nblk * nb), out_dtype, 1, 0, (tm, nb))
    n = b.shape[1]
    tn = tn or n
    tk = tk or k
    nk = k // tk
    return _mm_call(name, a, b, ((1,), (0,)), (n // tn, m // tm, nk),
                    pl.BlockSpec((tm, tk), lambda j, i, kk: (i, kk)), pl.BlockSpec((tk, tn), lambda j, i, kk: (kk, j)),
                    pl.BlockSpec((tm, tn), lambda j, i, kk: (i, j)), (m, n), out_dtype, nk, 2, (tm, tn))


def mm_nt(name, a, b, out_dtype, tm, tn=None):
    m = a.shape[0]
    tm = min(tm, m)
    if b.ndim == 3:
        kblk, n, kb = b.shape
        return _mm_call(name, a, b, ((1,), (1,)), (m // tm, kblk),
                        pl.BlockSpec((tm, kb), lambda i, kk: (i, kk)), pl.BlockSpec((None, n, kb), lambda i, kk: (kk, 0, 0)),
                        pl.BlockSpec((tm, n), lambda i, kk: (i, 0)), (m, n), out_dtype, kblk, 1, (tm, n))
    n, kc = b.shape
    tn = tn or n
    return _mm_call(name, a, b, ((1,), (1,)), (n // tn, m // tm),
                    pl.BlockSpec((tm, kc), lambda j, i: (i, 0)), pl.BlockSpec((tn, kc), lambda j, i: (j, 0)),
                    pl.BlockSpec((tm, tn), lambda j, i: (i, j)), (m, n), out_dtype, 1, 0, (tm, tn))


def mm_tn(name, a, b, out_dtype, tm, tr, nb=None):
    m, k = a.shape
    n = b.shape[1]
    tm = min(tm, m)
    nm = m // tm
    if nb is not None:
        return _mm_call(name, a, b, ((0,), (0,)), (n // nb, k // tr, nm),
                        pl.BlockSpec((tm, tr), lambda j, r, mm: (mm, r)), pl.BlockSpec((tm, nb), lambda j, r, mm: (mm, j)),
                        pl.BlockSpec((None, tr, nb), lambda j, r, mm: (j, r, 0)), (n // nb, k, nb), out_dtype, nm, 2, (tr, nb))
    return _mm_call(name, a, b, ((0,), (0,)), (k // tr, nm),
                    pl.BlockSpec((tm, tr), lambda r, mm: (mm, r)), pl.BlockSpec((tm, n), lambda r, mm: (mm, 0)),
                    pl.BlockSpec((tr, n), lambda r, mm: (r, 0)), (k, n), out_dtype, nm, 1, (tr, n))


def _row_call(name, body, grid, in_specs, out_specs, out_shape, arrays, scratch=(), aliases=None, vmem_blocks=0):
    return pl.pallas_call(
        body, name=name, grid=grid, in_specs=in_specs, out_specs=out_specs, out_shape=out_shape,
        scratch_shapes=list(scratch), input_output_aliases=aliases or {},
        compiler_params=_params(vmem_blocks, len(grid)),
    )(*arrays)


def _rows(tm, d, col=0):
    return pl.BlockSpec((tm, d), lambda i, *_: (i, col))


def _vec(d):
    return pl.BlockSpec((1, d), lambda *_: (0, 0))


def _rstd(x):
    return lax.rsqrt(jnp.mean(x * x, axis=-1, keepdims=True) + EPS)


def _rms_bwd(dy, x, g):
    r = _rstd(x)
    n = x * r
    w = dy * g
    dx = r * (w - n * jnp.mean(w * n, axis=-1, keepdims=True))
    return dx, jnp.sum(dy * n, axis=0, keepdims=True)


def _accumulate(ref, value, first):
    @pl.when(first)
    def _():
        ref[...] = value

    @pl.when(jnp.logical_not(first))
    def _():
        ref[...] += value


def rms_fwd(x, g, tm=256):
    s, d = x.shape
    tm = min(tm, s)

    def body(x_ref, g_ref, h_ref):
        xv = x_ref[...]
        h_ref[...] = (xv * _rstd(xv) * g_ref[...]).astype(BF16)

    return _row_call("rms_fwd", body, (s // tm,), [_rows(tm, d), _vec(d)], _rows(tm, d),
                     jax.ShapeDtypeStruct((s, d), BF16), (x, g), vmem_blocks=tm * d * 6)


def norm_res(x_in, o, g_post, g_next, tm=256):
    s, d = x_in.shape
    tm = min(tm, s)

    def body(x_ref, o_ref, gp_ref, gn_ref, xo_ref, h_ref):
        ov = o_ref[...]
        xo = x_ref[...] + (ov * _rstd(ov) * gp_ref[...])
        xo_ref[...] = xo
        h_ref[...] = (xo * _rstd(xo) * gn_ref[...]).astype(BF16)

    return _row_call("norm_res", body, (s // tm,), [_rows(tm, d), _rows(tm, d), _vec(d), _vec(d)],
                     [_rows(tm, d), _rows(tm, d)],
                     [jax.ShapeDtypeStruct((s, d), F32), jax.ShapeDtypeStruct((s, d), BF16)],
                     (x_in, o, g_post, g_next), vmem_blocks=tm * d * 14)


def final_norm_loss(x_in, o, g_post, target, tm=256):
    s, d = x_in.shape
    tm = min(tm, s)

    def body(x_ref, o_ref, gp_ref, t_ref, loss_ref, dy_ref, do_ref, dg_ref):
        first = pl.program_id(0) == 0
        ov = o_ref[...]
        g = gp_ref[...]
        diff = x_ref[...] + (ov * _rstd(ov) * g) - t_ref[...]
        _accumulate(loss_ref, jnp.sum(diff * diff, axis=0, keepdims=True), first)
        dy = diff * (1.0 / d)
        dy_ref[...] = dy
        do, dg = _rms_bwd(dy, ov, g)
        do_ref[...] = do.astype(BF16)
        _accumulate(dg_ref, dg, first)

    return _row_call("final_norm_loss", body, (s // tm,), [_rows(tm, d), _rows(tm, d), _vec(d), _rows(tm, d)],
                     [_vec(d), _rows(tm, d), _rows(tm, d), _vec(d)],
                     [jax.ShapeDtypeStruct((1, d), F32), jax.ShapeDtypeStruct((s, d), F32),
                      jax.ShapeDtypeStruct((s, d), BF16), jax.ShapeDtypeStruct((1, d), F32)],
                     (x_in, o, g_post, target), vmem_blocks=tm * d * 18)


def norm_bwd_in_out(dx_out, dh, x_in, g_pre, o_below, g_post_below, tm=256):
    s, d = x_in.shape
    tm = min(tm, s)

    def body(dxo_ref, dh_ref, x_ref, g_ref, o_ref, gb_ref, dxi_ref, do_ref, dg_ref, dgb_ref):
        first = pl.program_id(0) == 0
        dx, dg = _rms_bwd(dh_ref[...], x_ref[...], g_ref[...])
        dxi = dxo_ref[...] + dx
        dxi_ref[...] = dxi
        _accumulate(dg_ref, dg, first)
        do, dgb = _rms_bwd(dxi, o_ref[...], gb_ref[...])
        do_ref[...] = do.astype(BF16)
        _accumulate(dgb_ref, dgb, first)

    return _row_call("norm_bwd_in_out", body, (s // tm,),
                     [_rows(tm, d), _rows(tm, d), _rows(tm, d), _vec(d), _rows(tm, d), _vec(d)],
                     [_rows(tm, d), _rows(tm, d), _vec(d), _vec(d)],
                     [jax.ShapeDtypeStruct((s, d), F32), jax.ShapeDtypeStruct((s, d), BF16),
                      jax.ShapeDtypeStruct((1, d), F32), jax.ShapeDtypeStruct((1, d), F32)],
                     (dx_out, dh, x_in, g_pre, o_below, g_post_below), vmem_blocks=tm * d * 22)


def norm_bwd_in(dx_out, dh, x_in, g_pre, tm=256):
    s, d = x_in.shape
    tm = min(tm, s)

    def body(dxo_ref, dh_ref, x_ref, g_ref, dxi_ref, dg_ref):
        dx, dg = _rms_bwd(dh_ref[...], x_ref[...], g_ref[...])
        dxi_ref[...] = dxo_ref[...] + dx
        _accumulate(dg_ref, dg, pl.program_id(0) == 0)

    return _row_call("norm_bwd_in", body, (s // tm,), [_rows(tm, d), _rows(tm, d), _rows(tm, d), _vec(d)],
                     [_rows(tm, d), _vec(d)],
                     [jax.ShapeDtypeStruct((s, d), F32), jax.ShapeDtypeStruct((1, d), F32)],
                     (dx_out, dh, x_in, g_pre), vmem_blocks=tm * d * 16)


CONV_COLS = 256
CONV_ROWS = 32


def _depthwise(ext_ref, w, n_out, in_off, flip, emit):
    for r0 in range(0, n_out, CONV_ROWS):
        acc = None
        for k in range(CONV_WIDTH):
            off = r0 + in_off + (CONV_WIDTH - 1 - k if flip else k)
            term = w[k:k + 1, :] * ext_ref[off:off + CONV_ROWS, :]
            acc = term if acc is None else acc + term
        emit(r0, acc)


def glu_conv_fwd(proj, conv_w, conv_b, c_ch, tm=512):
    s = proj.shape[0]
    tm = min(tm, s)
    ncb = c_ch // CONV_COLS
    hb = tm // CONV_PAD

    def body(a_ref, g_ref, ah_ref, gh_ref, w_ref, b_ref, c_ref, ext_ref):
        i = pl.program_id(1)
        halo = ah_ref[...].astype(F32) * jax.nn.sigmoid(gh_ref[...].astype(F32))
        ext_ref[0:CONV_PAD, :] = jnp.where(i > 0, halo, 0.0)
        ext_ref[CONV_PAD:, :] = a_ref[...].astype(F32) * jax.nn.sigmoid(g_ref[...].astype(F32))
        w = w_ref[...]
        bias = b_ref[...]

        def emit(r0, acc):
            c_ref[r0:r0 + CONV_ROWS, :] = acc + bias

        _depthwise(ext_ref, w, tm, CONV_PAD - (CONV_WIDTH - 1), False, emit)

    main = lambda col0: pl.BlockSpec((tm, CONV_COLS), lambda c, i: (i, col0 + c))
    halo = lambda col0: pl.BlockSpec((CONV_PAD, CONV_COLS), lambda c, i: (jnp.maximum(i * hb - 1, 0), col0 + c))
    return _row_call("glu_conv_fwd", body, (ncb, s // tm),
                     [main(0), main(ncb), halo(0), halo(ncb),
                      pl.BlockSpec((CONV_PAD, CONV_COLS), lambda c, i: (0, c)), pl.BlockSpec((1, CONV_COLS), lambda c, i: (0, c))],
                     pl.BlockSpec((tm, CONV_COLS), lambda c, i: (i, c)), jax.ShapeDtypeStruct((s, c_ch), F32),
                     (proj, proj, proj, proj, conv_w, conv_b),
                     scratch=[pltpu.VMEM((tm + CONV_PAD, CONV_COLS), F32)], vmem_blocks=tm * CONV_COLS * 16)


def _layer_norm_stats(x):
    mu = jnp.mean(x, axis=-1, keepdims=True)
    xc = x - mu
    rstd = lax.rsqrt(jnp.mean(xc * xc, axis=-1, keepdims=True) + EPS)
    return xc * rstd, rstd


def _layer_norm_bwd(dy, xhat, rstd, g):
    dxh = dy * g
    return rstd * (dxh - jnp.mean(dxh, axis=-1, keepdims=True) - xhat * jnp.mean(dxh * xhat, axis=-1, keepdims=True))


def ln_silu_fwd(c, ln_g, ln_b, tm=256):
    s, d = c.shape
    tm = min(tm, s)

    def body(c_ref, g_ref, b_ref, s_ref):
        xhat, _ = _layer_norm_stats(c_ref[...])
        s_ref[...] = jax.nn.silu(xhat * g_ref[...] + b_ref[...]).astype(BF16)

    return _row_call("ln_silu_fwd", body, (s // tm,), [_rows(tm, d), _vec(d), _vec(d)], _rows(tm, d),
                     jax.ShapeDtypeStruct((s, d), BF16), (c, ln_g, ln_b), vmem_blocks=tm * d * 10)


def ln_silu_bwd(c, ds, ln_g, ln_b, tm=256):
    s, d = c.shape
    tm = min(tm, s)

    def body(c_ref, ds_ref, g_ref, b_ref, dc_ref, dg_ref, db_ref, dcb_ref):
        first = pl.program_id(0) == 0
        g = g_ref[...]
        xhat, rstd = _layer_norm_stats(c_ref[...])
        y = xhat * g + b_ref[...]
        sg = jax.nn.sigmoid(y)
        dln = ds_ref[...] * (sg * (1.0 + y * (1.0 - sg)))
        _accumulate(dg_ref, jnp.sum(dln * xhat, axis=0, keepdims=True), first)
        _accumulate(db_ref, jnp.sum(dln, axis=0, keepdims=True), first)
        dc = _layer_norm_bwd(dln, xhat, rstd, g)
        dc_ref[...] = dc
        _accumulate(dcb_ref, jnp.sum(dc, axis=0, keepdims=True), first)

    vec = jax.ShapeDtypeStruct((1, d), F32)
    return _row_call("ln_silu_bwd", body, (s // tm,), [_rows(tm, d), _rows(tm, d), _vec(d), _vec(d)],
                     [_rows(tm, d), _vec(d), _vec(d), _vec(d)], [jax.ShapeDtypeStruct((s, d), F32), vec, vec, vec],
                     (c, ds, ln_g, ln_b), vmem_blocks=tm * d * 20)


def conv_bwd(proj, dc, conv_w, c_ch, tm=512):
    s = proj.shape[0]
    tm = min(tm, s)
    ncb = c_ch // CONV_COLS
    hb = tm // CONV_PAD
    last_halo = s // CONV_PAD - 1
    n_i = s // tm

    def body(a_ref, g_ref, ah_ref, gh_ref, dc_ref, dcn_ref, w_ref, dglu_ref, dw_ref, ext_ref, dce_ref, dwacc_ref):
        i = pl.program_id(1)
        halo = ah_ref[...].astype(F32) * jax.nn.sigmoid(gh_ref[...].astype(F32))
        ext_ref[0:CONV_PAD, :] = jnp.where(i > 0, halo, 0.0)
        ext_ref[CONV_PAD:, :] = a_ref[...].astype(F32) * jax.nn.sigmoid(g_ref[...].astype(F32))
        dce_ref[0:tm, :] = dc_ref[...]
        dce_ref[tm:, :] = jnp.where(i < n_i - 1, dcn_ref[...], 0.0)
        w = w_ref[...]

        def emit(r0, acc):
            dglu_ref[r0:r0 + CONV_ROWS, :] = acc

        _depthwise(dce_ref, w, tm, 0, True, emit)

        for k in range(CONV_WIDTH):
            acc = None
            for r0 in range(0, tm, CONV_ROWS):
                off = r0 + CONV_PAD - (CONV_WIDTH - 1) + k
                term = dce_ref[r0:r0 + CONV_ROWS, :] * ext_ref[off:off + CONV_ROWS, :]
                acc = term if acc is None else acc + term
            dwacc_ref[k:k + 1, :] = jnp.sum(acc, axis=0, keepdims=True)
        dwacc_ref[CONV_WIDTH:, :] = jnp.zeros((CONV_PAD - CONV_WIDTH, CONV_COLS), F32)
        _accumulate(dw_ref, dwacc_ref[...], i == 0)

    main = lambda col0: pl.BlockSpec((tm, CONV_COLS), lambda c, i: (i, col0 + c))
    halo = lambda col0: pl.BlockSpec((CONV_PAD, CONV_COLS), lambda c, i: (jnp.maximum(i * hb - 1, 0), col0 + c))
    nxt = pl.BlockSpec((CONV_PAD, CONV_COLS), lambda c, i: (jnp.minimum((i + 1) * hb, last_halo), c))
    wspec = pl.BlockSpec((CONV_PAD, CONV_COLS), lambda c, i: (0, c))
    return _row_call("conv_bwd", body, (ncb, n_i),
                     [main(0), main(ncb), halo(0), halo(ncb), main(0), nxt, wspec],
                     [main(0), wspec],
                     [jax.ShapeDtypeStruct((s, c_ch), F32), jax.ShapeDtypeStruct((CONV_PAD, c_ch), F32)],
                     (proj, proj, proj, proj, dc, dc, conv_w),
                     scratch=[pltpu.VMEM((tm + CONV_PAD, CONV_COLS), F32), pltpu.VMEM((tm + CONV_PAD, CONV_COLS), F32),
                              pltpu.VMEM((CONV_PAD, CONV_COLS), F32)],
                     vmem_blocks=tm * CONV_COLS * 24)


def _two_phase_spec(tm, d, col0):
    return pl.BlockSpec((tm, d), lambda i, j: (i, col0 + j))


def glu_bwd(dproj, dglu, proj, c_ch, tm=256):
    s = proj.shape[0]
    tm = min(tm, s)

    def body(_, dglu_ref, a_ref, g_ref, out_ref, stash_ref):
        j = pl.program_id(1)

        @pl.when(j == 0)
        def _():
            dg = dglu_ref[...]
            sg = jax.nn.sigmoid(g_ref[...].astype(F32))
            out_ref[...] = (dg * sg).astype(BF16)
            stash_ref[...] = (dg * a_ref[...].astype(F32) * (sg * (1.0 - sg))).astype(BF16)

        @pl.when(j == 1)
        def _():
            out_ref[...] = stash_ref[...]

    rows = lambda col: pl.BlockSpec((tm, c_ch), lambda i, j: (i, col))
    return _row_call("glu_bwd", body, (s // tm, 2), [ANY, rows(0), rows(0), rows(1)], _two_phase_spec(tm, c_ch, 0),
                     jax.ShapeDtypeStruct(dproj.shape, BF16), (dproj, dglu, proj, proj),
                     scratch=[pltpu.VMEM((tm, c_ch), BF16)], aliases={0: 0}, vmem_blocks=tm * c_ch * 14)


_SQRT_HALF = 0.7071067811865476
_INV_SQRT_2PI = 0.3989422804014327


def _gelu_parts(x):
    cdf = 0.5 * (1.0 + lax.erf(x * _SQRT_HALF))
    return cdf, x * cdf


def _gelu_grad(x, cdf):
    return cdf + x * (_INV_SQRT_2PI * jnp.exp(-0.5 * x * x))


def _sgu_specs(tm, ch):
    grp = ch // SGU_GROUPS
    full3 = lambda shape: pl.BlockSpec(shape, lambda *_: (0, 0, 0))
    return grp, full3((SGU_GROUPS, CHUNK, CHUNK)), full3((SGU_GROUPS, CHUNK, grp))


def sgu_fwd(proj, ln_g, ln_b, w_mix, b_mix, ch, col0, tm=CHUNK):
    s = proj.shape[0]
    grp, wspec, bspec = _sgu_specs(tm, ch)

    def body(u_ref, v_ref, g_ref, b_ref, w_ref, bm_ref, p_ref, mix_ref):
        _, u = _gelu_parts(u_ref[...].astype(F32))
        _, v0 = _gelu_parts(v_ref[...].astype(F32))
        xhat, _ = _layer_norm_stats(v0)
        vn = (xhat * g_ref[...] + b_ref[...]).astype(BF16)
        for n in range(tm // CHUNK):
            for g in range(SGU_GROUPS):
                blk = vn[n * CHUNK:(n + 1) * CHUNK, g * grp:(g + 1) * grp]
                mix_ref[n * CHUNK:(n + 1) * CHUNK, g * grp:(g + 1) * grp] = (
                    jnp.dot(w_ref[g], blk, preferred_element_type=F32) + bm_ref[g])
        p_ref[...] = (u * mix_ref[...]).astype(BF16)

    return _row_call("sgu_fwd", body, (s // tm,),
                     [_rows(tm, ch, col0), _rows(tm, ch, col0 + 1), _vec(ch), _vec(ch), wspec, bspec], _rows(tm, ch),
                     jax.ShapeDtypeStruct((s, ch), BF16), (proj, proj, ln_g, ln_b, w_mix, b_mix),
                     scratch=[pltpu.VMEM((tm, ch), F32)], vmem_blocks=tm * ch * 30)


def sgu_bwd(dproj, proj, dp, ln_g, ln_b, w_mix, w_mix_t, b_mix, ch, col0, tm=CHUNK):
    s = proj.shape[0]
    grp, wspec, bspec = _sgu_specs(tm, ch)

    def body(_, u_ref, v_ref, dp_ref, g_ref, b_ref, w_ref, wt_ref, bm_ref,
             out_ref, dw_ref, dbm_ref, dg_ref, db_ref, stash_ref, mix_ref, dvn_ref):
        i = pl.program_id(0)
        j = pl.program_id(1)

        @pl.when(j == 0)
        def _():
            first = i == 0
            ub = u_ref[...].astype(F32)
            vb = v_ref[...].astype(F32)
            cdf_u, u = _gelu_parts(ub)
            cdf_v, v0 = _gelu_parts(vb)
            g = g_ref[...]
            xhat, rstd = _layer_norm_stats(v0)
            vn = (xhat * g + b_ref[...]).astype(BF16)
            dpv = dp_ref[...]
            dmix = dpv * u
            dmix_bf = dmix.astype(BF16)
            for n in range(tm // CHUNK):
                for k in range(SGU_GROUPS):
                    rows = slice(n * CHUNK, (n + 1) * CHUNK)
                    cols = slice(k * grp, (k + 1) * grp)
                    vblk = vn[rows, cols]
                    dblk = dmix_bf[rows, cols]
                    mix_ref[rows, cols] = jnp.dot(w_ref[k], vblk, preferred_element_type=F32) + bm_ref[k]
                    dvn_ref[rows, cols] = jnp.dot(wt_ref[k], dblk, preferred_element_type=F32)
                    dwk = lax.dot_general(dblk, vblk, (((1,), (1,)), ((), ())), preferred_element_type=F32)
                    dbk = jnp.broadcast_to(jnp.sum(dmix[rows, cols], axis=1, keepdims=True), (CHUNK, CHUNK))
                    if n == 0:
                        _accumulate(dw_ref.at[k], dwk, first)
                        _accumulate(dbm_ref.at[k], dbk, first)
                    else:
                        dw_ref[k] += dwk
                        dbm_ref[k] += dbk
            du = dpv * mix_ref[...]
            dvn = dvn_ref[...]
            _accumulate(dg_ref, jnp.sum(dvn * xhat, axis=0, keepdims=True), first)
            _accumulate(db_ref, jnp.sum(dvn, axis=0, keepdims=True), first)
            dv0 = _layer_norm_bwd(dvn, xhat, rstd, g)
            out_ref[...] = (du * _gelu_grad(ub, cdf_u)).astype(BF16)
            stash_ref[...] = (dv0 * _gelu_grad(vb, cdf_v)).astype(BF16)

        @pl.when(j == 1)
        def _():
            out_ref[...] = stash_ref[...]

    rows = lambda col: pl.BlockSpec((tm, ch), lambda i, j: (i, col))
    vec = pl.BlockSpec((1, ch), lambda i, j: (0, 0))
    acc3 = lambda: pl.BlockSpec((SGU_GROUPS, CHUNK, CHUNK), lambda i, j: (0, 0, 0))
    vshape = jax.ShapeDtypeStruct((1, ch), F32)
    mshape = jax.ShapeDtypeStruct((SGU_GROUPS, CHUNK, CHUNK), F32)
    return _row_call("sgu_bwd", body, (s // tm, 2),
                     [ANY, rows(col0), rows(col0 + 1), rows(0), vec, vec, wspec, wspec, bspec],
                     [_two_phase_spec(tm, ch, col0), acc3(), acc3(), vec, vec],
                     [jax.ShapeDtypeStruct(dproj.shape, BF16), mshape, mshape, vshape, vshape],
                     (dproj, proj, proj, dp, ln_g, ln_b, w_mix, w_mix_t, b_mix),
                     scratch=[pltpu.VMEM((tm, ch), BF16), pltpu.VMEM((tm, ch), F32), pltpu.VMEM((tm, ch), F32)],
                     aliases={0: 0}, vmem_blocks=tm * ch * 60)


def merge_fwd(y_a, y_b, proj, b_gate, col0, tm=256):
    s, d = y_a.shape
    tm = min(tm, s)

    def body(ya_ref, yb_ref, l0_ref, l1_ref, bg_ref, m_ref):
        g0 = jax.nn.sigmoid(l0_ref[...].astype(F32) + bg_ref[0:1, :])
        g1 = jax.nn.sigmoid(l1_ref[...].astype(F32) + bg_ref[1:2, :])
        m_ref[...] = (g0 * ya_ref[...] + g1 * yb_ref[...]).astype(BF16)

    return _row_call("merge_fwd", body, (s // tm,),
                     [_rows(tm, d), _rows(tm, d), _rows(tm, d, col0), _rows(tm, d, col0 + 1), pl.BlockSpec((2, d), lambda i: (0, 0))],
                     _rows(tm, d), jax.ShapeDtypeStruct((s, d), BF16), (y_a, y_b, proj, proj, b_gate), vmem_blocks=tm * d * 16)


def merge_bwd(dm, y_a, y_b, proj, b_gate, col0, tm=256):
    s, d = y_a.shape
    tm = min(tm, s)

    def body(dm_ref, ya_ref, yb_ref, l0_ref, l1_ref, bg_ref, dya_ref, dyb_ref, out_ref, dbg_ref, stash_ref):
        i = pl.program_id(0)
        j = pl.program_id(1)

        @pl.when(j == 0)
        def _():
            dmv = dm_ref[...]
            g0 = jax.nn.sigmoid(l0_ref[...].astype(F32) + bg_ref[0:1, :])
            g1 = jax.nn.sigmoid(l1_ref[...].astype(F32) + bg_ref[1:2, :])
            dya_ref[...] = (dmv * g0).astype(BF16)
            dyb_ref[...] = (dmv * g1).astype(BF16)
            dl0 = dmv * ya_ref[...] * (g0 * (1.0 - g0))
            dl1 = dmv * yb_ref[...] * (g1 * (1.0 - g1))
            _accumulate(dbg_ref.at[0:1, :], jnp.sum(dl0, axis=0, keepdims=True), i == 0)
            _accumulate(dbg_ref.at[1:2, :], jnp.sum(dl1, axis=0, keepdims=True), i == 0)
            out_ref[...] = dl0.astype(BF16)
            stash_ref[...] = dl1.astype(BF16)

        @pl.when(j == 1)
        def _():
            out_ref[...] = stash_ref[...]

    rows = lambda col: pl.BlockSpec((tm, d), lambda i, j: (i, col))
    bgspec = pl.BlockSpec((2, d), lambda i, j: (0, 0))
    return _row_call("merge_bwd", body, (s // tm, 2),
                     [rows(0), rows(0), rows(0), rows(col0), rows(col0 + 1), bgspec],
                     [rows(0), rows(0), _two_phase_spec(tm, d, col0), bgspec],
                     [jax.ShapeDtypeStruct((s, d), BF16), jax.ShapeDtypeStruct((s, d), BF16),
                      jax.ShapeDtypeStruct(proj.shape, BF16), jax.ShapeDtypeStruct((2, d), F32)],
                     (dm, y_a, y_b, proj, proj, b_gate), scratch=[pltpu.VMEM((tm, d), BF16)], vmem_blocks=tm * d * 30)


def swiglu_fwd(gu, tm=512):
    s, w2 = gu.shape
    tm = min(tm, s)
    nb = N_DEV // 2
    cb = w2 // N_DEV

    def body(g_ref, u_ref, f_ref):
        f_ref[...] = (jax.nn.silu(g_ref[...].astype(F32)) * u_ref[...].astype(F32)).astype(BF16)

    return _row_call("swiglu_fwd", body, (s // tm, nb),
                     [pl.BlockSpec((tm, cb), lambda i, j: (i, j)), pl.BlockSpec((tm, cb), lambda i, j: (i, j + nb))],
                     pl.BlockSpec((tm, cb), lambda i, j: (i, j)), jax.ShapeDtypeStruct((s, w2 // 2), BF16), (gu, gu),
                     vmem_blocks=tm * cb * 12)


def swiglu_bwd(gu, df, tm=512):
    s, w2 = gu.shape
    tm = min(tm, s)
    nb = N_DEV // 2
    cb = w2 // N_DEV

    def body(g_ref, u_ref, df_ref, out_ref):
        j = pl.program_id(1)
        g = g_ref[...].astype(F32)
        sg = jax.nn.sigmoid(g)
        dfv = df_ref[...].astype(F32)

        @pl.when(j < nb)
        def _():
            out_ref[...] = (dfv * u_ref[...].astype(F32) * (sg * (1.0 + g * (1.0 - sg)))).astype(BF16)

        @pl.when(j >= nb)
        def _():
            out_ref[...] = (dfv * (g * sg)).astype(BF16)

    return _row_call("swiglu_bwd", body, (s // tm, N_DEV),
                     [pl.BlockSpec((tm, cb), lambda i, j: (i, j % nb)), pl.BlockSpec((tm, cb), lambda i, j: (i, j % nb + nb)),
                      pl.BlockSpec((tm, cb), lambda i, j: (i, j % nb))],
                     pl.BlockSpec((tm, cb), lambda i, j: (i, j)), jax.ShapeDtypeStruct((s, w2), BF16), (gu, gu, df),
                     vmem_blocks=tm * cb * 16)


def _peers():
    x, y, c = lax.axis_index("x"), lax.axis_index("y"), lax.axis_index("c")
    me = 4 * x + 2 * y + c
    peers = []
    for k in range(1, N_DEV):
        px = 1 - x if k & 4 else x
        py = 1 - y if k & 2 else y
        pc = 1 - c if k & 1 else c
        peers.append(((px, py, pc), 4 * px + 2 * py + pc))
    return me, peers


def _exchange(name, arrays, scatter):
    n = len(arrays)

    def body(*refs):
        ins, outs = refs[:n], refs[n:2 * n]
        send_sems, recv_sems, local_sems = refs[2 * n:]
        me, peers = _peers()

        def remote(a, k):
            (pos, idx) = peers[k]
            src = ins[a].at[idx] if scatter else ins[a]
            return pltpu.make_async_remote_copy(src_ref=src, dst_ref=outs[a].at[me], send_sem=send_sems.at[a, k],
                                                recv_sem=recv_sems.at[a, k], device_id=pos, device_id_type=pl.DeviceIdType.MESH)

        def arrival(a, k):
            (pos, idx) = peers[k]
            src = ins[a].at[idx] if scatter else ins[a]
            return pltpu.make_async_remote_copy(src_ref=src, dst_ref=outs[a].at[idx], send_sem=send_sems.at[a, k],
                                                recv_sem=recv_sems.at[a, k], device_id=pos, device_id_type=pl.DeviceIdType.MESH)

        local = [pltpu.make_async_copy(ins[a].at[me] if scatter else ins[a], outs[a].at[me], local_sems.at[a]) for a in range(n)]
        sends = [remote(a, k) for k in range(N_DEV - 1) for a in range(n)]
        for cp in sends:
            cp.start()
        for cp in local:
            cp.start()
        for k in range(N_DEV - 1):
            for a in range(n):
                arrival(a, k).wait_recv()
        for cp in sends:
            cp.wait_send()
        for cp in local:
            cp.wait()

    out_shape = [jax.ShapeDtypeStruct(a.shape if scatter else (N_DEV,) + a.shape, a.dtype) for a in arrays]
    return pl.pallas_call(
        body, name=name, in_specs=[ANY] * n, out_specs=[ANY] * n, out_shape=out_shape,
        scratch_shapes=[pltpu.SemaphoreType.DMA((n, N_DEV - 1)), pltpu.SemaphoreType.DMA((n, N_DEV - 1)),
                        pltpu.SemaphoreType.DMA((n,))],
    )(*arrays)


def _row_tile(r, cap):
    if r <= cap:
        return r
    return max(t for t in range(16, cap + 1, 16) if r % t == 0)


def sum_adamw(name, parts, w, m, v, tr):
    nl, r, c = w.shape
    tr = _row_tile(r, tr)
    c1 = 1.0 - ADAM_B1 ** ADAM_STEP
    c2 = 1.0 - ADAM_B2 ** ADAM_STEP

    def body(*refs):
        part_refs = refs[:nl]
        w_ref, m_ref, v_ref, g_out, d_out, m_out, v_out = refs[nl:]
        layer = pl.program_id(0)
        for j in range(nl):
            @pl.when(layer == j)
            def _(j=j):
                g = part_refs[j][0].astype(F32)
                for p in range(1, N_DEV):
                    g = g + part_refs[j][p].astype(F32)
                mn = ADAM_B1 * m_ref[...] + (1.0 - ADAM_B1) * g
                vn = ADAM_B2 * v_ref[...] + (1.0 - ADAM_B2) * (g * g)
                g_out[...] = g
                m_out[...] = mn
                v_out[...] = vn
                d_out[...] = -ADAM_LR * ((mn / c1) / (jnp.sqrt(vn / c2) + ADAM_EPS) + ADAM_WD * w_ref[...])

    def part_spec(j):
        return pl.BlockSpec((N_DEV, tr, c), lambda l, i: (0, jnp.where(l == j, i, 0), 0))

    lspec = pl.BlockSpec((None, tr, c), lambda l, i: (l, i, 0))
    out = jax.ShapeDtypeStruct((nl, r, c), F32)
    return _row_call(name, body, (nl, r // tr), [part_spec(j) for j in range(nl)] + [lspec] * 3, [lspec] * 4, [out] * 4,
                     tuple(parts) + (w, m, v), vmem_blocks=nl * N_DEV * tr * c * parts[0].dtype.itemsize + 7 * tr * c * 4)


REPLICATED = ("norm_mix_pre", "norm_mix_post", "norm_ffn_pre", "norm_ffn_post", "conv_b", "conv_ln_g", "conv_ln_b",
              "sgu_ln_g", "sgu_ln_b", "w_spatial", "b_spatial")
MATRICES = ("w_in", "w_a_out", "w_b_out", "w_o", "w_gate_up", "w_down")


def local_step(x, target, rep, mats, b_gate, conv_w):
    s, d = x.shape
    causal = jnp.tril(jnp.ones((CHUNK, CHUNK), dtype=bool))
    row = lambda name, l: rep[name][l].reshape(1, -1)

    saved = []
    h = rms_fwd(x, row("norm_mix_pre", 0))
    for l in range(DEPTH):
        wl = mats[l]
        w_mix = jnp.where(causal[None], rep["w_spatial"][l], 0.0).astype(BF16)
        b_mix = jnp.broadcast_to(rep["b_spatial"][l][:, :, None], (SGU_GROUPS, CHUNK, d // SGU_GROUPS))
        proj = mm_nn("proj", h, wl["w_in"], BF16, 512)
        c = glu_conv_fwd(proj, conv_w[l], row("conv_b", l), d)
        s_act = ln_silu_fwd(c, row("conv_ln_g", l), row("conv_ln_b", l))
        p_act = sgu_fwd(proj, row("sgu_ln_g", l), row("sgu_ln_b", l), w_mix, b_mix, d, 2)
        y_a = mm_nn("branch_out", s_act, wl["w_a_out"], F32, 512)
        y_b = mm_nn("branch_out", p_act, wl["w_b_out"], F32, 512)
        merged = merge_fwd(y_a, y_b, proj, b_gate[l], 4)
        o = mm_nn("branch_out", merged, wl["w_o"], F32, 512)
        x_mid, h2 = norm_res(x, o, row("norm_mix_post", l), row("norm_ffn_pre", l))
        gu = mm_nn("gate_up", h2, wl["w_gate_up"], BF16, 512)
        f = swiglu_fwd(gu)
        o2 = mm_nn("down", f, wl["w_down"], F32, 512, tn=1024, tk=f.shape[1] // 4)
        saved.append(dict(x_in=x, h=h, proj=proj, c=c, s_act=s_act, p_act=p_act, y_a=y_a, y_b=y_b, merged=merged, o=o,
                          x_mid=x_mid, h2=h2, gu=gu, f=f, o2=o2, w_mix=w_mix, b_mix=b_mix))
        if l + 1 < DEPTH:
            x, h = norm_res(x_mid, o2, row("norm_ffn_post", l), row("norm_mix_pre", l + 1))

    top = saved[-1]
    loss_vec, dx, do2, dg_ffn_post = final_norm_loss(top["x_mid"], top["o2"], row("norm_ffn_post", DEPTH - 1), target)
    loss = (0.5 / d) * jnp.sum(loss_vec)

    grads = [None] * DEPTH
    for l in reversed(range(DEPTH)):
        sv, wl = saved[l], mats[l]
        g = {"norm_ffn_post": dg_ffn_post}
        df = mm_nt("d_down_in", do2, wl["w_down"], BF16, 512, tn=wl["w_down"].shape[0] // 4)
        g["w_down"] = mm_tn("d_down_w", sv["f"], do2, BF16, 512, sv["f"].shape[1] // 4)
        dgu = swiglu_bwd(sv["gu"], df)
        dh2 = mm_nt("d_gate_up_in", dgu, wl["w_gate_up"], F32, 1024)
        g["w_gate_up"] = mm_tn("d_gate_up_w", sv["h2"], dgu, BF16, 512, d // 2, nb=wl["w_gate_up"].shape[2])
        dx, do, g["norm_ffn_pre"], g["norm_mix_post"] = norm_bwd_in_out(
            dx, dh2, sv["x_mid"], row("norm_ffn_pre", l), sv["o"], row("norm_mix_post", l))
        dm = mm_nt("d_square_in", do, wl["w_o"], F32, 512)
        g["w_o"] = mm_tn("d_square_w", sv["merged"], do, BF16, 512, d // 2)
        dy_a, dy_b, dproj, g["b_gate"] = merge_bwd(dm, sv["y_a"], sv["y_b"], sv["proj"], b_gate[l], 4)
        ds = mm_nt("d_square_in", dy_a, wl["w_a_out"], F32, 512)
        g["w_a_out"] = mm_tn("d_square_w", sv["s_act"], dy_a, BF16, 512, d // 2)
        dp = mm_nt("d_square_in", dy_b, wl["w_b_out"], F32, 512)
        g["w_b_out"] = mm_tn("d_square_w", sv["p_act"], dy_b, BF16, 512, d // 2)
        dc, g["conv_ln_g"], g["conv_ln_b"], g["conv_b"] = ln_silu_bwd(sv["c"], ds, row("conv_ln_g", l), row("conv_ln_b", l))
        dglu, g["conv_w"] = conv_bwd(sv["proj"], dc, conv_w[l], d)
        dproj = glu_bwd(dproj, dglu, sv["proj"], d)
        dproj, dw_mix, db_mix, g["sgu_ln_g"], g["sgu_ln_b"] = sgu_bwd(
            dproj, sv["proj"], dp, row("sgu_ln_g", l), row("sgu_ln_b", l), sv["w_mix"],
            jnp.swapaxes(sv["w_mix"], 1, 2), sv["b_mix"], d, 2)
        g["w_spatial"] = jnp.where(causal[None], dw_mix, 0.0)
        g["b_spatial"] = db_mix[:, :, 0]
        dh = mm_nt("d_in_in", dproj, wl["w_in"], F32, 1024)
        g["w_in"] = mm_tn("d_in_w", sv["h"], dproj, BF16, 512, d // 2, nb=wl["w_in"].shape[2])
        if l > 0:
            below = saved[l - 1]
            dx, do2, g["norm_mix_pre"], dg_ffn_post = norm_bwd_in_out(
                dx, dh, sv["x_in"], row("norm_mix_pre", l), below["o2"], row("norm_ffn_post", l - 1))
        else:
            dx, g["norm_mix_pre"] = norm_bwd_in(dx, dh, sv["x_in"], row("norm_mix_pre", l))
        grads[l] = g
    return loss, dx, grads


def _pack_rows(arrays):
    return jnp.concatenate([a.reshape(-1, 128) for a in arrays], axis=0)


def _unpack_rows(packed, shapes):
    out, r0 = [], 0
    for shp in shapes:
        nr = math.prod(shp) // 128
        out.append(packed[r0:r0 + nr].reshape(shp))
        r0 += nr
    return out


def kernel(x, norm_mix_pre, norm_mix_post, norm_ffn_pre, norm_ffn_post, w_in, b_gate, conv_w, conv_b, conv_ln_g, conv_ln_b, w_a_out, sgu_ln_g, sgu_ln_b, w_spatial, b_spatial, w_b_out, w_o, w_gate_up, w_down, loss_target, m_norm_mix_pre, m_norm_mix_post, m_norm_ffn_pre, m_norm_ffn_post, m_w_in, m_b_gate, m_conv_w, m_conv_b, m_conv_ln_g, m_conv_ln_b, m_w_a_out, m_sgu_ln_g, m_sgu_ln_b, m_w_spatial, m_b_spatial, m_w_b_out, m_w_o, m_w_gate_up, m_w_down, v_norm_mix_pre, v_norm_mix_post, v_norm_ffn_pre, v_norm_ffn_post, v_w_in, v_b_gate, v_conv_w, v_conv_b, v_conv_ln_g, v_conv_ln_b, v_w_a_out, v_sgu_ln_g, v_sgu_ln_b, v_w_spatial, v_b_spatial, v_w_b_out, v_w_o, v_w_gate_up, v_w_down):
    names = ("norm_mix_pre", "norm_mix_post", "norm_ffn_pre", "norm_ffn_post", "w_in", "b_gate", "conv_w", "conv_b",
             "conv_ln_g", "conv_ln_b", "w_a_out", "sgu_ln_g", "sgu_ln_b", "w_spatial", "b_spatial", "w_b_out", "w_o",
             "w_gate_up", "w_down")
    w = dict(zip(names, (norm_mix_pre, norm_mix_post, norm_ffn_pre, norm_ffn_post, w_in, b_gate, conv_w, conv_b,
                         conv_ln_g, conv_ln_b, w_a_out, sgu_ln_g, sgu_ln_b, w_spatial, b_spatial, w_b_out, w_o,
                         w_gate_up, w_down)))
    m = dict(zip(names, (m_norm_mix_pre, m_norm_mix_post, m_norm_ffn_pre, m_norm_ffn_post, m_w_in, m_b_gate, m_conv_w,
                         m_conv_b, m_conv_ln_g, m_conv_ln_b, m_w_a_out, m_sgu_ln_g, m_sgu_ln_b, m_w_spatial,
                         m_b_spatial, m_w_b_out, m_w_o, m_w_gate_up, m_w_down)))
    v = dict(zip(names, (v_norm_mix_pre, v_norm_mix_post, v_norm_ffn_pre, v_norm_ffn_post, v_w_in, v_b_gate, v_conv_w,
                         v_conv_b, v_conv_ln_g, v_conv_ln_b, v_w_a_out, v_sgu_ln_g, v_sgu_ln_b, v_w_spatial,
                         v_b_spatial, v_w_b_out, v_w_o, v_w_gate_up, v_w_down)))
    d = x.shape[-1]
    shard_cols = d // N_DEV

    def small_pack(bg, cw):
        rows = jnp.concatenate([bg, cw], axis=1).reshape(DEPTH * (2 + CONV_WIDTH), shard_cols)
        return jnp.pad(rows, ((0, (-rows.shape[0]) % 8), (0, 0)))

    small_w, small_m, small_v = (small_pack(t["b_gate"], t["conv_w"]) for t in (w, m, v))

    mats = []
    for l in range(DEPTH):
        shards = [w[name][l].astype(BF16) for name in MATRICES]
        extra = [small_w] if l == 0 else []
        got = _exchange("gather_weights" if l else "gather_weights_first", shards + extra, scatter=False)
        full = dict(zip(MATRICES, got))
        for name in ("w_a_out", "w_b_out", "w_o", "w_down"):
            full[name] = full[name].reshape(-1, d)
        mats.append(full)
        if l == 0:
            small_full = got[-1][:, :DEPTH * (2 + CONV_WIDTH)].reshape(N_DEV, DEPTH, 2 + CONV_WIDTH, shard_cols)
            small_full = jnp.transpose(small_full, (1, 2, 0, 3)).reshape(DEPTH, 2 + CONV_WIDTH, d)
    b_gate_full = small_full[:, :2]
    conv_w_full = jnp.pad(small_full[:, 2:], ((0, 0), (0, CONV_PAD - CONV_WIDTH), (0, 0)))

    rep = {name: w[name] for name in REPLICATED}
    loss, grad_x, grads = local_step(x[0], loss_target[0], rep, mats, b_gate_full, conv_w_full)
    loss = lax.psum(loss, MESH_AXES)

    parts = {name: [None] * DEPTH for name in MATRICES}
    for l in reversed(range(DEPTH)):
        g = grads[l]
        send = [g["w_in"], g["w_a_out"].reshape(N_DEV, -1, d), g["w_b_out"].reshape(N_DEV, -1, d),
                g["w_o"].reshape(N_DEV, -1, d), g["w_gate_up"], g["w_down"].reshape(N_DEV, -1, d)]
        got = _exchange("scatter_grads", send, scatter=True)
        for name, arr in zip(MATRICES, got):
            parts[name][l] = arr
    small_g = jnp.stack([jnp.concatenate([grads[l]["b_gate"], grads[l]["conv_w"][:CONV_WIDTH]], axis=0) for l in range(DEPTH)])
    small_g = jnp.transpose(small_g.reshape(DEPTH * (2 + CONV_WIDTH), N_DEV, shard_cols), (1, 0, 2))
    small_g = jnp.pad(small_g, ((0, 0), (0, small_w.shape[0] - small_g.shape[1]), (0, 0)))
    rep_shapes = [w[name].shape for name in REPLICATED]
    rep_g = _pack_rows([jnp.stack([grads[l][name].reshape(w[name].shape[1:]) for l in range(DEPTH)]) for name in REPLICATED])
    small_parts, = _exchange("scatter_small", [small_g], scatter=True)
    rep_parts, = _exchange("gather_replicated", [rep_g], scatter=False)

    out = {}
    for name in MATRICES:
        res = sum_adamw("adamw_" + name, parts[name], w[name], m[name], v[name], 128)
        out[name] = res
    small_res = sum_adamw("adamw_small", [small_parts], small_w[None], small_m[None], small_v[None], small_w.shape[0])
    n_small = DEPTH * (2 + CONV_WIDTH)
    small_res = [r[0, :n_small].reshape(DEPTH, 2 + CONV_WIDTH, shard_cols) for r in small_res]
    out["b_gate"] = [r[:, :2] for r in small_res]
    out["conv_w"] = [r[:, 2:] for r in small_res]
    rep_res = sum_adamw("adamw_replicated", [rep_parts], *(_pack_rows([t[name] for name in REPLICATED])[None] for t in (w, m, v)), 672)
    rep_res = [_unpack_rows(r[0], rep_shapes) for r in rep_res]
    for i, name in enumerate(REPLICATED):
        out[name] = [r[i] for r in rep_res]

    return (loss, grad_x[None], *[out[name][0] for name in names], *[out[name][1] for name in names],
            *[out[name][2] for name in names], *[out[name][3] for name in names])
```

```python
import functools
import math

import jax
import jax.numpy as jnp
from jax import lax
from jax.experimental import pallas as pl
from jax.experimental.pallas import tpu as pltpu

F32 = jnp.float32
BF16 = jnp.bfloat16

DEPTH = 4
N_DEV = 8
EPS = 1e-6
CONV_WIDTH = 31
CONV_PAD = 32
CHUNK = 128
SGU_GROUPS = 8

ADAM_LR = 0.001
ADAM_B1 = 0.9
ADAM_B2 = 0.999
ADAM_EPS = 1e-08
ADAM_WD = 0.01
ADAM_STEP = 10

VMEM_BYTES_V7X = 64 * 1024 * 1024
VMEM_COMPILER_SLACK = 12 * 1024 * 1024
MESH_AXES = ("x", "y", "c")
ANY = pl.BlockSpec(memory_space=pl.ANY)


def _nbytes(shape, dtype):
    return math.prod(shape) * jnp.dtype(dtype).itemsize


def _params(block_bytes, ngrid, single_bytes=0):
    limit = min(2 * block_bytes + single_bytes + VMEM_COMPILER_SLACK, VMEM_BYTES_V7X - 4 * 1024 * 1024)
    return pltpu.CompilerParams(dimension_semantics=("arbitrary",) * ngrid, vmem_limit_bytes=int(limit))


def _deps(deps):
    return [t for t in deps if t is not None]


def _mm_body(dims, nk, kaxis, ndeps):
    def body(a_ref, b_ref, *rest):
        o_ref, *acc = rest[ndeps:]

        def prod():
            return lax.dot_general(a_ref[...], b_ref[...], (dims, ((), ())), preferred_element_type=F32)

        if nk == 1:
            o_ref[...] = prod().astype(o_ref.dtype)
            return
        acc_ref, = acc
        k = pl.program_id(kaxis)

        @pl.when(k == 0)
        def _():
            acc_ref[...] = prod()

        @pl.when(k > 0)
        def _():
            acc_ref[...] += prod()

        @pl.when(k == nk - 1)
        def _():
            o_ref[...] = acc_ref[...].astype(o_ref.dtype)

    return body


def _mm_call(name, a, b, dims, grid, a_spec, b_spec, o_spec, out_shape, out_dtype, nk, kaxis, acc_shape, deps=()):
    deps = _deps(deps)
    blocks = (_nbytes([d for d in a_spec.block_shape if d], a.dtype) + _nbytes([d for d in b_spec.block_shape if d], b.dtype)
              + _nbytes([d for d in o_spec.block_shape if d], out_dtype))
    scratch = [pltpu.VMEM(acc_shape, F32)] if nk > 1 else []
    acc_bytes = _nbytes(acc_shape, F32) * (2 if nk > 1 else 1)
    return pl.pallas_call(
        _mm_body(dims, nk, kaxis, len(deps)), name=name, grid=grid, in_specs=[a_spec, b_spec] + [ANY] * len(deps),
        out_specs=o_spec, out_shape=jax.ShapeDtypeStruct(out_shape, out_dtype), scratch_shapes=scratch,
        compiler_params=_params(blocks, len(grid), acc_bytes),
    )(a, b, *deps)


def mm_nn(name, a, b, out_dtype, tm, tn=None, tk=None):
    m, k = a.shape
    tm = min(tm, m)
    if b.ndim == 3:
        nblk, _, nb = b.shape
        return _mm_call(name, a, b, ((1,), (0,)), (nblk, m // tm),
                        pl.BlockSpec((tm, k), lambda j, i: (i, 0)), pl.BlockSpec((None, k, nb), lambda j, i: (j, 0, 0)),
                        pl.BlockSpec((tm, nb), lambda j, i: (i, j)), (m, nblk * nb), out_dtype, 1, 0, (tm, nb))
    n = b.shape[1]
    tn = tn or n
    tk = tk or k
    nk = k // tk
    return _mm_call(name, a, b, ((1,), (0,)), (n // tn, m // tm, nk),
                    pl.BlockSpec((tm, tk), lambda j, i, kk: (i, kk)), pl.BlockSpec((tk, tn), lambda j, i, kk: (kk, j)),
                    pl.BlockSpec((tm, tn), lambda j, i, kk: (i, j)), (m, n), out_dtype, nk, 2, (tm, tn))


def mm_nt(name, a, b, out_dtype, tm, tn=None, deps=()):
    m = a.shape[0]
    tm = min(tm, m)
    if b.ndim == 3:
        kblk, n, kb = b.shape
        return _mm_call(name, a, b, ((1,), (1,)), (m // tm, kblk),
                        pl.BlockSpec((tm, kb), lambda i, kk: (i, kk)), pl.BlockSpec((None, n, kb), lambda i, kk: (kk, 0, 0)),
                        pl.BlockSpec((tm, n), lambda i, kk: (i, 0)), (m, n), out_dtype, kblk, 1, (tm, n), deps)
    n, kc = b.shape
    tn = tn or n
    return _mm_call(name, a, b, ((1,), (1,)), (n // tn, m // tm),
                    pl.BlockSpec((tm, kc), lambda j, i: (i, 0)), pl.BlockSpec((tn, kc), lambda j, i: (j, 0)),
                    pl.BlockSpec((tm, tn), lambda j, i: (i, j)), (m, n), out_dtype, 1, 0, (tm, tn), deps)


def mm_tn(name, a, b, out_dtype, tm, tr, nb=None):
    m, k = a.shape
    n = b.shape[1]
    tm = min(tm, m)
    nm = m // tm
    if nb is not None:
        return _mm_call(name, a, b, ((0,), (0,)), (n // nb, k // tr, nm),
                        pl.BlockSpec((tm, tr), lambda j, r, mm: (mm, r)), pl.BlockSpec((tm, nb), lambda j, r, mm: (mm, j)),
                        pl.BlockSpec((None, tr, nb), lambda j, r, mm: (j, r, 0)), (n // nb, k, nb), out_dtype, nm, 2, (tr, nb))
    return _mm_call(name, a, b, ((0,), (0,)), (k // tr, nm),
                    pl.BlockSpec((tm, tr), lambda r, mm: (mm, r)), pl.BlockSpec((tm, n), lambda r, mm: (mm, 0)),
                    pl.BlockSpec((tr, n), lambda r, mm: (r, 0)), (k, n), out_dtype, nm, 1, (tr, n))


def _row_call(name, body, grid, in_specs, out_specs, out_shape, arrays, scratch=(), aliases=None, vmem_blocks=0, deps=()):
    deps = _deps(deps)
    nin = len(arrays)

    def with_deps(*refs):
        body(*refs[:nin], *refs[nin + len(deps):])

    return pl.pallas_call(
        with_deps, name=name, grid=grid, in_specs=list(in_specs) + [ANY] * len(deps), out_specs=out_specs,
        out_shape=out_shape, scratch_shapes=list(scratch), input_output_aliases=aliases or {},
        compiler_params=_params(vmem_blocks, len(grid)),
    )(*arrays, *deps)


def _rows(tm, d, col=0):
    return pl.BlockSpec((tm, d), lambda i, *_: (i, col))


def _vec(d):
    return pl.BlockSpec((1, d), lambda *_: (0, 0))


def _rstd(x):
    return lax.rsqrt(jnp.mean(x * x, axis=-1, keepdims=True) + EPS)


def _rms_bwd(dy, x, g):
    r = _rstd(x)
    n = x * r
    w = dy * g
    dx = r * (w - n * jnp.mean(w * n, axis=-1, keepdims=True))
    return dx, jnp.sum(dy * n, axis=0, keepdims=True)


def _accumulate(ref, value, first):
    @pl.when(first)
    def _():
        ref[...] = value

    @pl.when(jnp.logical_not(first))
    def _():
        ref[...] += value


def rms_fwd(x, g, tm=256, deps=()):
    s, d = x.shape
    tm = min(tm, s)

    def body(x_ref, g_ref, h_ref):
        xv = x_ref[...]
        h_ref[...] = (xv * _rstd(xv) * g_ref[...]).astype(BF16)

    return _row_call("rms_fwd", body, (s // tm,), [_rows(tm, d), _vec(d)], _rows(tm, d),
                     jax.ShapeDtypeStruct((s, d), BF16), (x, g), vmem_blocks=tm * d * 6, deps=deps)


def norm_res(x_in, o, g_post, g_next, tm=256):
    s, d = x_in.shape
    tm = min(tm, s)

    def body(x_ref, o_ref, gp_ref, gn_ref, xo_ref, h_ref):
        ov = o_ref[...]
        xo = x_ref[...] + (ov * _rstd(ov) * gp_ref[...])
        xo_ref[...] = xo
        h_ref[...] = (xo * _rstd(xo) * gn_ref[...]).astype(BF16)

    return _row_call("norm_res", body, (s // tm,), [_rows(tm, d), _rows(tm, d), _vec(d), _vec(d)],
                     [_rows(tm, d), _rows(tm, d)],
                     [jax.ShapeDtypeStruct((s, d), F32), jax.ShapeDtypeStruct((s, d), BF16)],
                     (x_in, o, g_post, g_next), vmem_blocks=tm * d * 14)


def final_norm_loss(x_in, o, g_post, target, tm=256):
    s, d = x_in.shape
    tm = min(tm, s)

    def body(x_ref, o_ref, gp_ref, t_ref, loss_ref, dy_ref, do_ref, dg_ref):
        first = pl.program_id(0) == 0
        ov = o_ref[...]
        g = gp_ref[...]
        diff = x_ref[...] + (ov * _rstd(ov) * g) - t_ref[...]
        _accumulate(loss_ref, jnp.sum(diff * diff, axis=0, keepdims=True), first)
        dy = diff * (1.0 / d)
        dy_ref[...] = dy
        do, dg = _rms_bwd(dy, ov, g)
        do_ref[...] = do.astype(BF16)
        _accumulate(dg_ref, dg, first)

    return _row_call("final_norm_loss", body, (s // tm,), [_rows(tm, d), _rows(tm, d), _vec(d), _rows(tm, d)],
                     [_vec(d), _rows(tm, d), _rows(tm, d), _vec(d)],
                     [jax.ShapeDtypeStruct((1, d), F32), jax.ShapeDtypeStruct((s, d), F32),
                      jax.ShapeDtypeStruct((s, d), BF16), jax.ShapeDtypeStruct((1, d), F32)],
                     (x_in, o, g_post, target), vmem_blocks=tm * d * 18)


def norm_bwd_in_out(dx_out, dh, x_in, g_pre, o_below, g_post_below, tm=256, deps=()):
    s, d = x_in.shape
    tm = min(tm, s)

    def body(dxo_ref, dh_ref, x_ref, g_ref, o_ref, gb_ref, dxi_ref, do_ref, dg_ref, dgb_ref):
        first = pl.program_id(0) == 0
        dx, dg = _rms_bwd(dh_ref[...], x_ref[...], g_ref[...])
        dxi = dxo_ref[...] + dx
        dxi_ref[...] = dxi
        _accumulate(dg_ref, dg, first)
        do, dgb = _rms_bwd(dxi, o_ref[...], gb_ref[...])
        do_ref[...] = do.astype(BF16)
        _accumulate(dgb_ref, dgb, first)

    return _row_call("norm_bwd_in_out", body, (s // tm,),
                     [_rows(tm, d), _rows(tm, d), _rows(tm, d), _vec(d), _rows(tm, d), _vec(d)],
                     [_rows(tm, d), _rows(tm, d), _vec(d), _vec(d)],
                     [jax.ShapeDtypeStruct((s, d), F32), jax.ShapeDtypeStruct((s, d), BF16),
                      jax.ShapeDtypeStruct((1, d), F32), jax.ShapeDtypeStruct((1, d), F32)],
                     (dx_out, dh, x_in, g_pre, o_below, g_post_below), vmem_blocks=tm * d * 22, deps=deps)


def norm_bwd_in(dx_out, dh, x_in, g_pre, tm=256):
    s, d = x_in.shape
    tm = min(tm, s)

    def body(dxo_ref, dh_ref, x_ref, g_ref, dxi_ref, dg_ref):
        dx, dg = _rms_bwd(dh_ref[...], x_ref[...], g_ref[...])
        dxi_ref[...] = dxo_ref[...] + dx
        _accumulate(dg_ref, dg, pl.program_id(0) == 0)

    return _row_call("norm_bwd_in", body, (s // tm,), [_rows(tm, d), _rows(tm, d), _rows(tm, d), _vec(d)],
                     [_rows(tm, d), _vec(d)],
                     [jax.ShapeDtypeStruct((s, d), F32), jax.ShapeDtypeStruct((1, d), F32)],
                     (dx_out, dh, x_in, g_pre), vmem_blocks=tm * d * 16)


CONV_COLS = 256
CONV_ROWS = 32


def _depthwise(ext_ref, w, n_out, in_off, flip, emit):
    for r0 in range(0, n_out, CONV_ROWS):
        acc = None
        for k in range(CONV_WIDTH):
            off = r0 + in_off + (CONV_WIDTH - 1 - k if flip else k)
            term = w[k:k + 1, :] * ext_ref[off:off + CONV_ROWS, :]
            acc = term if acc is None else acc + term
        emit(r0, acc)


def glu_conv_fwd(proj, conv_w, conv_b, c_ch, tm=512):
    s = proj.shape[0]
    tm = min(tm, s)
    ncb = c_ch // CONV_COLS
    hb = tm // CONV_PAD

    def body(a_ref, g_ref, ah_ref, gh_ref, w_ref, b_ref, c_ref, ext_ref):
        i = pl.program_id(1)
        halo = ah_ref[...].astype(F32) * jax.nn.sigmoid(gh_ref[...].astype(F32))
        ext_ref[0:CONV_PAD, :] = jnp.where(i > 0, halo, 0.0)
        ext_ref[CONV_PAD:, :] = a_ref[...].astype(F32) * jax.nn.sigmoid(g_ref[...].astype(F32))
        w = w_ref[...]
        bias = b_ref[...]

        def emit(r0, acc):
            c_ref[r0:r0 + CONV_ROWS, :] = acc + bias

        _depthwise(ext_ref, w, tm, CONV_PAD - (CONV_WIDTH - 1), False, emit)

    main = lambda col0: pl.BlockSpec((tm, CONV_COLS), lambda c, i: (i, col0 + c))
    halo = lambda col0: pl.BlockSpec((CONV_PAD, CONV_COLS), lambda c, i: (jnp.maximum(i * hb - 1, 0), col0 + c))
    return _row_call("glu_conv_fwd", body, (ncb, s // tm),
                     [main(0), main(ncb), halo(0), halo(ncb),
                      pl.BlockSpec((CONV_PAD, CONV_COLS), lambda c, i: (0, c)), pl.BlockSpec((1, CONV_COLS), lambda c, i: (0, c))],
                     pl.BlockSpec((tm, CONV_COLS), lambda c, i: (i, c)), jax.ShapeDtypeStruct((s, c_ch), F32),
                     (proj, proj, proj, proj, conv_w, conv_b),
                     scratch=[pltpu.VMEM((tm + CONV_PAD, CONV_COLS), F32)], vmem_blocks=tm * CONV_COLS * 16)


def _layer_norm_stats(x):
    mu = jnp.mean(x, axis=-1, keepdims=True)
    xc = x - mu
    rstd = lax.rsqrt(jnp.mean(xc * xc, axis=-1, keepdims=True) + EPS)
    return xc * rstd, rstd


def _layer_norm_bwd(dy, xhat, rstd, g):
    dxh = dy * g
    return rstd * (dxh - jnp.mean(dxh, axis=-1, keepdims=True) - xhat * jnp.mean(dxh * xhat, axis=-1, keepdims=True))


def ln_silu_fwd(c, ln_g, ln_b, tm=256):
    s, d = c.shape
    tm = min(tm, s)

    def body(c_ref, g_ref, b_ref, s_ref):
        xhat, _ = _layer_norm_stats(c_ref[...])
        s_ref[...] = jax.nn.silu(xhat * g_ref[...] + b_ref[...]).astype(BF16)

    return _row_call("ln_silu_fwd", body, (s // tm,), [_rows(tm, d), _vec(d), _vec(d)], _rows(tm, d),
                     jax.ShapeDtypeStruct((s, d), BF16), (c, ln_g, ln_b), vmem_blocks=tm * d * 10)


def ln_silu_bwd(c, ds, ln_g, ln_b, tm=256, deps=()):
    s, d = c.shape
    tm = min(tm, s)

    def body(c_ref, ds_ref, g_ref, b_ref, dc_ref, dg_ref, db_ref, dcb_ref):
        first = pl.program_id(0) == 0
        g = g_ref[...]
        xhat, rstd = _layer_norm_stats(c_ref[...])
        y = xhat * g + b_ref[...]
        sg = jax.nn.sigmoid(y)
        dln = ds_ref[...] * (sg * (1.0 + y * (1.0 - sg)))
        _accumulate(dg_ref, jnp.sum(dln * xhat, axis=0, keepdims=True), first)
        _accumulate(db_ref, jnp.sum(dln, axis=0, keepdims=True), first)
        dc = _layer_norm_bwd(dln, xhat, rstd, g)
        dc_ref[...] = dc
        _accumulate(dcb_ref, jnp.sum(dc, axis=0, keepdims=True), first)

    vec = jax.ShapeDtypeStruct((1, d), F32)
    return _row_call("ln_silu_bwd", body, (s // tm,), [_rows(tm, d), _rows(tm, d), _vec(d), _vec(d)],
                     [_rows(tm, d), _vec(d), _vec(d), _vec(d)], [jax.ShapeDtypeStruct((s, d), F32), vec, vec, vec],
                     (c, ds, ln_g, ln_b), vmem_blocks=tm * d * 20, deps=deps)


def conv_bwd(proj, dc, conv_w, c_ch, tm=512):
    s = proj.shape[0]
    tm = min(tm, s)
    ncb = c_ch // CONV_COLS
    hb = tm // CONV_PAD
    last_halo = s // CONV_PAD - 1
    n_i = s // tm

    def body(a_ref, g_ref, ah_ref, gh_ref, dc_ref, dcn_ref, w_ref, dglu_ref, dw_ref, ext_ref, dce_ref, dwacc_ref):
        i = pl.program_id(1)
        halo = ah_ref[...].astype(F32) * jax.nn.sigmoid(gh_ref[...].astype(F32))
        ext_ref[0:CONV_PAD, :] = jnp.where(i > 0, halo, 0.0)
        ext_ref[CONV_PAD:, :] = a_ref[...].astype(F32) * jax.nn.sigmoid(g_ref[...].astype(F32))
        dce_ref[0:tm, :] = dc_ref[...]
        dce_ref[tm:, :] = jnp.where(i < n_i - 1, dcn_ref[...], 0.0)
        w = w_ref[...]

        def emit(r0, acc):
            dglu_ref[r0:r0 + CONV_ROWS, :] = acc

        _depthwise(dce_ref, w, tm, 0, True, emit)

        for k in range(CONV_WIDTH):
            acc = None
            for r0 in range(0, tm, CONV_ROWS):
                off = r0 + CONV_PAD - (CONV_WIDTH - 1) + k
                term = dce_ref[r0:r0 + CONV_ROWS, :] * ext_ref[off:off + CONV_ROWS, :]
                acc = term if acc is None else acc + term
            dwacc_ref[k:k + 1, :] = jnp.sum(acc, axis=0, keepdims=True)
        dwacc_ref[CONV_WIDTH:, :] = jnp.zeros((CONV_PAD - CONV_WIDTH, CONV_COLS), F32)
        _accumulate(dw_ref, dwacc_ref[...], i == 0)

    main = lambda col0: pl.BlockSpec((tm, CONV_COLS), lambda c, i: (i, col0 + c))
    halo = lambda col0: pl.BlockSpec((CONV_PAD, CONV_COLS), lambda c, i: (jnp.maximum(i * hb - 1, 0), col0 + c))
    nxt = pl.BlockSpec((CONV_PAD, CONV_COLS), lambda c, i: (jnp.minimum((i + 1) * hb, last_halo), c))
    wspec = pl.BlockSpec((CONV_PAD, CONV_COLS), lambda c, i: (0, c))
    return _row_call("conv_bwd", body, (ncb, n_i),
                     [main(0), main(ncb), halo(0), halo(ncb), main(0), nxt, wspec],
                     [main(0), wspec],
                     [jax.ShapeDtypeStruct((s, c_ch), F32), jax.ShapeDtypeStruct((CONV_PAD, c_ch), F32)],
                     (proj, proj, proj, proj, dc, dc, conv_w),
                     scratch=[pltpu.VMEM((tm + CONV_PAD, CONV_COLS), F32), pltpu.VMEM((tm + CONV_PAD, CONV_COLS), F32),
                              pltpu.VMEM((CONV_PAD, CONV_COLS), F32)],
                     vmem_blocks=tm * CONV_COLS * 24)


def _two_phase_spec(tm, d, col0):
    return pl.BlockSpec((tm, d), lambda i, j: (i, col0 + j))


def glu_bwd(dproj, dglu, proj, c_ch, tm=256):
    s = proj.shape[0]
    tm = min(tm, s)

    def body(_, dglu_ref, a_ref, g_ref, out_ref, stash_ref):
        j = pl.program_id(1)

        @pl.when(j == 0)
        def _():
            dg = dglu_ref[...]
            sg = jax.nn.sigmoid(g_ref[...].astype(F32))
            out_ref[...] = (dg * sg).astype(BF16)
            stash_ref[...] = (dg * a_ref[...].astype(F32) * (sg * (1.0 - sg))).astype(BF16)

        @pl.when(j == 1)
        def _():
            out_ref[...] = stash_ref[...]

    rows = lambda col: pl.BlockSpec((tm, c_ch), lambda i, j: (i, col))
    return _row_call("glu_bwd", body, (s // tm, 2), [ANY, rows(0), rows(0), rows(1)], _two_phase_spec(tm, c_ch, 0),
                     jax.ShapeDtypeStruct(dproj.shape, BF16), (dproj, dglu, proj, proj),
                     scratch=[pltpu.VMEM((tm, c_ch), BF16)], aliases={0: 0}, vmem_blocks=tm * c_ch * 14)


_SQRT_HALF = 0.7071067811865476
_INV_SQRT_2PI = 0.3989422804014327


def _gelu_parts(x):
    cdf = 0.5 * (1.0 + lax.erf(x * _SQRT_HALF))
    return cdf, x * cdf


def _gelu_grad(x, cdf):
    return cdf + x * (_INV_SQRT_2PI * jnp.exp(-0.5 * x * x))


def _sgu_specs(tm, ch):
    grp = ch // SGU_GROUPS
    full3 = lambda shape: pl.BlockSpec(shape, lambda *_: (0, 0, 0))
    return grp, full3((SGU_GROUPS, CHUNK, CHUNK)), full3((SGU_GROUPS, CHUNK, grp))


def sgu_fwd(proj, ln_g, ln_b, w_mix, b_mix, ch, col0, tm=CHUNK):
    s = proj.shape[0]
    grp, wspec, bspec = _sgu_specs(tm, ch)

    def body(u_ref, v_ref, g_ref, b_ref, w_ref, bm_ref, p_ref, mix_ref):
        _, u = _gelu_parts(u_ref[...].astype(F32))
        _, v0 = _gelu_parts(v_ref[...].astype(F32))
        xhat, _ = _layer_norm_stats(v0)
        vn = (xhat * g_ref[...] + b_ref[...]).astype(BF16)
        for n in range(tm // CHUNK):
            for g in range(SGU_GROUPS):
                blk = vn[n * CHUNK:(n + 1) * CHUNK, g * grp:(g + 1) * grp]
                mix_ref[n * CHUNK:(n + 1) * CHUNK, g * grp:(g + 1) * grp] = (
                    jnp.dot(w_ref[g], blk, preferred_element_type=F32) + bm_ref[g])
        p_ref[...] = (u * mix_ref[...]).astype(BF16)

    return _row_call("sgu_fwd", body, (s // tm,),
                     [_rows(tm, ch, col0), _rows(tm, ch, col0 + 1), _vec(ch), _vec(ch), wspec, bspec], _rows(tm, ch),
                     jax.ShapeDtypeStruct((s, ch), BF16), (proj, proj, ln_g, ln_b, w_mix, b_mix),
                     scratch=[pltpu.VMEM((tm, ch), F32)], vmem_blocks=tm * ch * 30)


def sgu_bwd(dproj, proj, dp, ln_g, ln_b, w_mix, w_mix_t, b_mix, ch, col0, tm=CHUNK):
    s = proj.shape[0]
    grp, wspec, bspec = _sgu_specs(tm, ch)

    def body(_, u_ref, v_ref, dp_ref, g_ref, b_ref, w_ref, wt_ref, bm_ref,
             out_ref, dw_ref, dbm_ref, dg_ref, db_ref, stash_ref, mix_ref, dvn_ref):
        i = pl.program_id(0)
        j = pl.program_id(1)

        @pl.when(j == 0)
        def _():
            first = i == 0
            ub = u_ref[...].astype(F32)
            vb = v_ref[...].astype(F32)
            cdf_u, u = _gelu_parts(ub)
            cdf_v, v0 = _gelu_parts(vb)
            g = g_ref[...]
            xhat, rstd = _layer_norm_stats(v0)
            vn = (xhat * g + b_ref[...]).astype(BF16)
            dpv = dp_ref[...]
            dmix = dpv * u
            dmix_bf = dmix.astype(BF16)
            for n in range(tm // CHUNK):
                for k in range(SGU_GROUPS):
                    rows = slice(n * CHUNK, (n + 1) * CHUNK)
                    cols = slice(k * grp, (k + 1) * grp)
                    vblk = vn[rows, cols]
                    dblk = dmix_bf[rows, cols]
                    mix_ref[rows, cols] = jnp.dot(w_ref[k], vblk, preferred_element_type=F32) + bm_ref[k]
                    dvn_ref[rows, cols] = jnp.dot(wt_ref[k], dblk, preferred_element_type=F32)
                    dwk = lax.dot_general(dblk, vblk, (((1,), (1,)), ((), ())), preferred_element_type=F32)
                    dbk = jnp.broadcast_to(jnp.sum(dmix[rows, cols], axis=1, keepdims=True), (CHUNK, CHUNK))
                    if n == 0:
                        _accumulate(dw_ref.at[k], dwk, first)
                        _accumulate(dbm_ref.at[k], dbk, first)
                    else:
                        dw_ref[k] += dwk
                        dbm_ref[k] += dbk
            du = dpv * mix_ref[...]
            dvn = dvn_ref[...]
            _accumulate(dg_ref, jnp.sum(dvn * xhat, axis=0, keepdims=True), first)
            _accumulate(db_ref, jnp.sum(dvn, axis=0, keepdims=True), first)
            dv0 = _layer_norm_bwd(dvn, xhat, rstd, g)
            out_ref[...] = (du * _gelu_grad(ub, cdf_u)).astype(BF16)
            stash_ref[...] = (dv0 * _gelu_grad(vb, cdf_v)).astype(BF16)

        @pl.when(j == 1)
        def _():
            out_ref[...] = stash_ref[...]

    rows = lambda col: pl.BlockSpec((tm, ch), lambda i, j: (i, col))
    vec = pl.BlockSpec((1, ch), lambda i, j: (0, 0))
    acc3 = lambda: pl.BlockSpec((SGU_GROUPS, CHUNK, CHUNK), lambda i, j: (0, 0, 0))
    vshape = jax.ShapeDtypeStruct((1, ch), F32)
    mshape = jax.ShapeDtypeStruct((SGU_GROUPS, CHUNK, CHUNK), F32)
    return _row_call("sgu_bwd", body, (s // tm, 2),
                     [ANY, rows(col0), rows(col0 + 1), rows(0), vec, vec, wspec, wspec, bspec],
                     [_two_phase_spec(tm, ch, col0), acc3(), acc3(), vec, vec],
                     [jax.ShapeDtypeStruct(dproj.shape, BF16), mshape, mshape, vshape, vshape],
                     (dproj, proj, proj, dp, ln_g, ln_b, w_mix, w_mix_t, b_mix),
                     scratch=[pltpu.VMEM((tm, ch), BF16), pltpu.VMEM((tm, ch), F32), pltpu.VMEM((tm, ch), F32)],
                     aliases={0: 0}, vmem_blocks=tm * ch * 60)


def merge_fwd(y_a, y_b, proj, b_gate, col0, tm=256):
    s, d = y_a.shape
    tm = min(tm, s)

    def body(ya_ref, yb_ref, l0_ref, l1_ref, bg_ref, m_ref):
        g0 = jax.nn.sigmoid(l0_ref[...].astype(F32) + bg_ref[0:1, :])
        g1 = jax.nn.sigmoid(l1_ref[...].astype(F32) + bg_ref[1:2, :])
        m_ref[...] = (g0 * ya_ref[...] + g1 * yb_ref[...]).astype(BF16)

    return _row_call("merge_fwd", body, (s // tm,),
                     [_rows(tm, d), _rows(tm, d), _rows(tm, d, col0), _rows(tm, d, col0 + 1), pl.BlockSpec((2, d), lambda i: (0, 0))],
                     _rows(tm, d), jax.ShapeDtypeStruct((s, d), BF16), (y_a, y_b, proj, proj, b_gate), vmem_blocks=tm * d * 16)


def merge_bwd(dm, y_a, y_b, proj, b_gate, col0, tm=256, deps=()):
    s, d = y_a.shape
    tm = min(tm, s)

    def body(dm_ref, ya_ref, yb_ref, l0_ref, l1_ref, bg_ref, dya_ref, dyb_ref, out_ref, dbg_ref, stash_ref):
        i = pl.program_id(0)
        j = pl.program_id(1)

        @pl.when(j == 0)
        def _():
            dmv = dm_ref[...]
            g0 = jax.nn.sigmoid(l0_ref[...].astype(F32) + bg_ref[0:1, :])
            g1 = jax.nn.sigmoid(l1_ref[...].astype(F32) + bg_ref[1:2, :])
            dya_ref[...] = (dmv * g0).astype(BF16)
            dyb_ref[...] = (dmv * g1).astype(BF16)
            dl0 = dmv * ya_ref[...] * (g0 * (1.0 - g0))
            dl1 = dmv * yb_ref[...] * (g1 * (1.0 - g1))
            _accumulate(dbg_ref.at[0:1, :], jnp.sum(dl0, axis=0, keepdims=True), i == 0)
            _accumulate(dbg_ref.at[1:2, :], jnp.sum(dl1, axis=0, keepdims=True), i == 0)
            out_ref[...] = dl0.astype(BF16)
            stash_ref[...] = dl1.astype(BF16)

        @pl.when(j == 1)
        def _():
            out_ref[...] = stash_ref[...]

    rows = lambda col: pl.BlockSpec((tm, d), lambda i, j: (i, col))
    bgspec = pl.BlockSpec((2, d), lambda i, j: (0, 0))
    return _row_call("merge_bwd", body, (s // tm, 2),
                     [rows(0), rows(0), rows(0), rows(col0), rows(col0 + 1), bgspec],
                     [rows(0), rows(0), _two_phase_spec(tm, d, col0), bgspec],
                     [jax.ShapeDtypeStruct((s, d), BF16), jax.ShapeDtypeStruct((s, d), BF16),
                      jax.ShapeDtypeStruct(proj.shape, BF16), jax.ShapeDtypeStruct((2, d), F32)],
                     (dm, y_a, y_b, proj, proj, b_gate), scratch=[pltpu.VMEM((tm, d), BF16)], vmem_blocks=tm * d * 30, deps=deps)


def swiglu_fwd(gu, tm=512):
    s, w2 = gu.shape
    tm = min(tm, s)
    nb = N_DEV // 2
    cb = w2 // N_DEV

    def body(g_ref, u_ref, f_ref):
        f_ref[...] = (jax.nn.silu(g_ref[...].astype(F32)) * u_ref[...].astype(F32)).astype(BF16)

    return _row_call("swiglu_fwd", body, (s // tm, nb),
                     [pl.BlockSpec((tm, cb), lambda i, j: (i, j)), pl.BlockSpec((tm, cb), lambda i, j: (i, j + nb))],
                     pl.BlockSpec((tm, cb), lambda i, j: (i, j)), jax.ShapeDtypeStruct((s, w2 // 2), BF16), (gu, gu),
                     vmem_blocks=tm * cb * 12)


def swiglu_bwd(gu, df, tm=512, deps=()):
    s, w2 = gu.shape
    tm = min(tm, s)
    nb = N_DEV // 2
    cb = w2 // N_DEV

    def body(g_ref, u_ref, df_ref, out_ref):
        j = pl.program_id(1)
        g = g_ref[...].astype(F32)
        sg = jax.nn.sigmoid(g)
        dfv = df_ref[...].astype(F32)

        @pl.when(j < nb)
        def _():
            out_ref[...] = (dfv * u_ref[...].astype(F32) * (sg * (1.0 + g * (1.0 - sg)))).astype(BF16)

        @pl.when(j >= nb)
        def _():
            out_ref[...] = (dfv * (g * sg)).astype(BF16)

    return _row_call("swiglu_bwd", body, (s // tm, N_DEV),
                     [pl.BlockSpec((tm, cb), lambda i, j: (i, j % nb)), pl.BlockSpec((tm, cb), lambda i, j: (i, j % nb + nb)),
                      pl.BlockSpec((tm, cb), lambda i, j: (i, j % nb))],
                     pl.BlockSpec((tm, cb), lambda i, j: (i, j)), jax.ShapeDtypeStruct((s, w2), BF16), (gu, gu, df),
                     vmem_blocks=tm * cb * 16, deps=deps)


def _peers():
    x, y, c = lax.axis_index("x"), lax.axis_index("y"), lax.axis_index("c")
    me = 4 * x + 2 * y + c
    peers = []
    for k in range(1, N_DEV):
        px = 1 - x if k & 4 else x
        py = 1 - y if k & 2 else y
        pc = 1 - c if k & 1 else c
        peers.append(((px, py, pc), 4 * px + 2 * py + pc))
    return me, peers


def _exchange(name, arrays, scatter):
    n = len(arrays)

    def body(*refs):
        ins, outs = refs[:n], refs[n:2 * n]
        send_sems, recv_sems, local_sems = refs[2 * n:]
        me, peers = _peers()

        def remote(a, k):
            (pos, idx) = peers[k]
            src = ins[a].at[idx] if scatter else ins[a]
            return pltpu.make_async_remote_copy(src_ref=src, dst_ref=outs[a].at[me], send_sem=send_sems.at[a, k],
                                                recv_sem=recv_sems.at[a, k], device_id=pos, device_id_type=pl.DeviceIdType.MESH)

        def arrival(a, k):
            (pos, idx) = peers[k]
            src = ins[a].at[idx] if scatter else ins[a]
            return pltpu.make_async_remote_copy(src_ref=src, dst_ref=outs[a].at[idx], send_sem=send_sems.at[a, k],
                                                recv_sem=recv_sems.at[a, k], device_id=pos, device_id_type=pl.DeviceIdType.MESH)

        local = [pltpu.make_async_copy(ins[a].at[me] if scatter else ins[a], outs[a].at[me], local_sems.at[a]) for a in range(n)]
        sends = [remote(a, k) for k in range(N_DEV - 1) for a in range(n)]
        for cp in sends:
            cp.start()
        for cp in local:
            cp.start()
        for k in range(N_DEV - 1):
            for a in range(n):
                arrival(a, k).wait_recv()
        for cp in sends:
            cp.wait_send()
        for cp in local:
            cp.wait()

    out_shape = [jax.ShapeDtypeStruct(a.shape if scatter else (N_DEV,) + a.shape, a.dtype) for a in arrays]
    return pl.pallas_call(
        body, name=name, in_specs=[ANY] * n, out_specs=[ANY] * n, out_shape=out_shape,
        scratch_shapes=[pltpu.SemaphoreType.DMA((n, N_DEV - 1)), pltpu.SemaphoreType.DMA((n, N_DEV - 1)),
                        pltpu.SemaphoreType.DMA((n,))],
    )(*arrays)


HBM_SPEC = pl.BlockSpec(memory_space=pltpu.HBM)
SEM_SPEC = pl.BlockSpec(memory_space=pltpu.SEMAPHORE)
DATAFLOW_EFFECT = pltpu.SideEffectType.DATAFLOW_SIDE_EFFECTING


def _in_hbm(a):
    return pltpu.with_memory_space_constraint(a, pltpu.HBM)


def place_own(name, arrays, scatter):
    n = len(arrays)

    def body(*refs):
        ins, outs, sems = refs[:n], refs[n:2 * n], refs[2 * n]
        me, _ = _peers()
        copies = [pltpu.make_async_copy(ins[a].at[me] if scatter else ins[a], outs[a].at[me], sems.at[a]) for a in range(n)]
        for cp in copies:
            cp.start()
        for cp in copies:
            cp.wait()

    out_shape = [jax.ShapeDtypeStruct(a.shape if scatter else (N_DEV,) + a.shape, a.dtype) for a in arrays]
    return pl.pallas_call(body, name=name, in_specs=[ANY] * n, out_specs=[ANY] * n, out_shape=out_shape,
                          scratch_shapes=[pltpu.SemaphoreType.DMA((n,))])(*arrays)


def _split_copy(src, land, send_sem, recv_sem, peer, me, scatter, arriving):
    (pos, idx) = peer
    return pltpu.make_async_remote_copy(
        src_ref=src.at[idx] if scatter else src, dst_ref=land.at[idx if arriving else me],
        send_sem=send_sem, recv_sem=recv_sem, device_id=pos, device_id_type=pl.DeviceIdType.MESH)


def exchange_start(name, srcs, lands, scatter, after=None):
    n = len(srcs)
    extra = _deps([after])

    def body(*refs):
        src, land = refs[:n], refs[n:2 * n]
        outs = refs[2 * n + len(extra):]
        send_sems, recv_sems, token = outs[:n], outs[n:2 * n], outs[4 * n]
        me, peers = _peers()
        for k in range(N_DEV - 1):
            for a in range(n):
                _split_copy(src[a], land[a], send_sems[a].at[k], recv_sems[a].at[k], peers[k], me, scatter, False).start()
        token[...] = jnp.zeros_like(token)

    sems = [pltpu.SemaphoreType.DMA((N_DEV - 1,))] * (2 * n)
    thru = [pltpu.HBM(a.shape, a.dtype) for a in list(srcs) + list(lands)]
    res = pl.pallas_call(
        body, name=name, in_specs=[HBM_SPEC] * (2 * n) + [ANY] * len(extra),
        out_specs=[SEM_SPEC] * (2 * n) + [HBM_SPEC] * (2 * n) + [pl.BlockSpec(memory_space=pltpu.VMEM)],
        out_shape=sems + thru + [jax.ShapeDtypeStruct((8, 128), F32)],
        input_output_aliases={i: 2 * n + i for i in range(2 * n)},
        compiler_params=pltpu.CompilerParams(has_side_effects=DATAFLOW_EFFECT),
    )(*[_in_hbm(a) for a in list(srcs) + list(lands)], *extra)
    handles = [(res[a], res[n + a], res[2 * n + a], res[3 * n + a]) for a in range(n)]
    return handles, res[4 * n]


def exchange_wait(name, handle, scatter, after):
    send_sem, recv_sem, src, land = handle

    def body(src_ref, land_ref, send_ref, recv_ref, after_ref, src_out, land_out):
        me, peers = _peers()
        for k in range(N_DEV - 1):
            cp = _split_copy(src_ref, land_ref, send_ref.at[k], recv_ref.at[k], peers[k], me, scatter, True)
            cp.wait_send()
            cp.wait_recv()

    return pl.pallas_call(
        body, name=name, in_specs=[HBM_SPEC, HBM_SPEC, SEM_SPEC, SEM_SPEC, ANY], out_specs=[HBM_SPEC, HBM_SPEC],
        out_shape=[pltpu.HBM(src.shape, src.dtype), pltpu.HBM(land.shape, land.dtype)],
        input_output_aliases={0: 0, 1: 1}, compiler_params=pltpu.CompilerParams(has_side_effects=DATAFLOW_EFFECT),
    )(src, land, send_sem, recv_sem, after)[1]


def _row_tile(r, cap):
    if r <= cap:
        return r
    return max(t for t in range(16, cap + 1, 16) if r % t == 0)


def sum_adamw(name, parts, w, m, v, tr):
    nl, r, c = w.shape
    tr = _row_tile(r, tr)
    c1 = 1.0 - ADAM_B1 ** ADAM_STEP
    c2 = 1.0 - ADAM_B2 ** ADAM_STEP

    def body(*refs):
        part_refs = refs[:nl]
        w_ref, m_ref, v_ref, g_out, d_out, m_out, v_out = refs[nl:]
        layer = pl.program_id(0)
        for j in range(nl):
            @pl.when(layer == j)
            def _(j=j):
                g = part_refs[j][0].astype(F32)
                for p in range(1, N_DEV):
                    g = g + part_refs[j][p].astype(F32)
                mn = ADAM_B1 * m_ref[...] + (1.0 - ADAM_B1) * g
                vn = ADAM_B2 * v_ref[...] + (1.0 - ADAM_B2) * (g * g)
                g_out[...] = g
                m_out[...] = mn
                v_out[...] = vn
                d_out[...] = -ADAM_LR * ((mn / c1) / (jnp.sqrt(vn / c2) + ADAM_EPS) + ADAM_WD * w_ref[...])

    def part_spec(j):
        return pl.BlockSpec((N_DEV, tr, c), lambda l, i: (0, jnp.where(l == j, i, 0), 0))

    lspec = pl.BlockSpec((None, tr, c), lambda l, i: (l, i, 0))
    out = jax.ShapeDtypeStruct((nl, r, c), F32)
    return _row_call(name, body, (nl, r // tr), [part_spec(j) for j in range(nl)] + [lspec] * 3, [lspec] * 4, [out] * 4,
                     tuple(parts) + (w, m, v), vmem_blocks=nl * N_DEV * tr * c * parts[0].dtype.itemsize + 7 * tr * c * 4)


REPLICATED = ("norm_mix_pre", "norm_mix_post", "norm_ffn_pre", "norm_ffn_post", "conv_b", "conv_ln_g", "conv_ln_b",
              "sgu_ln_g", "sgu_ln_b", "w_spatial", "b_spatial")
MATRICES = ("w_in", "w_a_out", "w_b_out", "w_o", "w_gate_up", "w_down")


def local_step(x, target, rep, weight, emit, b_gate, conv_w, start_token=None):
    s, d = x.shape
    causal = jnp.tril(jnp.ones((CHUNK, CHUNK), dtype=bool))
    row = lambda name, l: rep[name][l].reshape(1, -1)

    saved = []
    h = rms_fwd(x, row("norm_mix_pre", 0), deps=[start_token])
    for l in range(DEPTH):
        w_mix = jnp.where(causal[None], rep["w_spatial"][l], 0.0).astype(BF16)
        b_mix = jnp.broadcast_to(rep["b_spatial"][l][:, :, None], (SGU_GROUPS, CHUNK, d // SGU_GROUPS))
        proj = mm_nn("proj", h, weight(l, "w_in", h), BF16, 512)
        c = glu_conv_fwd(proj, conv_w[l], row("conv_b", l), d)
        s_act = ln_silu_fwd(c, row("conv_ln_g", l), row("conv_ln_b", l))
        p_act = sgu_fwd(proj, row("sgu_ln_g", l), row("sgu_ln_b", l), w_mix, b_mix, d, 2)
        y_a = mm_nn("branch_out", s_act, weight(l, "w_a_out", s_act), F32, 512)
        y_b = mm_nn("branch_out", p_act, weight(l, "w_b_out", p_act), F32, 512)
        merged = merge_fwd(y_a, y_b, proj, b_gate[l], 4)
        o = mm_nn("branch_out", merged, weight(l, "w_o", merged), F32, 512)
        x_mid, h2 = norm_res(x, o, row("norm_mix_post", l), row("norm_ffn_pre", l))
        gu = mm_nn("gate_up", h2, weight(l, "w_gate_up", h2), BF16, 512)
        f = swiglu_fwd(gu)
        o2 = mm_nn("down", f, weight(l, "w_down", f), F32, 512, tn=1024, tk=f.shape[1] // 4)
        saved.append(dict(x_in=x, h=h, proj=proj, c=c, s_act=s_act, p_act=p_act, y_a=y_a, y_b=y_b, merged=merged, o=o,
                          x_mid=x_mid, h2=h2, gu=gu, f=f, o2=o2, w_mix=w_mix, b_mix=b_mix))
        if l + 1 < DEPTH:
            x, h = norm_res(x_mid, o2, row("norm_ffn_post", l), row("norm_mix_pre", l + 1))

    top = saved[-1]
    loss_vec, dx, do2, dg_ffn_post = final_norm_loss(top["x_mid"], top["o2"], row("norm_ffn_post", DEPTH - 1), target)
    loss = (0.5 / d) * jnp.sum(loss_vec)

    grads = [None] * DEPTH
    for l in reversed(range(DEPTH)):
        sv = saved[l]
        wl = {name: weight(l, name, None) for name in MATRICES}
        g = {"norm_ffn_post": dg_ffn_post}
        df = mm_nt("d_down_in", do2, wl["w_down"], BF16, 512, tn=wl["w_down"].shape[0] // 4)
        tok = emit(l, "w_down", mm_tn("d_down_w", sv["f"], do2, BF16, 512, sv["f"].shape[1] // 4))
        dgu = swiglu_bwd(sv["gu"], df, deps=[tok])
        dh2 = mm_nt("d_gate_up_in", dgu, wl["w_gate_up"], F32, 1024)
        tok = emit(l, "w_gate_up", mm_tn("d_gate_up_w", sv["h2"], dgu, BF16, 512, d // 2, nb=wl["w_gate_up"].shape[2]))
        dx, do, g["norm_ffn_pre"], g["norm_mix_post"] = norm_bwd_in_out(
            dx, dh2, sv["x_mid"], row("norm_ffn_pre", l), sv["o"], row("norm_mix_post", l), deps=[tok])
        dm = mm_nt("d_square_in", do, wl["w_o"], F32, 512)
        tok = emit(l, "w_o", mm_tn("d_square_w", sv["merged"], do, BF16, 512, d // 2))
        dy_a, dy_b, dproj, g["b_gate"] = merge_bwd(dm, sv["y_a"], sv["y_b"], sv["proj"], b_gate[l], 4, deps=[tok])
        ds = mm_nt("d_square_in", dy_a, wl["w_a_out"], F32, 512)
        tok = emit(l, "w_a_out", mm_tn("d_square_w", sv["s_act"], dy_a, BF16, 512, d // 2))
        dp = mm_nt("d_square_in", dy_b, wl["w_b_out"], F32, 512, deps=[tok])
        tok = emit(l, "w_b_out", mm_tn("d_square_w", sv["p_act"], dy_b, BF16, 512, d // 2))
        dc, g["conv_ln_g"], g["conv_ln_b"], g["conv_b"] = ln_silu_bwd(
            sv["c"], ds, row("conv_ln_g", l), row("conv_ln_b", l), deps=[tok])
        dglu, g["conv_w"] = conv_bwd(sv["proj"], dc, conv_w[l], d)
        dproj = glu_bwd(dproj, dglu, sv["proj"], d)
        dproj, dw_mix, db_mix, g["sgu_ln_g"], g["sgu_ln_b"] = sgu_bwd(
            dproj, sv["proj"], dp, row("sgu_ln_g", l), row("sgu_ln_b", l), sv["w_mix"],
            jnp.swapaxes(sv["w_mix"], 1, 2), sv["b_mix"], d, 2)
        g["w_spatial"] = jnp.where(causal[None], dw_mix, 0.0)
        g["b_spatial"] = db_mix[:, :, 0]
        tok = emit(l, "w_in", mm_tn("d_in_w", sv["h"], dproj, BF16, 512, d // 2, nb=wl["w_in"].shape[2]))
        dh = mm_nt("d_in_in", dproj, wl["w_in"], F32, 1024, deps=[tok])
        if l > 0:
            below = saved[l - 1]
            dx, do2, g["norm_mix_pre"], dg_ffn_post = norm_bwd_in_out(
                dx, dh, sv["x_in"], row("norm_mix_pre", l), below["o2"], row("norm_ffn_post", l - 1))
        else:
            dx, g["norm_mix_pre"] = norm_bwd_in(dx, dh, sv["x_in"], row("norm_mix_pre", l))
        grads[l] = g
    return loss, dx, grads


def _pack_rows(arrays):
    return jnp.concatenate([a.reshape(-1, 128) for a in arrays], axis=0)


def _unpack_rows(packed, shapes):
    out, r0 = [], 0
    for shp in shapes:
        nr = math.prod(shp) // 128
        out.append(packed[r0:r0 + nr].reshape(shp))
        r0 += nr
    return out


def kernel(x, norm_mix_pre, norm_mix_post, norm_ffn_pre, norm_ffn_post, w_in, b_gate, conv_w, conv_b, conv_ln_g, conv_ln_b, w_a_out, sgu_ln_g, sgu_ln_b, w_spatial, b_spatial, w_b_out, w_o, w_gate_up, w_down, loss_target, m_norm_mix_pre, m_norm_mix_post, m_norm_ffn_pre, m_norm_ffn_post, m_w_in, m_b_gate, m_conv_w, m_conv_b, m_conv_ln_g, m_conv_ln_b, m_w_a_out, m_sgu_ln_g, m_sgu_ln_b, m_w_spatial, m_b_spatial, m_w_b_out, m_w_o, m_w_gate_up, m_w_down, v_norm_mix_pre, v_norm_mix_post, v_norm_ffn_pre, v_norm_ffn_post, v_w_in, v_b_gate, v_conv_w, v_conv_b, v_conv_ln_g, v_conv_ln_b, v_w_a_out, v_sgu_ln_g, v_sgu_ln_b, v_w_spatial, v_b_spatial, v_w_b_out, v_w_o, v_w_gate_up, v_w_down):
    names = ("norm_mix_pre", "norm_mix_post", "norm_ffn_pre", "norm_ffn_post", "w_in", "b_gate", "conv_w", "conv_b",
             "conv_ln_g", "conv_ln_b", "w_a_out", "sgu_ln_g", "sgu_ln_b", "w_spatial", "b_spatial", "w_b_out", "w_o",
             "w_gate_up", "w_down")
    w = dict(zip(names, (norm_mix_pre, norm_mix_post, norm_ffn_pre, norm_ffn_post, w_in, b_gate, conv_w, conv_b,
                         conv_ln_g, conv_ln_b, w_a_out, sgu_ln_g, sgu_ln_b, w_spatial, b_spatial, w_b_out, w_o,
                         w_gate_up, w_down)))
    m = dict(zip(names, (m_norm_mix_pre, m_norm_mix_post, m_norm_ffn_pre, m_norm_ffn_post, m_w_in, m_b_gate, m_conv_w,
                         m_conv_b, m_conv_ln_g, m_conv_ln_b, m_w_a_out, m_sgu_ln_g, m_sgu_ln_b, m_w_spatial,
                         m_b_spatial, m_w_b_out, m_w_o, m_w_gate_up, m_w_down)))
    v = dict(zip(names, (v_norm_mix_pre, v_norm_mix_post, v_norm_ffn_pre, v_norm_ffn_post, v_w_in, v_b_gate, v_conv_w,
                         v_conv_b, v_conv_ln_g, v_conv_ln_b, v_w_a_out, v_sgu_ln_g, v_sgu_ln_b, v_w_spatial,
                         v_b_spatial, v_w_b_out, v_w_o, v_w_gate_up, v_w_down)))
    d = x.shape[-1]
    shard_cols = d // N_DEV

    def small_pack(bg, cw):
        rows = jnp.concatenate([bg, cw], axis=1).reshape(DEPTH * (2 + CONV_WIDTH), shard_cols)
        return jnp.pad(rows, ((0, (-rows.shape[0]) % 8), (0, 0)))

    small_w, small_m, small_v = (small_pack(t["b_gate"], t["conv_w"]) for t in (w, m, v))

    small_full, = _exchange("gather_small", [small_w], scatter=False)
    small_full = small_full[:, :DEPTH * (2 + CONV_WIDTH)].reshape(N_DEV, DEPTH, 2 + CONV_WIDTH, shard_cols)
    small_full = jnp.transpose(small_full, (1, 2, 0, 3)).reshape(DEPTH, 2 + CONV_WIDTH, d)
    b_gate_full = small_full[:, :2]
    conv_w_full = jnp.pad(small_full[:, 2:], ((0, 0), (0, CONV_PAD - CONV_WIDTH), (0, 0)))

    gathers, token = {}, small_full
    for l in range(DEPTH):
        shards = [w[name][l].astype(BF16) for name in MATRICES]
        lands = place_own(f"place_shards_{l}", shards, scatter=False)
        handles, token = exchange_start(f"gather_start_{l}", shards, lands, scatter=False, after=token)
        for name, handle in zip(MATRICES, handles):
            gathers[l, name] = handle
    gathered = {}

    def weight(l, name, after):
        if (l, name) not in gathered:
            full = exchange_wait(f"gather_wait_{name}_{l}", gathers[l, name], False, after)
            gathered[l, name] = full if name in ("w_in", "w_gate_up") else full.reshape(-1, d)
        return gathered[l, name]

    scatters = {}

    def emit(l, name, g):
        chunks = g if g.ndim == 3 else g.reshape(N_DEV, -1, d)
        land, = place_own(f"place_grad_{name}_{l}", [chunks], scatter=True)
        (scatters[l, name],), tok = exchange_start(f"scatter_start_{name}_{l}", [chunks], [land], scatter=True)
        return tok

    rep = {name: w[name] for name in REPLICATED}
    loss, grad_x, grads = local_step(x[0], loss_target[0], rep, weight, emit, b_gate_full, conv_w_full, token)
    loss = lax.psum(loss, MESH_AXES)

    small_g = jnp.stack([jnp.concatenate([grads[l]["b_gate"], grads[l]["conv_w"][:CONV_WIDTH]], axis=0) for l in range(DEPTH)])
    small_g = jnp.transpose(small_g.reshape(DEPTH * (2 + CONV_WIDTH), N_DEV, shard_cols), (1, 0, 2))
    small_g = jnp.pad(small_g, ((0, 0), (0, small_w.shape[0] - small_g.shape[1]), (0, 0)))
    rep_shapes = [w[name].shape for name in REPLICATED]
    rep_g = _pack_rows([jnp.stack([grads[l][name].reshape(w[name].shape[1:]) for l in range(DEPTH)]) for name in REPLICATED])
    small_land, = place_own("place_small", [small_g], scatter=True)
    (small_handle,), tok_small = exchange_start("scatter_start_small", [small_g], [small_land], scatter=True)
    rep_land, = place_own("place_replicated", [rep_g], scatter=False)
    (rep_handle,), tok_rep = exchange_start("gather_start_replicated", [rep_g], [rep_land], scatter=False, after=tok_small)

    out = {}
    after = tok_rep
    for name in ("w_down", "w_gate_up", "w_o", "w_a_out", "w_b_out", "w_in"):
        parts = [exchange_wait(f"scatter_wait_{name}_{l}", scatters[l, name], True, after) for l in range(DEPTH)]
        out[name] = sum_adamw("adamw_" + name, parts, w[name], m[name], v[name], 128)
        after = out[name][0]
    small_parts = exchange_wait("scatter_wait_small", small_handle, True, after)
    rep_parts = exchange_wait("gather_wait_replicated", rep_handle, False, after)
    small_res = sum_adamw("adamw_small", [small_parts], small_w[None], small_m[None], small_v[None], small_w.shape[0])
    n_small = DEPTH * (2 + CONV_WIDTH)
    small_res = [r[0, :n_small].reshape(DEPTH, 2 + CONV_WIDTH, shard_cols) for r in small_res]
    out["b_gate"] = [r[:, :2] for r in small_res]
    out["conv_w"] = [r[:, 2:] for r in small_res]
    rep_res = sum_adamw("adamw_replicated", [rep_parts], *(_pack_rows([t[name] for name in REPLICATED])[None] for t in (w, m, v)), 672)
    rep_res = [_unpack_rows(r[0], rep_shapes) for r in rep_res]
    for i, name in enumerate(REPLICATED):
        out[name] = [r[i] for r in rep_res]

    return (loss, grad_x[None], *[out[name][0] for name in names], *[out[name][1] for name in names],
            *[out[name][2] for name in names], *[out[name][3] for name in names])
```

```python
import functools
import math

import jax
import jax.numpy as jnp
from jax import lax
from jax.experimental import pallas as pl
from jax.experimental.pallas import tpu as pltpu

F32 = jnp.float32
BF16 = jnp.bfloat16

DEPTH = 4
N_DEV = 8
EPS = 1e-6
CONV_WIDTH = 31
CONV_PAD = 32
CHUNK = 128
SGU_GROUPS = 8

ADAM_LR = 0.001
ADAM_B1 = 0.9
ADAM_B2 = 0.999
ADAM_EPS = 1e-08
ADAM_WD = 0.01
ADAM_STEP = 10

VMEM_BYTES_V7X = 64 * 1024 * 1024
VMEM_COMPILER_SLACK = 12 * 1024 * 1024
MESH_AXES = ("x", "y", "c")
ANY = pl.BlockSpec(memory_space=pl.ANY)


def _nbytes(shape, dtype):
    return math.prod(shape) * jnp.dtype(dtype).itemsize


def _params(block_bytes, ngrid, single_bytes=0):
    limit = min(2 * block_bytes + single_bytes + VMEM_COMPILER_SLACK, VMEM_BYTES_V7X - 4 * 1024 * 1024)
    return pltpu.CompilerParams(dimension_semantics=("arbitrary",) * ngrid, vmem_limit_bytes=int(limit))


def _deps(deps):
    return [t for t in deps if t is not None]


def _mm_body(dims, nk, kaxis, ndeps):
    def body(a_ref, b_ref, *rest):
        o_ref, *acc = rest[ndeps:]

        def prod():
            return lax.dot_general(a_ref[...], b_ref[...], (dims, ((), ())), preferred_element_type=F32)

        if nk == 1:
            o_ref[...] = prod().astype(o_ref.dtype)
            return
        acc_ref, = acc
        k = pl.program_id(kaxis)
        p = prod()

        @pl.when(k == 0)
        def _():
            acc_ref[...] = p

        @pl.when(k > 0)
        def _():
            acc_ref[...] += p

        @pl.when(k == nk - 1)
        def _():
            o_ref[...] = acc_ref[...].astype(o_ref.dtype)

    return body


def _mm_call(name, a, b, dims, grid, a_spec, b_spec, o_spec, out_shape, out_dtype, nk, kaxis, acc_shape, deps=()):
    deps = _deps(deps)
    blocks = (_nbytes([d for d in a_spec.block_shape if d], a.dtype) + _nbytes([d for d in b_spec.block_shape if d], b.dtype)
              + _nbytes([d for d in o_spec.block_shape if d], out_dtype))
    scratch = [pltpu.VMEM(acc_shape, F32)] if nk > 1 else []
    acc_bytes = _nbytes(acc_shape, F32) * (2 if nk > 1 else 1)
    return pl.pallas_call(
        _mm_body(dims, nk, kaxis, len(deps)), name=name, grid=grid, in_specs=[a_spec, b_spec] + [ANY] * len(deps),
        out_specs=o_spec, out_shape=jax.ShapeDtypeStruct(out_shape, out_dtype), scratch_shapes=scratch,
        compiler_params=_params(blocks, len(grid), acc_bytes),
    )(a, b, *deps)


def mm_nn(name, a, b, out_dtype, tm, tn=None, tk=None):
    m, k = a.shape
    tm = min(tm, m)
    if b.ndim == 3:
        nblk, _, nb = b.shape
        return _mm_call(name, a, b, ((1,), (0,)), (nblk, m // tm),
                        pl.BlockSpec((tm, k), lambda j, i: (i, 0)), pl.BlockSpec((None, k, nb), lambda j, i: (j, 0, 0)),
                        pl.BlockSpec((tm, nb), lambda j, i: (i, j)), (m, nblk * nb), out_dtype, 1, 0, (tm, nb))
    n = b.shape[1]
    tn = tn or n
    tk = tk or k
    nk = k // tk
    return _mm_call(name, a, b, ((1,), (0,)), (n // tn, m // tm, nk),
                    pl.BlockSpec((tm, tk), lambda j, i, kk: (i, kk)), pl.BlockSpec((tk, tn), lambda j, i, kk: (kk, j)),
                    pl.BlockSpec((tm, tn), lambda j, i, kk: (i, j)), (m, n), out_dtype, nk, 2, (tm, tn))


def mm_nt(name, a, b, out_dtype, tm, tn=None, deps=()):
    m = a.shape[0]
    tm = min(tm, m)
    if b.ndim == 3:
        kblk, n, kb = b.shape
        return _mm_call(name, a, b, ((1,), (1,)), (m // tm, kblk),
                        pl.BlockSpec((tm, kb), lambda i, kk: (i, kk)), pl.BlockSpec((None, n, kb), lambda i, kk: (kk, 0, 0)),
                        pl.BlockSpec((tm, n), lambda i, kk: (i, 0)), (m, n), out_dtype, kblk, 1, (tm, n), deps)
    n, kc = b.shape
    tn = tn or n
    return _mm_call(name, a, b, ((1,), (1,)), (n // tn, m // tm),
                    pl.BlockSpec((tm, kc), lambda j, i: (i, 0)), pl.BlockSpec((tn, kc), lambda j, i: (j, 0)),
                    pl.BlockSpec((tm, tn), lambda j, i: (i, j)), (m, n), out_dtype, 1, 0, (tm, tn), deps)


def mm_tn(name, a, b, out_dtype, tm, tr, nb=None):
    m, k = a.shape
    n = b.shape[1]
    tm = min(tm, m)
    nm = m // tm
    if nb is not None:
        return _mm_call(name, a, b, ((0,), (0,)), (n // nb, k // tr, nm),
                        pl.BlockSpec((tm, tr), lambda j, r, mm: (mm, r)), pl.BlockSpec((tm, nb), lambda j, r, mm: (mm, j)),
                        pl.BlockSpec((None, tr, nb), lambda j, r, mm: (j, r, 0)), (n // nb, k, nb), out_dtype, nm, 2, (tr, nb))
    return _mm_call(name, a, b, ((0,), (0,)), (k // tr, nm),
                    pl.BlockSpec((tm, tr), lambda r, mm: (mm, r)), pl.BlockSpec((tm, n), lambda r, mm: (mm, 0)),
                    pl.BlockSpec((tr, n), lambda r, mm: (r, 0)), (k, n), out_dtype, nm, 1, (tr, n))


def _row_call(name, body, grid, in_specs, out_specs, out_shape, arrays, scratch=(), aliases=None, vmem_blocks=0, deps=()):
    deps = _deps(deps)
    nin = len(arrays)

    def with_deps(*refs):
        body(*refs[:nin], *refs[nin + len(deps):])

    return pl.pallas_call(
        with_deps, name=name, grid=grid, in_specs=list(in_specs) + [ANY] * len(deps), out_specs=out_specs,
        out_shape=out_shape, scratch_shapes=list(scratch), input_output_aliases=aliases or {},
        compiler_params=_params(vmem_blocks, len(grid)),
    )(*arrays, *deps)


def _rows(tm, d, col=0):
    return pl.BlockSpec((tm, d), lambda i, *_: (i, col))


def _vec(d):
    return pl.BlockSpec((1, d), lambda *_: (0, 0))


def _rstd(x):
    return lax.rsqrt(jnp.mean(x * x, axis=-1, keepdims=True) + EPS)


def _rms_bwd(dy, x, g):
    r = _rstd(x)
    n = x * r
    w = dy * g
    dx = r * (w - n * jnp.mean(w * n, axis=-1, keepdims=True))
    return dx, jnp.sum(dy * n, axis=0, keepdims=True)


def _accumulate(ref, value, first):
    @pl.when(first)
    def _():
        ref[...] = value

    @pl.when(jnp.logical_not(first))
    def _():
        ref[...] += value


def rms_fwd(x, g, tm=256, deps=()):
    s, d = x.shape
    tm = min(tm, s)

    def body(x_ref, g_ref, h_ref):
        xv = x_ref[...]
        h_ref[...] = (xv * _rstd(xv) * g_ref[...]).astype(BF16)

    return _row_call("rms_fwd", body, (s // tm,), [_rows(tm, d), _vec(d)], _rows(tm, d),
                     jax.ShapeDtypeStruct((s, d), BF16), (x, g), vmem_blocks=tm * d * 6, deps=deps)


def norm_res(x_in, o, g_post, g_next, tm=256):
    s, d = x_in.shape
    tm = min(tm, s)

    def body(x_ref, o_ref, gp_ref, gn_ref, xo_ref, h_ref):
        ov = o_ref[...]
        xo = x_ref[...] + (ov * _rstd(ov) * gp_ref[...])
        xo_ref[...] = xo
        h_ref[...] = (xo * _rstd(xo) * gn_ref[...]).astype(BF16)

    return _row_call("norm_res", body, (s // tm,), [_rows(tm, d), _rows(tm, d), _vec(d), _vec(d)],
                     [_rows(tm, d), _rows(tm, d)],
                     [jax.ShapeDtypeStruct((s, d), F32), jax.ShapeDtypeStruct((s, d), BF16)],
                     (x_in, o, g_post, g_next), vmem_blocks=tm * d * 14)


def final_norm_loss(x_in, o, g_post, target, tm=256):
    s, d = x_in.shape
    tm = min(tm, s)

    def body(x_ref, o_ref, gp_ref, t_ref, loss_ref, dy_ref, do_ref, dg_ref):
        first = pl.program_id(0) == 0
        ov = o_ref[...]
        g = gp_ref[...]
        diff = x_ref[...] + (ov * _rstd(ov) * g) - t_ref[...]
        _accumulate(loss_ref, jnp.sum(diff * diff, axis=0, keepdims=True), first)
        dy = diff * (1.0 / d)
        dy_ref[...] = dy
        do, dg = _rms_bwd(dy, ov, g)
        do_ref[...] = do.astype(BF16)
        _accumulate(dg_ref, dg, first)

    return _row_call("final_norm_loss", body, (s // tm,), [_rows(tm, d), _rows(tm, d), _vec(d), _rows(tm, d)],
                     [_vec(d), _rows(tm, d), _rows(tm, d), _vec(d)],
                     [jax.ShapeDtypeStruct((1, d), F32), jax.ShapeDtypeStruct((s, d), F32),
                      jax.ShapeDtypeStruct((s, d), BF16), jax.ShapeDtypeStruct((1, d), F32)],
                     (x_in, o, g_post, target), vmem_blocks=tm * d * 18)


def norm_bwd_in_out(dx_out, dh, x_in, g_pre, o_below, g_post_below, tm=256, deps=()):
    s, d = x_in.shape
    tm = min(tm, s)

    def body(dxo_ref, dh_ref, x_ref, g_ref, o_ref, gb_ref, dxi_ref, do_ref, dg_ref, dgb_ref):
        first = pl.program_id(0) == 0
        dx, dg = _rms_bwd(dh_ref[...], x_ref[...], g_ref[...])
        dxi = dxo_ref[...] + dx
        dxi_ref[...] = dxi
        _accumulate(dg_ref, dg, first)
        do, dgb = _rms_bwd(dxi, o_ref[...], gb_ref[...])
        do_ref[...] = do.astype(BF16)
        _accumulate(dgb_ref, dgb, first)

    return _row_call("norm_bwd_in_out", body, (s // tm,),
                     [_rows(tm, d), _rows(tm, d), _rows(tm, d), _vec(d), _rows(tm, d), _vec(d)],
                     [_rows(tm, d), _rows(tm, d), _vec(d), _vec(d)],
                     [jax.ShapeDtypeStruct((s, d), F32), jax.ShapeDtypeStruct((s, d), BF16),
                      jax.ShapeDtypeStruct((1, d), F32), jax.ShapeDtypeStruct((1, d), F32)],
                     (dx_out, dh, x_in, g_pre, o_below, g_post_below), vmem_blocks=tm * d * 22, deps=deps)


def norm_bwd_in(dx_out, dh, x_in, g_pre, tm=256):
    s, d = x_in.shape
    tm = min(tm, s)

    def body(dxo_ref, dh_ref, x_ref, g_ref, dxi_ref, dg_ref):
        dx, dg = _rms_bwd(dh_ref[...], x_ref[...], g_ref[...])
        dxi_ref[...] = dxo_ref[...] + dx
        _accumulate(dg_ref, dg, pl.program_id(0) == 0)

    return _row_call("norm_bwd_in", body, (s // tm,), [_rows(tm, d), _rows(tm, d), _rows(tm, d), _vec(d)],
                     [_rows(tm, d), _vec(d)],
                     [jax.ShapeDtypeStruct((s, d), F32), jax.ShapeDtypeStruct((1, d), F32)],
                     (dx_out, dh, x_in, g_pre), vmem_blocks=tm * d * 16)


CONV_COLS = 256
CONV_ROWS = 32


SUBLANES = 8


def _shifted_copies(ext_ref, sh_ref):
    n = sh_ref.shape[1]
    for r in range(SUBLANES):
        sh_ref[r] = ext_ref[r:r + n, :]


def _window(sh_ref, off):
    return sh_ref[off % SUBLANES, off - off % SUBLANES:off - off % SUBLANES + CONV_ROWS, :]


def _depthwise(sh_ref, w, n_out, in_off, flip, emit):
    for r0 in range(0, n_out, CONV_ROWS):
        acc = None
        for k in range(CONV_WIDTH):
            term = w[k:k + 1, :] * _window(sh_ref, r0 + in_off + (CONV_WIDTH - 1 - k if flip else k))
            acc = term if acc is None else acc + term
        emit(r0, acc)


def _ext_scratch(tm):
    return [pltpu.VMEM((tm + CONV_PAD + SUBLANES, CONV_COLS), F32), pltpu.VMEM((SUBLANES, tm + CONV_PAD, CONV_COLS), F32)]


def glu_conv_fwd(proj, conv_w, conv_b, c_ch, tm=512):
    s = proj.shape[0]
    tm = min(tm, s)
    ncb = c_ch // CONV_COLS
    hb = tm // CONV_PAD

    def body(a_ref, g_ref, ah_ref, gh_ref, w_ref, b_ref, c_ref, ext_ref, sh_ref):
        i = pl.program_id(1)
        halo = ah_ref[...].astype(F32) * jax.nn.sigmoid(gh_ref[...].astype(F32))
        ext_ref[0:CONV_PAD, :] = jnp.where(i > 0, halo, 0.0)
        ext_ref[CONV_PAD:CONV_PAD + tm, :] = a_ref[...].astype(F32) * jax.nn.sigmoid(g_ref[...].astype(F32))
        ext_ref[CONV_PAD + tm:, :] = jnp.zeros((SUBLANES, CONV_COLS), F32)
        _shifted_copies(ext_ref, sh_ref)
        w = w_ref[...]
        bias = b_ref[...]

        def emit(r0, acc):
            c_ref[r0:r0 + CONV_ROWS, :] = acc + bias

        _depthwise(sh_ref, w, tm, CONV_PAD - (CONV_WIDTH - 1), False, emit)

    main = lambda col0: pl.BlockSpec((tm, CONV_COLS), lambda c, i: (i, col0 + c))
    halo = lambda col0: pl.BlockSpec((CONV_PAD, CONV_COLS), lambda c, i: (jnp.maximum(i * hb - 1, 0), col0 + c))
    return _row_call("glu_conv_fwd", body, (ncb, s // tm),
                     [main(0), main(ncb), halo(0), halo(ncb),
                      pl.BlockSpec((CONV_PAD, CONV_COLS), lambda c, i: (0, c)), pl.BlockSpec((1, CONV_COLS), lambda c, i: (0, c))],
                     pl.BlockSpec((tm, CONV_COLS), lambda c, i: (i, c)), jax.ShapeDtypeStruct((s, c_ch), F32),
                     (proj, proj, proj, proj, conv_w, conv_b),
                     scratch=_ext_scratch(tm), vmem_blocks=tm * CONV_COLS * 32)


def _layer_norm_stats(x):
    mu = jnp.mean(x, axis=-1, keepdims=True)
    xc = x - mu
    rstd = lax.rsqrt(jnp.mean(xc * xc, axis=-1, keepdims=True) + EPS)
    return xc * rstd, rstd


def _layer_norm_bwd(dy, xhat, rstd, g):
    dxh = dy * g
    return rstd * (dxh - jnp.mean(dxh, axis=-1, keepdims=True) - xhat * jnp.mean(dxh * xhat, axis=-1, keepdims=True))


def ln_silu_fwd(c, ln_g, ln_b, tm=256):
    s, d = c.shape
    tm = min(tm, s)

    def body(c_ref, g_ref, b_ref, s_ref):
        xhat, _ = _layer_norm_stats(c_ref[...])
        s_ref[...] = jax.nn.silu(xhat * g_ref[...] + b_ref[...]).astype(BF16)

    return _row_call("ln_silu_fwd", body, (s // tm,), [_rows(tm, d), _vec(d), _vec(d)], _rows(tm, d),
                     jax.ShapeDtypeStruct((s, d), BF16), (c, ln_g, ln_b), vmem_blocks=tm * d * 10)


def ln_silu_bwd(c, ds, ln_g, ln_b, tm=256, deps=()):
    s, d = c.shape
    tm = min(tm, s)

    def body(c_ref, ds_ref, g_ref, b_ref, dc_ref, dg_ref, db_ref, dcb_ref):
        first = pl.program_id(0) == 0
        g = g_ref[...]
        xhat, rstd = _layer_norm_stats(c_ref[...])
        y = xhat * g + b_ref[...]
        sg = jax.nn.sigmoid(y)
        dln = ds_ref[...] * (sg * (1.0 + y * (1.0 - sg)))
        _accumulate(dg_ref, jnp.sum(dln * xhat, axis=0, keepdims=True), first)
        _accumulate(db_ref, jnp.sum(dln, axis=0, keepdims=True), first)
        dc = _layer_norm_bwd(dln, xhat, rstd, g)
        dc_ref[...] = dc
        _accumulate(dcb_ref, jnp.sum(dc, axis=0, keepdims=True), first)

    vec = jax.ShapeDtypeStruct((1, d), F32)
    return _row_call("ln_silu_bwd", body, (s // tm,), [_rows(tm, d), _rows(tm, d), _vec(d), _vec(d)],
                     [_rows(tm, d), _vec(d), _vec(d), _vec(d)], [jax.ShapeDtypeStruct((s, d), F32), vec, vec, vec],
                     (c, ds, ln_g, ln_b), vmem_blocks=tm * d * 20, deps=deps)


def conv_bwd(proj, dc, conv_w, c_ch, tm=512):
    s = proj.shape[0]
    tm = min(tm, s)
    ncb = c_ch // CONV_COLS
    hb = tm // CONV_PAD
    last_halo = s // CONV_PAD - 1
    n_i = s // tm

    def body(a_ref, g_ref, ah_ref, gh_ref, dc_ref, dcn_ref, w_ref, dglu_ref, dw_ref,
             ext_ref, sh_ref, dce_ref, dsh_ref, dwacc_ref):
        i = pl.program_id(1)
        zeros = jnp.zeros((SUBLANES, CONV_COLS), F32)
        halo = ah_ref[...].astype(F32) * jax.nn.sigmoid(gh_ref[...].astype(F32))
        ext_ref[0:CONV_PAD, :] = jnp.where(i > 0, halo, 0.0)
        ext_ref[CONV_PAD:CONV_PAD + tm, :] = a_ref[...].astype(F32) * jax.nn.sigmoid(g_ref[...].astype(F32))
        ext_ref[CONV_PAD + tm:, :] = zeros
        _shifted_copies(ext_ref, sh_ref)
        dce_ref[0:tm, :] = dc_ref[...]
        dce_ref[tm:tm + CONV_PAD, :] = jnp.where(i < n_i - 1, dcn_ref[...], 0.0)
        dce_ref[tm + CONV_PAD:, :] = zeros
        _shifted_copies(dce_ref, dsh_ref)
        w = w_ref[...]

        def emit(r0, acc):
            dglu_ref[r0:r0 + CONV_ROWS, :] = acc

        _depthwise(dsh_ref, w, tm, 0, True, emit)

        for k in range(CONV_WIDTH):
            acc = None
            for r0 in range(0, tm, CONV_ROWS):
                term = dce_ref[r0:r0 + CONV_ROWS, :] * _window(sh_ref, r0 + CONV_PAD - (CONV_WIDTH - 1) + k)
                acc = term if acc is None else acc + term
            dwacc_ref[k:k + 1, :] = jnp.sum(acc, axis=0, keepdims=True)
        dwacc_ref[CONV_WIDTH:, :] = jnp.zeros((CONV_PAD - CONV_WIDTH, CONV_COLS), F32)
        _accumulate(dw_ref, dwacc_ref[...], i == 0)

    main = lambda col0: pl.BlockSpec((tm, CONV_COLS), lambda c, i: (i, col0 + c))
    halo = lambda col0: pl.BlockSpec((CONV_PAD, CONV_COLS), lambda c, i: (jnp.maximum(i * hb - 1, 0), col0 + c))
    nxt = pl.BlockSpec((CONV_PAD, CONV_COLS), lambda c, i: (jnp.minimum((i + 1) * hb, last_halo), c))
    wspec = pl.BlockSpec((CONV_PAD, CONV_COLS), lambda c, i: (0, c))
    return _row_call("conv_bwd", body, (ncb, n_i),
                     [main(0), main(ncb), halo(0), halo(ncb), main(0), nxt, wspec],
                     [main(0), wspec],
                     [jax.ShapeDtypeStruct((s, c_ch), F32), jax.ShapeDtypeStruct((CONV_PAD, c_ch), F32)],
                     (proj, proj, proj, proj, dc, dc, conv_w),
                     scratch=_ext_scratch(tm) + _ext_scratch(tm) + [pltpu.VMEM((CONV_PAD, CONV_COLS), F32)],
                     vmem_blocks=tm * CONV_COLS * 56)


def _two_phase_spec(tm, d, col0):
    return pl.BlockSpec((tm, d), lambda i, j: (i, col0 + j))


def glu_bwd(dproj, dglu, proj, c_ch, tm=256):
    s = proj.shape[0]
    tm = min(tm, s)

    def body(_, dglu_ref, a_ref, g_ref, out_ref, stash_ref):
        j = pl.program_id(1)

        @pl.when(j == 0)
        def _():
            dg = dglu_ref[...]
            sg = jax.nn.sigmoid(g_ref[...].astype(F32))
            out_ref[...] = (dg * sg).astype(BF16)
            stash_ref[...] = (dg * a_ref[...].astype(F32) * (sg * (1.0 - sg))).astype(BF16)

        @pl.when(j == 1)
        def _():
            out_ref[...] = stash_ref[...]

    rows = lambda col: pl.BlockSpec((tm, c_ch), lambda i, j: (i, col))
    return _row_call("glu_bwd", body, (s // tm, 2), [ANY, rows(0), rows(0), rows(1)], _two_phase_spec(tm, c_ch, 0),
                     jax.ShapeDtypeStruct(dproj.shape, BF16), (dproj, dglu, proj, proj),
                     scratch=[pltpu.VMEM((tm, c_ch), BF16)], aliases={0: 0}, vmem_blocks=tm * c_ch * 14)


_SQRT_HALF = 0.7071067811865476
_INV_SQRT_2PI = 0.3989422804014327


def _gelu_parts(x):
    cdf = 0.5 * (1.0 + lax.erf(x * _SQRT_HALF))
    return cdf, x * cdf


def _gelu_grad(x, cdf):
    return cdf + x * (_INV_SQRT_2PI * jnp.exp(-0.5 * x * x))


def _sgu_specs(tm, ch):
    grp = ch // SGU_GROUPS
    full3 = lambda shape: pl.BlockSpec(shape, lambda *_: (0, 0, 0))
    return grp, full3((SGU_GROUPS, CHUNK, CHUNK)), full3((SGU_GROUPS, CHUNK, grp))


def sgu_fwd(proj, ln_g, ln_b, w_mix, b_mix, ch, col0, tm=CHUNK):
    s = proj.shape[0]
    grp, wspec, bspec = _sgu_specs(tm, ch)

    def body(u_ref, v_ref, g_ref, b_ref, w_ref, bm_ref, p_ref, mix_ref):
        _, u = _gelu_parts(u_ref[...].astype(F32))
        _, v0 = _gelu_parts(v_ref[...].astype(F32))
        xhat, _ = _layer_norm_stats(v0)
        vn = (xhat * g_ref[...] + b_ref[...]).astype(BF16)
        for n in range(tm // CHUNK):
            for g in range(SGU_GROUPS):
                blk = vn[n * CHUNK:(n + 1) * CHUNK, g * grp:(g + 1) * grp]
                mix_ref[n * CHUNK:(n + 1) * CHUNK, g * grp:(g + 1) * grp] = (
                    jnp.dot(w_ref[g], blk, preferred_element_type=F32) + bm_ref[g])
        p_ref[...] = (u * mix_ref[...]).astype(BF16)

    return _row_call("sgu_fwd", body, (s // tm,),
                     [_rows(tm, ch, col0), _rows(tm, ch, col0 + 1), _vec(ch), _vec(ch), wspec, bspec], _rows(tm, ch),
                     jax.ShapeDtypeStruct((s, ch), BF16), (proj, proj, ln_g, ln_b, w_mix, b_mix),
                     scratch=[pltpu.VMEM((tm, ch), F32)], vmem_blocks=tm * ch * 30)


def sgu_bwd(dproj, proj, dp, ln_g, ln_b, w_mix, w_mix_t, b_mix, ch, col0, tm=CHUNK):
    s = proj.shape[0]
    grp, wspec, bspec = _sgu_specs(tm, ch)

    def body(_, u_ref, v_ref, dp_ref, g_ref, b_ref, w_ref, wt_ref, bm_ref,
             out_ref, dw_ref, dbm_ref, dg_ref, db_ref, stash_ref, mix_ref, dvn_ref):
        i = pl.program_id(0)
        j = pl.program_id(1)

        @pl.when(j == 0)
        def _():
            first = i == 0
            ub = u_ref[...].astype(F32)
            vb = v_ref[...].astype(F32)
            cdf_u, u = _gelu_parts(ub)
            cdf_v, v0 = _gelu_parts(vb)
            g = g_ref[...]
            xhat, rstd = _layer_norm_stats(v0)
            vn = (xhat * g + b_ref[...]).astype(BF16)
            dpv = dp_ref[...]
            dmix = dpv * u
            dmix_bf = dmix.astype(BF16)
            for n in range(tm // CHUNK):
                for k in range(SGU_GROUPS):
                    rows = slice(n * CHUNK, (n + 1) * CHUNK)
                    cols = slice(k * grp, (k + 1) * grp)
                    vblk = vn[rows, cols]
                    dblk = dmix_bf[rows, cols]
                    mix_ref[rows, cols] = jnp.dot(w_ref[k], vblk, preferred_element_type=F32) + bm_ref[k]
                    dvn_ref[rows, cols] = jnp.dot(wt_ref[k], dblk, preferred_element_type=F32)
                    dwk = lax.dot_general(dblk, vblk, (((1,), (1,)), ((), ())), preferred_element_type=F32)
                    dbk = jnp.broadcast_to(jnp.sum(dmix[rows, cols], axis=1, keepdims=True), (CHUNK, CHUNK))
                    if n == 0:
                        _accumulate(dw_ref.at[k], dwk, first)
                        _accumulate(dbm_ref.at[k], dbk, first)
                    else:
                        dw_ref[k] += dwk
                        dbm_ref[k] += dbk
            du = dpv * mix_ref[...]
            dvn = dvn_ref[...]
            _accumulate(dg_ref, jnp.sum(dvn * xhat, axis=0, keepdims=True), first)
            _accumulate(db_ref, jnp.sum(dvn, axis=0, keepdims=True), first)
            dv0 = _layer_norm_bwd(dvn, xhat, rstd, g)
            out_ref[...] = (du * _gelu_grad(ub, cdf_u)).astype(BF16)
            stash_ref[...] = (dv0 * _gelu_grad(vb, cdf_v)).astype(BF16)

        @pl.when(j == 1)
        def _():
            out_ref[...] = stash_ref[...]

    rows = lambda col: pl.BlockSpec((tm, ch), lambda i, j: (i, col))
    vec = pl.BlockSpec((1, ch), lambda i, j: (0, 0))
    acc3 = lambda: pl.BlockSpec((SGU_GROUPS, CHUNK, CHUNK), lambda i, j: (0, 0, 0))
    vshape = jax.ShapeDtypeStruct((1, ch), F32)
    mshape = jax.ShapeDtypeStruct((SGU_GROUPS, CHUNK, CHUNK), F32)
    return _row_call("sgu_bwd", body, (s // tm, 2),
                     [ANY, rows(col0), rows(col0 + 1), rows(0), vec, vec, wspec, wspec, bspec],
                     [_two_phase_spec(tm, ch, col0), acc3(), acc3(), vec, vec],
                     [jax.ShapeDtypeStruct(dproj.shape, BF16), mshape, mshape, vshape, vshape],
                     (dproj, proj, proj, dp, ln_g, ln_b, w_mix, w_mix_t, b_mix),
                     scratch=[pltpu.VMEM((tm, ch), BF16), pltpu.VMEM((tm, ch), F32), pltpu.VMEM((tm, ch), F32)],
                     aliases={0: 0}, vmem_blocks=tm * ch * 60)


def merge_fwd(y_a, y_b, proj, b_gate, col0, tm=256):
    s, d = y_a.shape
    tm = min(tm, s)

    def body(ya_ref, yb_ref, l0_ref, l1_ref, bg_ref, m_ref):
        g0 = jax.nn.sigmoid(l0_ref[...].astype(F32) + bg_ref[0:1, :])
        g1 = jax.nn.sigmoid(l1_ref[...].astype(F32) + bg_ref[1:2, :])
        m_ref[...] = (g0 * ya_ref[...] + g1 * yb_ref[...]).astype(BF16)

    return _row_call("merge_fwd", body, (s // tm,),
                     [_rows(tm, d), _rows(tm, d), _rows(tm, d, col0), _rows(tm, d, col0 + 1), pl.BlockSpec((2, d), lambda i: (0, 0))],
                     _rows(tm, d), jax.ShapeDtypeStruct((s, d), BF16), (y_a, y_b, proj, proj, b_gate), vmem_blocks=tm * d * 16)


def merge_bwd(dm, y_a, y_b, proj, b_gate, col0, tm=256, deps=()):
    s, d = y_a.shape
    tm = min(tm, s)

    def body(dm_ref, ya_ref, yb_ref, l0_ref, l1_ref, bg_ref, dya_ref, dyb_ref, out_ref, dbg_ref, stash_ref):
        i = pl.program_id(0)
        j = pl.program_id(1)

        @pl.when(j == 0)
        def _():
            dmv = dm_ref[...]
            g0 = jax.nn.sigmoid(l0_ref[...].astype(F32) + bg_ref[0:1, :])
            g1 = jax.nn.sigmoid(l1_ref[...].astype(F32) + bg_ref[1:2, :])
            dya_ref[...] = (dmv * g0).astype(BF16)
            dyb_ref[...] = (dmv * g1).astype(BF16)
            dl0 = dmv * ya_ref[...] * (g0 * (1.0 - g0))
            dl1 = dmv * yb_ref[...] * (g1 * (1.0 - g1))
            _accumulate(dbg_ref.at[0:1, :], jnp.sum(dl0, axis=0, keepdims=True), i == 0)
            _accumulate(dbg_ref.at[1:2, :], jnp.sum(dl1, axis=0, keepdims=True), i == 0)
            out_ref[...] = dl0.astype(BF16)
            stash_ref[...] = dl1.astype(BF16)

        @pl.when(j == 1)
        def _():
            out_ref[...] = stash_ref[...]

    rows = lambda col: pl.BlockSpec((tm, d), lambda i, j: (i, col))
    bgspec = pl.BlockSpec((2, d), lambda i, j: (0, 0))
    return _row_call("merge_bwd", body, (s // tm, 2),
                     [rows(0), rows(0), rows(0), rows(col0), rows(col0 + 1), bgspec],
                     [rows(0), rows(0), _two_phase_spec(tm, d, col0), bgspec],
                     [jax.ShapeDtypeStruct((s, d), BF16), jax.ShapeDtypeStruct((s, d), BF16),
                      jax.ShapeDtypeStruct(proj.shape, BF16), jax.ShapeDtypeStruct((2, d), F32)],
                     (dm, y_a, y_b, proj, proj, b_gate), scratch=[pltpu.VMEM((tm, d), BF16)], vmem_blocks=tm * d * 30, deps=deps)


def swiglu_fwd(gu, tm=512):
    s, w2 = gu.shape
    tm = min(tm, s)
    nb = N_DEV // 2
    cb = w2 // N_DEV

    def body(g_ref, u_ref, f_ref):
        f_ref[...] = (jax.nn.silu(g_ref[...].astype(F32)) * u_ref[...].astype(F32)).astype(BF16)

    return _row_call("swiglu_fwd", body, (s // tm, nb),
                     [pl.BlockSpec((tm, cb), lambda i, j: (i, j)), pl.BlockSpec((tm, cb), lambda i, j: (i, j + nb))],
                     pl.BlockSpec((tm, cb), lambda i, j: (i, j)), jax.ShapeDtypeStruct((s, w2 // 2), BF16), (gu, gu),
                     vmem_blocks=tm * cb * 12)


def swiglu_bwd(gu, df, tm=512, deps=()):
    s, w2 = gu.shape
    tm = min(tm, s)
    nb = N_DEV // 2
    cb = w2 // N_DEV

    def body(g_ref, u_ref, df_ref, out_ref, stash_ref):
        j = pl.program_id(1)

        @pl.when(j < nb)
        def _():
            g = g_ref[...].astype(F32)
            sg = jax.nn.sigmoid(g)
            dfv = df_ref[...].astype(F32)
            out_ref[...] = (dfv * u_ref[...].astype(F32) * (sg * (1.0 + g * (1.0 - sg)))).astype(BF16)
            stash_ref[j] = (dfv * (g * sg)).astype(BF16)

        @pl.when(j >= nb)
        def _():
            out_ref[...] = stash_ref[j - nb]

    held = lambda col0: (lambda i, j: (i, jnp.minimum(j, nb - 1) + col0))
    return _row_call("swiglu_bwd", body, (s // tm, N_DEV),
                     [pl.BlockSpec((tm, cb), held(0)), pl.BlockSpec((tm, cb), held(nb)), pl.BlockSpec((tm, cb), held(0))],
                     pl.BlockSpec((tm, cb), lambda i, j: (i, j)), jax.ShapeDtypeStruct((s, w2), BF16), (gu, gu, df),
                     scratch=[pltpu.VMEM((nb, tm, cb), BF16)], vmem_blocks=tm * cb * (16 + nb), deps=deps)


def _peers():
    x, y, c = lax.axis_index("x"), lax.axis_index("y"), lax.axis_index("c")
    me = 4 * x + 2 * y + c
    peers = []
    for k in range(1, N_DEV):
        px = 1 - x if k & 4 else x
        py = 1 - y if k & 2 else y
        pc = 1 - c if k & 1 else c
        peers.append(((px, py, pc), 4 * px + 2 * py + pc))
    return me, peers


def _exchange(name, arrays, scatter):
    n = len(arrays)

    def body(*refs):
        ins, outs = refs[:n], refs[n:2 * n]
        send_sems, recv_sems, local_sems = refs[2 * n:]
        me, peers = _peers()

        def remote(a, k):
            (pos, idx) = peers[k]
            src = ins[a].at[idx] if scatter else ins[a]
            return pltpu.make_async_remote_copy(src_ref=src, dst_ref=outs[a].at[me], send_sem=send_sems.at[a, k],
                                                recv_sem=recv_sems.at[a, k], device_id=pos, device_id_type=pl.DeviceIdType.MESH)

        def arrival(a, k):
            (pos, idx) = peers[k]
            src = ins[a].at[idx] if scatter else ins[a]
            return pltpu.make_async_remote_copy(src_ref=src, dst_ref=outs[a].at[idx], send_sem=send_sems.at[a, k],
                                                recv_sem=recv_sems.at[a, k], device_id=pos, device_id_type=pl.DeviceIdType.MESH)

        local = [pltpu.make_async_copy(ins[a].at[me] if scatter else ins[a], outs[a].at[me], local_sems.at[a]) for a in range(n)]
        sends = [remote(a, k) for k in range(N_DEV - 1) for a in range(n)]
        for cp in sends:
            cp.start()
        for cp in local:
            cp.start()
        for k in range(N_DEV - 1):
            for a in range(n):
                arrival(a, k).wait_recv()
        for cp in sends:
            cp.wait_send()
        for cp in local:
            cp.wait()

    out_shape = [jax.ShapeDtypeStruct(a.shape if scatter else (N_DEV,) + a.shape, a.dtype) for a in arrays]
    return pl.pallas_call(
        body, name=name, in_specs=[ANY] * n, out_specs=[ANY] * n, out_shape=out_shape,
        scratch_shapes=[pltpu.SemaphoreType.DMA((n, N_DEV - 1)), pltpu.SemaphoreType.DMA((n, N_DEV - 1)),
                        pltpu.SemaphoreType.DMA((n,))],
    )(*arrays)


HBM_SPEC = pl.BlockSpec(memory_space=pltpu.HBM)
SEM_SPEC = pl.BlockSpec(memory_space=pltpu.SEMAPHORE)
DATAFLOW_EFFECT = pltpu.SideEffectType.DATAFLOW_SIDE_EFFECTING
ALL_PEERS = (1, 2, 3, 4, 5, 6, 7)
SIBLING = 1
SAME_CORE_PEERS = (2, 4, 6)


def _in_hbm(a):
    return pltpu.with_memory_space_constraint(a, pltpu.HBM)


def fill_own_slot(name, me, src, block, dtype):
    _, r, c = src.shape
    tr = _row_tile(r, 256)

    def body(me_ref, src_ref, out_ref):
        out_ref[...] = src_ref[...].astype(dtype)

    if block is None:
        src_index = lambda i, me_ref: (me_ref[0], i, 0)
    else:
        src_index = lambda i, me_ref: (block, i, 0)
    grid_spec = pltpu.PrefetchScalarGridSpec(
        num_scalar_prefetch=1, grid=(r // tr,), in_specs=[pl.BlockSpec((None, tr, c), src_index)],
        out_specs=pl.BlockSpec((None, tr, c), lambda i, me_ref: (me_ref[0], i, 0)))
    return pl.pallas_call(body, name=name, grid_spec=grid_spec, out_shape=jax.ShapeDtypeStruct((N_DEV, r, c), dtype),
                          compiler_params=_params(tr * c * (src.dtype.itemsize + jnp.dtype(dtype).itemsize), 1))(me, src)


def _split_copy(src, land, send_sem, recv_sem, peer, me, arriving):
    (pos, idx) = peer
    return pltpu.make_async_remote_copy(
        src_ref=land.at[me] if src is None else src.at[idx], dst_ref=land.at[idx if arriving else me],
        send_sem=send_sem, recv_sem=recv_sem, device_id=pos, device_id_type=pl.DeviceIdType.MESH)


def exchange_start(name, lands, peer_ks, srcs=None, after=None):
    n = len(lands)
    ns = n if srcs is not None else 0
    extra = _deps([after])
    bufs = (list(srcs) if srcs is not None else []) + list(lands)

    def body(*refs):
        src, land = refs[:ns], refs[ns:ns + n]
        outs = refs[ns + n + len(extra):]
        send_sems, recv_sems, token = outs[:n], outs[n:2 * n], outs[2 * n + ns + n]
        me, peers = _peers()
        for a in range(n):
            for j, k in enumerate(peer_ks):
                _split_copy(src[a] if ns else None, land[a], send_sems[a].at[j], recv_sems[a].at[j], peers[k - 1], me, False).start()
        token[...] = jnp.zeros_like(token)

    sems = [pltpu.SemaphoreType.DMA((len(peer_ks),))] * (2 * n)
    res = pl.pallas_call(
        body, name=name, in_specs=[HBM_SPEC] * len(bufs) + [ANY] * len(extra),
        out_specs=[SEM_SPEC] * (2 * n) + [HBM_SPEC] * len(bufs) + [pl.BlockSpec(memory_space=pltpu.VMEM)],
        out_shape=sems + [pltpu.HBM(a.shape, a.dtype) for a in bufs] + [jax.ShapeDtypeStruct((8, 128), F32)],
        input_output_aliases={i: 2 * n + i for i in range(len(bufs))},
        compiler_params=pltpu.CompilerParams(has_side_effects=DATAFLOW_EFFECT),
    )(*[_in_hbm(a) for a in bufs], *extra)
    handles = [(res[a], res[n + a], res[2 * n + a] if ns else None, res[2 * n + ns + a]) for a in range(n)]
    return handles, res[2 * n + ns + n]


def exchange_wait(name, handle, peer_ks, after):
    send_sem, recv_sem, src, land = handle
    bufs = ([src] if src is not None else []) + [land]
    nb = len(bufs)

    def body(*refs):
        src_ref = refs[0] if nb == 2 else None
        land_ref, send_ref, recv_ref = refs[nb - 1], refs[nb], refs[nb + 1]
        me, peers = _peers()
        for j, k in enumerate(peer_ks):
            cp = _split_copy(src_ref, land_ref, send_ref.at[j], recv_ref.at[j], peers[k - 1], me, True)
            cp.wait_send()
            cp.wait_recv()

    return pl.pallas_call(
        body, name=name, in_specs=[HBM_SPEC] * nb + [SEM_SPEC, SEM_SPEC, ANY], out_specs=[HBM_SPEC] * nb,
        out_shape=[pltpu.HBM(a.shape, a.dtype) for a in bufs],
        input_output_aliases={i: i for i in range(nb)}, compiler_params=pltpu.CompilerParams(has_side_effects=DATAFLOW_EFFECT),
    )(*bufs, send_sem, recv_sem, after)[nb - 1]


def forward_to_sibling(name, land):
    def body(_, buf, send_sems, recv_sems):
        me, peers = _peers()
        sibling_pos, _ = peers[SIBLING - 1]

        def copy(j, slot):
            return pltpu.make_async_remote_copy(src_ref=buf.at[slot], dst_ref=buf.at[slot], send_sem=send_sems.at[j],
                                                recv_sem=recv_sems.at[j], device_id=sibling_pos, device_id_type=pl.DeviceIdType.MESH)

        sends = [copy(j, peers[k - 1][1]) for j, k in enumerate(SAME_CORE_PEERS)]
        for cp in sends:
            cp.start()
        for j, k in enumerate(SAME_CORE_PEERS):
            copy(j, peers[(k | SIBLING) - 1][1]).wait_recv()
        for cp in sends:
            cp.wait_send()

    return pl.pallas_call(
        body, name=name, in_specs=[ANY], out_specs=ANY, out_shape=jax.ShapeDtypeStruct(land.shape, land.dtype),
        input_output_aliases={0: 0},
        scratch_shapes=[pltpu.SemaphoreType.DMA((len(SAME_CORE_PEERS),)), pltpu.SemaphoreType.DMA((len(SAME_CORE_PEERS),))],
    )(land)


def _row_tile(r, cap):
    if r <= cap:
        return r
    return max(t for t in range(16, cap + 1, 16) if r % t == 0)


def sum_adamw(name, parts, w, m, v, tr):
    nl, r, c = w.shape
    tr = _row_tile(r, tr)
    c1 = 1.0 - ADAM_B1 ** ADAM_STEP
    c2 = 1.0 - ADAM_B2 ** ADAM_STEP

    def body(*refs):
        part_refs = refs[:nl]
        w_ref, m_ref, v_ref, g_out, d_out, m_out, v_out = refs[nl:]
        layer = pl.program_id(0)
        for j in range(nl):
            @pl.when(layer == j)
            def _(j=j):
                g = part_refs[j][0].astype(F32)
                for p in range(1, N_DEV):
                    g = g + part_refs[j][p].astype(F32)
                mn = ADAM_B1 * m_ref[...] + (1.0 - ADAM_B1) * g
                vn = ADAM_B2 * v_ref[...] + (1.0 - ADAM_B2) * (g * g)
                g_out[...] = g
                m_out[...] = mn
                v_out[...] = vn
                d_out[...] = -ADAM_LR * ((mn / c1) / (jnp.sqrt(vn / c2) + ADAM_EPS) + ADAM_WD * w_ref[...])

    def part_spec(j):
        return pl.BlockSpec((N_DEV, tr, c), lambda l, i: (0, jnp.where(l == j, i, 0), 0))

    lspec = pl.BlockSpec((None, tr, c), lambda l, i: (l, i, 0))
    out = jax.ShapeDtypeStruct((nl, r, c), F32)
    return _row_call(name, body, (nl, r // tr), [part_spec(j) for j in range(nl)] + [lspec] * 3, [lspec] * 4, [out] * 4,
                     tuple(parts) + (w, m, v), vmem_blocks=nl * N_DEV * tr * c * parts[0].dtype.itemsize + 7 * tr * c * 4)


REPLICATED = ("norm_mix_pre", "norm_mix_post", "norm_ffn_pre", "norm_ffn_post", "conv_b", "conv_ln_g", "conv_ln_b",
              "sgu_ln_g", "sgu_ln_b", "w_spatial", "b_spatial")
MATRICES = ("w_in", "w_a_out", "w_b_out", "w_o", "w_gate_up", "w_down")


def local_step(x, target, rep, weight, emit, b_gate, conv_w, start_token=None):
    s, d = x.shape
    causal = jnp.tril(jnp.ones((CHUNK, CHUNK), dtype=bool))
    row = lambda name, l: rep[name][l].reshape(1, -1)

    saved = []
    h = rms_fwd(x, row("norm_mix_pre", 0), deps=[start_token])
    for l in range(DEPTH):
        w_mix = jnp.where(causal[None], rep["w_spatial"][l], 0.0).astype(BF16)
        b_mix = jnp.broadcast_to(rep["b_spatial"][l][:, :, None], (SGU_GROUPS, CHUNK, d // SGU_GROUPS))
        proj = mm_nn("proj", h, weight(l, "w_in", h), BF16, 512)
        c = glu_conv_fwd(proj, conv_w[l], row("conv_b", l), d)
        s_act = ln_silu_fwd(c, row("conv_ln_g", l), row("conv_ln_b", l))
        p_act = sgu_fwd(proj, row("sgu_ln_g", l), row("sgu_ln_b", l), w_mix, b_mix, d, 2)
        y_a = mm_nn("branch_out", s_act, weight(l, "w_a_out", s_act), F32, 512)
        y_b = mm_nn("branch_out", p_act, weight(l, "w_b_out", p_act), F32, 512)
        merged = merge_fwd(y_a, y_b, proj, b_gate[l], 4)
        o = mm_nn("branch_out", merged, weight(l, "w_o", merged), F32, 512)
        x_mid, h2 = norm_res(x, o, row("norm_mix_post", l), row("norm_ffn_pre", l))
        gu = mm_nn("gate_up", h2, weight(l, "w_gate_up", h2), BF16, 512)
        f = swiglu_fwd(gu)
        o2 = mm_nn("down", f, weight(l, "w_down", f), F32, 512, tn=1024)
        saved.append(dict(x_in=x, h=h, proj=proj, c=c, s_act=s_act, p_act=p_act, y_a=y_a, y_b=y_b, merged=merged, o=o,
                          x_mid=x_mid, h2=h2, gu=gu, f=f, o2=o2, w_mix=w_mix, b_mix=b_mix))
        if l + 1 < DEPTH:
            x, h = norm_res(x_mid, o2, row("norm_ffn_post", l), row("norm_mix_pre", l + 1))

    top = saved[-1]
    loss_vec, dx, do2, dg_ffn_post = final_norm_loss(top["x_mid"], top["o2"], row("norm_ffn_post", DEPTH - 1), target)
    loss = (0.5 / d) * jnp.sum(loss_vec)

    grads = [None] * DEPTH
    for l in reversed(range(DEPTH)):
        sv = saved[l]
        wl = {name: weight(l, name, None) for name in MATRICES}
        g = {"norm_ffn_post": dg_ffn_post}
        df = mm_nt("d_down_in", do2, wl["w_down"], BF16, 512, tn=wl["w_down"].shape[0] // 4)
        tok = emit(l, "w_down", mm_tn("d_down_w", sv["f"], do2, BF16, 1024, sv["f"].shape[1] // 4))
        dgu = swiglu_bwd(sv["gu"], df, deps=[tok])
        dh2 = mm_nt("d_gate_up_in", dgu, wl["w_gate_up"], F32, 1024)
        tok = emit(l, "w_gate_up", mm_tn("d_gate_up_w", sv["h2"], dgu, BF16, 2048, d // 2, nb=wl["w_gate_up"].shape[2]))
        dx, do, g["norm_ffn_pre"], g["norm_mix_post"] = norm_bwd_in_out(
            dx, dh2, sv["x_mid"], row("norm_ffn_pre", l), sv["o"], row("norm_mix_post", l), deps=[tok])
        dm = mm_nt("d_square_in", do, wl["w_o"], F32, 512)
        tok = emit(l, "w_o", mm_tn("d_square_w", sv["merged"], do, BF16, 2048, d // 2))
        dy_a, dy_b, dproj, g["b_gate"] = merge_bwd(dm, sv["y_a"], sv["y_b"], sv["proj"], b_gate[l], 4, deps=[tok])
        ds = mm_nt("d_square_in", dy_a, wl["w_a_out"], F32, 512)
        tok = emit(l, "w_a_out", mm_tn("d_square_w", sv["s_act"], dy_a, BF16, 2048, d // 2))
        dp = mm_nt("d_square_in", dy_b, wl["w_b_out"], F32, 512, deps=[tok])
        tok = emit(l, "w_b_out", mm_tn("d_square_w", sv["p_act"], dy_b, BF16, 2048, d // 2))
        dc, g["conv_ln_g"], g["conv_ln_b"], g["conv_b"] = ln_silu_bwd(
            sv["c"], ds, row("conv_ln_g", l), row("conv_ln_b", l), deps=[tok])
        dglu, g["conv_w"] = conv_bwd(sv["proj"], dc, conv_w[l], d)
        dproj = glu_bwd(dproj, dglu, sv["proj"], d)
        dproj, dw_mix, db_mix, g["sgu_ln_g"], g["sgu_ln_b"] = sgu_bwd(
            dproj, sv["proj"], dp, row("sgu_ln_g", l), row("sgu_ln_b", l), sv["w_mix"],
            jnp.swapaxes(sv["w_mix"], 1, 2), sv["b_mix"], d, 2)
        g["w_spatial"] = jnp.where(causal[None], dw_mix, 0.0)
        g["b_spatial"] = db_mix[:, :, 0]
        tok = emit(l, "w_in", mm_tn("d_in_w", sv["h"], dproj, BF16, 2048, d // 2, nb=wl["w_in"].shape[2]))
        dh = mm_nt("d_in_in", dproj, wl["w_in"], F32, 1024, deps=[tok])
        if l > 0:
            below = saved[l - 1]
            dx, do2, g["norm_mix_pre"], dg_ffn_post = norm_bwd_in_out(
                dx, dh, sv["x_in"], row("norm_mix_pre", l), below["o2"], row("norm_ffn_post", l - 1))
        else:
            dx, g["norm_mix_pre"] = norm_bwd_in(dx, dh, sv["x_in"], row("norm_mix_pre", l))
        grads[l] = g
    return loss, dx, grads


def _pack_rows(arrays):
    return jnp.concatenate([a.reshape(-1, 128) for a in arrays], axis=0)


def _unpack_rows(packed, shapes):
    out, r0 = [], 0
    for shp in shapes:
        nr = math.prod(shp) // 128
        out.append(packed[r0:r0 + nr].reshape(shp))
        r0 += nr
    return out


def kernel(x, norm_mix_pre, norm_mix_post, norm_ffn_pre, norm_ffn_post, w_in, b_gate, conv_w, conv_b, conv_ln_g, conv_ln_b, w_a_out, sgu_ln_g, sgu_ln_b, w_spatial, b_spatial, w_b_out, w_o, w_gate_up, w_down, loss_target, m_norm_mix_pre, m_norm_mix_post, m_norm_ffn_pre, m_norm_ffn_post, m_w_in, m_b_gate, m_conv_w, m_conv_b, m_conv_ln_g, m_conv_ln_b, m_w_a_out, m_sgu_ln_g, m_sgu_ln_b, m_w_spatial, m_b_spatial, m_w_b_out, m_w_o, m_w_gate_up, m_w_down, v_norm_mix_pre, v_norm_mix_post, v_norm_ffn_pre, v_norm_ffn_post, v_w_in, v_b_gate, v_conv_w, v_conv_b, v_conv_ln_g, v_conv_ln_b, v_w_a_out, v_sgu_ln_g, v_sgu_ln_b, v_w_spatial, v_b_spatial, v_w_b_out, v_w_o, v_w_gate_up, v_w_down):
    names = ("norm_mix_pre", "norm_mix_post", "norm_ffn_pre", "norm_ffn_post", "w_in", "b_gate", "conv_w", "conv_b",
             "conv_ln_g", "conv_ln_b", "w_a_out", "sgu_ln_g", "sgu_ln_b", "w_spatial", "b_spatial", "w_b_out", "w_o",
             "w_gate_up", "w_down")
    w = dict(zip(names, (norm_mix_pre, norm_mix_post, norm_ffn_pre, norm_ffn_post, w_in, b_gate, conv_w, conv_b,
                         conv_ln_g, conv_ln_b, w_a_out, sgu_ln_g, sgu_ln_b, w_spatial, b_spatial, w_b_out, w_o,
                         w_gate_up, w_down)))
    m = dict(zip(names, (m_norm_mix_pre, m_norm_mix_post, m_norm_ffn_pre, m_norm_ffn_post, m_w_in, m_b_gate, m_conv_w,
                         m_conv_b, m_conv_ln_g, m_conv_ln_b, m_w_a_out, m_sgu_ln_g, m_sgu_ln_b, m_w_spatial,
                         m_b_spatial, m_w_b_out, m_w_o, m_w_gate_up, m_w_down)))
    v = dict(zip(names, (v_norm_mix_pre, v_norm_mix_post, v_norm_ffn_pre, v_norm_ffn_post, v_w_in, v_b_gate, v_conv_w,
                         v_conv_b, v_conv_ln_g, v_conv_ln_b, v_w_a_out, v_sgu_ln_g, v_sgu_ln_b, v_w_spatial,
                         v_b_spatial, v_w_b_out, v_w_o, v_w_gate_up, v_w_down)))
    d = x.shape[-1]
    shard_cols = d // N_DEV

    def small_pack(bg, cw):
        rows = jnp.concatenate([bg, cw], axis=1).reshape(DEPTH * (2 + CONV_WIDTH), shard_cols)
        return jnp.pad(rows, ((0, (-rows.shape[0]) % 8), (0, 0)))

    small_w, small_m, small_v = (small_pack(t["b_gate"], t["conv_w"]) for t in (w, m, v))

    small_full, = _exchange("gather_small", [small_w], scatter=False)
    small_full = small_full[:, :DEPTH * (2 + CONV_WIDTH)].reshape(N_DEV, DEPTH, 2 + CONV_WIDTH, shard_cols)
    small_full = jnp.transpose(small_full, (1, 2, 0, 3)).reshape(DEPTH, 2 + CONV_WIDTH, d)
    b_gate_full = small_full[:, :2]
    conv_w_full = jnp.pad(small_full[:, 2:], ((0, 0), (0, CONV_PAD - CONV_WIDTH), (0, 0)))

    me = (4 * lax.axis_index("x") + 2 * lax.axis_index("y") + lax.axis_index("c")).astype(jnp.int32).reshape(1)
    first_level = (SIBLING,) + SAME_CORE_PEERS
    gathers, token = {}, small_full
    for l in range(DEPTH):
        lands = [fill_own_slot(f"cast_{name}", me, w[name], l, BF16) for name in MATRICES]
        handles, token = exchange_start(f"gather_start_{l}", lands, first_level, after=token)
        for name, handle in zip(MATRICES, handles):
            gathers[l, name] = handle
    gathered = {}

    def weight(l, name, after):
        if (l, name) not in gathered:
            land = exchange_wait(f"gather_wait_{name}_{l}", gathers[l, name], first_level, after)
            full = forward_to_sibling(f"forward_{name}_{l}", land)
            gathered[l, name] = full if name in ("w_in", "w_gate_up") else full.reshape(-1, d)
        return gathered[l, name]

    scatters = {}

    def emit(l, name, g):
        chunks = g if g.ndim == 3 else g.reshape(N_DEV, -1, d)
        land = fill_own_slot(f"own_grad_{name}", me, chunks, None, BF16)
        (scatters[l, name],), tok = exchange_start(f"scatter_start_{name}_{l}", [land], ALL_PEERS, srcs=[chunks])
        return tok

    rep = {name: w[name] for name in REPLICATED}
    loss, grad_x, grads = local_step(x[0], loss_target[0], rep, weight, emit, b_gate_full, conv_w_full, token)
    loss = lax.psum(loss, MESH_AXES)

    small_g = jnp.stack([jnp.concatenate([grads[l]["b_gate"], grads[l]["conv_w"][:CONV_WIDTH]], axis=0) for l in range(DEPTH)])
    small_g = jnp.transpose(small_g.reshape(DEPTH * (2 + CONV_WIDTH), N_DEV, shard_cols), (1, 0, 2))
    small_g = jnp.pad(small_g, ((0, 0), (0, small_w.shape[0] - small_g.shape[1]), (0, 0)))
    rep_shapes = [w[name].shape for name in REPLICATED]
    rep_g = _pack_rows([jnp.stack([grads[l][name].reshape(w[name].shape[1:]) for l in range(DEPTH)]) for name in REPLICATED])
    small_land = fill_own_slot("own_small", me, small_g, None, F32)
    (small_handle,), tok_small = exchange_start("scatter_start_small", [small_land], ALL_PEERS, srcs=[small_g])
    rep_land = fill_own_slot("own_replicated", me, rep_g[None], 0, F32)
    (rep_handle,), tok_rep = exchange_start("gather_start_replicated", [rep_land], ALL_PEERS, after=tok_small)

    out = {}
    after = tok_rep
    for name in ("w_down", "w_gate_up", "w_o", "w_a_out", "w_b_out", "w_in"):
        parts = [exchange_wait(f"scatter_wait_{name}_{l}", scatters[l, name], ALL_PEERS, after) for l in range(DEPTH)]
        out[name] = sum_adamw("adamw_" + name, parts, w[name], m[name], v[name], 128)
        after = out[name][0]
    small_parts = exchange_wait("scatter_wait_small", small_handle, ALL_PEERS, after)
    rep_parts = exchange_wait("gather_wait_replicated", rep_handle, ALL_PEERS, after)
    small_res = sum_adamw("adamw_small", [small_parts], small_w[None], small_m[None], small_v[None], small_w.shape[0])
    n_small = DEPTH * (2 + CONV_WIDTH)
    small_res = [r[0, :n_small].reshape(DEPTH, 2 + CONV_WIDTH, shard_cols) for r in small_res]
    out["b_gate"] = [r[:, :2] for r in small_res]
    out["conv_w"] = [r[:, 2:] for r in small_res]
    rep_res = sum_adamw("adamw_replicated", [rep_parts], *(_pack_rows([t[name] for name in REPLICATED])[None] for t in (w, m, v)), 672)
    rep_res = [_unpack_rows(r[0], rep_shapes) for r in rep_res]
    for i, name in enumerate(REPLICATED):
        out[name] = [r[i] for r in rep_res]

    return (loss, grad_x[None], *[out[name][0] for name in names], *[out[name][1] for name in names],
            *[out[name][2] for name in names], *[out[name][3] for name in names])
```

```python
import functools
import math

import jax
import jax.numpy as jnp
from jax import lax
from jax.experimental import pallas as pl
from jax.experimental.pallas import tpu as pltpu

F32 = jnp.float32
BF16 = jnp.bfloat16

DEPTH = 4
N_DEV = 8
EPS = 1e-6
CONV_WIDTH = 31
CONV_PAD = 32
CHUNK = 128
SGU_GROUPS = 8

ADAM_LR = 0.001
ADAM_B1 = 0.9
ADAM_B2 = 0.999
ADAM_EPS = 1e-08
ADAM_WD = 0.01
ADAM_STEP = 10

VMEM_BYTES_V7X = 64 * 1024 * 1024
VMEM_COMPILER_SLACK = 12 * 1024 * 1024
MESH_AXES = ("x", "y", "c")
ANY = pl.BlockSpec(memory_space=pl.ANY)


def _nbytes(shape, dtype):
    return math.prod(shape) * jnp.dtype(dtype).itemsize


def _params(block_bytes, ngrid, single_bytes=0):
    limit = min(2 * block_bytes + single_bytes + VMEM_COMPILER_SLACK, VMEM_BYTES_V7X - 4 * 1024 * 1024)
    return pltpu.CompilerParams(dimension_semantics=("arbitrary",) * ngrid, vmem_limit_bytes=int(limit))


def _deps(deps):
    return [t for t in deps if t is not None]


def _mm_body(dims, nk, kaxis, ndeps):
    def body(a_ref, b_ref, *rest):
        o_ref, *acc = rest[ndeps:]

        def prod():
            return lax.dot_general(a_ref[...], b_ref[...], (dims, ((), ())), preferred_element_type=F32)

        if nk == 1:
            o_ref[...] = prod().astype(o_ref.dtype)
            return
        acc_ref, = acc
        k = pl.program_id(kaxis)
        p = prod()

        @pl.when(k == 0)
        def _():
            acc_ref[...] = p

        @pl.when(k > 0)
        def _():
            acc_ref[...] += p

        @pl.when(k == nk - 1)
        def _():
            o_ref[...] = acc_ref[...].astype(o_ref.dtype)

    return body


def _mm_call(name, a, b, dims, grid, a_spec, b_spec, o_spec, out_shape, out_dtype, nk, kaxis, acc_shape, deps=()):
    deps = _deps(deps)
    blocks = (_nbytes([d for d in a_spec.block_shape if d], a.dtype) + _nbytes([d for d in b_spec.block_shape if d], b.dtype)
              + _nbytes([d for d in o_spec.block_shape if d], out_dtype))
    scratch = [pltpu.VMEM(acc_shape, F32)] if nk > 1 else []
    acc_bytes = _nbytes(acc_shape, F32) * (2 if nk > 1 else 1)
    return pl.pallas_call(
        _mm_body(dims, nk, kaxis, len(deps)), name=name, grid=grid, in_specs=[a_spec, b_spec] + [ANY] * len(deps),
        out_specs=o_spec, out_shape=jax.ShapeDtypeStruct(out_shape, out_dtype), scratch_shapes=scratch,
        compiler_params=_params(blocks, len(grid), acc_bytes),
    )(_in_hbm(a), _in_hbm(b), *deps)


def mm_nn(name, a, b_and_token, out_dtype, tm, tn=None, tk=None):
    b, token = b_and_token
    m, k = a.shape
    tm = min(tm, m)
    if b.ndim == 3:
        nblk, _, nb = b.shape
        return _mm_call(name, a, b, ((1,), (0,)), (nblk, m // tm),
                        pl.BlockSpec((tm, k), lambda j, i: (i, 0)), pl.BlockSpec((None, k, nb), lambda j, i: (j, 0, 0)),
                        pl.BlockSpec((tm, nb), lambda j, i: (i, j)), (m, nblk * nb), out_dtype, 1, 0, (tm, nb), [token])
    n = b.shape[1]
    tn = tn or n
    tk = tk or k
    nk = k // tk
    return _mm_call(name, a, b, ((1,), (0,)), (n // tn, m // tm, nk),
                    pl.BlockSpec((tm, tk), lambda j, i, kk: (i, kk)), pl.BlockSpec((tk, tn), lambda j, i, kk: (kk, j)),
                    pl.BlockSpec((tm, tn), lambda j, i, kk: (i, j)), (m, n), out_dtype, nk, 2, (tm, tn), [token])


def mm_nt(name, a, b, out_dtype, tm, tn=None, deps=()):
    m = a.shape[0]
    tm = min(tm, m)
    if b.ndim == 3:
        kblk, n, kb = b.shape
        return _mm_call(name, a, b, ((1,), (1,)), (m // tm, kblk),
                        pl.BlockSpec((tm, kb), lambda i, kk: (i, kk)), pl.BlockSpec((None, n, kb), lambda i, kk: (kk, 0, 0)),
                        pl.BlockSpec((tm, n), lambda i, kk: (i, 0)), (m, n), out_dtype, kblk, 1, (tm, n), deps)
    n, kc = b.shape
    tn = tn or n
    return _mm_call(name, a, b, ((1,), (1,)), (n // tn, m // tm),
                    pl.BlockSpec((tm, kc), lambda j, i: (i, 0)), pl.BlockSpec((tn, kc), lambda j, i: (j, 0)),
                    pl.BlockSpec((tm, tn), lambda j, i: (i, j)), (m, n), out_dtype, 1, 0, (tm, tn), deps)


def mm_tn(name, a, b, out_dtype, tm, tr, nb=None):
    m, k = a.shape
    n = b.shape[1]
    tm = min(tm, m)
    nm = m // tm
    if nb is not None:
        return _mm_call(name, a, b, ((0,), (0,)), (n // nb, k // tr, nm),
                        pl.BlockSpec((tm, tr), lambda j, r, mm: (mm, r)), pl.BlockSpec((tm, nb), lambda j, r, mm: (mm, j)),
                        pl.BlockSpec((None, tr, nb), lambda j, r, mm: (j, r, 0)), (n // nb, k, nb), out_dtype, nm, 2, (tr, nb))
    return _mm_call(name, a, b, ((0,), (0,)), (k // tr, nm),
                    pl.BlockSpec((tm, tr), lambda r, mm: (mm, r)), pl.BlockSpec((tm, n), lambda r, mm: (mm, 0)),
                    pl.BlockSpec((tr, n), lambda r, mm: (r, 0)), (k, n), out_dtype, nm, 1, (tr, n))


def _row_call(name, body, grid, in_specs, out_specs, out_shape, arrays, scratch=(), aliases=None, vmem_blocks=0, deps=()):
    deps = _deps(deps)
    nin = len(arrays)

    def with_deps(*refs):
        body(*refs[:nin], *refs[nin + len(deps):])

    return pl.pallas_call(
        with_deps, name=name, grid=grid, in_specs=list(in_specs) + [ANY] * len(deps), out_specs=out_specs,
        out_shape=out_shape, scratch_shapes=list(scratch), input_output_aliases=aliases or {},
        compiler_params=_params(vmem_blocks, len(grid)),
    )(*[_in_hbm(a) for a in arrays], *deps)


def _rows(tm, d, col=0):
    return pl.BlockSpec((tm, d), lambda i, *_: (i, col))


def _vec(d):
    return pl.BlockSpec((1, d), lambda *_: (0, 0))


def _rstd(x):
    return lax.rsqrt(jnp.mean(x * x, axis=-1, keepdims=True) + EPS)


def _rms_bwd(dy, x, g):
    r = _rstd(x)
    n = x * r
    w = dy * g
    dx = r * (w - n * jnp.mean(w * n, axis=-1, keepdims=True))
    return dx, jnp.sum(dy * n, axis=0, keepdims=True)


def _accumulate(ref, value, first):
    @pl.when(first)
    def _():
        ref[...] = value

    @pl.when(jnp.logical_not(first))
    def _():
        ref[...] += value


def rms_fwd(x, g, tm=256, deps=()):
    s, d = x.shape
    tm = min(tm, s)

    def body(x_ref, g_ref, h_ref):
        xv = x_ref[...]
        h_ref[...] = (xv * _rstd(xv) * g_ref[...]).astype(BF16)

    return _row_call("rms_fwd", body, (s // tm,), [_rows(tm, d), _vec(d)], _rows(tm, d),
                     jax.ShapeDtypeStruct((s, d), BF16), (x, g), vmem_blocks=tm * d * 6, deps=deps)


def norm_res(x_in, o, g_post, g_next, tm=256):
    s, d = x_in.shape
    tm = min(tm, s)

    def body(x_ref, o_ref, gp_ref, gn_ref, xo_ref, h_ref):
        ov = o_ref[...]
        xo = x_ref[...] + (ov * _rstd(ov) * gp_ref[...])
        xo_ref[...] = xo
        h_ref[...] = (xo * _rstd(xo) * gn_ref[...]).astype(BF16)

    return _row_call("norm_res", body, (s // tm,), [_rows(tm, d), _rows(tm, d), _vec(d), _vec(d)],
                     [_rows(tm, d), _rows(tm, d)],
                     [jax.ShapeDtypeStruct((s, d), F32), jax.ShapeDtypeStruct((s, d), BF16)],
                     (x_in, o, g_post, g_next), vmem_blocks=tm * d * 14)


def final_norm_loss(x_in, o, g_post, target, tm=256):
    s, d = x_in.shape
    tm = min(tm, s)

    def body(x_ref, o_ref, gp_ref, t_ref, loss_ref, dy_ref, do_ref, dg_ref):
        first = pl.program_id(0) == 0
        ov = o_ref[...]
        g = gp_ref[...]
        diff = x_ref[...] + (ov * _rstd(ov) * g) - t_ref[...]
        _accumulate(loss_ref, jnp.sum(diff * diff, axis=0, keepdims=True), first)
        dy = diff * (1.0 / d)
        dy_ref[...] = dy
        do, dg = _rms_bwd(dy, ov, g)
        do_ref[...] = do.astype(BF16)
        _accumulate(dg_ref, dg, first)

    return _row_call("final_norm_loss", body, (s // tm,), [_rows(tm, d), _rows(tm, d), _vec(d), _rows(tm, d)],
                     [_vec(d), _rows(tm, d), _rows(tm, d), _vec(d)],
                     [jax.ShapeDtypeStruct((1, d), F32), jax.ShapeDtypeStruct((s, d), F32),
                      jax.ShapeDtypeStruct((s, d), BF16), jax.ShapeDtypeStruct((1, d), F32)],
                     (x_in, o, g_post, target), vmem_blocks=tm * d * 18)


def norm_bwd_in_out(dx_out, dh, x_in, g_pre, o_below, g_post_below, tm=256, deps=()):
    s, d = x_in.shape
    tm = min(tm, s)

    def body(dxo_ref, dh_ref, x_ref, g_ref, o_ref, gb_ref, dxi_ref, do_ref, dg_ref, dgb_ref):
        first = pl.program_id(0) == 0
        dx, dg = _rms_bwd(dh_ref[...], x_ref[...], g_ref[...])
        dxi = dxo_ref[...] + dx
        dxi_ref[...] = dxi
        _accumulate(dg_ref, dg, first)
        do, dgb = _rms_bwd(dxi, o_ref[...], gb_ref[...])
        do_ref[...] = do.astype(BF16)
        _accumulate(dgb_ref, dgb, first)

    return _row_call("norm_bwd_in_out", body, (s // tm,),
                     [_rows(tm, d), _rows(tm, d), _rows(tm, d), _vec(d), _rows(tm, d), _vec(d)],
                     [_rows(tm, d), _rows(tm, d), _vec(d), _vec(d)],
                     [jax.ShapeDtypeStruct((s, d), F32), jax.ShapeDtypeStruct((s, d), BF16),
                      jax.ShapeDtypeStruct((1, d), F32), jax.ShapeDtypeStruct((1, d), F32)],
                     (dx_out, dh, x_in, g_pre, o_below, g_post_below), vmem_blocks=tm * d * 22, deps=deps)


def norm_bwd_in(dx_out, dh, x_in, g_pre, tm=256):
    s, d = x_in.shape
    tm = min(tm, s)

    def body(dxo_ref, dh_ref, x_ref, g_ref, dxi_ref, dg_ref):
        dx, dg = _rms_bwd(dh_ref[...], x_ref[...], g_ref[...])
        dxi_ref[...] = dxo_ref[...] + dx
        _accumulate(dg_ref, dg, pl.program_id(0) == 0)

    return _row_call("norm_bwd_in", body, (s // tm,), [_rows(tm, d), _rows(tm, d), _rows(tm, d), _vec(d)],
                     [_rows(tm, d), _vec(d)],
                     [jax.ShapeDtypeStruct((s, d), F32), jax.ShapeDtypeStruct((1, d), F32)],
                     (dx_out, dh, x_in, g_pre), vmem_blocks=tm * d * 16)


CONV_COLS = 256
CONV_ROWS = 32


SUBLANES = 8


def _shifted_copies(ext_ref, sh_ref):
    n = sh_ref.shape[1]
    for r in range(SUBLANES):
        sh_ref[r] = ext_ref[r:r + n, :]


def _window(sh_ref, off):
    return sh_ref[off % SUBLANES, off - off % SUBLANES:off - off % SUBLANES + CONV_ROWS, :]


def _depthwise(sh_ref, w, n_out, in_off, flip, emit):
    for r0 in range(0, n_out, CONV_ROWS):
        acc = None
        for k in range(CONV_WIDTH):
            term = w[k:k + 1, :] * _window(sh_ref, r0 + in_off + (CONV_WIDTH - 1 - k if flip else k))
            acc = term if acc is None else acc + term
        emit(r0, acc)


def _ext_scratch(tm):
    return [pltpu.VMEM((tm + CONV_PAD + SUBLANES, CONV_COLS), F32), pltpu.VMEM((SUBLANES, tm + CONV_PAD, CONV_COLS), F32)]


def glu_conv_fwd(proj, conv_w, conv_b, c_ch, tm=512):
    s = proj.shape[0]
    tm = min(tm, s)
    ncb = c_ch // CONV_COLS
    hb = tm // CONV_PAD

    def body(a_ref, g_ref, ah_ref, gh_ref, w_ref, b_ref, c_ref, ext_ref, sh_ref):
        i = pl.program_id(1)
        halo = ah_ref[...].astype(F32) * jax.nn.sigmoid(gh_ref[...].astype(F32))
        ext_ref[0:CONV_PAD, :] = jnp.where(i > 0, halo, 0.0)
        ext_ref[CONV_PAD:CONV_PAD + tm, :] = a_ref[...].astype(F32) * jax.nn.sigmoid(g_ref[...].astype(F32))
        ext_ref[CONV_PAD + tm:, :] = jnp.zeros((SUBLANES, CONV_COLS), F32)
        _shifted_copies(ext_ref, sh_ref)
        w = w_ref[...]
        bias = b_ref[...]

        def emit(r0, acc):
            c_ref[r0:r0 + CONV_ROWS, :] = acc + bias

        _depthwise(sh_ref, w, tm, CONV_PAD - (CONV_WIDTH - 1), False, emit)

    main = lambda col0: pl.BlockSpec((tm, CONV_COLS), lambda c, i: (i, col0 + c))
    halo = lambda col0: pl.BlockSpec((CONV_PAD, CONV_COLS), lambda c, i: (jnp.maximum(i * hb - 1, 0), col0 + c))
    return _row_call("glu_conv_fwd", body, (ncb, s // tm),
                     [main(0), main(ncb), halo(0), halo(ncb),
                      pl.BlockSpec((CONV_PAD, CONV_COLS), lambda c, i: (0, c)), pl.BlockSpec((1, CONV_COLS), lambda c, i: (0, c))],
                     pl.BlockSpec((tm, CONV_COLS), lambda c, i: (i, c)), jax.ShapeDtypeStruct((s, c_ch), F32),
                     (proj, proj, proj, proj, conv_w, conv_b),
                     scratch=_ext_scratch(tm), vmem_blocks=tm * CONV_COLS * 32)


def _layer_norm_stats(x):
    mu = jnp.mean(x, axis=-1, keepdims=True)
    xc = x - mu
    rstd = lax.rsqrt(jnp.mean(xc * xc, axis=-1, keepdims=True) + EPS)
    return xc * rstd, rstd


def _layer_norm_bwd(dy, xhat, rstd, g):
    dxh = dy * g
    return rstd * (dxh - jnp.mean(dxh, axis=-1, keepdims=True) - xhat * jnp.mean(dxh * xhat, axis=-1, keepdims=True))


def ln_silu_fwd(c, ln_g, ln_b, tm=256):
    s, d = c.shape
    tm = min(tm, s)

    def body(c_ref, g_ref, b_ref, s_ref):
        xhat, _ = _layer_norm_stats(c_ref[...])
        s_ref[...] = jax.nn.silu(xhat * g_ref[...] + b_ref[...]).astype(BF16)

    return _row_call("ln_silu_fwd", body, (s // tm,), [_rows(tm, d), _vec(d), _vec(d)], _rows(tm, d),
                     jax.ShapeDtypeStruct((s, d), BF16), (c, ln_g, ln_b), vmem_blocks=tm * d * 10)


def ln_silu_bwd(c, ds, ln_g, ln_b, tm=256, deps=()):
    s, d = c.shape
    tm = min(tm, s)

    def body(c_ref, ds_ref, g_ref, b_ref, dc_ref, dg_ref, db_ref, dcb_ref):
        first = pl.program_id(0) == 0
        g = g_ref[...]
        xhat, rstd = _layer_norm_stats(c_ref[...])
        y = xhat * g + b_ref[...]
        sg = jax.nn.sigmoid(y)
        dln = ds_ref[...] * (sg * (1.0 + y * (1.0 - sg)))
        _accumulate(dg_ref, jnp.sum(dln * xhat, axis=0, keepdims=True), first)
        _accumulate(db_ref, jnp.sum(dln, axis=0, keepdims=True), first)
        dc = _layer_norm_bwd(dln, xhat, rstd, g)
        dc_ref[...] = dc
        _accumulate(dcb_ref, jnp.sum(dc, axis=0, keepdims=True), first)

    vec = jax.ShapeDtypeStruct((1, d), F32)
    return _row_call("ln_silu_bwd", body, (s // tm,), [_rows(tm, d), _rows(tm, d), _vec(d), _vec(d)],
                     [_rows(tm, d), _vec(d), _vec(d), _vec(d)], [jax.ShapeDtypeStruct((s, d), F32), vec, vec, vec],
                     (c, ds, ln_g, ln_b), vmem_blocks=tm * d * 20, deps=deps)


def conv_bwd(proj, dc, conv_w, c_ch, tm=512):
    s = proj.shape[0]
    tm = min(tm, s)
    ncb = c_ch // CONV_COLS
    hb = tm // CONV_PAD
    last_halo = s // CONV_PAD - 1
    n_i = s // tm

    def body(a_ref, g_ref, ah_ref, gh_ref, dc_ref, dcn_ref, w_ref, dglu_ref, dw_ref,
             ext_ref, sh_ref, dce_ref, dsh_ref, dwacc_ref):
        i = pl.program_id(1)
        zeros = jnp.zeros((SUBLANES, CONV_COLS), F32)
        halo = ah_ref[...].astype(F32) * jax.nn.sigmoid(gh_ref[...].astype(F32))
        ext_ref[0:CONV_PAD, :] = jnp.where(i > 0, halo, 0.0)
        ext_ref[CONV_PAD:CONV_PAD + tm, :] = a_ref[...].astype(F32) * jax.nn.sigmoid(g_ref[...].astype(F32))
        ext_ref[CONV_PAD + tm:, :] = zeros
        _shifted_copies(ext_ref, sh_ref)
        dce_ref[0:tm, :] = dc_ref[...]
        dce_ref[tm:tm + CONV_PAD, :] = jnp.where(i < n_i - 1, dcn_ref[...], 0.0)
        dce_ref[tm + CONV_PAD:, :] = zeros
        _shifted_copies(dce_ref, dsh_ref)
        w = w_ref[...]

        def emit(r0, acc):
            dglu_ref[r0:r0 + CONV_ROWS, :] = acc

        _depthwise(dsh_ref, w, tm, 0, True, emit)

        for k in range(CONV_WIDTH):
            acc = None
            for r0 in range(0, tm, CONV_ROWS):
                term = dce_ref[r0:r0 + CONV_ROWS, :] * _window(sh_ref, r0 + CONV_PAD - (CONV_WIDTH - 1) + k)
                acc = term if acc is None else acc + term
            dwacc_ref[k:k + 1, :] = jnp.sum(acc, axis=0, keepdims=True)
        dwacc_ref[CONV_WIDTH:, :] = jnp.zeros((CONV_PAD - CONV_WIDTH, CONV_COLS), F32)
        _accumulate(dw_ref, dwacc_ref[...], i == 0)

    main = lambda col0: pl.BlockSpec((tm, CONV_COLS), lambda c, i: (i, col0 + c))
    halo = lambda col0: pl.BlockSpec((CONV_PAD, CONV_COLS), lambda c, i: (jnp.maximum(i * hb - 1, 0), col0 + c))
    nxt = pl.BlockSpec((CONV_PAD, CONV_COLS), lambda c, i: (jnp.minimum((i + 1) * hb, last_halo), c))
    wspec = pl.BlockSpec((CONV_PAD, CONV_COLS), lambda c, i: (0, c))
    return _row_call("conv_bwd", body, (ncb, n_i),
                     [main(0), main(ncb), halo(0), halo(ncb), main(0), nxt, wspec],
                     [main(0), wspec],
                     [jax.ShapeDtypeStruct((s, c_ch), F32), jax.ShapeDtypeStruct((CONV_PAD, c_ch), F32)],
                     (proj, proj, proj, proj, dc, dc, conv_w),
                     scratch=_ext_scratch(tm) + _ext_scratch(tm) + [pltpu.VMEM((CONV_PAD, CONV_COLS), F32)],
                     vmem_blocks=tm * CONV_COLS * 56)


ELEMENTWISE_COLS = 512


def _col_chunks(d):
    return [slice(c0, c0 + ELEMENTWISE_COLS) for c0 in range(0, d, ELEMENTWISE_COLS)]


def _pair_spec(tm, d, pair):
    return pl.BlockSpec((tm, 2 * d), lambda i: (i, pair))


def glu_bwd(dproj, dglu, proj, c_ch, tm=256):
    s = proj.shape[0]
    tm = min(tm, s)

    def body(_, dglu_ref, a_ref, g_ref, out_ref):
        for cols in _col_chunks(c_ch):
            dg = dglu_ref[:, cols]
            sg = jax.nn.sigmoid(g_ref[:, cols].astype(F32))
            out_ref[:, cols] = (dg * sg).astype(BF16)
            out_ref[:, c_ch + cols.start:c_ch + cols.stop] = (dg * a_ref[:, cols].astype(F32) * (sg * (1.0 - sg))).astype(BF16)

    return _row_call("glu_bwd", body, (s // tm,), [ANY, _rows(tm, c_ch), _rows(tm, c_ch), _rows(tm, c_ch, 1)],
                     _pair_spec(tm, c_ch, 0), jax.ShapeDtypeStruct(dproj.shape, BF16), (dproj, dglu, proj, proj),
                     aliases={0: 0}, vmem_blocks=tm * c_ch * 12)


_SQRT_HALF = 0.7071067811865476
_INV_SQRT_2PI = 0.3989422804014327


def _gelu_parts(x):
    cdf = 0.5 * (1.0 + lax.erf(x * _SQRT_HALF))
    return cdf, x * cdf


def _gelu_grad(x, cdf):
    return cdf + x * (_INV_SQRT_2PI * jnp.exp(-0.5 * x * x))


def _sgu_specs(tm, ch):
    grp = ch // SGU_GROUPS
    full3 = lambda shape: pl.BlockSpec(shape, lambda *_: (0, 0, 0))
    return grp, full3((SGU_GROUPS, CHUNK, CHUNK)), full3((SGU_GROUPS, CHUNK, grp))


def sgu_fwd(proj, ln_g, ln_b, w_mix, b_mix, ch, col0, tm=CHUNK):
    s = proj.shape[0]
    grp, wspec, bspec = _sgu_specs(tm, ch)

    def body(u_ref, v_ref, g_ref, b_ref, w_ref, bm_ref, p_ref, mix_ref):
        _, u = _gelu_parts(u_ref[...].astype(F32))
        _, v0 = _gelu_parts(v_ref[...].astype(F32))
        xhat, _ = _layer_norm_stats(v0)
        vn = (xhat * g_ref[...] + b_ref[...]).astype(BF16)
        for n in range(tm // CHUNK):
            for g in range(SGU_GROUPS):
                blk = vn[n * CHUNK:(n + 1) * CHUNK, g * grp:(g + 1) * grp]
                mix_ref[n * CHUNK:(n + 1) * CHUNK, g * grp:(g + 1) * grp] = (
                    jnp.dot(w_ref[g], blk, preferred_element_type=F32) + bm_ref[g])
        p_ref[...] = (u * mix_ref[...]).astype(BF16)

    return _row_call("sgu_fwd", body, (s // tm,),
                     [_rows(tm, ch, col0), _rows(tm, ch, col0 + 1), _vec(ch), _vec(ch), wspec, bspec], _rows(tm, ch),
                     jax.ShapeDtypeStruct((s, ch), BF16), (proj, proj, ln_g, ln_b, w_mix, b_mix),
                     scratch=[pltpu.VMEM((tm, ch), F32)], vmem_blocks=tm * ch * 30)


def sgu_bwd(dproj, proj, dp, ln_g, ln_b, w_mix, w_mix_t, b_mix, ch, col0, tm=CHUNK):
    s = proj.shape[0]
    assert tm == CHUNK and col0 % 2 == 0
    grp, wspec, bspec = _sgu_specs(tm, ch)
    groups = [slice(k * grp, (k + 1) * grp) for k in range(SGU_GROUPS)]

    def body(_, u_ref, v_ref, dp_ref, g_ref, b_ref, w_ref, wt_ref, bm_ref,
             out_ref, dw_ref, dbm_ref, dg_ref, db_ref, u_s, gu_s, gv_s, xh_s, dvn_s):
        first = pl.program_id(0) == 0
        row_sum = lambda x: jnp.sum(x, axis=1, keepdims=True)
        total = None
        for cols in groups:
            ub = u_ref[:, cols].astype(F32)
            vb = v_ref[:, cols].astype(F32)
            cdf_u, u = _gelu_parts(ub)
            cdf_v, v0 = _gelu_parts(vb)
            u_s[:, cols] = u
            gu_s[:, cols] = _gelu_grad(ub, cdf_u)
            gv_s[:, cols] = _gelu_grad(vb, cdf_v)
            xh_s[:, cols] = v0
            total = row_sum(v0) if total is None else total + row_sum(v0)
        mu = total * (1.0 / ch)
        total = None
        for cols in groups:
            xc = xh_s[:, cols] - mu
            xh_s[:, cols] = xc
            total = row_sum(xc * xc) if total is None else total + row_sum(xc * xc)
        rstd = lax.rsqrt(total * (1.0 / ch) + EPS)

        t1 = t2 = None
        for k, cols in enumerate(groups):
            g = g_ref[:, cols]
            xhat = xh_s[:, cols] * rstd
            xh_s[:, cols] = xhat
            vn = (xhat * g + b_ref[:, cols]).astype(BF16)
            dpk = dp_ref[:, cols]
            dmix = dpk * u_s[:, cols]
            dmix_bf = dmix.astype(BF16)
            mixed = jnp.dot(w_ref[k], vn, preferred_element_type=F32) + bm_ref[k]
            out_ref[:, cols] = (dpk * mixed * gu_s[:, cols]).astype(BF16)
            dvn = jnp.dot(wt_ref[k], dmix_bf, preferred_element_type=F32)
            dvn_s[:, cols] = dvn
            _accumulate(dw_ref.at[k], lax.dot_general(dmix_bf, vn, (((1,), (1,)), ((), ())), preferred_element_type=F32), first)
            _accumulate(dbm_ref.at[k], jnp.broadcast_to(row_sum(dmix), (CHUNK, CHUNK)), first)
            _accumulate(dg_ref.at[:, cols], jnp.sum(dvn * xhat, axis=0, keepdims=True), first)
            _accumulate(db_ref.at[:, cols], jnp.sum(dvn, axis=0, keepdims=True), first)
            dxh = dvn * g
            t1 = row_sum(dxh) if t1 is None else t1 + row_sum(dxh)
            t2 = row_sum(dxh * xhat) if t2 is None else t2 + row_sum(dxh * xhat)
        m1 = t1 * (1.0 / ch)
        m2 = t2 * (1.0 / ch)
        for cols in groups:
            dv0 = rstd * (dvn_s[:, cols] * g_ref[:, cols] - m1 - xh_s[:, cols] * m2)
            out_ref[:, ch + cols.start:ch + cols.stop] = (dv0 * gv_s[:, cols]).astype(BF16)

    vec = _vec(ch)
    acc3 = lambda: pl.BlockSpec((SGU_GROUPS, CHUNK, CHUNK), lambda i: (0, 0, 0))
    vshape = jax.ShapeDtypeStruct((1, ch), F32)
    mshape = jax.ShapeDtypeStruct((SGU_GROUPS, CHUNK, CHUNK), F32)
    return _row_call("sgu_bwd", body, (s // tm,),
                     [ANY, _rows(tm, ch, col0), _rows(tm, ch, col0 + 1), _rows(tm, ch), vec, vec, wspec, wspec, bspec],
                     [_pair_spec(tm, ch, col0 // 2), acc3(), acc3(), vec, vec],
                     [jax.ShapeDtypeStruct(dproj.shape, BF16), mshape, mshape, vshape, vshape],
                     (dproj, proj, proj, dp, ln_g, ln_b, w_mix, w_mix_t, b_mix),
                     scratch=[pltpu.VMEM((tm, ch), F32)] * 5, aliases={0: 0}, vmem_blocks=tm * ch * 40)


def merge_fwd(y_a, y_b, proj, b_gate, col0, tm=256):
    s, d = y_a.shape
    tm = min(tm, s)

    def body(ya_ref, yb_ref, l0_ref, l1_ref, bg_ref, m_ref):
        g0 = jax.nn.sigmoid(l0_ref[...].astype(F32) + bg_ref[0:1, :])
        g1 = jax.nn.sigmoid(l1_ref[...].astype(F32) + bg_ref[1:2, :])
        m_ref[...] = (g0 * ya_ref[...] + g1 * yb_ref[...]).astype(BF16)

    return _row_call("merge_fwd", body, (s // tm,),
                     [_rows(tm, d), _rows(tm, d), _rows(tm, d, col0), _rows(tm, d, col0 + 1), pl.BlockSpec((2, d), lambda i: (0, 0))],
                     _rows(tm, d), jax.ShapeDtypeStruct((s, d), BF16), (y_a, y_b, proj, proj, b_gate), vmem_blocks=tm * d * 16)


def merge_bwd(dm, y_a, y_b, proj, b_gate, col0, tm=256, deps=()):
    s, d = y_a.shape
    tm = min(tm, s)

    assert col0 % 2 == 0

    def body(dm_ref, ya_ref, yb_ref, l0_ref, l1_ref, bg_ref, dya_ref, dyb_ref, out_ref, dbg_ref):
        first = pl.program_id(0) == 0
        for cols in _col_chunks(d):
            dmv = dm_ref[:, cols]
            g0 = jax.nn.sigmoid(l0_ref[:, cols].astype(F32) + bg_ref[0:1, cols])
            g1 = jax.nn.sigmoid(l1_ref[:, cols].astype(F32) + bg_ref[1:2, cols])
            dya_ref[:, cols] = (dmv * g0).astype(BF16)
            dyb_ref[:, cols] = (dmv * g1).astype(BF16)
            dl0 = dmv * ya_ref[:, cols] * (g0 * (1.0 - g0))
            dl1 = dmv * yb_ref[:, cols] * (g1 * (1.0 - g1))
            _accumulate(dbg_ref.at[0:1, cols], jnp.sum(dl0, axis=0, keepdims=True), first)
            _accumulate(dbg_ref.at[1:2, cols], jnp.sum(dl1, axis=0, keepdims=True), first)
            out_ref[:, cols] = dl0.astype(BF16)
            out_ref[:, d + cols.start:d + cols.stop] = dl1.astype(BF16)

    bgspec = pl.BlockSpec((2, d), lambda i: (0, 0))
    return _row_call("merge_bwd", body, (s // tm,),
                     [_rows(tm, d), _rows(tm, d), _rows(tm, d), _rows(tm, d, col0), _rows(tm, d, col0 + 1), bgspec],
                     [_rows(tm, d), _rows(tm, d), _pair_spec(tm, d, col0 // 2), bgspec],
                     [jax.ShapeDtypeStruct((s, d), BF16), jax.ShapeDtypeStruct((s, d), BF16),
                      jax.ShapeDtypeStruct(proj.shape, BF16), jax.ShapeDtypeStruct((2, d), F32)],
                     (dm, y_a, y_b, proj, proj, b_gate), vmem_blocks=tm * d * 24, deps=deps)


def swiglu_fwd(gu, tm=512):
    s, w2 = gu.shape
    tm = min(tm, s)
    nb = N_DEV // 2
    cb = w2 // N_DEV

    def body(g_ref, u_ref, f_ref):
        f_ref[...] = (jax.nn.silu(g_ref[...].astype(F32)) * u_ref[...].astype(F32)).astype(BF16)

    return _row_call("swiglu_fwd", body, (s // tm, nb),
                     [pl.BlockSpec((tm, cb), lambda i, j: (i, j)), pl.BlockSpec((tm, cb), lambda i, j: (i, j + nb))],
                     pl.BlockSpec((tm, cb), lambda i, j: (i, j)), jax.ShapeDtypeStruct((s, w2 // 2), BF16), (gu, gu),
                     vmem_blocks=tm * cb * 12)


def swiglu_bwd(gu, df, tm=256, deps=()):
    s, w2 = gu.shape
    tm = min(tm, s)
    half = w2 // 2
    chunk = w2 // N_DEV

    def body(g_ref, u_ref, df_ref, out_ref):
        for c0 in range(0, half, chunk):
            cols = slice(c0, c0 + chunk)
            g = g_ref[:, cols].astype(F32)
            sg = jax.nn.sigmoid(g)
            dfv = df_ref[:, cols].astype(F32)
            out_ref[:, cols] = (dfv * u_ref[:, cols].astype(F32) * (sg * (1.0 + g * (1.0 - sg)))).astype(BF16)
            out_ref[:, half + c0:half + c0 + chunk] = (dfv * (g * sg)).astype(BF16)

    return _row_call("swiglu_bwd", body, (s // tm,), [_rows(tm, half), _rows(tm, half, 1), _rows(tm, half)],
                     _rows(tm, w2), jax.ShapeDtypeStruct((s, w2), BF16), (gu, gu, df), vmem_blocks=tm * w2 * 5, deps=deps)


def _peers():
    x, y, c = lax.axis_index("x"), lax.axis_index("y"), lax.axis_index("c")
    me = 4 * x + 2 * y + c
    peers = []
    for k in range(1, N_DEV):
        px = 1 - x if k & 4 else x
        py = 1 - y if k & 2 else y
        pc = 1 - c if k & 1 else c
        peers.append(((px, py, pc), 4 * px + 2 * py + pc))
    return me, peers


def _exchange(name, arrays, scatter):
    n = len(arrays)

    def body(*refs):
        ins, outs = refs[:n], refs[n:2 * n]
        send_sems, recv_sems, local_sems = refs[2 * n:]
        me, peers = _peers()

        def remote(a, k):
            (pos, idx) = peers[k]
            src = ins[a].at[idx] if scatter else ins[a]
            return pltpu.make_async_remote_copy(src_ref=src, dst_ref=outs[a].at[me], send_sem=send_sems.at[a, k],
                                                recv_sem=recv_sems.at[a, k], device_id=pos, device_id_type=pl.DeviceIdType.MESH)

        def arrival(a, k):
            (pos, idx) = peers[k]
            src = ins[a].at[idx] if scatter else ins[a]
            return pltpu.make_async_remote_copy(src_ref=src, dst_ref=outs[a].at[idx], send_sem=send_sems.at[a, k],
                                                recv_sem=recv_sems.at[a, k], device_id=pos, device_id_type=pl.DeviceIdType.MESH)

        local = [pltpu.make_async_copy(ins[a].at[me] if scatter else ins[a], outs[a].at[me], local_sems.at[a]) for a in range(n)]
        sends = [remote(a, k) for k in range(N_DEV - 1) for a in range(n)]
        for cp in sends:
            cp.start()
        for cp in local:
            cp.start()
        for k in range(N_DEV - 1):
            for a in range(n):
                arrival(a, k).wait_recv()
        for cp in sends:
            cp.wait_send()
        for cp in local:
            cp.wait()

    out_shape = [jax.ShapeDtypeStruct(a.shape if scatter else (N_DEV,) + a.shape, a.dtype) for a in arrays]
    return pl.pallas_call(
        body, name=name, in_specs=[ANY] * n, out_specs=[ANY] * n, out_shape=out_shape,
        scratch_shapes=[pltpu.SemaphoreType.DMA((n, N_DEV - 1)), pltpu.SemaphoreType.DMA((n, N_DEV - 1)),
                        pltpu.SemaphoreType.DMA((n,))],
    )(*arrays)


HBM_SPEC = pl.BlockSpec(memory_space=pltpu.HBM)
SEM_SPEC = pl.BlockSpec(memory_space=pltpu.SEMAPHORE)
DATAFLOW_EFFECT = pltpu.SideEffectType.DATAFLOW_SIDE_EFFECTING
ALL_PEERS = (1, 2, 3, 4, 5, 6, 7)
SIBLING = 1
SAME_CORE_PEERS = (2, 4, 6)


def _in_hbm(a):
    return pltpu.with_memory_space_constraint(a, pltpu.HBM)


def fill_own_slot(name, me, src, block, dtype):
    _, r, c = src.shape
    tr = _row_tile(r, 256)

    def body(me_ref, src_ref, out_ref):
        out_ref[...] = src_ref[...].astype(dtype)

    if block is None:
        src_index = lambda i, me_ref: (me_ref[0], i, 0)
    else:
        src_index = lambda i, me_ref: (block, i, 0)
    grid_spec = pltpu.PrefetchScalarGridSpec(
        num_scalar_prefetch=1, grid=(r // tr,), in_specs=[pl.BlockSpec((None, tr, c), src_index)],
        out_specs=pl.BlockSpec((None, tr, c), lambda i, me_ref: (me_ref[0], i, 0)))
    return pl.pallas_call(body, name=name, grid_spec=grid_spec, out_shape=jax.ShapeDtypeStruct((N_DEV, r, c), dtype),
                          compiler_params=_params(tr * c * (src.dtype.itemsize + jnp.dtype(dtype).itemsize), 1))(me, src)


def _split_copy(src, land, send_sem, recv_sem, k, peers, me, arriving, forward):
    pos, idx = peers[k - 1]
    if forward:
        pos = peers[SIBLING - 1][0]
        slot = peers[(k | SIBLING) - 1][1] if arriving else idx
        src_ref, dst_ref = land.at[slot], land.at[slot]
    else:
        src_ref = land.at[me] if src is None else src.at[idx]
        dst_ref = land.at[idx if arriving else me]
    return pltpu.make_async_remote_copy(src_ref=src_ref, dst_ref=dst_ref, send_sem=send_sem, recv_sem=recv_sem,
                                        device_id=pos, device_id_type=pl.DeviceIdType.MESH)


def exchange_start(name, lands, peer_ks, srcs=None, after=None, forward=False):
    n = len(lands)
    ns = n if srcs is not None else 0
    extra = _deps([after])
    bufs = (list(srcs) if srcs is not None else []) + list(lands)

    def body(*refs):
        src, land = refs[:ns], refs[ns:ns + n]
        outs = refs[ns + n + len(extra):]
        send_sems, recv_sems, token = outs[:n], outs[n:2 * n], outs[2 * n + ns + n]
        me, peers = _peers()
        for a in range(n):
            for j, k in enumerate(peer_ks):
                _split_copy(src[a] if ns else None, land[a], send_sems[a].at[j], recv_sems[a].at[j], k, peers, me,
                            False, forward).start()
        token[...] = jnp.zeros_like(token)

    sems = [pltpu.SemaphoreType.DMA((len(peer_ks),))] * (2 * n)
    res = pl.pallas_call(
        body, name=name, in_specs=[HBM_SPEC] * len(bufs) + [ANY] * len(extra),
        out_specs=[SEM_SPEC] * (2 * n) + [HBM_SPEC] * len(bufs) + [pl.BlockSpec(memory_space=pltpu.VMEM)],
        out_shape=sems + [pltpu.HBM(a.shape, a.dtype) for a in bufs] + [jax.ShapeDtypeStruct((8, 128), F32)],
        input_output_aliases={i: 2 * n + i for i in range(len(bufs))},
        compiler_params=pltpu.CompilerParams(has_side_effects=DATAFLOW_EFFECT),
    )(*[_in_hbm(a) for a in bufs], *extra)
    handles = [(res[a], res[n + a], res[2 * n + a] if ns else None, res[2 * n + ns + a]) for a in range(n)]
    return handles, res[2 * n + ns + n]


def exchange_wait(name, handle, peer_ks, after, forward=False):
    send_sem, recv_sem, src, land = handle
    bufs = ([src] if src is not None else []) + [land]
    nb = len(bufs)

    def body(*refs):
        src_ref = refs[0] if nb == 2 else None
        land_ref, send_ref, recv_ref = refs[nb - 1], refs[nb], refs[nb + 1]
        me, peers = _peers()
        for j, k in enumerate(peer_ks):
            cp = _split_copy(src_ref, land_ref, send_ref.at[j], recv_ref.at[j], k, peers, me, True, forward)
            cp.wait_send()
            cp.wait_recv()

    return pl.pallas_call(
        body, name=name, in_specs=[HBM_SPEC] * nb + [SEM_SPEC, SEM_SPEC, ANY], out_specs=[HBM_SPEC] * nb,
        out_shape=[pltpu.HBM(a.shape, a.dtype) for a in bufs],
        input_output_aliases={i: i for i in range(nb)}, compiler_params=pltpu.CompilerParams(has_side_effects=DATAFLOW_EFFECT),
    )(*bufs, send_sem, recv_sem, after)[nb - 1]


def _row_tile(r, cap):
    if r <= cap:
        return r
    return max(t for t in range(16, cap + 1, 16) if r % t == 0)


def sum_adamw(name, parts, w, m, v, tr):
    nl, r, c = w.shape
    tr = _row_tile(r, tr)
    c1 = 1.0 - ADAM_B1 ** ADAM_STEP
    c2 = 1.0 - ADAM_B2 ** ADAM_STEP

    def body(*refs):
        part_refs = refs[:nl]
        w_ref, m_ref, v_ref, g_out, d_out, m_out, v_out = refs[nl:]
        layer = pl.program_id(0)
        for j in range(nl):
            @pl.when(layer == j)
            def _(j=j):
                g = part_refs[j][0].astype(F32)
                for p in range(1, N_DEV):
                    g = g + part_refs[j][p].astype(F32)
                mn = ADAM_B1 * m_ref[...] + (1.0 - ADAM_B1) * g
                vn = ADAM_B2 * v_ref[...] + (1.0 - ADAM_B2) * (g * g)
                g_out[...] = g
                m_out[...] = mn
                v_out[...] = vn
                d_out[...] = -ADAM_LR * ((mn / c1) / (jnp.sqrt(vn / c2) + ADAM_EPS) + ADAM_WD * w_ref[...])

    def part_spec(j):
        return pl.BlockSpec((N_DEV, tr, c), lambda l, i: (0, jnp.where(l == j, i, 0), 0))

    lspec = pl.BlockSpec((None, tr, c), lambda l, i: (l, i, 0))
    out = jax.ShapeDtypeStruct((nl, r, c), F32)
    return _row_call(name, body, (nl, r // tr), [part_spec(j) for j in range(nl)] + [lspec] * 3, [lspec] * 4, [out] * 4,
                     tuple(parts) + (w, m, v), vmem_blocks=nl * N_DEV * tr * c * parts[0].dtype.itemsize + 7 * tr * c * 4)


REPLICATED = ("norm_mix_pre", "norm_mix_post", "norm_ffn_pre", "norm_ffn_post", "conv_b", "conv_ln_g", "conv_ln_b",
              "sgu_ln_g", "sgu_ln_b", "w_spatial", "b_spatial")
MATRICES = ("w_in", "w_a_out", "w_b_out", "w_o", "w_gate_up", "w_down")


def local_step(x, target, rep, weight, emit, b_gate, conv_w, start_token=None):
    s, d = x.shape
    causal = jnp.tril(jnp.ones((CHUNK, CHUNK), dtype=bool))
    row = lambda name, l: rep[name][l].reshape(1, -1)

    saved = []
    h = rms_fwd(x, row("norm_mix_pre", 0), deps=[start_token])
    for l in range(DEPTH):
        w_mix = jnp.where(causal[None], rep["w_spatial"][l], 0.0).astype(BF16)
        b_mix = jnp.broadcast_to(rep["b_spatial"][l][:, :, None], (SGU_GROUPS, CHUNK, d // SGU_GROUPS))
        proj = mm_nn("proj", h, weight(l, "w_in", h), BF16, 512)
        c = glu_conv_fwd(proj, conv_w[l], row("conv_b", l), d)
        s_act = ln_silu_fwd(c, row("conv_ln_g", l), row("conv_ln_b", l))
        p_act = sgu_fwd(proj, row("sgu_ln_g", l), row("sgu_ln_b", l), w_mix, b_mix, d, 2)
        y_a = mm_nn("branch_out", s_act, weight(l, "w_a_out", s_act), F32, 512)
        y_b = mm_nn("branch_out", p_act, weight(l, "w_b_out", p_act), F32, 512)
        merged = merge_fwd(y_a, y_b, proj, b_gate[l], 4)
        o = mm_nn("branch_out", merged, weight(l, "w_o", merged), F32, 512)
        x_mid, h2 = norm_res(x, o, row("norm_mix_post", l), row("norm_ffn_pre", l))
        gu = mm_nn("gate_up", h2, weight(l, "w_gate_up", h2), BF16, 512)
        f = swiglu_fwd(gu)
        o2 = mm_nn("down", f, weight(l, "w_down", f), F32, 512, tn=1024)
        saved.append(dict(x_in=x, h=h, proj=proj, c=c, s_act=s_act, p_act=p_act, y_a=y_a, y_b=y_b, merged=merged, o=o,
                          x_mid=x_mid, h2=h2, gu=gu, f=f, o2=o2, w_mix=w_mix, b_mix=b_mix))
        if l + 1 < DEPTH:
            x, h = norm_res(x_mid, o2, row("norm_ffn_post", l), row("norm_mix_pre", l + 1))

    top = saved[-1]
    loss_vec, dx, do2, dg_ffn_post = final_norm_loss(top["x_mid"], top["o2"], row("norm_ffn_post", DEPTH - 1), target)
    loss = (0.5 / d) * jnp.sum(loss_vec)

    grads = [None] * DEPTH
    for l in reversed(range(DEPTH)):
        sv = saved[l]
        wl = {name: weight(l, name, None)[0] for name in MATRICES}
        g = {"norm_ffn_post": dg_ffn_post}
        df = mm_nt("d_down_in", do2, wl["w_down"], BF16, 512, tn=wl["w_down"].shape[0] // 4)
        tok = emit(l, "w_down", mm_tn("d_down_w", sv["f"], do2, BF16, 512, sv["f"].shape[1] // 4))
        dgu = swiglu_bwd(sv["gu"], df, deps=[tok])
        dh2 = mm_nt("d_gate_up_in", dgu, wl["w_gate_up"], F32, 1024)
        tok = emit(l, "w_gate_up", mm_tn("d_gate_up_w", sv["h2"], dgu, BF16, 2048, d // 2, nb=wl["w_gate_up"].shape[2]))
        dx, do, g["norm_ffn_pre"], g["norm_mix_post"] = norm_bwd_in_out(
            dx, dh2, sv["x_mid"], row("norm_ffn_pre", l), sv["o"], row("norm_mix_post", l), deps=[tok])
        dm = mm_nt("d_square_in", do, wl["w_o"], F32, 512)
        tok = emit(l, "w_o", mm_tn("d_square_w", sv["merged"], do, BF16, 1024, d // 2))
        dy_a, dy_b, dproj, g["b_gate"] = merge_bwd(dm, sv["y_a"], sv["y_b"], sv["proj"], b_gate[l], 4, deps=[tok])
        ds = mm_nt("d_square_in", dy_a, wl["w_a_out"], F32, 512)
        tok = emit(l, "w_a_out", mm_tn("d_square_w", sv["s_act"], dy_a, BF16, 1024, d // 2))
        dp = mm_nt("d_square_in", dy_b, wl["w_b_out"], F32, 512, deps=[tok])
        tok = emit(l, "w_b_out", mm_tn("d_square_w", sv["p_act"], dy_b, BF16, 1024, d // 2))
        dc, g["conv_ln_g"], g["conv_ln_b"], g["conv_b"] = ln_silu_bwd(
            sv["c"], ds, row("conv_ln_g", l), row("conv_ln_b", l), deps=[tok])
        dglu, g["conv_w"] = conv_bwd(sv["proj"], dc, conv_w[l], d)
        dproj = glu_bwd(dproj, dglu, sv["proj"], d)
        dproj, dw_mix, db_mix, g["sgu_ln_g"], g["sgu_ln_b"] = sgu_bwd(
            dproj, sv["proj"], dp, row("sgu_ln_g", l), row("sgu_ln_b", l), sv["w_mix"],
            jnp.swapaxes(sv["w_mix"], 1, 2), sv["b_mix"], d, 2)
        g["w_spatial"] = jnp.where(causal[None], dw_mix, 0.0)
        g["b_spatial"] = db_mix[:, :, 0]
        tok = emit(l, "w_in", mm_tn("d_in_w", sv["h"], dproj, BF16, 2048, d // 2, nb=wl["w_in"].shape[2]))
        dh = mm_nt("d_in_in", dproj, wl["w_in"], F32, 1024, deps=[tok])
        if l > 0:
            below = saved[l - 1]
            dx, do2, g["norm_mix_pre"], dg_ffn_post = norm_bwd_in_out(
                dx, dh, sv["x_in"], row("norm_mix_pre", l), below["o2"], row("norm_ffn_post", l - 1))
        else:
            dx, g["norm_mix_pre"] = norm_bwd_in(dx, dh, sv["x_in"], row("norm_mix_pre", l))
        grads[l] = g
    return loss, dx, grads


def _pack_rows(arrays):
    return jnp.concatenate([a.reshape(-1, 128) for a in arrays], axis=0)


def _unpack_rows(packed, shapes):
    out, r0 = [], 0
    for shp in shapes:
        nr = math.prod(shp) // 128
        out.append(packed[r0:r0 + nr].reshape(shp))
        r0 += nr
    return out


def kernel(x, norm_mix_pre, norm_mix_post, norm_ffn_pre, norm_ffn_post, w_in, b_gate, conv_w, conv_b, conv_ln_g, conv_ln_b, w_a_out, sgu_ln_g, sgu_ln_b, w_spatial, b_spatial, w_b_out, w_o, w_gate_up, w_down, loss_target, m_norm_mix_pre, m_norm_mix_post, m_norm_ffn_pre, m_norm_ffn_post, m_w_in, m_b_gate, m_conv_w, m_conv_b, m_conv_ln_g, m_conv_ln_b, m_w_a_out, m_sgu_ln_g, m_sgu_ln_b, m_w_spatial, m_b_spatial, m_w_b_out, m_w_o, m_w_gate_up, m_w_down, v_norm_mix_pre, v_norm_mix_post, v_norm_ffn_pre, v_norm_ffn_post, v_w_in, v_b_gate, v_conv_w, v_conv_b, v_conv_ln_g, v_conv_ln_b, v_w_a_out, v_sgu_ln_g, v_sgu_ln_b, v_w_spatial, v_b_spatial, v_w_b_out, v_w_o, v_w_gate_up, v_w_down):
    names = ("norm_mix_pre", "norm_mix_post", "norm_ffn_pre", "norm_ffn_post", "w_in", "b_gate", "conv_w", "conv_b",
             "conv_ln_g", "conv_ln_b", "w_a_out", "sgu_ln_g", "sgu_ln_b", "w_spatial", "b_spatial", "w_b_out", "w_o",
             "w_gate_up", "w_down")
    w = dict(zip(names, (norm_mix_pre, norm_mix_post, norm_ffn_pre, norm_ffn_post, w_in, b_gate, conv_w, conv_b,
                         conv_ln_g, conv_ln_b, w_a_out, sgu_ln_g, sgu_ln_b, w_spatial, b_spatial, w_b_out, w_o,
                         w_gate_up, w_down)))
    m = dict(zip(names, (m_norm_mix_pre, m_norm_mix_post, m_norm_ffn_pre, m_norm_ffn_post, m_w_in, m_b_gate, m_conv_w,
                         m_conv_b, m_conv_ln_g, m_conv_ln_b, m_w_a_out, m_sgu_ln_g, m_sgu_ln_b, m_w_spatial,
                         m_b_spatial, m_w_b_out, m_w_o, m_w_gate_up, m_w_down)))
    v = dict(zip(names, (v_norm_mix_pre, v_norm_mix_post, v_norm_ffn_pre, v_norm_ffn_post, v_w_in, v_b_gate, v_conv_w,
                         v_conv_b, v_conv_ln_g, v_conv_ln_b, v_w_a_out, v_sgu_ln_g, v_sgu_ln_b, v_w_spatial,
                         v_b_spatial, v_w_b_out, v_w_o, v_w_gate_up, v_w_down)))
    d = x.shape[-1]
    shard_cols = d // N_DEV

    def small_pack(bg, cw):
        rows = jnp.concatenate([bg, cw], axis=1).reshape(DEPTH * (2 + CONV_WIDTH), shard_cols)
        return jnp.pad(rows, ((0, (-rows.shape[0]) % 8), (0, 0)))

    small_w, small_m, small_v = (small_pack(t["b_gate"], t["conv_w"]) for t in (w, m, v))

    small_full, = _exchange("gather_small", [small_w], scatter=False)
    small_full = small_full[:, :DEPTH * (2 + CONV_WIDTH)].reshape(N_DEV, DEPTH, 2 + CONV_WIDTH, shard_cols)
    small_full = jnp.transpose(small_full, (1, 2, 0, 3)).reshape(DEPTH, 2 + CONV_WIDTH, d)
    b_gate_full = small_full[:, :2]
    conv_w_full = jnp.pad(small_full[:, 2:], ((0, 0), (0, CONV_PAD - CONV_WIDTH), (0, 0)))

    me = (4 * lax.axis_index("x") + 2 * lax.axis_index("y") + lax.axis_index("c")).astype(jnp.int32).reshape(1)
    first_level = (SIBLING,) + SAME_CORE_PEERS
    gathers, token = {}, small_full
    for l in range(DEPTH):
        lands = [fill_own_slot(f"cast_{name}", me, w[name], l, BF16) for name in MATRICES]
        handles, token = exchange_start(f"gather_start_{l}", lands, first_level, after=token)
        for name, handle in zip(MATRICES, handles):
            gathers[l, name] = handle
    use_order = [(l, name) for l in range(DEPTH) for name in MATRICES]
    forwards, gathered = {}, {}

    def start_forward(i, after):
        if i >= len(use_order) or use_order[i] in forwards:
            return None
        l, name = use_order[i]
        land = exchange_wait(f"gather_wait_{name}_{l}", gathers[l, name], first_level, after)
        (forwards[l, name],), tok = exchange_start(f"forward_start_{name}_{l}", [land], SAME_CORE_PEERS, forward=True)
        return tok

    def weight(l, name, after):
        if (l, name) not in gathered:
            i = use_order.index((l, name))
            start_forward(i, after)
            tok = start_forward(i + 1, after)
            full = exchange_wait(f"forward_wait_{name}_{l}", forwards[l, name], SAME_CORE_PEERS, after, forward=True)
            gathered[l, name] = full if name in ("w_in", "w_gate_up") else full.reshape(-1, d)
            return gathered[l, name], tok
        return gathered[l, name], None

    scatters = {}

    def emit(l, name, g):
        chunks = g if g.ndim == 3 else g.reshape(N_DEV, -1, d)
        land = fill_own_slot(f"own_grad_{name}", me, chunks, None, BF16)
        (scatters[l, name],), tok = exchange_start(f"scatter_start_{name}_{l}", [land], ALL_PEERS, srcs=[chunks])
        return tok

    rep = {name: w[name] for name in REPLICATED}
    loss, grad_x, grads = local_step(x[0], loss_target[0], rep, weight, emit, b_gate_full, conv_w_full, token)
    loss = lax.psum(loss, MESH_AXES)

    small_g = jnp.stack([jnp.concatenate([grads[l]["b_gate"], grads[l]["conv_w"][:CONV_WIDTH]], axis=0) for l in range(DEPTH)])
    small_g = jnp.transpose(small_g.reshape(DEPTH * (2 + CONV_WIDTH), N_DEV, shard_cols), (1, 0, 2))
    small_g = jnp.pad(small_g, ((0, 0), (0, small_w.shape[0] - small_g.shape[1]), (0, 0)))
    rep_shapes = [w[name].shape for name in REPLICATED]
    rep_g = _pack_rows([jnp.stack([grads[l][name].reshape(w[name].shape[1:]) for l in range(DEPTH)]) for name in REPLICATED])
    small_land = fill_own_slot("own_small", me, small_g, None, F32)
    (small_handle,), tok_small = exchange_start("scatter_start_small", [small_land], ALL_PEERS, srcs=[small_g])
    rep_land = fill_own_slot("own_replicated", me, rep_g[None], 0, F32)
    (rep_handle,), tok_rep = exchange_start("gather_start_replicated", [rep_land], ALL_PEERS, after=tok_small)

    out = {}
    after = tok_rep
    for name in ("w_down", "w_gate_up", "w_o", "w_a_out", "w_b_out", "w_in"):
        parts = [exchange_wait(f"scatter_wait_{name}_{l}", scatters[l, name], ALL_PEERS, after) for l in range(DEPTH)]
        out[name] = sum_adamw("adamw_" + name, parts, w[name], m[name], v[name], 128)
        after = out[name][0]
    small_parts = exchange_wait("scatter_wait_small", small_handle, ALL_PEERS, after)
    rep_parts = exchange_wait("gather_wait_replicated", rep_handle, ALL_PEERS, after)
    small_res = sum_adamw("adamw_small", [small_parts], small_w[None], small_m[None], small_v[None], small_w.shape[0])
    n_small = DEPTH * (2 + CONV_WIDTH)
    small_res = [r[0, :n_small].reshape(DEPTH, 2 + CONV_WIDTH, shard_cols) for r in small_res]
    out["b_gate"] = [r[:, :2] for r in small_res]
    out["conv_w"] = [r[:, 2:] for r in small_res]
    rep_res = sum_adamw("adamw_replicated", [rep_parts], *(_pack_rows([t[name] for name in REPLICATED])[None] for t in (w, m, v)), 672)
    rep_res = [_unpack_rows(r[0], rep_shapes) for r in rep_res]
    for i, name in enumerate(REPLICATED):
        out[name] = [r[i] for r in rep_res]

    return (loss, grad_x[None], *[out[name][0] for name in names], *[out[name][1] for name in names],
            *[out[name][2] for name in names], *[out[name][3] for name in names])
```

```python
import functools
import math

import jax
import jax.numpy as jnp
from jax import lax
from jax.experimental import pallas as pl
from jax.experimental.pallas import tpu as pltpu

F32 = jnp.float32
BF16 = jnp.bfloat16

DEPTH = 4
N_DEV = 8
EPS = 1e-6
CONV_WIDTH = 31
CONV_PAD = 32
CHUNK = 128
SGU_GROUPS = 8

ADAM_LR = 0.001
ADAM_B1 = 0.9
ADAM_B2 = 0.999
ADAM_EPS = 1e-08
ADAM_WD = 0.01
ADAM_STEP = 10

VMEM_BYTES_V7X = 64 * 1024 * 1024
VMEM_COMPILER_SLACK = 12 * 1024 * 1024
MESH_AXES = ("x", "y", "c")
ANY = pl.BlockSpec(memory_space=pl.ANY)


def _nbytes(shape, dtype):
    return math.prod(shape) * jnp.dtype(dtype).itemsize


def _params(block_bytes, ngrid, single_bytes=0):
    limit = min(2 * block_bytes + single_bytes + VMEM_COMPILER_SLACK, VMEM_BYTES_V7X - 4 * 1024 * 1024)
    return pltpu.CompilerParams(dimension_semantics=("arbitrary",) * ngrid, vmem_limit_bytes=int(limit))


def _deps(deps):
    return [t for t in deps if t is not None]


def _mm_body(dims, nk, kaxis, ndeps):
    def body(a_ref, b_ref, *rest):
        o_ref, *acc = rest[ndeps:]

        def prod():
            return lax.dot_general(a_ref[...], b_ref[...], (dims, ((), ())), preferred_element_type=F32)

        if nk == 1:
            o_ref[...] = prod().astype(o_ref.dtype)
            return
        acc_ref, = acc
        k = pl.program_id(kaxis)
        p = prod()

        @pl.when(k == 0)
        def _():
            acc_ref[...] = p

        @pl.when(k > 0)
        def _():
            acc_ref[...] += p

        @pl.when(k == nk - 1)
        def _():
            o_ref[...] = acc_ref[...].astype(o_ref.dtype)

    return body


def _mm_call(name, a, b, dims, grid, a_spec, b_spec, o_spec, out_shape, out_dtype, nk, kaxis, acc_shape, deps=()):
    deps = _deps(deps)
    blocks = (_nbytes([d for d in a_spec.block_shape if d], a.dtype) + _nbytes([d for d in b_spec.block_shape if d], b.dtype)
              + _nbytes([d for d in o_spec.block_shape if d], out_dtype))
    scratch = [pltpu.VMEM(acc_shape, F32)] if nk > 1 else []
    acc_bytes = _nbytes(acc_shape, F32) * (2 if nk > 1 else 1)
    return pl.pallas_call(
        _mm_body(dims, nk, kaxis, len(deps)), name=name, grid=grid, in_specs=[a_spec, b_spec] + [ANY] * len(deps),
        out_specs=o_spec, out_shape=jax.ShapeDtypeStruct(out_shape, out_dtype), scratch_shapes=scratch,
        compiler_params=_params(blocks, len(grid), acc_bytes),
    )(a, b, *deps)


def mm_nn(name, a, b_and_token, out_dtype, tm, tn=None, tk=None):
    b, token = b_and_token
    m, k = a.shape
    tm = min(tm, m)
    if b.ndim == 3:
        nblk, _, nb = b.shape
        return _mm_call(name, a, b, ((1,), (0,)), (nblk, m // tm),
                        pl.BlockSpec((tm, k), lambda j, i: (i, 0)), pl.BlockSpec((None, k, nb), lambda j, i: (j, 0, 0)),
                        pl.BlockSpec((tm, nb), lambda j, i: (i, j)), (m, nblk * nb), out_dtype, 1, 0, (tm, nb), [token])
    n = b.shape[1]
    tn = tn or n
    tk = tk or k
    nk = k // tk
    return _mm_call(name, a, b, ((1,), (0,)), (n // tn, m // tm, nk),
                    pl.BlockSpec((tm, tk), lambda j, i, kk: (i, kk)), pl.BlockSpec((tk, tn), lambda j, i, kk: (kk, j)),
                    pl.BlockSpec((tm, tn), lambda j, i, kk: (i, j)), (m, n), out_dtype, nk, 2, (tm, tn), [token])


def mm_nt(name, a, b, out_dtype, tm, tn=None, deps=()):
    m = a.shape[0]
    tm = min(tm, m)
    if b.ndim == 3:
        kblk, n, kb = b.shape
        return _mm_call(name, a, b, ((1,), (1,)), (m // tm, kblk),
                        pl.BlockSpec((tm, kb), lambda i, kk: (i, kk)), pl.BlockSpec((None, n, kb), lambda i, kk: (kk, 0, 0)),
                        pl.BlockSpec((tm, n), lambda i, kk: (i, 0)), (m, n), out_dtype, kblk, 1, (tm, n), deps)
    n, kc = b.shape
    tn = tn or n
    return _mm_call(name, a, b, ((1,), (1,)), (n // tn, m // tm),
                    pl.BlockSpec((tm, kc), lambda j, i: (i, 0)), pl.BlockSpec((tn, kc), lambda j, i: (j, 0)),
                    pl.BlockSpec((tm, tn), lambda j, i: (i, j)), (m, n), out_dtype, 1, 0, (tm, tn), deps)


def mm_tn(name, a, b, out_dtype, tm, tr, nb=None):
    m, k = a.shape
    n = b.shape[1]
    tm = min(tm, m)
    nm = m // tm
    if nb is not None:
        return _mm_call(name, a, b, ((0,), (0,)), (n // nb, k // tr, nm),
                        pl.BlockSpec((tm, tr), lambda j, r, mm: (mm, r)), pl.BlockSpec((tm, nb), lambda j, r, mm: (mm, j)),
                        pl.BlockSpec((None, tr, nb), lambda j, r, mm: (j, r, 0)), (n // nb, k, nb), out_dtype, nm, 2, (tr, nb))
    return _mm_call(name, a, b, ((0,), (0,)), (k // tr, nm),
                    pl.BlockSpec((tm, tr), lambda r, mm: (mm, r)), pl.BlockSpec((tm, n), lambda r, mm: (mm, 0)),
                    pl.BlockSpec((tr, n), lambda r, mm: (r, 0)), (k, n), out_dtype, nm, 1, (tr, n))


def _row_call(name, body, grid, in_specs, out_specs, out_shape, arrays, scratch=(), aliases=None, vmem_blocks=0, deps=()):
    deps = _deps(deps)
    nin = len(arrays)

    def with_deps(*refs):
        body(*refs[:nin], *refs[nin + len(deps):])

    return pl.pallas_call(
        with_deps, name=name, grid=grid, in_specs=list(in_specs) + [ANY] * len(deps), out_specs=out_specs,
        out_shape=out_shape, scratch_shapes=list(scratch), input_output_aliases=aliases or {},
        compiler_params=_params(vmem_blocks, len(grid)),
    )(*arrays, *deps)


def _rows(tm, d, col=0):
    return pl.BlockSpec((tm, d), lambda i, *_: (i, col))


def _vec(d):
    return pl.BlockSpec((1, d), lambda *_: (0, 0))


def _rstd(x):
    return lax.rsqrt(jnp.mean(x * x, axis=-1, keepdims=True) + EPS)


def _rms_bwd(dy, x, g):
    r = _rstd(x)
    n = x * r
    w = dy * g
    dx = r * (w - n * jnp.mean(w * n, axis=-1, keepdims=True))
    return dx, jnp.sum(dy * n, axis=0, keepdims=True)


def _accumulate(ref, value, first):
    @pl.when(first)
    def _():
        ref[...] = value

    @pl.when(jnp.logical_not(first))
    def _():
        ref[...] += value


def rms_fwd(x, g, tm=256, deps=()):
    s, d = x.shape
    tm = min(tm, s)

    def body(x_ref, g_ref, h_ref):
        xv = x_ref[...]
        h_ref[...] = (xv * _rstd(xv) * g_ref[...]).astype(BF16)

    return _row_call("rms_fwd", body, (s // tm,), [_rows(tm, d), _vec(d)], _rows(tm, d),
                     jax.ShapeDtypeStruct((s, d), BF16), (x, g), vmem_blocks=tm * d * 6, deps=deps)


def norm_res(x_in, o, g_post, g_next, tm=256):
    s, d = x_in.shape
    tm = min(tm, s)

    def body(x_ref, o_ref, gp_ref, gn_ref, xo_ref, h_ref):
        ov = o_ref[...]
        xo = x_ref[...] + (ov * _rstd(ov) * gp_ref[...])
        xo_ref[...] = xo
        h_ref[...] = (xo * _rstd(xo) * gn_ref[...]).astype(BF16)

    return _row_call("norm_res", body, (s // tm,), [_rows(tm, d), _rows(tm, d), _vec(d), _vec(d)],
                     [_rows(tm, d), _rows(tm, d)],
                     [jax.ShapeDtypeStruct((s, d), F32), jax.ShapeDtypeStruct((s, d), BF16)],
                     (x_in, o, g_post, g_next), vmem_blocks=tm * d * 14)


def final_norm_loss(x_in, o, g_post, target, tm=256):
    s, d = x_in.shape
    tm = min(tm, s)

    def body(x_ref, o_ref, gp_ref, t_ref, loss_ref, dy_ref, do_ref, dg_ref):
        first = pl.program_id(0) == 0
        ov = o_ref[...]
        g = gp_ref[...]
        diff = x_ref[...] + (ov * _rstd(ov) * g) - t_ref[...]
        _accumulate(loss_ref, jnp.sum(diff * diff, axis=0, keepdims=True), first)
        dy = diff * (1.0 / d)
        dy_ref[...] = dy
        do, dg = _rms_bwd(dy, ov, g)
        do_ref[...] = do.astype(BF16)
        _accumulate(dg_ref, dg, first)

    return _row_call("final_norm_loss", body, (s // tm,), [_rows(tm, d), _rows(tm, d), _vec(d), _rows(tm, d)],
                     [_vec(d), _rows(tm, d), _rows(tm, d), _vec(d)],
                     [jax.ShapeDtypeStruct((1, d), F32), jax.ShapeDtypeStruct((s, d), F32),
                      jax.ShapeDtypeStruct((s, d), BF16), jax.ShapeDtypeStruct((1, d), F32)],
                     (x_in, o, g_post, target), vmem_blocks=tm * d * 18)


def norm_bwd_in_out(dx_out, dh, x_in, g_pre, o_below, g_post_below, tm=256, deps=()):
    s, d = x_in.shape
    tm = min(tm, s)

    def body(dxo_ref, dh_ref, x_ref, g_ref, o_ref, gb_ref, dxi_ref, do_ref, dg_ref, dgb_ref):
        first = pl.program_id(0) == 0
        dx, dg = _rms_bwd(dh_ref[...], x_ref[...], g_ref[...])
        dxi = dxo_ref[...] + dx
        dxi_ref[...] = dxi
        _accumulate(dg_ref, dg, first)
        do, dgb = _rms_bwd(dxi, o_ref[...], gb_ref[...])
        do_ref[...] = do.astype(BF16)
        _accumulate(dgb_ref, dgb, first)

    return _row_call("norm_bwd_in_out", body, (s // tm,),
                     [_rows(tm, d), _rows(tm, d), _rows(tm, d), _vec(d), _rows(tm, d), _vec(d)],
                     [_rows(tm, d), _rows(tm, d), _vec(d), _vec(d)],
                     [jax.ShapeDtypeStruct((s, d), F32), jax.ShapeDtypeStruct((s, d), BF16),
                      jax.ShapeDtypeStruct((1, d), F32), jax.ShapeDtypeStruct((1, d), F32)],
                     (dx_out, dh, x_in, g_pre, o_below, g_post_below), vmem_blocks=tm * d * 22, deps=deps)


def norm_bwd_in(dx_out, dh, x_in, g_pre, tm=256):
    s, d = x_in.shape
    tm = min(tm, s)

    def body(dxo_ref, dh_ref, x_ref, g_ref, dxi_ref, dg_ref):
        dx, dg = _rms_bwd(dh_ref[...], x_ref[...], g_ref[...])
        dxi_ref[...] = dxo_ref[...] + dx
        _accumulate(dg_ref, dg, pl.program_id(0) == 0)

    return _row_call("norm_bwd_in", body, (s // tm,), [_rows(tm, d), _rows(tm, d), _rows(tm, d), _vec(d)],
                     [_rows(tm, d), _vec(d)],
                     [jax.ShapeDtypeStruct((s, d), F32), jax.ShapeDtypeStruct((1, d), F32)],
                     (dx_out, dh, x_in, g_pre), vmem_blocks=tm * d * 16)


CONV_COLS = 256
CONV_ROWS = 32


SUBLANES = 8


def _shifted_copies(ext_ref, sh_ref):
    n = sh_ref.shape[1]
    for r in range(SUBLANES):
        sh_ref[r] = ext_ref[r:r + n, :]


def _window(sh_ref, off):
    return sh_ref[off % SUBLANES, off - off % SUBLANES:off - off % SUBLANES + CONV_ROWS, :]


def _depthwise(sh_ref, w, n_out, in_off, flip, emit):
    for r0 in range(0, n_out, CONV_ROWS):
        acc = None
        for k in range(CONV_WIDTH):
            term = w[k:k + 1, :] * _window(sh_ref, r0 + in_off + (CONV_WIDTH - 1 - k if flip else k))
            acc = term if acc is None else acc + term
        emit(r0, acc)


def _ext_scratch(tm):
    return [pltpu.VMEM((tm + CONV_PAD + SUBLANES, CONV_COLS), F32), pltpu.VMEM((SUBLANES, tm + CONV_PAD, CONV_COLS), F32)]


def glu_conv_fwd(proj, conv_w, conv_b, c_ch, tm=512):
    s = proj.shape[0]
    tm = min(tm, s)
    ncb = c_ch // CONV_COLS
    hb = tm // CONV_PAD

    def body(a_ref, g_ref, ah_ref, gh_ref, w_ref, b_ref, c_ref, ext_ref, sh_ref):
        i = pl.program_id(1)
        halo = ah_ref[...].astype(F32) * jax.nn.sigmoid(gh_ref[...].astype(F32))
        ext_ref[0:CONV_PAD, :] = jnp.where(i > 0, halo, 0.0)
        ext_ref[CONV_PAD:CONV_PAD + tm, :] = a_ref[...].astype(F32) * jax.nn.sigmoid(g_ref[...].astype(F32))
        ext_ref[CONV_PAD + tm:, :] = jnp.zeros((SUBLANES, CONV_COLS), F32)
        _shifted_copies(ext_ref, sh_ref)
        w = w_ref[...]
        bias = b_ref[...]

        def emit(r0, acc):
            c_ref[r0:r0 + CONV_ROWS, :] = acc + bias

        _depthwise(sh_ref, w, tm, CONV_PAD - (CONV_WIDTH - 1), False, emit)

    main = lambda col0: pl.BlockSpec((tm, CONV_COLS), lambda c, i: (i, col0 + c))
    halo = lambda col0: pl.BlockSpec((CONV_PAD, CONV_COLS), lambda c, i: (jnp.maximum(i * hb - 1, 0), col0 + c))
    return _row_call("glu_conv_fwd", body, (ncb, s // tm),
                     [main(0), main(ncb), halo(0), halo(ncb),
                      pl.BlockSpec((CONV_PAD, CONV_COLS), lambda c, i: (0, c)), pl.BlockSpec((1, CONV_COLS), lambda c, i: (0, c))],
                     pl.BlockSpec((tm, CONV_COLS), lambda c, i: (i, c)), jax.ShapeDtypeStruct((s, c_ch), F32),
                     (proj, proj, proj, proj, conv_w, conv_b),
                     scratch=_ext_scratch(tm), vmem_blocks=tm * CONV_COLS * 32)


def _layer_norm_stats(x):
    mu = jnp.mean(x, axis=-1, keepdims=True)
    xc = x - mu
    rstd = lax.rsqrt(jnp.mean(xc * xc, axis=-1, keepdims=True) + EPS)
    return xc * rstd, rstd


def _layer_norm_bwd(dy, xhat, rstd, g):
    dxh = dy * g
    return rstd * (dxh - jnp.mean(dxh, axis=-1, keepdims=True) - xhat * jnp.mean(dxh * xhat, axis=-1, keepdims=True))


def ln_silu_fwd(c, ln_g, ln_b, tm=256):
    s, d = c.shape
    tm = min(tm, s)

    def body(c_ref, g_ref, b_ref, s_ref):
        xhat, _ = _layer_norm_stats(c_ref[...])
        s_ref[...] = jax.nn.silu(xhat * g_ref[...] + b_ref[...]).astype(BF16)

    return _row_call("ln_silu_fwd", body, (s // tm,), [_rows(tm, d), _vec(d), _vec(d)], _rows(tm, d),
                     jax.ShapeDtypeStruct((s, d), BF16), (c, ln_g, ln_b), vmem_blocks=tm * d * 10)


def ln_silu_bwd(c, ds, ln_g, ln_b, tm=256, deps=()):
    s, d = c.shape
    tm = min(tm, s)

    def body(c_ref, ds_ref, g_ref, b_ref, dc_ref, dg_ref, db_ref, dcb_ref):
        first = pl.program_id(0) == 0
        g = g_ref[...]
        xhat, rstd = _layer_norm_stats(c_ref[...])
        y = xhat * g + b_ref[...]
        sg = jax.nn.sigmoid(y)
        dln = ds_ref[...] * (sg * (1.0 + y * (1.0 - sg)))
        _accumulate(dg_ref, jnp.sum(dln * xhat, axis=0, keepdims=True), first)
        _accumulate(db_ref, jnp.sum(dln, axis=0, keepdims=True), first)
        dc = _layer_norm_bwd(dln, xhat, rstd, g)
        dc_ref[...] = dc
        _accumulate(dcb_ref, jnp.sum(dc, axis=0, keepdims=True), first)

    vec = jax.ShapeDtypeStruct((1, d), F32)
    return _row_call("ln_silu_bwd", body, (s // tm,), [_rows(tm, d), _rows(tm, d), _vec(d), _vec(d)],
                     [_rows(tm, d), _vec(d), _vec(d), _vec(d)], [jax.ShapeDtypeStruct((s, d), F32), vec, vec, vec],
                     (c, ds, ln_g, ln_b), vmem_blocks=tm * d * 20, deps=deps)


def conv_bwd(proj, dc, conv_w, c_ch, tm=512):
    s = proj.shape[0]
    tm = min(tm, s)
    ncb = c_ch // CONV_COLS
    hb = tm // CONV_PAD
    last_halo = s // CONV_PAD - 1
    n_i = s // tm

    def body(a_ref, g_ref, ah_ref, gh_ref, dc_ref, dcn_ref, w_ref, dglu_ref, dw_ref,
             ext_ref, sh_ref, dce_ref, dsh_ref, dwacc_ref):
        i = pl.program_id(1)
        zeros = jnp.zeros((SUBLANES, CONV_COLS), F32)
        halo = ah_ref[...].astype(F32) * jax.nn.sigmoid(gh_ref[...].astype(F32))
        ext_ref[0:CONV_PAD, :] = jnp.where(i > 0, halo, 0.0)
        ext_ref[CONV_PAD:CONV_PAD + tm, :] = a_ref[...].astype(F32) * jax.nn.sigmoid(g_ref[...].astype(F32))
        ext_ref[CONV_PAD + tm:, :] = zeros
        _shifted_copies(ext_ref, sh_ref)
        dce_ref[0:tm, :] = dc_ref[...]
        dce_ref[tm:tm + CONV_PAD, :] = jnp.where(i < n_i - 1, dcn_ref[...], 0.0)
        dce_ref[tm + CONV_PAD:, :] = zeros
        _shifted_copies(dce_ref, dsh_ref)
        w = w_ref[...]

        def emit(r0, acc):
            dglu_ref[r0:r0 + CONV_ROWS, :] = acc

        _depthwise(dsh_ref, w, tm, 0, True, emit)

        for k in range(CONV_WIDTH):
            acc = None
            for r0 in range(0, tm, CONV_ROWS):
                term = dce_ref[r0:r0 + CONV_ROWS, :] * _window(sh_ref, r0 + CONV_PAD - (CONV_WIDTH - 1) + k)
                acc = term if acc is None else acc + term
            dwacc_ref[k:k + 1, :] = jnp.sum(acc, axis=0, keepdims=True)
        dwacc_ref[CONV_WIDTH:, :] = jnp.zeros((CONV_PAD - CONV_WIDTH, CONV_COLS), F32)
        _accumulate(dw_ref, dwacc_ref[...], i == 0)

    main = lambda col0: pl.BlockSpec((tm, CONV_COLS), lambda c, i: (i, col0 + c))
    halo = lambda col0: pl.BlockSpec((CONV_PAD, CONV_COLS), lambda c, i: (jnp.maximum(i * hb - 1, 0), col0 + c))
    nxt = pl.BlockSpec((CONV_PAD, CONV_COLS), lambda c, i: (jnp.minimum((i + 1) * hb, last_halo), c))
    wspec = pl.BlockSpec((CONV_PAD, CONV_COLS), lambda c, i: (0, c))
    return _row_call("conv_bwd", body, (ncb, n_i),
                     [main(0), main(ncb), halo(0), halo(ncb), main(0), nxt, wspec],
                     [main(0), wspec],
                     [jax.ShapeDtypeStruct((s, c_ch), F32), jax.ShapeDtypeStruct((CONV_PAD, c_ch), F32)],
                     (proj, proj, proj, proj, dc, dc, conv_w),
                     scratch=_ext_scratch(tm) + _ext_scratch(tm) + [pltpu.VMEM((CONV_PAD, CONV_COLS), F32)],
                     vmem_blocks=tm * CONV_COLS * 56)


ELEMENTWISE_COLS = 512


def _col_chunks(d):
    return [slice(c0, c0 + ELEMENTWISE_COLS) for c0 in range(0, d, ELEMENTWISE_COLS)]


def _pair_spec(tm, d, pair):
    return pl.BlockSpec((tm, 2 * d), lambda i: (i, pair))


def glu_bwd(dproj, dglu, proj, c_ch, tm=256):
    s = proj.shape[0]
    tm = min(tm, s)

    def body(_, dglu_ref, a_ref, g_ref, out_ref):
        for cols in _col_chunks(c_ch):
            dg = dglu_ref[:, cols]
            sg = jax.nn.sigmoid(g_ref[:, cols].astype(F32))
            out_ref[:, cols] = (dg * sg).astype(BF16)
            out_ref[:, c_ch + cols.start:c_ch + cols.stop] = (dg * a_ref[:, cols].astype(F32) * (sg * (1.0 - sg))).astype(BF16)

    return _row_call("glu_bwd", body, (s // tm,), [ANY, _rows(tm, c_ch), _rows(tm, c_ch), _rows(tm, c_ch, 1)],
                     _pair_spec(tm, c_ch, 0), jax.ShapeDtypeStruct(dproj.shape, BF16), (dproj, dglu, proj, proj),
                     aliases={0: 0}, vmem_blocks=tm * c_ch * 12)


_SQRT_HALF = 0.7071067811865476
_INV_SQRT_2PI = 0.3989422804014327


def _gelu_parts(x):
    cdf = 0.5 * (1.0 + lax.erf(x * _SQRT_HALF))
    return cdf, x * cdf


def _gelu_grad(x, cdf):
    return cdf + x * (_INV_SQRT_2PI * jnp.exp(-0.5 * x * x))


def _sgu_specs(tm, ch):
    grp = ch // SGU_GROUPS
    full3 = lambda shape: pl.BlockSpec(shape, lambda *_: (0, 0, 0))
    return grp, full3((SGU_GROUPS, CHUNK, CHUNK)), full3((SGU_GROUPS, CHUNK, grp))


def sgu_fwd(proj, ln_g, ln_b, w_mix, b_mix, ch, col0, tm=CHUNK):
    s = proj.shape[0]
    grp, wspec, bspec = _sgu_specs(tm, ch)

    def body(u_ref, v_ref, g_ref, b_ref, w_ref, bm_ref, p_ref, mix_ref):
        _, u = _gelu_parts(u_ref[...].astype(F32))
        _, v0 = _gelu_parts(v_ref[...].astype(F32))
        xhat, _ = _layer_norm_stats(v0)
        vn = (xhat * g_ref[...] + b_ref[...]).astype(BF16)
        for n in range(tm // CHUNK):
            for g in range(SGU_GROUPS):
                blk = vn[n * CHUNK:(n + 1) * CHUNK, g * grp:(g + 1) * grp]
                mix_ref[n * CHUNK:(n + 1) * CHUNK, g * grp:(g + 1) * grp] = (
                    jnp.dot(w_ref[g], blk, preferred_element_type=F32) + bm_ref[g])
        p_ref[...] = (u * mix_ref[...]).astype(BF16)

    return _row_call("sgu_fwd", body, (s // tm,),
                     [_rows(tm, ch, col0), _rows(tm, ch, col0 + 1), _vec(ch), _vec(ch), wspec, bspec], _rows(tm, ch),
                     jax.ShapeDtypeStruct((s, ch), BF16), (proj, proj, ln_g, ln_b, w_mix, b_mix),
                     scratch=[pltpu.VMEM((tm, ch), F32)], vmem_blocks=tm * ch * 30)


def sgu_bwd(dproj, proj, dp, ln_g, ln_b, w_mix, w_mix_t, b_mix, ch, col0, tm=CHUNK):
    s = proj.shape[0]
    assert tm == CHUNK and col0 % 2 == 0
    grp, wspec, bspec = _sgu_specs(tm, ch)
    groups = [slice(k * grp, (k + 1) * grp) for k in range(SGU_GROUPS)]

    def body(_, u_ref, v_ref, dp_ref, g_ref, b_ref, w_ref, wt_ref, bm_ref,
             out_ref, dw_ref, dbm_ref, dg_ref, db_ref, u_s, gu_s, gv_s, xh_s, dvn_s):
        first = pl.program_id(0) == 0
        row_sum = lambda x: jnp.sum(x, axis=1, keepdims=True)
        total = None
        for cols in groups:
            ub = u_ref[:, cols].astype(F32)
            vb = v_ref[:, cols].astype(F32)
            cdf_u, u = _gelu_parts(ub)
            cdf_v, v0 = _gelu_parts(vb)
            u_s[:, cols] = u
            gu_s[:, cols] = _gelu_grad(ub, cdf_u)
            gv_s[:, cols] = _gelu_grad(vb, cdf_v)
            xh_s[:, cols] = v0
            total = row_sum(v0) if total is None else total + row_sum(v0)
        mu = total * (1.0 / ch)
        total = None
        for cols in groups:
            xc = xh_s[:, cols] - mu
            xh_s[:, cols] = xc
            total = row_sum(xc * xc) if total is None else total + row_sum(xc * xc)
        rstd = lax.rsqrt(total * (1.0 / ch) + EPS)

        t1 = t2 = None
        for k, cols in enumerate(groups):
            g = g_ref[:, cols]
            xhat = xh_s[:, cols] * rstd
            xh_s[:, cols] = xhat
            vn = (xhat * g + b_ref[:, cols]).astype(BF16)
            dpk = dp_ref[:, cols]
            dmix = dpk * u_s[:, cols]
            dmix_bf = dmix.astype(BF16)
            mixed = jnp.dot(w_ref[k], vn, preferred_element_type=F32) + bm_ref[k]
            out_ref[:, cols] = (dpk * mixed * gu_s[:, cols]).astype(BF16)
            dvn = jnp.dot(wt_ref[k], dmix_bf, preferred_element_type=F32)
            dvn_s[:, cols] = dvn
            _accumulate(dw_ref.at[k], lax.dot_general(dmix_bf, vn, (((1,), (1,)), ((), ())), preferred_element_type=F32), first)
            _accumulate(dbm_ref.at[k], jnp.broadcast_to(row_sum(dmix), (CHUNK, CHUNK)), first)
            _accumulate(dg_ref.at[:, cols], jnp.sum(dvn * xhat, axis=0, keepdims=True), first)
            _accumulate(db_ref.at[:, cols], jnp.sum(dvn, axis=0, keepdims=True), first)
            dxh = dvn * g
            t1 = row_sum(dxh) if t1 is None else t1 + row_sum(dxh)
            t2 = row_sum(dxh * xhat) if t2 is None else t2 + row_sum(dxh * xhat)
        m1 = t1 * (1.0 / ch)
        m2 = t2 * (1.0 / ch)
        for cols in groups:
            dv0 = rstd * (dvn_s[:, cols] * g_ref[:, cols] - m1 - xh_s[:, cols] * m2)
            out_ref[:, ch + cols.start:ch + cols.stop] = (dv0 * gv_s[:, cols]).astype(BF16)

    vec = _vec(ch)
    acc3 = lambda: pl.BlockSpec((SGU_GROUPS, CHUNK, CHUNK), lambda i: (0, 0, 0))
    vshape = jax.ShapeDtypeStruct((1, ch), F32)
    mshape = jax.ShapeDtypeStruct((SGU_GROUPS, CHUNK, CHUNK), F32)
    return _row_call("sgu_bwd", body, (s // tm,),
                     [ANY, _rows(tm, ch, col0), _rows(tm, ch, col0 + 1), _rows(tm, ch), vec, vec, wspec, wspec, bspec],
                     [_pair_spec(tm, ch, col0 // 2), acc3(), acc3(), vec, vec],
                     [jax.ShapeDtypeStruct(dproj.shape, BF16), mshape, mshape, vshape, vshape],
                     (dproj, proj, proj, dp, ln_g, ln_b, w_mix, w_mix_t, b_mix),
                     scratch=[pltpu.VMEM((tm, ch), F32)] * 5, aliases={0: 0}, vmem_blocks=tm * ch * 40)


def branches_merge(s_act, p_act, wa_and_token, wb_and_token, proj, b_gate, col0, tm=256, tn=1024):
    (wa, tok_a), (wb, tok_b) = wa_and_token, wb_and_token
    deps = _deps([tok_a, tok_b])
    s, d = s_act.shape
    tm = min(tm, s)
    per = d // tn

    def body(s_ref, p_ref, wa_ref, wb_ref, l0_ref, l1_ref, bg_ref, *rest):
        ya_ref, yb_ref, m_ref = rest[len(deps):]
        ya = jnp.dot(s_ref[...], wa_ref[...], preferred_element_type=F32)
        yb = jnp.dot(p_ref[...], wb_ref[...], preferred_element_type=F32)
        ya_ref[...] = ya
        yb_ref[...] = yb
        g0 = jax.nn.sigmoid(l0_ref[...].astype(F32) + bg_ref[0:1, :])
        g1 = jax.nn.sigmoid(l1_ref[...].astype(F32) + bg_ref[1:2, :])
        m_ref[...] = (g0 * ya + g1 * yb).astype(BF16)

    act = pl.BlockSpec((tm, d), lambda j, i: (i, 0))
    wgt = pl.BlockSpec((d, tn), lambda j, i: (0, j))
    logits = lambda col: pl.BlockSpec((tm, tn), lambda j, i: (i, col * per + j))
    oblk = pl.BlockSpec((tm, tn), lambda j, i: (i, j))
    return pl.pallas_call(
        body, name="branches_merge", grid=(per, s // tm),
        in_specs=[act, act, wgt, wgt, logits(col0), logits(col0 + 1), pl.BlockSpec((2, tn), lambda j, i: (0, j))] + [ANY] * len(deps),
        out_specs=[oblk, oblk, oblk],
        out_shape=[jax.ShapeDtypeStruct((s, d), F32), jax.ShapeDtypeStruct((s, d), F32), jax.ShapeDtypeStruct((s, d), BF16)],
        compiler_params=_params(2 * tm * d * 2 + 2 * d * tn * 2 + 2 * tm * tn * 2 + tm * tn * 10, 2, 4 * tm * tn * 4),
    )(s_act, p_act, wa, wb, proj, proj, b_gate, *deps)


def merge_bwd(dm, y_a, y_b, proj, b_gate, col0, tm=256, deps=()):
    s, d = y_a.shape
    tm = min(tm, s)

    assert col0 % 2 == 0

    def body(dm_ref, ya_ref, yb_ref, l0_ref, l1_ref, bg_ref, dya_ref, dyb_ref, out_ref, dbg_ref):
        first = pl.program_id(0) == 0
        for cols in _col_chunks(d):
            dmv = dm_ref[:, cols]
            g0 = jax.nn.sigmoid(l0_ref[:, cols].astype(F32) + bg_ref[0:1, cols])
            g1 = jax.nn.sigmoid(l1_ref[:, cols].astype(F32) + bg_ref[1:2, cols])
            dya_ref[:, cols] = (dmv * g0).astype(BF16)
            dyb_ref[:, cols] = (dmv * g1).astype(BF16)
            dl0 = dmv * ya_ref[:, cols] * (g0 * (1.0 - g0))
            dl1 = dmv * yb_ref[:, cols] * (g1 * (1.0 - g1))
            _accumulate(dbg_ref.at[0:1, cols], jnp.sum(dl0, axis=0, keepdims=True), first)
            _accumulate(dbg_ref.at[1:2, cols], jnp.sum(dl1, axis=0, keepdims=True), first)
            out_ref[:, cols] = dl0.astype(BF16)
            out_ref[:, d + cols.start:d + cols.stop] = dl1.astype(BF16)

    bgspec = pl.BlockSpec((2, d), lambda i: (0, 0))
    return _row_call("merge_bwd", body, (s // tm,),
                     [_rows(tm, d), _rows(tm, d), _rows(tm, d), _rows(tm, d, col0), _rows(tm, d, col0 + 1), bgspec],
                     [_rows(tm, d), _rows(tm, d), _pair_spec(tm, d, col0 // 2), bgspec],
                     [jax.ShapeDtypeStruct((s, d), BF16), jax.ShapeDtypeStruct((s, d), BF16),
                      jax.ShapeDtypeStruct(proj.shape, BF16), jax.ShapeDtypeStruct((2, d), F32)],
                     (dm, y_a, y_b, proj, proj, b_gate), vmem_blocks=tm * d * 24, deps=deps)


def gate_up_swiglu(h, w_and_token, tm=256):
    w, token = w_and_token
    deps = _deps([token])
    s, d = h.shape
    nblk, _, nb = w.shape
    half = nblk // 2
    tm = min(tm, s)

    def body(h_ref, wg_ref, wu_ref, *rest):
        g_ref, u_ref, f_ref = rest[len(deps):]
        hv = h_ref[...]
        g = jnp.dot(hv, wg_ref[...], preferred_element_type=F32)
        u = jnp.dot(hv, wu_ref[...], preferred_element_type=F32)
        g_ref[...] = g.astype(BF16)
        u_ref[...] = u.astype(BF16)
        f_ref[...] = (jax.nn.silu(g) * u).astype(BF16)

    out = jax.ShapeDtypeStruct((s, half * nb), BF16)
    oblk = pl.BlockSpec((tm, nb), lambda j, i: (i, j))
    return pl.pallas_call(
        body, name="gate_up_swiglu", grid=(half, s // tm),
        in_specs=[pl.BlockSpec((tm, d), lambda j, i: (i, 0)), pl.BlockSpec((None, d, nb), lambda j, i: (j, 0, 0)),
                  pl.BlockSpec((None, d, nb), lambda j, i: (j + half, 0, 0))] + [ANY] * len(deps),
        out_specs=[oblk, oblk, oblk], out_shape=[out, out, out],
        compiler_params=_params(tm * d * 2 + 2 * d * nb * 2 + 3 * tm * nb * 2, 2, 4 * tm * nb * 4),
    )(h, w, w, *deps)


def swiglu_bwd(g_act, u_act, df, tm=256, deps=()):
    s, half = g_act.shape
    w2 = 2 * half
    tm = min(tm, s)
    chunk = w2 // N_DEV

    def body(g_ref, u_ref, df_ref, out_ref):
        for c0 in range(0, half, chunk):
            cols = slice(c0, c0 + chunk)
            g = g_ref[:, cols].astype(F32)
            sg = jax.nn.sigmoid(g)
            dfv = df_ref[:, cols].astype(F32)
            out_ref[:, cols] = (dfv * u_ref[:, cols].astype(F32) * (sg * (1.0 + g * (1.0 - sg)))).astype(BF16)
            out_ref[:, half + c0:half + c0 + chunk] = (dfv * (g * sg)).astype(BF16)

    return _row_call("swiglu_bwd", body, (s // tm,), [_rows(tm, half), _rows(tm, half), _rows(tm, half)],
                     _rows(tm, w2), jax.ShapeDtypeStruct((s, w2), BF16), (g_act, u_act, df), vmem_blocks=tm * w2 * 5, deps=deps)


def _peers():
    x, y, c = lax.axis_index("x"), lax.axis_index("y"), lax.axis_index("c")
    me = 4 * x + 2 * y + c
    peers = []
    for k in range(1, N_DEV):
        px = 1 - x if k & 4 else x
        py = 1 - y if k & 2 else y
        pc = 1 - c if k & 1 else c
        peers.append(((px, py, pc), 4 * px + 2 * py + pc))
    return me, peers


def _exchange(name, arrays, scatter):
    n = len(arrays)

    def body(*refs):
        ins, outs = refs[:n], refs[n:2 * n]
        send_sems, recv_sems, local_sems = refs[2 * n:]
        me, peers = _peers()

        def remote(a, k):
            (pos, idx) = peers[k]
            src = ins[a].at[idx] if scatter else ins[a]
            return pltpu.make_async_remote_copy(src_ref=src, dst_ref=outs[a].at[me], send_sem=send_sems.at[a, k],
                                                recv_sem=recv_sems.at[a, k], device_id=pos, device_id_type=pl.DeviceIdType.MESH)

        def arrival(a, k):
            (pos, idx) = peers[k]
            src = ins[a].at[idx] if scatter else ins[a]
            return pltpu.make_async_remote_copy(src_ref=src, dst_ref=outs[a].at[idx], send_sem=send_sems.at[a, k],
                                                recv_sem=recv_sems.at[a, k], device_id=pos, device_id_type=pl.DeviceIdType.MESH)

        local = [pltpu.make_async_copy(ins[a].at[me] if scatter else ins[a], outs[a].at[me], local_sems.at[a]) for a in range(n)]
        sends = [remote(a, k) for k in range(N_DEV - 1) for a in range(n)]
        for cp in sends:
            cp.start()
        for cp in local:
            cp.start()
        for k in range(N_DEV - 1):
            for a in range(n):
                arrival(a, k).wait_recv()
        for cp in sends:
            cp.wait_send()
        for cp in local:
            cp.wait()

    out_shape = [jax.ShapeDtypeStruct(a.shape if scatter else (N_DEV,) + a.shape, a.dtype) for a in arrays]
    return pl.pallas_call(
        body, name=name, in_specs=[ANY] * n, out_specs=[ANY] * n, out_shape=out_shape,
        scratch_shapes=[pltpu.SemaphoreType.DMA((n, N_DEV - 1)), pltpu.SemaphoreType.DMA((n, N_DEV - 1)),
                        pltpu.SemaphoreType.DMA((n,))],
    )(*arrays)


HBM_SPEC = pl.BlockSpec(memory_space=pltpu.HBM)
SEM_SPEC = pl.BlockSpec(memory_space=pltpu.SEMAPHORE)
DATAFLOW_EFFECT = pltpu.SideEffectType.DATAFLOW_SIDE_EFFECTING
ALL_PEERS = (1, 2, 3, 4, 5, 6, 7)
SIBLING = 1
SAME_CORE_PEERS = (2, 4, 6)


def _in_hbm(a):
    return pltpu.with_memory_space_constraint(a, pltpu.HBM)


def fill_own_slot(name, me, src, block, dtype):
    _, r, c = src.shape
    tr = _row_tile(r, 256)

    def body(me_ref, src_ref, out_ref):
        out_ref[...] = src_ref[...].astype(dtype)

    if block is None:
        src_index = lambda i, me_ref: (me_ref[0], i, 0)
    else:
        src_index = lambda i, me_ref: (block, i, 0)
    grid_spec = pltpu.PrefetchScalarGridSpec(
        num_scalar_prefetch=1, grid=(r // tr,), in_specs=[pl.BlockSpec((None, tr, c), src_index)],
        out_specs=pl.BlockSpec((None, tr, c), lambda i, me_ref: (me_ref[0], i, 0)))
    return pl.pallas_call(body, name=name, grid_spec=grid_spec, out_shape=jax.ShapeDtypeStruct((N_DEV, r, c), dtype),
                          compiler_params=_params(tr * c * (src.dtype.itemsize + jnp.dtype(dtype).itemsize), 1))(me, src)


def _split_copy(src, land, send_sem, recv_sem, k, peers, me, arriving, forward):
    pos, idx = peers[k - 1]
    if forward:
        pos = peers[SIBLING - 1][0]
        slot = peers[(k | SIBLING) - 1][1] if arriving else idx
        src_ref, dst_ref = land.at[slot], land.at[slot]
    else:
        src_ref = land.at[me] if src is None else src.at[idx]
        dst_ref = land.at[idx if arriving else me]
    return pltpu.make_async_remote_copy(src_ref=src_ref, dst_ref=dst_ref, send_sem=send_sem, recv_sem=recv_sem,
                                        device_id=pos, device_id_type=pl.DeviceIdType.MESH)


def exchange_start(name, lands, peer_ks, srcs=None, after=None, forward=False):
    n = len(lands)
    ns = n if srcs is not None else 0
    extra = _deps([after])
    bufs = (list(srcs) if srcs is not None else []) + list(lands)

    def body(*refs):
        src, land = refs[:ns], refs[ns:ns + n]
        outs = refs[ns + n + len(extra):]
        send_sems, recv_sems, token = outs[:n], outs[n:2 * n], outs[2 * n + ns + n]
        me, peers = _peers()
        for a in range(n):
            for j, k in enumerate(peer_ks):
                _split_copy(src[a] if ns else None, land[a], send_sems[a].at[j], recv_sems[a].at[j], k, peers, me,
                            False, forward).start()
        token[...] = jnp.zeros_like(token)

    sems = [pltpu.SemaphoreType.DMA((len(peer_ks),))] * (2 * n)
    res = pl.pallas_call(
        body, name=name, in_specs=[HBM_SPEC] * len(bufs) + [ANY] * len(extra),
        out_specs=[SEM_SPEC] * (2 * n) + [HBM_SPEC] * len(bufs) + [pl.BlockSpec(memory_space=pltpu.VMEM)],
        out_shape=sems + [pltpu.HBM(a.shape, a.dtype) for a in bufs] + [jax.ShapeDtypeStruct((8, 128), F32)],
        input_output_aliases={i: 2 * n + i for i in range(len(bufs))},
        compiler_params=pltpu.CompilerParams(has_side_effects=DATAFLOW_EFFECT),
    )(*[_in_hbm(a) for a in bufs], *extra)
    handles = [(res[a], res[n + a], res[2 * n + a] if ns else None, res[2 * n + ns + a]) for a in range(n)]
    return handles, res[2 * n + ns + n]


def exchange_wait(name, handle, peer_ks, after, forward=False):
    send_sem, recv_sem, src, land = handle
    bufs = ([src] if src is not None else []) + [land]
    nb = len(bufs)

    def body(*refs):
        src_ref = refs[0] if nb == 2 else None
        land_ref, send_ref, recv_ref = refs[nb - 1], refs[nb], refs[nb + 1]
        me, peers = _peers()
        for j, k in enumerate(peer_ks):
            cp = _split_copy(src_ref, land_ref, send_ref.at[j], recv_ref.at[j], k, peers, me, True, forward)
            cp.wait_send()
            cp.wait_recv()

    return pl.pallas_call(
        body, name=name, in_specs=[HBM_SPEC] * nb + [SEM_SPEC, SEM_SPEC, ANY], out_specs=[HBM_SPEC] * nb,
        out_shape=[pltpu.HBM(a.shape, a.dtype) for a in bufs],
        input_output_aliases={i: i for i in range(nb)}, compiler_params=pltpu.CompilerParams(has_side_effects=DATAFLOW_EFFECT),
    )(*bufs, send_sem, recv_sem, after)[nb - 1]


def _row_tile(r, cap):
    if r <= cap:
        return r
    return max(t for t in range(16, cap + 1, 16) if r % t == 0)


def sum_adamw(name, parts, w, m, v, tr):
    nl, r, c = w.shape
    tr = _row_tile(r, tr)
    c1 = 1.0 - ADAM_B1 ** ADAM_STEP
    c2 = 1.0 - ADAM_B2 ** ADAM_STEP

    def body(*refs):
        part_refs = refs[:nl]
        w_ref, m_ref, v_ref, g_out, d_out, m_out, v_out = refs[nl:]
        layer = pl.program_id(0)
        for j in range(nl):
            @pl.when(layer == j)
            def _(j=j):
                g = part_refs[j][0].astype(F32)
                for p in range(1, N_DEV):
                    g = g + part_refs[j][p].astype(F32)
                mn = ADAM_B1 * m_ref[...] + (1.0 - ADAM_B1) * g
                vn = ADAM_B2 * v_ref[...] + (1.0 - ADAM_B2) * (g * g)
                g_out[...] = g
                m_out[...] = mn
                v_out[...] = vn
                d_out[...] = -ADAM_LR * ((mn / c1) / (jnp.sqrt(vn / c2) + ADAM_EPS) + ADAM_WD * w_ref[...])

    def part_spec(j):
        return pl.BlockSpec((N_DEV, tr, c), lambda l, i: (0, jnp.where(l == j, i, 0), 0))

    lspec = pl.BlockSpec((None, tr, c), lambda l, i: (l, i, 0))
    out = jax.ShapeDtypeStruct((nl, r, c), F32)
    return _row_call(name, body, (nl, r // tr), [part_spec(j) for j in range(nl)] + [lspec] * 3, [lspec] * 4, [out] * 4,
                     tuple(parts) + (w, m, v), vmem_blocks=nl * N_DEV * tr * c * parts[0].dtype.itemsize + 7 * tr * c * 4)


REPLICATED = ("norm_mix_pre", "norm_mix_post", "norm_ffn_pre", "norm_ffn_post", "conv_b", "conv_ln_g", "conv_ln_b",
              "sgu_ln_g", "sgu_ln_b", "w_spatial", "b_spatial")
MATRICES = ("w_in", "w_a_out", "w_b_out", "w_o", "w_gate_up", "w_down")


def local_step(x, target, rep, weight, emit, b_gate, conv_w, start_token=None):
    s, d = x.shape
    causal = jnp.tril(jnp.ones((CHUNK, CHUNK), dtype=bool))
    row = lambda name, l: rep[name][l].reshape(1, -1)

    saved = []
    h = rms_fwd(x, row("norm_mix_pre", 0), deps=[start_token])
    for l in range(DEPTH):
        w_mix = jnp.where(causal[None], rep["w_spatial"][l], 0.0).astype(BF16)
        b_mix = jnp.broadcast_to(rep["b_spatial"][l][:, :, None], (SGU_GROUPS, CHUNK, d // SGU_GROUPS))
        proj = mm_nn("proj", h, weight(l, "w_in", h), BF16, 512)
        c = glu_conv_fwd(proj, conv_w[l], row("conv_b", l), d)
        s_act = ln_silu_fwd(c, row("conv_ln_g", l), row("conv_ln_b", l))
        p_act = sgu_fwd(proj, row("sgu_ln_g", l), row("sgu_ln_b", l), w_mix, b_mix, d, 2)
        y_a, y_b, merged = branches_merge(s_act, p_act, weight(l, "w_a_out", s_act), weight(l, "w_b_out", p_act),
                                          proj, b_gate[l], 4)
        o = mm_nn("branch_out", merged, weight(l, "w_o", merged), F32, 512)
        x_mid, h2 = norm_res(x, o, row("norm_mix_post", l), row("norm_ffn_pre", l))
        g_act, u_act, f = gate_up_swiglu(h2, weight(l, "w_gate_up", h2))
        o2 = mm_nn("down", f, weight(l, "w_down", f), F32, 512, tn=1024)
        saved.append(dict(x_in=x, h=h, proj=proj, c=c, s_act=s_act, p_act=p_act, y_a=y_a, y_b=y_b, merged=merged, o=o,
                          x_mid=x_mid, h2=h2, g_act=g_act, u_act=u_act, f=f, o2=o2, w_mix=w_mix, b_mix=b_mix))
        if l + 1 < DEPTH:
            x, h = norm_res(x_mid, o2, row("norm_ffn_post", l), row("norm_mix_pre", l + 1))

    top = saved[-1]
    loss_vec, dx, do2, dg_ffn_post = final_norm_loss(top["x_mid"], top["o2"], row("norm_ffn_post", DEPTH - 1), target)
    loss = (0.5 / d) * jnp.sum(loss_vec)

    grads = [None] * DEPTH
    for l in reversed(range(DEPTH)):
        sv = saved[l]
        wl = {name: weight(l, name, None)[0] for name in MATRICES}
        g = {"norm_ffn_post": dg_ffn_post}
        df = mm_nt("d_down_in", do2, wl["w_down"], BF16, 512, tn=wl["w_down"].shape[0] // 4)
        tok = emit(l, "w_down", mm_tn("d_down_w", sv["f"], do2, BF16, 512, sv["f"].shape[1] // 4))
        dgu = swiglu_bwd(sv["g_act"], sv["u_act"], df, deps=[tok])
        dh2 = mm_nt("d_gate_up_in", dgu, wl["w_gate_up"], F32, 1024)
        tok = emit(l, "w_gate_up", mm_tn("d_gate_up_w", sv["h2"], dgu, BF16, 2048, d // 2, nb=wl["w_gate_up"].shape[2]))
        dx, do, g["norm_ffn_pre"], g["norm_mix_post"] = norm_bwd_in_out(
            dx, dh2, sv["x_mid"], row("norm_ffn_pre", l), sv["o"], row("norm_mix_post", l), deps=[tok])
        dm = mm_nt("d_square_in", do, wl["w_o"], F32, 512)
        tok = emit(l, "w_o", mm_tn("d_square_w", sv["merged"], do, BF16, 512, d // 2))
        dy_a, dy_b, dproj, g["b_gate"] = merge_bwd(dm, sv["y_a"], sv["y_b"], sv["proj"], b_gate[l], 4, deps=[tok])
        ds = mm_nt("d_square_in", dy_a, wl["w_a_out"], F32, 512)
        tok = emit(l, "w_a_out", mm_tn("d_square_w", sv["s_act"], dy_a, BF16, 512, d // 2))
        dp = mm_nt("d_square_in", dy_b, wl["w_b_out"], F32, 512, deps=[tok])
        tok = emit(l, "w_b_out", mm_tn("d_square_w", sv["p_act"], dy_b, BF16, 512, d // 2))
        dc, g["conv_ln_g"], g["conv_ln_b"], g["conv_b"] = ln_silu_bwd(
            sv["c"], ds, row("conv_ln_g", l), row("conv_ln_b", l), deps=[tok])
        dglu, g["conv_w"] = conv_bwd(sv["proj"], dc, conv_w[l], d)
        dproj = glu_bwd(dproj, dglu, sv["proj"], d)
        dproj, dw_mix, db_mix, g["sgu_ln_g"], g["sgu_ln_b"] = sgu_bwd(
            dproj, sv["proj"], dp, row("sgu_ln_g", l), row("sgu_ln_b", l), sv["w_mix"],
            jnp.swapaxes(sv["w_mix"], 1, 2), sv["b_mix"], d, 2)
        g["w_spatial"] = jnp.where(causal[None], dw_mix, 0.0)
        g["b_spatial"] = db_mix[:, :, 0]
        tok = emit(l, "w_in", mm_tn("d_in_w", sv["h"], dproj, BF16, 2048, d // 2, nb=wl["w_in"].shape[2]))
        dh = mm_nt("d_in_in", dproj, wl["w_in"], F32, 1024, deps=[tok])
        if l > 0:
            below = saved[l - 1]
            dx, do2, g["norm_mix_pre"], dg_ffn_post = norm_bwd_in_out(
                dx, dh, sv["x_in"], row("norm_mix_pre", l), below["o2"], row("norm_ffn_post", l - 1))
        else:
            dx, g["norm_mix_pre"] = norm_bwd_in(dx, dh, sv["x_in"], row("norm_mix_pre", l))
        grads[l] = g
    return loss, dx, grads


def _pack_rows(arrays):
    return jnp.concatenate([a.reshape(-1, 128) for a in arrays], axis=0)


def _unpack_rows(packed, shapes):
    out, r0 = [], 0
    for shp in shapes:
        nr = math.prod(shp) // 128
        out.append(packed[r0:r0 + nr].reshape(shp))
        r0 += nr
    return out


def kernel(x, norm_mix_pre, norm_mix_post, norm_ffn_pre, norm_ffn_post, w_in, b_gate, conv_w, conv_b, conv_ln_g, conv_ln_b, w_a_out, sgu_ln_g, sgu_ln_b, w_spatial, b_spatial, w_b_out, w_o, w_gate_up, w_down, loss_target, m_norm_mix_pre, m_norm_mix_post, m_norm_ffn_pre, m_norm_ffn_post, m_w_in, m_b_gate, m_conv_w, m_conv_b, m_conv_ln_g, m_conv_ln_b, m_w_a_out, m_sgu_ln_g, m_sgu_ln_b, m_w_spatial, m_b_spatial, m_w_b_out, m_w_o, m_w_gate_up, m_w_down, v_norm_mix_pre, v_norm_mix_post, v_norm_ffn_pre, v_norm_ffn_post, v_w_in, v_b_gate, v_conv_w, v_conv_b, v_conv_ln_g, v_conv_ln_b, v_w_a_out, v_sgu_ln_g, v_sgu_ln_b, v_w_spatial, v_b_spatial, v_w_b_out, v_w_o, v_w_gate_up, v_w_down):
    names = ("norm_mix_pre", "norm_mix_post", "norm_ffn_pre", "norm_ffn_post", "w_in", "b_gate", "conv_w", "conv_b",
             "conv_ln_g", "conv_ln_b", "w_a_out", "sgu_ln_g", "sgu_ln_b", "w_spatial", "b_spatial", "w_b_out", "w_o",
             "w_gate_up", "w_down")
    w = dict(zip(names, (norm_mix_pre, norm_mix_post, norm_ffn_pre, norm_ffn_post, w_in, b_gate, conv_w, conv_b,
                         conv_ln_g, conv_ln_b, w_a_out, sgu_ln_g, sgu_ln_b, w_spatial, b_spatial, w_b_out, w_o,
                         w_gate_up, w_down)))
    m = dict(zip(names, (m_norm_mix_pre, m_norm_mix_post, m_norm_ffn_pre, m_norm_ffn_post, m_w_in, m_b_gate, m_conv_w,
                         m_conv_b, m_conv_ln_g, m_conv_ln_b, m_w_a_out, m_sgu_ln_g, m_sgu_ln_b, m_w_spatial,
                         m_b_spatial, m_w_b_out, m_w_o, m_w_gate_up, m_w_down)))
    v = dict(zip(names, (v_norm_mix_pre, v_norm_mix_post, v_norm_ffn_pre, v_norm_ffn_post, v_w_in, v_b_gate, v_conv_w,
                         v_conv_b, v_conv_ln_g, v_conv_ln_b, v_w_a_out, v_sgu_ln_g, v_sgu_ln_b, v_w_spatial,
                         v_b_spatial, v_w_b_out, v_w_o, v_w_gate_up, v_w_down)))
    d = x.shape[-1]
    shard_cols = d // N_DEV

    def small_pack(bg, cw):
        rows = jnp.concatenate([bg, cw], axis=1).reshape(DEPTH * (2 + CONV_WIDTH), shard_cols)
        return jnp.pad(rows, ((0, (-rows.shape[0]) % 8), (0, 0)))

    small_w, small_m, small_v = (small_pack(t["b_gate"], t["conv_w"]) for t in (w, m, v))

    small_full, = _exchange("gather_small", [small_w], scatter=False)
    small_full = small_full[:, :DEPTH * (2 + CONV_WIDTH)].reshape(N_DEV, DEPTH, 2 + CONV_WIDTH, shard_cols)
    small_full = jnp.transpose(small_full, (1, 2, 0, 3)).reshape(DEPTH, 2 + CONV_WIDTH, d)
    b_gate_full = small_full[:, :2]
    conv_w_full = jnp.pad(small_full[:, 2:], ((0, 0), (0, CONV_PAD - CONV_WIDTH), (0, 0)))

    me = (4 * lax.axis_index("x") + 2 * lax.axis_index("y") + lax.axis_index("c")).astype(jnp.int32).reshape(1)
    first_level = (SIBLING,) + SAME_CORE_PEERS
    gathers, token = {}, small_full
    for l in range(DEPTH):
        lands = [fill_own_slot(f"cast_{name}", me, w[name], l, BF16) for name in MATRICES]
        handles, token = exchange_start(f"gather_start_{l}", lands, first_level, after=token)
        for name, handle in zip(MATRICES, handles):
            gathers[l, name] = handle
    use_order = [(l, name) for l in range(DEPTH) for name in MATRICES]
    forwards, gathered = {}, {}

    def start_forward(i, after):
        if i >= len(use_order) or use_order[i] in forwards:
            return None
        l, name = use_order[i]
        land = exchange_wait(f"gather_wait_{name}_{l}", gathers[l, name], first_level, after)
        (forwards[l, name],), tok = exchange_start(f"forward_start_{name}_{l}", [land], SAME_CORE_PEERS, forward=True)
        return tok

    def weight(l, name, after):
        if (l, name) not in gathered:
            i = use_order.index((l, name))
            start_forward(i, after)
            tok = start_forward(i + 1, after)
            full = exchange_wait(f"forward_wait_{name}_{l}", forwards[l, name], SAME_CORE_PEERS, after, forward=True)
            gathered[l, name] = full if name in ("w_in", "w_gate_up") else full.reshape(-1, d)
            return gathered[l, name], tok
        return gathered[l, name], None

    scatters = {}

    def emit(l, name, g):
        chunks = g if g.ndim == 3 else g.reshape(N_DEV, -1, d)
        land = fill_own_slot(f"own_grad_{name}", me, chunks, None, BF16)
        (scatters[l, name],), tok = exchange_start(f"scatter_start_{name}_{l}", [land], ALL_PEERS, srcs=[chunks])
        return tok

    rep = {name: w[name] for name in REPLICATED}
    loss, grad_x, grads = local_step(x[0], loss_target[0], rep, weight, emit, b_gate_full, conv_w_full, token)
    loss = lax.psum(loss, MESH_AXES)

    small_g = jnp.stack([jnp.concatenate([grads[l]["b_gate"], grads[l]["conv_w"][:CONV_WIDTH]], axis=0) for l in range(DEPTH)])
    small_g = jnp.transpose(small_g.reshape(DEPTH * (2 + CONV_WIDTH), N_DEV, shard_cols), (1, 0, 2))
    small_g = jnp.pad(small_g, ((0, 0), (0, small_w.shape[0] - small_g.shape[1]), (0, 0)))
    rep_shapes = [w[name].shape for name in REPLICATED]
    rep_g = _pack_rows([jnp.stack([grads[l][name].reshape(w[name].shape[1:]) for l in range(DEPTH)]) for name in REPLICATED])
    small_land = fill_own_slot("own_small", me, small_g, None, F32)
    (small_handle,), tok_small = exchange_start("scatter_start_small", [small_land], ALL_PEERS, srcs=[small_g])
    rep_land = fill_own_slot("own_replicated", me, rep_g[None], 0, F32)
    (rep_handle,), tok_rep = exchange_start("gather_start_replicated", [rep_land], ALL_PEERS, after=tok_small)

    out = {}
    after = tok_rep
    for name in ("w_down", "w_gate_up", "w_o", "w_a_out", "w_b_out", "w_in"):
        parts = [exchange_wait(f"scatter_wait_{name}_{l}", scatters[l, name], ALL_PEERS, after) for l in range(DEPTH)]
        out[name] = sum_adamw("adamw_" + name, parts, w[name], m[name], v[name], 128)
        after = out[name][0]
    small_parts = exchange_wait("scatter_wait_small", small_handle, ALL_PEERS, after)
    rep_parts = exchange_wait("gather_wait_replicated", rep_handle, ALL_PEERS, after)
    small_res = sum_adamw("adamw_small", [small_parts], small_w[None], small_m[None], small_v[None], small_w.shape[0])
    n_small = DEPTH * (2 + CONV_WIDTH)
    small_res = [r[0, :n_small].reshape(DEPTH, 2 + CONV_WIDTH, shard_cols) for r in small_res]
    out["b_gate"] = [r[:, :2] for r in small_res]
    out["conv_w"] = [r[:, 2:] for r in small_res]
    rep_res = sum_adamw("adamw_replicated", [rep_parts], *(_pack_rows([t[name] for name in REPLICATED])[None] for t in (w, m, v)), 672)
    rep_res = [_unpack_rows(r[0], rep_shapes) for r in rep_res]
    for i, name in enumerate(REPLICATED):
        out[name] = [r[i] for r in rep_res]

    return (loss, grad_x[None], *[out[name][0] for name in names], *[out[name][1] for name in names],
            *[out[name][2] for name in names], *[out[name][3] for name in names])
```

```python
import functools
import math

import jax
import jax.numpy as jnp
from jax import lax
from jax.experimental import pallas as pl
from jax.experimental.pallas import tpu as pltpu

F32 = jnp.float32
BF16 = jnp.bfloat16

DEPTH = 4
N_DEV = 8
EPS = 1e-6
CONV_WIDTH = 31
CONV_PAD = 32
CHUNK = 128
SGU_GROUPS = 8

ADAM_LR = 0.001
ADAM_B1 = 0.9
ADAM_B2 = 0.999
ADAM_EPS = 1e-08
ADAM_WD = 0.01
ADAM_STEP = 10

VMEM_BYTES_V7X = 64 * 1024 * 1024
VMEM_COMPILER_SLACK = 12 * 1024 * 1024
MESH_AXES = ("x", "y", "c")
ANY = pl.BlockSpec(memory_space=pl.ANY)


def _nbytes(shape, dtype):
    return math.prod(shape) * jnp.dtype(dtype).itemsize


def _params(block_bytes, ngrid, single_bytes=0):
    limit = min(2 * block_bytes + single_bytes + VMEM_COMPILER_SLACK, VMEM_BYTES_V7X - 4 * 1024 * 1024)
    return pltpu.CompilerParams(dimension_semantics=("arbitrary",) * ngrid, vmem_limit_bytes=int(limit))


def _in_hbm(a):
    return pltpu.with_memory_space_constraint(a, pltpu.HBM)


def _out_hbm(shape, dtype):
    return pltpu.HBM(tuple(shape), dtype)


def _deps(deps):
    return [t for t in deps if t is not None]


def _mm_body(dims, nk, kaxis, ndeps):
    def body(a_ref, b_ref, *rest):
        o_ref, *acc = rest[ndeps:]

        def prod():
            return lax.dot_general(a_ref[...], b_ref[...], (dims, ((), ())), preferred_element_type=F32)

        if nk == 1:
            o_ref[...] = prod().astype(o_ref.dtype)
            return
        acc_ref, = acc
        k = pl.program_id(kaxis)

        @pl.when(k == 0)
        def _():
            acc_ref[...] = prod()

        @pl.when(k > 0)
        def _():
            acc_ref[...] += prod()

        @pl.when(k == nk - 1)
        def _():
            o_ref[...] = acc_ref[...].astype(o_ref.dtype)

    return body


def _mm_call(name, a, b, dims, grid, a_spec, b_spec, o_spec, out_shape, out_dtype, nk, kaxis, acc_shape, deps=()):
    deps = _deps(deps)
    blocks = (_nbytes([d for d in a_spec.block_shape if d], a.dtype) + _nbytes([d for d in b_spec.block_shape if d], b.dtype)
              + _nbytes([d for d in o_spec.block_shape if d], out_dtype))
    scratch = [pltpu.VMEM(acc_shape, F32)] if nk > 1 else []
    acc_bytes = _nbytes(acc_shape, F32) * (2 if nk > 1 else 1)
    return pl.pallas_call(
        _mm_body(dims, nk, kaxis, len(deps)), name=name, grid=grid, in_specs=[a_spec, b_spec] + [ANY] * len(deps),
        out_specs=o_spec, out_shape=_out_hbm(out_shape, out_dtype), scratch_shapes=scratch,
        compiler_params=_params(blocks, len(grid), acc_bytes),
    )(_in_hbm(a), _in_hbm(b), *deps)


def mm_nn(name, a, b_and_token, out_dtype, tm, tn=None, tk=None):
    b, token = b_and_token
    m, k = a.shape
    tm = min(tm, m)
    if b.ndim == 3:
        nblk, _, nb = b.shape
        return _mm_call(name, a, b, ((1,), (0,)), (nblk, m // tm),
                        pl.BlockSpec((tm, k), lambda j, i: (i, 0)), pl.BlockSpec((None, k, nb), lambda j, i: (j, 0, 0)),
                        pl.BlockSpec((tm, nb), lambda j, i: (i, j)), (m, nblk * nb), out_dtype, 1, 0, (tm, nb), [token])
    n = b.shape[1]
    tn = tn or n
    tk = tk or k
    nk = k // tk
    return _mm_call(name, a, b, ((1,), (0,)), (n // tn, m // tm, nk),
                    pl.BlockSpec((tm, tk), lambda j, i, kk: (i, kk)), pl.BlockSpec((tk, tn), lambda j, i, kk: (kk, j)),
                    pl.BlockSpec((tm, tn), lambda j, i, kk: (i, j)), (m, n), out_dtype, nk, 2, (tm, tn), [token])


def mm_nt(name, a, b, out_dtype, tm, tn=None, deps=()):
    m = a.shape[0]
    tm = min(tm, m)
    if b.ndim == 3:
        kblk, n, kb = b.shape
        return _mm_call(name, a, b, ((1,), (1,)), (m // tm, kblk),
                        pl.BlockSpec((tm, kb), lambda i, kk: (i, kk)), pl.BlockSpec((None, n, kb), lambda i, kk: (kk, 0, 0)),
                        pl.BlockSpec((tm, n), lambda i, kk: (i, 0)), (m, n), out_dtype, kblk, 1, (tm, n), deps)
    n, kc = b.shape
    tn = tn or n
    return _mm_call(name, a, b, ((1,), (1,)), (n // tn, m // tm),
                    pl.BlockSpec((tm, kc), lambda j, i: (i, 0)), pl.BlockSpec((tn, kc), lambda j, i: (j, 0)),
                    pl.BlockSpec((tm, tn), lambda j, i: (i, j)), (m, n), out_dtype, 1, 0, (tm, tn), deps)


def mm_tn(name, a, b, out_dtype, tm, tr, nb=None):
    m, k = a.shape
    n = b.shape[1]
    tm = min(tm, m)
    nm = m // tm
    if nb is not None:
        return _mm_call(name, a, b, ((0,), (0,)), (n // nb, k // tr, nm),
                        pl.BlockSpec((tm, tr), lambda j, r, mm: (mm, r)), pl.BlockSpec((tm, nb), lambda j, r, mm: (mm, j)),
                        pl.BlockSpec((None, tr, nb), lambda j, r, mm: (j, r, 0)), (n // nb, k, nb), out_dtype, nm, 2, (tr, nb))
    return _mm_call(name, a, b, ((0,), (0,)), (k // tr, nm),
                    pl.BlockSpec((tm, tr), lambda r, mm: (mm, r)), pl.BlockSpec((tm, n), lambda r, mm: (mm, 0)),
                    pl.BlockSpec((tr, n), lambda r, mm: (r, 0)), (k, n), out_dtype, nm, 1, (tr, n))


def _row_call(name, body, grid, in_specs, out_specs, out_shape, arrays, scratch=(), aliases=None, vmem_blocks=0, deps=()):
    deps = _deps(deps)
    nin = len(arrays)

    def with_deps(*refs):
        body(*refs[:nin], *refs[nin + len(deps):])

    single = not isinstance(out_shape, (list, tuple))
    outs = [_out_hbm(o.shape, o.dtype) for o in ([out_shape] if single else out_shape)]
    return pl.pallas_call(
        with_deps, name=name, grid=grid, in_specs=list(in_specs) + [ANY] * len(deps), out_specs=out_specs,
        out_shape=outs[0] if single else outs, scratch_shapes=list(scratch), input_output_aliases=aliases or {},
        compiler_params=_params(vmem_blocks, len(grid)),
    )(*[_in_hbm(a) for a in arrays], *deps)


def _rows(tm, d, col=0):
    return pl.BlockSpec((tm, d), lambda i, *_: (i, col))


def _vec(d):
    return pl.BlockSpec((1, d), lambda *_: (0, 0))


def _rstd(x):
    return lax.rsqrt(jnp.mean(x * x, axis=-1, keepdims=True) + EPS)


def _rms_bwd(dy, x, g):
    r = _rstd(x)
    n = x * r
    w = dy * g
    dx = r * (w - n * jnp.mean(w * n, axis=-1, keepdims=True))
    return dx, jnp.sum(dy * n, axis=0, keepdims=True)


def _accumulate(ref, value, first):
    @pl.when(first)
    def _():
        ref[...] = value

    @pl.when(jnp.logical_not(first))
    def _():
        ref[...] += value


def rms_fwd(x, g, tm=256, deps=()):
    s, d = x.shape
    tm = min(tm, s)

    def body(x_ref, g_ref, h_ref):
        xv = x_ref[...]
        h_ref[...] = (xv * _rstd(xv) * g_ref[...]).astype(BF16)

    return _row_call("rms_fwd", body, (s // tm,), [_rows(tm, d), _vec(d)], _rows(tm, d),
                     jax.ShapeDtypeStruct((s, d), BF16), (x, g), vmem_blocks=tm * d * 6, deps=deps)


def norm_res(x_in, o, g_post, g_next, tm=256):
    s, d = x_in.shape
    tm = min(tm, s)

    def body(x_ref, o_ref, gp_ref, gn_ref, xo_ref, h_ref):
        ov = o_ref[...]
        xo = x_ref[...] + (ov * _rstd(ov) * gp_ref[...])
        xo_ref[...] = xo
        h_ref[...] = (xo * _rstd(xo) * gn_ref[...]).astype(BF16)

    return _row_call("norm_res", body, (s // tm,), [_rows(tm, d), _rows(tm, d), _vec(d), _vec(d)],
                     [_rows(tm, d), _rows(tm, d)],
                     [jax.ShapeDtypeStruct((s, d), F32), jax.ShapeDtypeStruct((s, d), BF16)],
                     (x_in, o, g_post, g_next), vmem_blocks=tm * d * 14)


def final_norm_loss(x_in, o, g_post, target, tm=256):
    s, d = x_in.shape
    tm = min(tm, s)

    def body(x_ref, o_ref, gp_ref, t_ref, loss_ref, dy_ref, do_ref, dg_ref):
        first = pl.program_id(0) == 0
        ov = o_ref[...]
        g = gp_ref[...]
        diff = x_ref[...] + (ov * _rstd(ov) * g) - t_ref[...]
        _accumulate(loss_ref, jnp.sum(diff * diff, axis=0, keepdims=True), first)
        dy = diff * (1.0 / d)
        dy_ref[...] = dy
        do, dg = _rms_bwd(dy, ov, g)
        do_ref[...] = do.astype(BF16)
        _accumulate(dg_ref, dg, first)

    return _row_call("final_norm_loss", body, (s // tm,), [_rows(tm, d), _rows(tm, d), _vec(d), _rows(tm, d)],
                     [_vec(d), _rows(tm, d), _rows(tm, d), _vec(d)],
                     [jax.ShapeDtypeStruct((1, d), F32), jax.ShapeDtypeStruct((s, d), F32),
                      jax.ShapeDtypeStruct((s, d), BF16), jax.ShapeDtypeStruct((1, d), F32)],
                     (x_in, o, g_post, target), vmem_blocks=tm * d * 18)


def norm_bwd_in_out(dx_out, dh, x_in, g_pre, o_below, g_post_below, tm=256, deps=()):
    s, d = x_in.shape
    tm = min(tm, s)

    def body(dxo_ref, dh_ref, x_ref, g_ref, o_ref, gb_ref, dxi_ref, do_ref, dg_ref, dgb_ref):
        first = pl.program_id(0) == 0
        dx, dg = _rms_bwd(dh_ref[...], x_ref[...], g_ref[...])
        dxi = dxo_ref[...] + dx
        dxi_ref[...] = dxi
        _accumulate(dg_ref, dg, first)
        do, dgb = _rms_bwd(dxi, o_ref[...], gb_ref[...])
        do_ref[...] = do.astype(BF16)
        _accumulate(dgb_ref, dgb, first)

    return _row_call("norm_bwd_in_out", body, (s // tm,),
                     [_rows(tm, d), _rows(tm, d), _rows(tm, d), _vec(d), _rows(tm, d), _vec(d)],
                     [_rows(tm, d), _rows(tm, d), _vec(d), _vec(d)],
                     [jax.ShapeDtypeStruct((s, d), F32), jax.ShapeDtypeStruct((s, d), BF16),
                      jax.ShapeDtypeStruct((1, d), F32), jax.ShapeDtypeStruct((1, d), F32)],
                     (dx_out, dh, x_in, g_pre, o_below, g_post_below), vmem_blocks=tm * d * 22, deps=deps)


def norm_bwd_in(dx_out, dh, x_in, g_pre, tm=256):
    s, d = x_in.shape
    tm = min(tm, s)

    def body(dxo_ref, dh_ref, x_ref, g_ref, dxi_ref, dg_ref):
        dx, dg = _rms_bwd(dh_ref[...], x_ref[...], g_ref[...])
        dxi_ref[...] = dxo_ref[...] + dx
        _accumulate(dg_ref, dg, pl.program_id(0) == 0)

    return _row_call("norm_bwd_in", body, (s // tm,), [_rows(tm, d), _rows(tm, d), _rows(tm, d), _vec(d)],
                     [_rows(tm, d), _vec(d)],
                     [jax.ShapeDtypeStruct((s, d), F32), jax.ShapeDtypeStruct((1, d), F32)],
                     (dx_out, dh, x_in, g_pre), vmem_blocks=tm * d * 16)


CONV_COLS = 256
CONV_ROWS = 32


SUBLANES = 8


def _shifted_copies(ext_ref, sh_ref):
    n = sh_ref.shape[1]
    for r in range(SUBLANES):
        sh_ref[r] = ext_ref[r:r + n, :]


def _window(sh_ref, off):
    return sh_ref[off % SUBLANES, off - off % SUBLANES:off - off % SUBLANES + CONV_ROWS, :]


def _depthwise(sh_ref, w, n_out, in_off, flip, emit):
    for r0 in range(0, n_out, CONV_ROWS):
        acc = None
        for k in range(CONV_WIDTH):
            term = w[k:k + 1, :] * _window(sh_ref, r0 + in_off + (CONV_WIDTH - 1 - k if flip else k))
            acc = term if acc is None else acc + term
        emit(r0, acc)


def _ext_scratch(tm):
    return [pltpu.VMEM((tm + CONV_PAD + SUBLANES, CONV_COLS), F32), pltpu.VMEM((SUBLANES, tm + CONV_PAD, CONV_COLS), F32)]


def glu_conv_fwd(proj, conv_w, conv_b, c_ch, tm=512):
    s = proj.shape[0]
    tm = min(tm, s)
    ncb = c_ch // CONV_COLS
    hb = tm // CONV_PAD

    def body(a_ref, g_ref, ah_ref, gh_ref, w_ref, b_ref, c_ref, ext_ref, sh_ref):
        i = pl.program_id(1)
        halo = ah_ref[...].astype(F32) * jax.nn.sigmoid(gh_ref[...].astype(F32))
        ext_ref[0:CONV_PAD, :] = jnp.where(i > 0, halo, 0.0)
        ext_ref[CONV_PAD:CONV_PAD + tm, :] = a_ref[...].astype(F32) * jax.nn.sigmoid(g_ref[...].astype(F32))
        ext_ref[CONV_PAD + tm:, :] = jnp.zeros((SUBLANES, CONV_COLS), F32)
        _shifted_copies(ext_ref, sh_ref)
        w = w_ref[...]
        bias = b_ref[...]

        def emit(r0, acc):
            c_ref[r0:r0 + CONV_ROWS, :] = acc + bias

        _depthwise(sh_ref, w, tm, CONV_PAD - (CONV_WIDTH - 1), False, emit)

    main = lambda col0: pl.BlockSpec((tm, CONV_COLS), lambda c, i: (i, col0 + c))
    halo = lambda col0: pl.BlockSpec((CONV_PAD, CONV_COLS), lambda c, i: (jnp.maximum(i * hb - 1, 0), col0 + c))
    return _row_call("glu_conv_fwd", body, (ncb, s // tm),
                     [main(0), main(ncb), halo(0), halo(ncb),
                      pl.BlockSpec((CONV_PAD, CONV_COLS), lambda c, i: (0, c)), pl.BlockSpec((1, CONV_COLS), lambda c, i: (0, c))],
                     pl.BlockSpec((tm, CONV_COLS), lambda c, i: (i, c)), jax.ShapeDtypeStruct((s, c_ch), F32),
                     (proj, proj, proj, proj, conv_w, conv_b),
                     scratch=_ext_scratch(tm), vmem_blocks=tm * CONV_COLS * 32)


def _layer_norm_stats(x):
    mu = jnp.mean(x, axis=-1, keepdims=True)
    xc = x - mu
    rstd = lax.rsqrt(jnp.mean(xc * xc, axis=-1, keepdims=True) + EPS)
    return xc * rstd, rstd


def _layer_norm_bwd(dy, xhat, rstd, g):
    dxh = dy * g
    return rstd * (dxh - jnp.mean(dxh, axis=-1, keepdims=True) - xhat * jnp.mean(dxh * xhat, axis=-1, keepdims=True))


def ln_silu_fwd(c, ln_g, ln_b, tm=256):
    s, d = c.shape
    tm = min(tm, s)

    def body(c_ref, g_ref, b_ref, s_ref):
        xhat, _ = _layer_norm_stats(c_ref[...])
        s_ref[...] = jax.nn.silu(xhat * g_ref[...] + b_ref[...]).astype(BF16)

    return _row_call("ln_silu_fwd", body, (s // tm,), [_rows(tm, d), _vec(d), _vec(d)], _rows(tm, d),
                     jax.ShapeDtypeStruct((s, d), BF16), (c, ln_g, ln_b), vmem_blocks=tm * d * 10)


def ln_silu_bwd(c, ds, ln_g, ln_b, tm=256, deps=()):
    s, d = c.shape
    tm = min(tm, s)

    def body(c_ref, ds_ref, g_ref, b_ref, dc_ref, dg_ref, db_ref, dcb_ref):
        first = pl.program_id(0) == 0
        g = g_ref[...]
        xhat, rstd = _layer_norm_stats(c_ref[...])
        y = xhat * g + b_ref[...]
        sg = jax.nn.sigmoid(y)
        dln = ds_ref[...] * (sg * (1.0 + y * (1.0 - sg)))
        _accumulate(dg_ref, jnp.sum(dln * xhat, axis=0, keepdims=True), first)
        _accumulate(db_ref, jnp.sum(dln, axis=0, keepdims=True), first)
        dc = _layer_norm_bwd(dln, xhat, rstd, g)
        dc_ref[...] = dc
        _accumulate(dcb_ref, jnp.sum(dc, axis=0, keepdims=True), first)

    vec = jax.ShapeDtypeStruct((1, d), F32)
    return _row_call("ln_silu_bwd", body, (s // tm,), [_rows(tm, d), _rows(tm, d), _vec(d), _vec(d)],
                     [_rows(tm, d), _vec(d), _vec(d), _vec(d)], [jax.ShapeDtypeStruct((s, d), F32), vec, vec, vec],
                     (c, ds, ln_g, ln_b), vmem_blocks=tm * d * 20, deps=deps)


def conv_bwd(proj, dc, conv_w, c_ch, tm=512):
    s = proj.shape[0]
    tm = min(tm, s)
    ncb = c_ch // CONV_COLS
    hb = tm // CONV_PAD
    last_halo = s // CONV_PAD - 1
    n_i = s // tm

    def body(a_ref, g_ref, ah_ref, gh_ref, dc_ref, dcn_ref, w_ref, dglu_ref, dw_ref,
             ext_ref, sh_ref, dce_ref, dsh_ref, dwacc_ref):
        i = pl.program_id(1)
        zeros = jnp.zeros((SUBLANES, CONV_COLS), F32)
        halo = ah_ref[...].astype(F32) * jax.nn.sigmoid(gh_ref[...].astype(F32))
        ext_ref[0:CONV_PAD, :] = jnp.where(i > 0, halo, 0.0)
        ext_ref[CONV_PAD:CONV_PAD + tm, :] = a_ref[...].astype(F32) * jax.nn.sigmoid(g_ref[...].astype(F32))
        ext_ref[CONV_PAD + tm:, :] = zeros
        _shifted_copies(ext_ref, sh_ref)
        dce_ref[0:tm, :] = dc_ref[...]
        dce_ref[tm:tm + CONV_PAD, :] = jnp.where(i < n_i - 1, dcn_ref[...], 0.0)
        dce_ref[tm + CONV_PAD:, :] = zeros
        _shifted_copies(dce_ref, dsh_ref)
        w = w_ref[...]

        def emit(r0, acc):
            dglu_ref[r0:r0 + CONV_ROWS, :] = acc

        _depthwise(dsh_ref, w, tm, 0, True, emit)

        for k in range(CONV_WIDTH):
            acc = None
            for r0 in range(0, tm, CONV_ROWS):
                term = dce_ref[r0:r0 + CONV_ROWS, :] * _window(sh_ref, r0 + CONV_PAD - (CONV_WIDTH - 1) + k)
                acc = term if acc is None else acc + term
            dwacc_ref[k:k + 1, :] = jnp.sum(acc, axis=0, keepdims=True)
        dwacc_ref[CONV_WIDTH:, :] = jnp.zeros((CONV_PAD - CONV_WIDTH, CONV_COLS), F32)
        _accumulate(dw_ref, dwacc_ref[...], i == 0)

    main = lambda col0: pl.BlockSpec((tm, CONV_COLS), lambda c, i: (i, col0 + c))
    halo = lambda col0: pl.BlockSpec((CONV_PAD, CONV_COLS), lambda c, i: (jnp.maximum(i * hb - 1, 0), col0 + c))
    nxt = pl.BlockSpec((CONV_PAD, CONV_COLS), lambda c, i: (jnp.minimum((i + 1) * hb, last_halo), c))
    wspec = pl.BlockSpec((CONV_PAD, CONV_COLS), lambda c, i: (0, c))
    return _row_call("conv_bwd", body, (ncb, n_i),
                     [main(0), main(ncb), halo(0), halo(ncb), main(0), nxt, wspec],
                     [main(0), wspec],
                     [jax.ShapeDtypeStruct((s, c_ch), F32), jax.ShapeDtypeStruct((CONV_PAD, c_ch), F32)],
                     (proj, proj, proj, proj, dc, dc, conv_w),
                     scratch=_ext_scratch(tm) + _ext_scratch(tm) + [pltpu.VMEM((CONV_PAD, CONV_COLS), F32)],
                     vmem_blocks=tm * CONV_COLS * 56)


ELEMENTWISE_COLS = 512


def _col_chunks(d):
    return [slice(c0, c0 + ELEMENTWISE_COLS) for c0 in range(0, d, ELEMENTWISE_COLS)]


def _pair_spec(tm, d, pair):
    return pl.BlockSpec((tm, 2 * d), lambda i: (i, pair))


def glu_bwd(dproj, dglu, proj, c_ch, tm=256):
    s = proj.shape[0]
    tm = min(tm, s)

    def body(_, dglu_ref, a_ref, g_ref, out_ref):
        for cols in _col_chunks(c_ch):
            dg = dglu_ref[:, cols]
            sg = jax.nn.sigmoid(g_ref[:, cols].astype(F32))
            out_ref[:, cols] = (dg * sg).astype(BF16)
            out_ref[:, c_ch + cols.start:c_ch + cols.stop] = (dg * a_ref[:, cols].astype(F32) * (sg * (1.0 - sg))).astype(BF16)

    return _row_call("glu_bwd", body, (s // tm,), [ANY, _rows(tm, c_ch), _rows(tm, c_ch), _rows(tm, c_ch, 1)],
                     _pair_spec(tm, c_ch, 0), jax.ShapeDtypeStruct(dproj.shape, BF16), (dproj, dglu, proj, proj),
                     aliases={0: 0}, vmem_blocks=tm * c_ch * 12)


_SQRT_HALF = 0.7071067811865476
_INV_SQRT_2PI = 0.3989422804014327


def _gelu_parts(x):
    cdf = 0.5 * (1.0 + lax.erf(x * _SQRT_HALF))
    return cdf, x * cdf


def _gelu_grad(x, cdf):
    return cdf + x * (_INV_SQRT_2PI * jnp.exp(-0.5 * x * x))


def _sgu_specs(tm, ch):
    grp = ch // SGU_GROUPS
    full3 = lambda shape: pl.BlockSpec(shape, lambda *_: (0, 0, 0))
    return grp, full3((SGU_GROUPS, CHUNK, CHUNK)), full3((SGU_GROUPS, CHUNK, grp))


def sgu_fwd(proj, ln_g, ln_b, w_mix, b_mix, ch, col0, tm=CHUNK):
    s = proj.shape[0]
    grp, wspec, bspec = _sgu_specs(tm, ch)

    def body(u_ref, v_ref, g_ref, b_ref, w_ref, bm_ref, p_ref, mix_ref):
        _, u = _gelu_parts(u_ref[...].astype(F32))
        _, v0 = _gelu_parts(v_ref[...].astype(F32))
        xhat, _ = _layer_norm_stats(v0)
        vn = (xhat * g_ref[...] + b_ref[...]).astype(BF16)
        for n in range(tm // CHUNK):
            for g in range(SGU_GROUPS):
                blk = vn[n * CHUNK:(n + 1) * CHUNK, g * grp:(g + 1) * grp]
                mix_ref[n * CHUNK:(n + 1) * CHUNK, g * grp:(g + 1) * grp] = (
                    jnp.dot(w_ref[g], blk, preferred_element_type=F32) + bm_ref[g])
        p_ref[...] = (u * mix_ref[...]).astype(BF16)

    return _row_call("sgu_fwd", body, (s // tm,),
                     [_rows(tm, ch, col0), _rows(tm, ch, col0 + 1), _vec(ch), _vec(ch), wspec, bspec], _rows(tm, ch),
                     jax.ShapeDtypeStruct((s, ch), BF16), (proj, proj, ln_g, ln_b, w_mix, b_mix),
                     scratch=[pltpu.VMEM((tm, ch), F32)], vmem_blocks=tm * ch * 30)


def sgu_bwd(dproj, proj, dp, ln_g, ln_b, w_mix, w_mix_t, b_mix, ch, col0, tm=CHUNK):
    s = proj.shape[0]
    assert tm == CHUNK and col0 % 2 == 0
    grp, wspec, bspec = _sgu_specs(tm, ch)
    groups = [slice(k * grp, (k + 1) * grp) for k in range(SGU_GROUPS)]

    def body(_, u_ref, v_ref, dp_ref, g_ref, b_ref, w_ref, wt_ref, bm_ref,
             out_ref, dw_ref, dbm_ref, dg_ref, db_ref, u_s, gu_s, gv_s, xh_s, dvn_s):
        first = pl.program_id(0) == 0
        row_sum = lambda x: jnp.sum(x, axis=1, keepdims=True)
        total = None
        for cols in groups:
            ub = u_ref[:, cols].astype(F32)
            vb = v_ref[:, cols].astype(F32)
            cdf_u, u = _gelu_parts(ub)
            cdf_v, v0 = _gelu_parts(vb)
            u_s[:, cols] = u
            gu_s[:, cols] = _gelu_grad(ub, cdf_u)
            gv_s[:, cols] = _gelu_grad(vb, cdf_v)
            xh_s[:, cols] = v0
            total = row_sum(v0) if total is None else total + row_sum(v0)
        mu = total * (1.0 / ch)
        total = None
        for cols in groups:
            xc = xh_s[:, cols] - mu
            xh_s[:, cols] = xc
            total = row_sum(xc * xc) if total is None else total + row_sum(xc * xc)
        rstd = lax.rsqrt(total * (1.0 / ch) + EPS)

        t1 = t2 = None
        for k, cols in enumerate(groups):
            g = g_ref[:, cols]
            xhat = xh_s[:, cols] * rstd
            xh_s[:, cols] = xhat
            vn = (xhat * g + b_ref[:, cols]).astype(BF16)
            dpk = dp_ref[:, cols]
            dmix = dpk * u_s[:, cols]
            dmix_bf = dmix.astype(BF16)
            mixed = jnp.dot(w_ref[k], vn, preferred_element_type=F32) + bm_ref[k]
            out_ref[:, cols] = (dpk * mixed * gu_s[:, cols]).astype(BF16)
            dvn = jnp.dot(wt_ref[k], dmix_bf, preferred_element_type=F32)
            dvn_s[:, cols] = dvn
            _accumulate(dw_ref.at[k], lax.dot_general(dmix_bf, vn, (((1,), (1,)), ((), ())), preferred_element_type=F32), first)
            _accumulate(dbm_ref.at[k], jnp.broadcast_to(row_sum(dmix), (CHUNK, CHUNK)), first)
            _accumulate(dg_ref.at[:, cols], jnp.sum(dvn * xhat, axis=0, keepdims=True), first)
            _accumulate(db_ref.at[:, cols], jnp.sum(dvn, axis=0, keepdims=True), first)
            dxh = dvn * g
            t1 = row_sum(dxh) if t1 is None else t1 + row_sum(dxh)
            t2 = row_sum(dxh * xhat) if t2 is None else t2 + row_sum(dxh * xhat)
        m1 = t1 * (1.0 / ch)
        m2 = t2 * (1.0 / ch)
        for cols in groups:
            dv0 = rstd * (dvn_s[:, cols] * g_ref[:, cols] - m1 - xh_s[:, cols] * m2)
            out_ref[:, ch + cols.start:ch + cols.stop] = (dv0 * gv_s[:, cols]).astype(BF16)

    vec = _vec(ch)
    acc3 = lambda: pl.BlockSpec((SGU_GROUPS, CHUNK, CHUNK), lambda i: (0, 0, 0))
    vshape = jax.ShapeDtypeStruct((1, ch), F32)
    mshape = jax.ShapeDtypeStruct((SGU_GROUPS, CHUNK, CHUNK), F32)
    return _row_call("sgu_bwd", body, (s // tm,),
                     [ANY, _rows(tm, ch, col0), _rows(tm, ch, col0 + 1), _rows(tm, ch), vec, vec, wspec, wspec, bspec],
                     [_pair_spec(tm, ch, col0 // 2), acc3(), acc3(), vec, vec],
                     [jax.ShapeDtypeStruct(dproj.shape, BF16), mshape, mshape, vshape, vshape],
                     (dproj, proj, proj, dp, ln_g, ln_b, w_mix, w_mix_t, b_mix),
                     scratch=[pltpu.VMEM((tm, ch), F32)] * 5, aliases={0: 0}, vmem_blocks=tm * ch * 40)


def branches_merge(s_act, p_act, wa_and_token, wb_and_token, proj, b_gate, col0, tm=256, tn=1024):
    (wa, tok_a), (wb, tok_b) = wa_and_token, wb_and_token
    deps = _deps([tok_a, tok_b])
    s, d = s_act.shape
    tm = min(tm, s)
    per = d // tn

    def body(s_ref, p_ref, wa_ref, wb_ref, l0_ref, l1_ref, bg_ref, *rest):
        ya_ref, yb_ref, m_ref = rest[len(deps):]
        ya = jnp.dot(s_ref[...], wa_ref[...], preferred_element_type=F32)
        yb = jnp.dot(p_ref[...], wb_ref[...], preferred_element_type=F32)
        ya_ref[...] = ya
        yb_ref[...] = yb
        g0 = jax.nn.sigmoid(l0_ref[...].astype(F32) + bg_ref[0:1, :])
        g1 = jax.nn.sigmoid(l1_ref[...].astype(F32) + bg_ref[1:2, :])
        m_ref[...] = (g0 * ya + g1 * yb).astype(BF16)

    act = pl.BlockSpec((tm, d), lambda j, i: (i, 0))
    wgt = pl.BlockSpec((d, tn), lambda j, i: (0, j))
    logits = lambda col: pl.BlockSpec((tm, tn), lambda j, i: (i, col * per + j))
    oblk = pl.BlockSpec((tm, tn), lambda j, i: (i, j))
    return pl.pallas_call(
        body, name="branches_merge", grid=(per, s // tm),
        in_specs=[act, act, wgt, wgt, logits(col0), logits(col0 + 1), pl.BlockSpec((2, tn), lambda j, i: (0, j))] + [ANY] * len(deps),
        out_specs=[oblk, oblk, oblk],
        out_shape=[_out_hbm((s, d), F32), _out_hbm((s, d), F32), _out_hbm((s, d), BF16)],
        compiler_params=_params(2 * tm * d * 2 + 2 * d * tn * 2 + 2 * tm * tn * 2 + tm * tn * 10, 2, 4 * tm * tn * 4),
    )(*[_in_hbm(a) for a in (s_act, p_act, wa, wb, proj, proj, b_gate)], *deps)


def merge_bwd(dm, y_a, y_b, proj, b_gate, col0, tm=256, deps=()):
    s, d = y_a.shape
    tm = min(tm, s)

    assert col0 % 2 == 0

    def body(dm_ref, ya_ref, yb_ref, l0_ref, l1_ref, bg_ref, dya_ref, dyb_ref, out_ref, dbg_ref):
        first = pl.program_id(0) == 0
        for cols in _col_chunks(d):
            dmv = dm_ref[:, cols]
            g0 = jax.nn.sigmoid(l0_ref[:, cols].astype(F32) + bg_ref[0:1, cols])
            g1 = jax.nn.sigmoid(l1_ref[:, cols].astype(F32) + bg_ref[1:2, cols])
            dya_ref[:, cols] = (dmv * g0).astype(BF16)
            dyb_ref[:, cols] = (dmv * g1).astype(BF16)
            dl0 = dmv * ya_ref[:, cols] * (g0 * (1.0 - g0))
            dl1 = dmv * yb_ref[:, cols] * (g1 * (1.0 - g1))
            _accumulate(dbg_ref.at[0:1, cols], jnp.sum(dl0, axis=0, keepdims=True), first)
            _accumulate(dbg_ref.at[1:2, cols], jnp.sum(dl1, axis=0, keepdims=True), first)
            out_ref[:, cols] = dl0.astype(BF16)
            out_ref[:, d + cols.start:d + cols.stop] = dl1.astype(BF16)

    bgspec = pl.BlockSpec((2, d), lambda i: (0, 0))
    return _row_call("merge_bwd", body, (s // tm,),
                     [_rows(tm, d), _rows(tm, d), _rows(tm, d), _rows(tm, d, col0), _rows(tm, d, col0 + 1), bgspec],
                     [_rows(tm, d), _rows(tm, d), _pair_spec(tm, d, col0 // 2), bgspec],
                     [jax.ShapeDtypeStruct((s, d), BF16), jax.ShapeDtypeStruct((s, d), BF16),
                      jax.ShapeDtypeStruct(proj.shape, BF16), jax.ShapeDtypeStruct((2, d), F32)],
                     (dm, y_a, y_b, proj, proj, b_gate), vmem_blocks=tm * d * 24, deps=deps)


def gate_up_swiglu(h, w_and_token, tm=256):
    w, token = w_and_token
    deps = _deps([token])
    s, d = h.shape
    nblk, _, nb = w.shape
    half = nblk // 2
    tm = min(tm, s)

    def body(h_ref, wg_ref, wu_ref, *rest):
        g_ref, u_ref, f_ref = rest[len(deps):]
        hv = h_ref[...]
        g = jnp.dot(hv, wg_ref[...], preferred_element_type=F32)
        u = jnp.dot(hv, wu_ref[...], preferred_element_type=F32)
        g_ref[...] = g.astype(BF16)
        u_ref[...] = u.astype(BF16)
        f_ref[...] = (jax.nn.silu(g) * u).astype(BF16)

    out = _out_hbm((s, half * nb), BF16)
    oblk = pl.BlockSpec((tm, nb), lambda j, i: (i, j))
    return pl.pallas_call(
        body, name="gate_up_swiglu", grid=(half, s // tm),
        in_specs=[pl.BlockSpec((tm, d), lambda j, i: (i, 0)), pl.BlockSpec((None, d, nb), lambda j, i: (j, 0, 0)),
                  pl.BlockSpec((None, d, nb), lambda j, i: (j + half, 0, 0))] + [ANY] * len(deps),
        out_specs=[oblk, oblk, oblk], out_shape=[out, out, out],
        compiler_params=_params(tm * d * 2 + 2 * d * nb * 2 + 3 * tm * nb * 2, 2, 4 * tm * nb * 4),
    )(_in_hbm(h), _in_hbm(w), _in_hbm(w), *deps)


def swiglu_bwd(g_act, u_act, df, tm=256, deps=()):
    s, half = g_act.shape
    w2 = 2 * half
    tm = min(tm, s)
    chunk = w2 // N_DEV

    def body(g_ref, u_ref, df_ref, out_ref):
        for c0 in range(0, half, chunk):
            cols = slice(c0, c0 + chunk)
            g = g_ref[:, cols].astype(F32)
            sg = jax.nn.sigmoid(g)
            dfv = df_ref[:, cols].astype(F32)
            out_ref[:, cols] = (dfv * u_ref[:, cols].astype(F32) * (sg * (1.0 + g * (1.0 - sg)))).astype(BF16)
            out_ref[:, half + c0:half + c0 + chunk] = (dfv * (g * sg)).astype(BF16)

    return _row_call("swiglu_bwd", body, (s // tm,), [_rows(tm, half), _rows(tm, half), _rows(tm, half)],
                     _rows(tm, w2), jax.ShapeDtypeStruct((s, w2), BF16), (g_act, u_act, df), vmem_blocks=tm * w2 * 5, deps=deps)


def _peers():
    x, y, c = lax.axis_index("x"), lax.axis_index("y"), lax.axis_index("c")
    me = 4 * x + 2 * y + c
    peers = []
    for k in range(1, N_DEV):
        px = 1 - x if k & 4 else x
        py = 1 - y if k & 2 else y
        pc = 1 - c if k & 1 else c
        peers.append(((px, py, pc), 4 * px + 2 * py + pc))
    return me, peers


def _exchange(name, arrays, scatter):
    n = len(arrays)

    def body(*refs):
        ins, outs = refs[:n], refs[n:2 * n]
        send_sems, recv_sems, local_sems = refs[2 * n:]
        me, peers = _peers()

        def remote(a, k):
            (pos, idx) = peers[k]
            src = ins[a].at[idx] if scatter else ins[a]
            return pltpu.make_async_remote_copy(src_ref=src, dst_ref=outs[a].at[me], send_sem=send_sems.at[a, k],
                                                recv_sem=recv_sems.at[a, k], device_id=pos, device_id_type=pl.DeviceIdType.MESH)

        def arrival(a, k):
            (pos, idx) = peers[k]
            src = ins[a].at[idx] if scatter else ins[a]
            return pltpu.make_async_remote_copy(src_ref=src, dst_ref=outs[a].at[idx], send_sem=send_sems.at[a, k],
                                                recv_sem=recv_sems.at[a, k], device_id=pos, device_id_type=pl.DeviceIdType.MESH)

        local = [pltpu.make_async_copy(ins[a].at[me] if scatter else ins[a], outs[a].at[me], local_sems.at[a]) for a in range(n)]
        sends = [remote(a, k) for k in range(N_DEV - 1) for a in range(n)]
        for cp in sends:
            cp.start()
        for cp in local:
            cp.start()
        for k in range(N_DEV - 1):
            for a in range(n):
                arrival(a, k).wait_recv()
        for cp in sends:
            cp.wait_send()
        for cp in local:
            cp.wait()

    out_shape = [jax.ShapeDtypeStruct(a.shape if scatter else (N_DEV,) + a.shape, a.dtype) for a in arrays]
    return pl.pallas_call(
        body, name=name, in_specs=[ANY] * n, out_specs=[ANY] * n, out_shape=out_shape,
        scratch_shapes=[pltpu.SemaphoreType.DMA((n, N_DEV - 1)), pltpu.SemaphoreType.DMA((n, N_DEV - 1)),
                        pltpu.SemaphoreType.DMA((n,))],
    )(*arrays)


HBM_SPEC = pl.BlockSpec(memory_space=pltpu.HBM)
SEM_SPEC = pl.BlockSpec(memory_space=pltpu.SEMAPHORE)
DATAFLOW_EFFECT = pltpu.SideEffectType.DATAFLOW_SIDE_EFFECTING
ALL_PEERS = (1, 2, 3, 4, 5, 6, 7)
SIBLING = 1
SAME_CORE_PEERS = (2, 4, 6)


def fill_own_slot(name, me, src, block, dtype):
    _, r, c = src.shape
    tr = _row_tile(r, 256)

    def body(me_ref, src_ref, out_ref):
        out_ref[...] = src_ref[...].astype(dtype)

    if block is None:
        src_index = lambda i, me_ref: (me_ref[0], i, 0)
    else:
        src_index = lambda i, me_ref: (block, i, 0)
    grid_spec = pltpu.PrefetchScalarGridSpec(
        num_scalar_prefetch=1, grid=(r // tr,), in_specs=[pl.BlockSpec((None, tr, c), src_index)],
        out_specs=pl.BlockSpec((None, tr, c), lambda i, me_ref: (me_ref[0], i, 0)))
    return pl.pallas_call(body, name=name, grid_spec=grid_spec, out_shape=_out_hbm((N_DEV, r, c), dtype),
                          compiler_params=_params(tr * c * (src.dtype.itemsize + jnp.dtype(dtype).itemsize), 1))(me, _in_hbm(src))


def _split_copy(src, land, send_sem, recv_sem, k, peers, me, arriving, forward):
    pos, idx = peers[k - 1]
    if forward:
        pos = peers[SIBLING - 1][0]
        slot = peers[(k | SIBLING) - 1][1] if arriving else idx
        src_ref, dst_ref = land.at[slot], land.at[slot]
    else:
        src_ref = land.at[me] if src is None else src.at[idx]
        dst_ref = land.at[idx if arriving else me]
    return pltpu.make_async_remote_copy(src_ref=src_ref, dst_ref=dst_ref, send_sem=send_sem, recv_sem=recv_sem,
                                        device_id=pos, device_id_type=pl.DeviceIdType.MESH)


def exchange_start(name, lands, peer_ks, srcs=None, after=None, forward=False):
    n = len(lands)
    ns = n if srcs is not None else 0
    extra = _deps([after])
    bufs = (list(srcs) if srcs is not None else []) + list(lands)

    def body(*refs):
        src, land = refs[:ns], refs[ns:ns + n]
        outs = refs[ns + n + len(extra):]
        send_sems, recv_sems, token = outs[:n], outs[n:2 * n], outs[2 * n + ns + n]
        me, peers = _peers()
        for a in range(n):
            for j, k in enumerate(peer_ks):
                _split_copy(src[a] if ns else None, land[a], send_sems[a].at[j], recv_sems[a].at[j], k, peers, me,
                            False, forward).start()
        token[...] = jnp.zeros_like(token)

    sems = [pltpu.SemaphoreType.DMA((len(peer_ks),))] * (2 * n)
    res = pl.pallas_call(
        body, name=name, in_specs=[HBM_SPEC] * len(bufs) + [ANY] * len(extra),
        out_specs=[SEM_SPEC] * (2 * n) + [HBM_SPEC] * len(bufs) + [pl.BlockSpec(memory_space=pltpu.VMEM)],
        out_shape=sems + [pltpu.HBM(a.shape, a.dtype) for a in bufs] + [jax.ShapeDtypeStruct((8, 128), F32)],
        input_output_aliases={i: 2 * n + i for i in range(len(bufs))},
        compiler_params=pltpu.CompilerParams(has_side_effects=DATAFLOW_EFFECT),
    )(*[_in_hbm(a) for a in bufs], *extra)
    handles = [(res[a], res[n + a], res[2 * n + a] if ns else None, res[2 * n + ns + a]) for a in range(n)]
    return handles, res[2 * n + ns + n]


def exchange_wait(name, handle, peer_ks, after, forward=False):
    send_sem, recv_sem, src, land = handle
    bufs = ([src] if src is not None else []) + [land]
    nb = len(bufs)

    def body(*refs):
        src_ref = refs[0] if nb == 2 else None
        land_ref, send_ref, recv_ref = refs[nb - 1], refs[nb], refs[nb + 1]
        me, peers = _peers()
        for j, k in enumerate(peer_ks):
            cp = _split_copy(src_ref, land_ref, send_ref.at[j], recv_ref.at[j], k, peers, me, True, forward)
            cp.wait_send()
            cp.wait_recv()

    return pl.pallas_call(
        body, name=name, in_specs=[HBM_SPEC] * nb + [SEM_SPEC, SEM_SPEC, ANY], out_specs=[HBM_SPEC] * nb,
        out_shape=[pltpu.HBM(a.shape, a.dtype) for a in bufs],
        input_output_aliases={i: i for i in range(nb)}, compiler_params=pltpu.CompilerParams(has_side_effects=DATAFLOW_EFFECT),
    )(*bufs, send_sem, recv_sem, after)[nb - 1]


def _row_tile(r, cap):
    if r <= cap:
        return r
    return max(t for t in range(16, cap + 1, 16) if r % t == 0)


def sum_adamw(name, parts, w, m, v, tr):
    nl, r, c = w.shape
    tr = _row_tile(r, tr)
    c1 = 1.0 - ADAM_B1 ** ADAM_STEP
    c2 = 1.0 - ADAM_B2 ** ADAM_STEP

    def body(*refs):
        part_refs = refs[:nl]
        w_ref, m_ref, v_ref, g_out, d_out, m_out, v_out = refs[nl:]
        layer = pl.program_id(0)
        for j in range(nl):
            @pl.when(layer == j)
            def _(j=j):
                g = part_refs[j][0].astype(F32)
                for p in range(1, N_DEV):
                    g = g + part_refs[j][p].astype(F32)
                mn = ADAM_B1 * m_ref[...] + (1.0 - ADAM_B1) * g
                vn = ADAM_B2 * v_ref[...] + (1.0 - ADAM_B2) * (g * g)
                g_out[...] = g
                m_out[...] = mn
                v_out[...] = vn
                d_out[...] = -ADAM_LR * ((mn / c1) / (jnp.sqrt(vn / c2) + ADAM_EPS) + ADAM_WD * w_ref[...])

    def part_spec(j):
        return pl.BlockSpec((N_DEV, tr, c), lambda l, i: (0, jnp.where(l == j, i, 0), 0))

    lspec = pl.BlockSpec((None, tr, c), lambda l, i: (l, i, 0))
    out = jax.ShapeDtypeStruct((nl, r, c), F32)
    return _row_call(name, body, (nl, r // tr), [part_spec(j) for j in range(nl)] + [lspec] * 3, [lspec] * 4, [out] * 4,
                     tuple(parts) + (w, m, v), vmem_blocks=nl * N_DEV * tr * c * parts[0].dtype.itemsize + 7 * tr * c * 4)


REPLICATED = ("norm_mix_pre", "norm_mix_post", "norm_ffn_pre", "norm_ffn_post", "conv_b", "conv_ln_g", "conv_ln_b",
              "sgu_ln_g", "sgu_ln_b", "w_spatial", "b_spatial")
MATRICES = ("w_in", "w_a_out", "w_b_out", "w_o", "w_gate_up", "w_down")


def local_step(x, target, rep, weight, emit, b_gate, conv_w, start_token=None):
    s, d = x.shape
    causal = jnp.tril(jnp.ones((CHUNK, CHUNK), dtype=bool))
    row = lambda name, l: rep[name][l].reshape(1, -1)

    saved = []
    h = rms_fwd(x, row("norm_mix_pre", 0), deps=[start_token])
    for l in range(DEPTH):
        w_mix = jnp.where(causal[None], rep["w_spatial"][l], 0.0).astype(BF16)
        b_mix = jnp.broadcast_to(rep["b_spatial"][l][:, :, None], (SGU_GROUPS, CHUNK, d // SGU_GROUPS))
        proj = mm_nn("proj", h, weight(l, "w_in", h), BF16, 512)
        c = glu_conv_fwd(proj, conv_w[l], row("conv_b", l), d)
        s_act = ln_silu_fwd(c, row("conv_ln_g", l), row("conv_ln_b", l))
        p_act = sgu_fwd(proj, row("sgu_ln_g", l), row("sgu_ln_b", l), w_mix, b_mix, d, 2)
        y_a, y_b, merged = branches_merge(s_act, p_act, weight(l, "w_a_out", s_act), weight(l, "w_b_out", p_act),
                                          proj, b_gate[l], 4)
        o = mm_nn("branch_out", merged, weight(l, "w_o", merged), F32, 512)
        x_mid, h2 = norm_res(x, o, row("norm_mix_post", l), row("norm_ffn_pre", l))
        g_act, u_act, f = gate_up_swiglu(h2, weight(l, "w_gate_up", h2))
        o2 = mm_nn("down", f, weight(l, "w_down", f), F32, 512, tn=1024)
        saved.append(dict(x_in=x, h=h, proj=proj, c=c, s_act=s_act, p_act=p_act, y_a=y_a, y_b=y_b, merged=merged, o=o,
                          x_mid=x_mid, h2=h2, g_act=g_act, u_act=u_act, f=f, o2=o2, w_mix=w_mix, b_mix=b_mix))
        if l + 1 < DEPTH:
            x, h = norm_res(x_mid, o2, row("norm_ffn_post", l), row("norm_mix_pre", l + 1))

    top = saved[-1]
    loss_vec, dx, do2, dg_ffn_post = final_norm_loss(top["x_mid"], top["o2"], row("norm_ffn_post", DEPTH - 1), target)
    loss = (0.5 / d) * jnp.sum(loss_vec)

    grads = [None] * DEPTH
    for l in reversed(range(DEPTH)):
        sv = saved[l]
        wl = {name: weight(l, name, None)[0] for name in MATRICES}
        g = {"norm_ffn_post": dg_ffn_post}
        df = mm_nt("d_down_in", do2, wl["w_down"], BF16, 512, tn=wl["w_down"].shape[0] // 4)
        tok = emit(l, "w_down", mm_tn("d_down_w", sv["f"], do2, BF16, 512, sv["f"].shape[1] // 4))
        dgu = swiglu_bwd(sv["g_act"], sv["u_act"], df, deps=[tok])
        dh2 = mm_nt("d_gate_up_in", dgu, wl["w_gate_up"], F32, 1024)
        tok = emit(l, "w_gate_up", mm_tn("d_gate_up_w", sv["h2"], dgu, BF16, 2048, d // 2, nb=wl["w_gate_up"].shape[2]))
        dx, do, g["norm_ffn_pre"], g["norm_mix_post"] = norm_bwd_in_out(
            dx, dh2, sv["x_mid"], row("norm_ffn_pre", l), sv["o"], row("norm_mix_post", l), deps=[tok])
        dm = mm_nt("d_square_in", do, wl["w_o"], F32, 512)
        tok = emit(l, "w_o", mm_tn("d_square_w", sv["merged"], do, BF16, 512, d // 2))
        dy_a, dy_b, dproj, g["b_gate"] = merge_bwd(dm, sv["y_a"], sv["y_b"], sv["proj"], b_gate[l], 4, deps=[tok])
        ds = mm_nt("d_square_in", dy_a, wl["w_a_out"], F32, 512)
        tok = emit(l, "w_a_out", mm_tn("d_square_w", sv["s_act"], dy_a, BF16, 512, d // 2))
        dp = mm_nt("d_square_in", dy_b, wl["w_b_out"], F32, 512, deps=[tok])
        tok = emit(l, "w_b_out", mm_tn("d_square_w", sv["p_act"], dy_b, BF16, 512, d // 2))
        dc, g["conv_ln_g"], g["conv_ln_b"], g["conv_b"] = ln_silu_bwd(
            sv["c"], ds, row("conv_ln_g", l), row("conv_ln_b", l), deps=[tok])
        dglu, g["conv_w"] = conv_bwd(sv["proj"], dc, conv_w[l], d)
        dproj = glu_bwd(dproj, dglu, sv["proj"], d)
        dproj, dw_mix, db_mix, g["sgu_ln_g"], g["sgu_ln_b"] = sgu_bwd(
            dproj, sv["proj"], dp, row("sgu_ln_g", l), row("sgu_ln_b", l), sv["w_mix"],
            jnp.swapaxes(sv["w_mix"], 1, 2), sv["b_mix"], d, 2)
        g["w_spatial"] = jnp.where(causal[None], dw_mix, 0.0)
        g["b_spatial"] = db_mix[:, :, 0]
        tok = emit(l, "w_in", mm_tn("d_in_w", sv["h"], dproj, BF16, 2048, d // 2, nb=wl["w_in"].shape[2]))
        dh = mm_nt("d_in_in", dproj, wl["w_in"], F32, 1024, deps=[tok])
        if l > 0:
            below = saved[l - 1]
            dx, do2, g["norm_mix_pre"], dg_ffn_post = norm_bwd_in_out(
                dx, dh, sv["x_in"], row("norm_mix_pre", l), below["o2"], row("norm_ffn_post", l - 1))
        else:
            dx, g["norm_mix_pre"] = norm_bwd_in(dx, dh, sv["x_in"], row("norm_mix_pre", l))
        grads[l] = g
    return loss, dx, grads


def _pack_rows(arrays):
    return jnp.concatenate([a.reshape(-1, 128) for a in arrays], axis=0)


def _unpack_rows(packed, shapes):
    out, r0 = [], 0
    for shp in shapes:
        nr = math.prod(shp) // 128
        out.append(packed[r0:r0 + nr].reshape(shp))
        r0 += nr
    return out


def kernel(x, norm_mix_pre, norm_mix_post, norm_ffn_pre, norm_ffn_post, w_in, b_gate, conv_w, conv_b, conv_ln_g, conv_ln_b, w_a_out, sgu_ln_g, sgu_ln_b, w_spatial, b_spatial, w_b_out, w_o, w_gate_up, w_down, loss_target, m_norm_mix_pre, m_norm_mix_post, m_norm_ffn_pre, m_norm_ffn_post, m_w_in, m_b_gate, m_conv_w, m_conv_b, m_conv_ln_g, m_conv_ln_b, m_w_a_out, m_sgu_ln_g, m_sgu_ln_b, m_w_spatial, m_b_spatial, m_w_b_out, m_w_o, m_w_gate_up, m_w_down, v_norm_mix_pre, v_norm_mix_post, v_norm_ffn_pre, v_norm_ffn_post, v_w_in, v_b_gate, v_conv_w, v_conv_b, v_conv_ln_g, v_conv_ln_b, v_w_a_out, v_sgu_ln_g, v_sgu_ln_b, v_w_spatial, v_b_spatial, v_w_b_out, v_w_o, v_w_gate_up, v_w_down):
    names = ("norm_mix_pre", "norm_mix_post", "norm_ffn_pre", "norm_ffn_post", "w_in", "b_gate", "conv_w", "conv_b",
             "conv_ln_g", "conv_ln_b", "w_a_out", "sgu_ln_g", "sgu_ln_b", "w_spatial", "b_spatial", "w_b_out", "w_o",
             "w_gate_up", "w_down")
    w = dict(zip(names, (norm_mix_pre, norm_mix_post, norm_ffn_pre, norm_ffn_post, w_in, b_gate, conv_w, conv_b,
                         conv_ln_g, conv_ln_b, w_a_out, sgu_ln_g, sgu_ln_b, w_spatial, b_spatial, w_b_out, w_o,
                         w_gate_up, w_down)))
    m = dict(zip(names, (m_norm_mix_pre, m_norm_mix_post, m_norm_ffn_pre, m_norm_ffn_post, m_w_in, m_b_gate, m_conv_w,
                         m_conv_b, m_conv_ln_g, m_conv_ln_b, m_w_a_out, m_sgu_ln_g, m_sgu_ln_b, m_w_spatial,
                         m_b_spatial, m_w_b_out, m_w_o, m_w_gate_up, m_w_down)))
    v = dict(zip(names, (v_norm_mix_pre, v_norm_mix_post, v_norm_ffn_pre, v_norm_ffn_post, v_w_in, v_b_gate, v_conv_w,
                         v_conv_b, v_conv_ln_g, v_conv_ln_b, v_w_a_out, v_sgu_ln_g, v_sgu_ln_b, v_w_spatial,
                         v_b_spatial, v_w_b_out, v_w_o, v_w_gate_up, v_w_down)))
    d = x.shape[-1]
    shard_cols = d // N_DEV

    def small_pack(bg, cw):
        rows = jnp.concatenate([bg, cw], axis=1).reshape(DEPTH * (2 + CONV_WIDTH), shard_cols)
        return jnp.pad(rows, ((0, (-rows.shape[0]) % 8), (0, 0)))

    small_w, small_m, small_v = (small_pack(t["b_gate"], t["conv_w"]) for t in (w, m, v))

    small_full, = _exchange("gather_small", [small_w], scatter=False)
    small_full = small_full[:, :DEPTH * (2 + CONV_WIDTH)].reshape(N_DEV, DEPTH, 2 + CONV_WIDTH, shard_cols)
    small_full = jnp.transpose(small_full, (1, 2, 0, 3)).reshape(DEPTH, 2 + CONV_WIDTH, d)
    b_gate_full = small_full[:, :2]
    conv_w_full = jnp.pad(small_full[:, 2:], ((0, 0), (0, CONV_PAD - CONV_WIDTH), (0, 0)))

    me = (4 * lax.axis_index("x") + 2 * lax.axis_index("y") + lax.axis_index("c")).astype(jnp.int32).reshape(1)
    first_level = (SIBLING,) + SAME_CORE_PEERS
    gathers, token = {}, small_full
    for l in range(DEPTH):
        lands = [fill_own_slot(f"cast_{name}", me, w[name], l, BF16) for name in MATRICES]
        handles, token = exchange_start(f"gather_start_{l}", lands, first_level, after=token)
        for name, handle in zip(MATRICES, handles):
            gathers[l, name] = handle
    use_order = [(l, name) for l in range(DEPTH) for name in MATRICES]
    forwards, gathered = {}, {}

    def start_forward(i, after):
        if i >= len(use_order) or use_order[i] in forwards:
            return None
        l, name = use_order[i]
        land = exchange_wait(f"gather_wait_{name}_{l}", gathers[l, name], first_level, after)
        (forwards[l, name],), tok = exchange_start(f"forward_start_{name}_{l}", [land], SAME_CORE_PEERS, forward=True)
        return tok

    def weight(l, name, after):
        if (l, name) not in gathered:
            i = use_order.index((l, name))
            start_forward(i, after)
            tok = start_forward(i + 1, after)
            full = exchange_wait(f"forward_wait_{name}_{l}", forwards[l, name], SAME_CORE_PEERS, after, forward=True)
            gathered[l, name] = full if name in ("w_in", "w_gate_up") else full.reshape(-1, d)
            return gathered[l, name], tok
        return gathered[l, name], None

    scatters = {}

    def emit(l, name, g):
        chunks = g if g.ndim == 3 else g.reshape(N_DEV, -1, d)
        land = fill_own_slot(f"own_grad_{name}", me, chunks, None, BF16)
        (scatters[l, name],), tok = exchange_start(f"scatter_start_{name}_{l}", [land], ALL_PEERS, srcs=[chunks])
        return tok

    rep = {name: w[name] for name in REPLICATED}
    loss, grad_x, grads = local_step(x[0], loss_target[0], rep, weight, emit, b_gate_full, conv_w_full, token)
    loss = lax.psum(loss, MESH_AXES)

    small_g = jnp.stack([jnp.concatenate([grads[l]["b_gate"], grads[l]["conv_w"][:CONV_WIDTH]], axis=0) for l in range(DEPTH)])
    small_g = jnp.transpose(small_g.reshape(DEPTH * (2 + CONV_WIDTH), N_DEV, shard_cols), (1, 0, 2))
    small_g = jnp.pad(small_g, ((0, 0), (0, small_w.shape[0] - small_g.shape[1]), (0, 0)))
    rep_shapes = [w[name].shape for name in REPLICATED]
    rep_g = _pack_rows([jnp.stack([grads[l][name].reshape(w[name].shape[1:]) for l in range(DEPTH)]) for name in REPLICATED])
    small_land = fill_own_slot("own_small", me, small_g, None, F32)
    (small_handle,), tok_small = exchange_start("scatter_start_small", [small_land], ALL_PEERS, srcs=[small_g])
    rep_land = fill_own_slot("own_replicated", me, rep_g[None], 0, F32)
    (rep_handle,), tok_rep = exchange_start("gather_start_replicated", [rep_land], ALL_PEERS, after=tok_small)

    out = {}
    after = tok_rep
    for name in ("w_down", "w_gate_up", "w_o", "w_a_out", "w_b_out", "w_in"):
        parts = [exchange_wait(f"scatter_wait_{name}_{l}", scatters[l, name], ALL_PEERS, after) for l in range(DEPTH)]
        out[name] = sum_adamw("adamw_" + name, parts, w[name], m[name], v[name], 128)
        after = out[name][0]
    small_parts = exchange_wait("scatter_wait_small", small_handle, ALL_PEERS, after)
    rep_parts = exchange_wait("gather_wait_replicated", rep_handle, ALL_PEERS, after)
    small_res = sum_adamw("adamw_small", [small_parts], small_w[None], small_m[None], small_v[None], small_w.shape[0])
    n_small = DEPTH * (2 + CONV_WIDTH)
    small_res = [r[0, :n_small].reshape(DEPTH, 2 + CONV_WIDTH, shard_cols) for r in small_res]
    out["b_gate"] = [r[:, :2] for r in small_res]
    out["conv_w"] = [r[:, 2:] for r in small_res]
    rep_res = sum_adamw("adamw_replicated", [rep_parts], *(_pack_rows([t[name] for name in REPLICATED])[None] for t in (w, m, v)), 672)
    rep_res = [_unpack_rows(r[0], rep_shapes) for r in rep_res]
    for i, name in enumerate(REPLICATED):
        out[name] = [r[i] for r in rep_res]

    return (loss, grad_x[None], *[out[name][0] for name in names], *[out[name][1] for name in names],
            *[out[name][2] for name in names], *[out[name][3] for name in names])
```

```python
import functools
import math

import jax
import jax.numpy as jnp
from jax import lax
from jax.experimental import pallas as pl
from jax.experimental.pallas import tpu as pltpu

F32 = jnp.float32
BF16 = jnp.bfloat16

DEPTH = 4
N_DEV = 8
EPS = 1e-6
CONV_WIDTH = 31
CONV_PAD = 32
CHUNK = 128
SGU_GROUPS = 8

ADAM_LR = 0.001
ADAM_B1 = 0.9
ADAM_B2 = 0.999
ADAM_EPS = 1e-08
ADAM_WD = 0.01
ADAM_STEP = 10

VMEM_BYTES_V7X = 64 * 1024 * 1024
VMEM_COMPILER_SLACK = 12 * 1024 * 1024
MESH_AXES = ("x", "y", "c")
ANY = pl.BlockSpec(memory_space=pl.ANY)


def _nbytes(shape, dtype):
    return math.prod(shape) * jnp.dtype(dtype).itemsize


def _params(block_bytes, ngrid, single_bytes=0):
    limit = min(2 * block_bytes + single_bytes + VMEM_COMPILER_SLACK, VMEM_BYTES_V7X - 4 * 1024 * 1024)
    return pltpu.CompilerParams(dimension_semantics=("arbitrary",) * ngrid, vmem_limit_bytes=int(limit))


def _in_hbm(a):
    return pltpu.with_memory_space_constraint(a, pltpu.HBM)


def _out_hbm(shape, dtype):
    return pltpu.HBM(tuple(shape), dtype)


def _deps(deps):
    return [t for t in deps if t is not None]


def _mm_body(dims, nk, kaxis, ndeps):
    def body(a_ref, b_ref, *rest):
        o_ref, *acc = rest[ndeps:]

        def prod():
            return lax.dot_general(a_ref[...], b_ref[...], (dims, ((), ())), preferred_element_type=F32)

        if nk == 1:
            o_ref[...] = prod().astype(o_ref.dtype)
            return
        acc_ref, = acc
        k = pl.program_id(kaxis)

        @pl.when(k == 0)
        def _():
            acc_ref[...] = prod()

        @pl.when(k > 0)
        def _():
            acc_ref[...] += prod()

        @pl.when(k == nk - 1)
        def _():
            o_ref[...] = acc_ref[...].astype(o_ref.dtype)

    return body


def _mm_call(name, a, b, dims, grid, a_spec, b_spec, o_spec, out_shape, out_dtype, nk, kaxis, acc_shape, deps=()):
    deps = _deps(deps)
    blocks = (_nbytes([d for d in a_spec.block_shape if d], a.dtype) + _nbytes([d for d in b_spec.block_shape if d], b.dtype)
              + _nbytes([d for d in o_spec.block_shape if d], out_dtype))
    scratch = [pltpu.VMEM(acc_shape, F32)] if nk > 1 else []
    acc_bytes = _nbytes(acc_shape, F32) * (2 if nk > 1 else 1)
    return pl.pallas_call(
        _mm_body(dims, nk, kaxis, len(deps)), name=name, grid=grid, in_specs=[a_spec, b_spec] + [ANY] * len(deps),
        out_specs=o_spec, out_shape=_out_hbm(out_shape, out_dtype), scratch_shapes=scratch,
        compiler_params=_params(blocks, len(grid), acc_bytes),
    )(_in_hbm(a), _in_hbm(b), *deps)


def mm_nn(name, a, b_and_token, out_dtype, tm, tn=None, tk=None):
    b, token = b_and_token
    m, k = a.shape
    tm = min(tm, m)
    if b.ndim == 3:
        nblk, _, nb = b.shape
        return _mm_call(name, a, b, ((1,), (0,)), (nblk, m // tm),
                        pl.BlockSpec((tm, k), lambda j, i: (i, 0)), pl.BlockSpec((None, k, nb), lambda j, i: (j, 0, 0)),
                        pl.BlockSpec((tm, nb), lambda j, i: (i, j)), (m, nblk * nb), out_dtype, 1, 0, (tm, nb), [token])
    n = b.shape[1]
    tn = tn or n
    tk = tk or k
    nk = k // tk
    return _mm_call(name, a, b, ((1,), (0,)), (n // tn, m // tm, nk),
                    pl.BlockSpec((tm, tk), lambda j, i, kk: (i, kk)), pl.BlockSpec((tk, tn), lambda j, i, kk: (kk, j)),
                    pl.BlockSpec((tm, tn), lambda j, i, kk: (i, j)), (m, n), out_dtype, nk, 2, (tm, tn), [token])


def mm_nt(name, a, b, out_dtype, tm, tn=None, deps=()):
    m = a.shape[0]
    tm = min(tm, m)
    if b.ndim == 3:
        kblk, n, kb = b.shape
        return _mm_call(name, a, b, ((1,), (1,)), (m // tm, kblk),
                        pl.BlockSpec((tm, kb), lambda i, kk: (i, kk)), pl.BlockSpec((None, n, kb), lambda i, kk: (kk, 0, 0)),
                        pl.BlockSpec((tm, n), lambda i, kk: (i, 0)), (m, n), out_dtype, kblk, 1, (tm, n), deps)
    n, kc = b.shape
    tn = tn or n
    return _mm_call(name, a, b, ((1,), (1,)), (n // tn, m // tm),
                    pl.BlockSpec((tm, kc), lambda j, i: (i, 0)), pl.BlockSpec((tn, kc), lambda j, i: (j, 0)),
                    pl.BlockSpec((tm, tn), lambda j, i: (i, j)), (m, n), out_dtype, 1, 0, (tm, tn), deps)


def mm_tn(name, a, b, out_dtype, tm, tr, nb=None):
    m, k = a.shape
    n = b.shape[1]
    tm = min(tm, m)
    nm = m // tm
    if nb is not None:
        return _mm_call(name, a, b, ((0,), (0,)), (n // nb, k // tr, nm),
                        pl.BlockSpec((tm, tr), lambda j, r, mm: (mm, r)), pl.BlockSpec((tm, nb), lambda j, r, mm: (mm, j)),
                        pl.BlockSpec((None, tr, nb), lambda j, r, mm: (j, r, 0)), (n // nb, k, nb), out_dtype, nm, 2, (tr, nb))
    return _mm_call(name, a, b, ((0,), (0,)), (k // tr, nm),
                    pl.BlockSpec((tm, tr), lambda r, mm: (mm, r)), pl.BlockSpec((tm, n), lambda r, mm: (mm, 0)),
                    pl.BlockSpec((tr, n), lambda r, mm: (r, 0)), (k, n), out_dtype, nm, 1, (tr, n))


def _row_call(name, body, grid, in_specs, out_specs, out_shape, arrays, scratch=(), aliases=None, vmem_blocks=0, deps=()):
    deps = _deps(deps)
    nin = len(arrays)

    def with_deps(*refs):
        body(*refs[:nin], *refs[nin + len(deps):])

    single = not isinstance(out_shape, (list, tuple))
    outs = [_out_hbm(o.shape, o.dtype) for o in ([out_shape] if single else out_shape)]
    return pl.pallas_call(
        with_deps, name=name, grid=grid, in_specs=list(in_specs) + [ANY] * len(deps), out_specs=out_specs,
        out_shape=outs[0] if single else outs, scratch_shapes=list(scratch), input_output_aliases=aliases or {},
        compiler_params=_params(vmem_blocks, len(grid)),
    )(*[_in_hbm(a) for a in arrays], *deps)


def _rows(tm, d, col=0):
    return pl.BlockSpec((tm, d), lambda i, *_: (i, col))


def _vec(d):
    return pl.BlockSpec((1, d), lambda *_: (0, 0))


def _rstd(x):
    return lax.rsqrt(jnp.mean(x * x, axis=-1, keepdims=True) + EPS)


def _rms_bwd(dy, x, g):
    r = _rstd(x)
    n = x * r
    w = dy * g
    dx = r * (w - n * jnp.mean(w * n, axis=-1, keepdims=True))
    return dx, jnp.sum(dy * n, axis=0, keepdims=True)


def _accumulate(ref, value, first):
    @pl.when(first)
    def _():
        ref[...] = value

    @pl.when(jnp.logical_not(first))
    def _():
        ref[...] += value


def rms_fwd(x, g, tm=256, deps=()):
    s, d = x.shape
    tm = min(tm, s)

    def body(x_ref, g_ref, h_ref):
        xv = x_ref[...]
        h_ref[...] = (xv * _rstd(xv) * g_ref[...]).astype(BF16)

    return _row_call("rms_fwd", body, (s // tm,), [_rows(tm, d), _vec(d)], _rows(tm, d),
                     jax.ShapeDtypeStruct((s, d), BF16), (x, g), vmem_blocks=tm * d * 6, deps=deps)


def norm_res(x_in, o, g_post, g_next, tm=256):
    s, d = x_in.shape
    tm = min(tm, s)

    def body(x_ref, o_ref, gp_ref, gn_ref, xo_ref, h_ref):
        ov = o_ref[...]
        xo = x_ref[...] + (ov * _rstd(ov) * gp_ref[...])
        xo_ref[...] = xo
        h_ref[...] = (xo * _rstd(xo) * gn_ref[...]).astype(BF16)

    return _row_call("norm_res", body, (s // tm,), [_rows(tm, d), _rows(tm, d), _vec(d), _vec(d)],
                     [_rows(tm, d), _rows(tm, d)],
                     [jax.ShapeDtypeStruct((s, d), F32), jax.ShapeDtypeStruct((s, d), BF16)],
                     (x_in, o, g_post, g_next), vmem_blocks=tm * d * 14)


def final_norm_loss(x_in, o, g_post, target, tm=256):
    s, d = x_in.shape
    tm = min(tm, s)

    def body(x_ref, o_ref, gp_ref, t_ref, loss_ref, dy_ref, do_ref, dg_ref):
        first = pl.program_id(0) == 0
        ov = o_ref[...]
        g = gp_ref[...]
        diff = x_ref[...] + (ov * _rstd(ov) * g) - t_ref[...]
        _accumulate(loss_ref, jnp.sum(diff * diff, axis=0, keepdims=True), first)
        dy = diff * (1.0 / d)
        dy_ref[...] = dy
        do, dg = _rms_bwd(dy, ov, g)
        do_ref[...] = do.astype(BF16)
        _accumulate(dg_ref, dg, first)

    return _row_call("final_norm_loss", body, (s // tm,), [_rows(tm, d), _rows(tm, d), _vec(d), _rows(tm, d)],
                     [_vec(d), _rows(tm, d), _rows(tm, d), _vec(d)],
                     [jax.ShapeDtypeStruct((1, d), F32), jax.ShapeDtypeStruct((s, d), F32),
                      jax.ShapeDtypeStruct((s, d), BF16), jax.ShapeDtypeStruct((1, d), F32)],
                     (x_in, o, g_post, target), vmem_blocks=tm * d * 18)


def norm_bwd_in_out(dx_out, dh, x_in, g_pre, o_below, g_post_below, tm=256, deps=()):
    s, d = x_in.shape
    tm = min(tm, s)

    def body(dxo_ref, dh_ref, x_ref, g_ref, o_ref, gb_ref, dxi_ref, do_ref, dg_ref, dgb_ref):
        first = pl.program_id(0) == 0
        dx, dg = _rms_bwd(dh_ref[...], x_ref[...], g_ref[...])
        dxi = dxo_ref[...] + dx
        dxi_ref[...] = dxi
        _accumulate(dg_ref, dg, first)
        do, dgb = _rms_bwd(dxi, o_ref[...], gb_ref[...])
        do_ref[...] = do.astype(BF16)
        _accumulate(dgb_ref, dgb, first)

    return _row_call("norm_bwd_in_out", body, (s // tm,),
                     [_rows(tm, d), _rows(tm, d), _rows(tm, d), _vec(d), _rows(tm, d), _vec(d)],
                     [_rows(tm, d), _rows(tm, d), _vec(d), _vec(d)],
                     [jax.ShapeDtypeStruct((s, d), F32), jax.ShapeDtypeStruct((s, d), BF16),
                      jax.ShapeDtypeStruct((1, d), F32), jax.ShapeDtypeStruct((1, d), F32)],
                     (dx_out, dh, x_in, g_pre, o_below, g_post_below), vmem_blocks=tm * d * 22, deps=deps)


def norm_bwd_in(dx_out, dh, x_in, g_pre, tm=256):
    s, d = x_in.shape
    tm = min(tm, s)

    def body(dxo_ref, dh_ref, x_ref, g_ref, dxi_ref, dg_ref):
        dx, dg = _rms_bwd(dh_ref[...], x_ref[...], g_ref[...])
        dxi_ref[...] = dxo_ref[...] + dx
        _accumulate(dg_ref, dg, pl.program_id(0) == 0)

    return _row_call("norm_bwd_in", body, (s // tm,), [_rows(tm, d), _rows(tm, d), _rows(tm, d), _vec(d)],
                     [_rows(tm, d), _vec(d)],
                     [jax.ShapeDtypeStruct((s, d), F32), jax.ShapeDtypeStruct((1, d), F32)],
                     (dx_out, dh, x_in, g_pre), vmem_blocks=tm * d * 16)


CONV_COLS = 256
CONV_ROWS = 32


SUBLANES = 8


def _shifted_copies(ext_ref, sh_ref):
    n = sh_ref.shape[1]
    for r in range(SUBLANES):
        sh_ref[r] = ext_ref[r:r + n, :]


def _window(sh_ref, off):
    return sh_ref[off % SUBLANES, off - off % SUBLANES:off - off % SUBLANES + CONV_ROWS, :]


def _depthwise(sh_ref, w, n_out, in_off, flip, emit):
    for r0 in range(0, n_out, CONV_ROWS):
        acc = None
        for k in range(CONV_WIDTH):
            term = w[k:k + 1, :] * _window(sh_ref, r0 + in_off + (CONV_WIDTH - 1 - k if flip else k))
            acc = term if acc is None else acc + term
        emit(r0, acc)


def _ext_scratch(tm):
    return [pltpu.VMEM((tm + CONV_PAD + SUBLANES, CONV_COLS), F32), pltpu.VMEM((SUBLANES, tm + CONV_PAD, CONV_COLS), F32)]


def glu_conv_fwd(proj, conv_w, conv_b, c_ch, tm=512):
    s = proj.shape[0]
    tm = min(tm, s)
    ncb = c_ch // CONV_COLS
    hb = tm // CONV_PAD

    def body(a_ref, g_ref, ah_ref, gh_ref, w_ref, b_ref, c_ref, ext_ref, sh_ref):
        i = pl.program_id(1)
        halo = ah_ref[...].astype(F32) * jax.nn.sigmoid(gh_ref[...].astype(F32))
        ext_ref[0:CONV_PAD, :] = jnp.where(i > 0, halo, 0.0)
        ext_ref[CONV_PAD:CONV_PAD + tm, :] = a_ref[...].astype(F32) * jax.nn.sigmoid(g_ref[...].astype(F32))
        ext_ref[CONV_PAD + tm:, :] = jnp.zeros((SUBLANES, CONV_COLS), F32)
        _shifted_copies(ext_ref, sh_ref)
        w = w_ref[...]
        bias = b_ref[...]

        def emit(r0, acc):
            c_ref[r0:r0 + CONV_ROWS, :] = acc + bias

        _depthwise(sh_ref, w, tm, CONV_PAD - (CONV_WIDTH - 1), False, emit)

    main = lambda col0: pl.BlockSpec((tm, CONV_COLS), lambda c, i: (i, col0 + c))
    halo = lambda col0: pl.BlockSpec((CONV_PAD, CONV_COLS), lambda c, i: (jnp.maximum(i * hb - 1, 0), col0 + c))
    return _row_call("glu_conv_fwd", body, (ncb, s // tm),
                     [main(0), main(ncb), halo(0), halo(ncb),
                      pl.BlockSpec((CONV_PAD, CONV_COLS), lambda c, i: (0, c)), pl.BlockSpec((1, CONV_COLS), lambda c, i: (0, c))],
                     pl.BlockSpec((tm, CONV_COLS), lambda c, i: (i, c)), jax.ShapeDtypeStruct((s, c_ch), F32),
                     (proj, proj, proj, proj, conv_w, conv_b),
                     scratch=_ext_scratch(tm), vmem_blocks=tm * CONV_COLS * 32)


def _layer_norm_stats(x):
    mu = jnp.mean(x, axis=-1, keepdims=True)
    xc = x - mu
    rstd = lax.rsqrt(jnp.mean(xc * xc, axis=-1, keepdims=True) + EPS)
    return xc * rstd, rstd


def _layer_norm_bwd(dy, xhat, rstd, g):
    dxh = dy * g
    return rstd * (dxh - jnp.mean(dxh, axis=-1, keepdims=True) - xhat * jnp.mean(dxh * xhat, axis=-1, keepdims=True))


def ln_silu_fwd(c, ln_g, ln_b, tm=256):
    s, d = c.shape
    tm = min(tm, s)

    def body(c_ref, g_ref, b_ref, s_ref):
        xhat, _ = _layer_norm_stats(c_ref[...])
        s_ref[...] = jax.nn.silu(xhat * g_ref[...] + b_ref[...]).astype(BF16)

    return _row_call("ln_silu_fwd", body, (s // tm,), [_rows(tm, d), _vec(d), _vec(d)], _rows(tm, d),
                     jax.ShapeDtypeStruct((s, d), BF16), (c, ln_g, ln_b), vmem_blocks=tm * d * 10)


def ln_silu_bwd(c, ds, ln_g, ln_b, tm=256, deps=()):
    s, d = c.shape
    tm = min(tm, s)

    def body(c_ref, ds_ref, g_ref, b_ref, dc_ref, dg_ref, db_ref, dcb_ref):
        first = pl.program_id(0) == 0
        g = g_ref[...]
        xhat, rstd = _layer_norm_stats(c_ref[...])
        y = xhat * g + b_ref[...]
        sg = jax.nn.sigmoid(y)
        dln = ds_ref[...] * (sg * (1.0 + y * (1.0 - sg)))
        _accumulate(dg_ref, jnp.sum(dln * xhat, axis=0, keepdims=True), first)
        _accumulate(db_ref, jnp.sum(dln, axis=0, keepdims=True), first)
        dc = _layer_norm_bwd(dln, xhat, rstd, g)
        dc_ref[...] = dc
        _accumulate(dcb_ref, jnp.sum(dc, axis=0, keepdims=True), first)

    vec = jax.ShapeDtypeStruct((1, d), F32)
    return _row_call("ln_silu_bwd", body, (s // tm,), [_rows(tm, d), _rows(tm, d), _vec(d), _vec(d)],
                     [_rows(tm, d), _vec(d), _vec(d), _vec(d)], [jax.ShapeDtypeStruct((s, d), F32), vec, vec, vec],
                     (c, ds, ln_g, ln_b), vmem_blocks=tm * d * 20, deps=deps)


def conv_bwd(proj, dc, conv_w, c_ch, tm=512):
    s = proj.shape[0]
    tm = min(tm, s)
    ncb = c_ch // CONV_COLS
    hb = tm // CONV_PAD
    last_halo = s // CONV_PAD - 1
    n_i = s // tm

    def body(a_ref, g_ref, ah_ref, gh_ref, dc_ref, dcn_ref, w_ref, dglu_ref, dw_ref,
             ext_ref, sh_ref, dce_ref, dsh_ref, dwacc_ref):
        i = pl.program_id(1)
        zeros = jnp.zeros((SUBLANES, CONV_COLS), F32)
        halo = ah_ref[...].astype(F32) * jax.nn.sigmoid(gh_ref[...].astype(F32))
        ext_ref[0:CONV_PAD, :] = jnp.where(i > 0, halo, 0.0)
        ext_ref[CONV_PAD:CONV_PAD + tm, :] = a_ref[...].astype(F32) * jax.nn.sigmoid(g_ref[...].astype(F32))
        ext_ref[CONV_PAD + tm:, :] = zeros
        _shifted_copies(ext_ref, sh_ref)
        dce_ref[0:tm, :] = dc_ref[...]
        dce_ref[tm:tm + CONV_PAD, :] = jnp.where(i < n_i - 1, dcn_ref[...], 0.0)
        dce_ref[tm + CONV_PAD:, :] = zeros
        _shifted_copies(dce_ref, dsh_ref)
        w = w_ref[...]

        def emit(r0, acc):
            dglu_ref[r0:r0 + CONV_ROWS, :] = acc

        _depthwise(dsh_ref, w, tm, 0, True, emit)

        for k in range(CONV_WIDTH):
            acc = None
            for r0 in range(0, tm, CONV_ROWS):
                term = dce_ref[r0:r0 + CONV_ROWS, :] * _window(sh_ref, r0 + CONV_PAD - (CONV_WIDTH - 1) + k)
                acc = term if acc is None else acc + term
            dwacc_ref[k:k + 1, :] = jnp.sum(acc, axis=0, keepdims=True)
        dwacc_ref[CONV_WIDTH:, :] = jnp.zeros((CONV_PAD - CONV_WIDTH, CONV_COLS), F32)
        _accumulate(dw_ref, dwacc_ref[...], i == 0)

    main = lambda col0: pl.BlockSpec((tm, CONV_COLS), lambda c, i: (i, col0 + c))
    halo = lambda col0: pl.BlockSpec((CONV_PAD, CONV_COLS), lambda c, i: (jnp.maximum(i * hb - 1, 0), col0 + c))
    nxt = pl.BlockSpec((CONV_PAD, CONV_COLS), lambda c, i: (jnp.minimum((i + 1) * hb, last_halo), c))
    wspec = pl.BlockSpec((CONV_PAD, CONV_COLS), lambda c, i: (0, c))
    return _row_call("conv_bwd", body, (ncb, n_i),
                     [main(0), main(ncb), halo(0), halo(ncb), main(0), nxt, wspec],
                     [main(0), wspec],
                     [jax.ShapeDtypeStruct((s, c_ch), F32), jax.ShapeDtypeStruct((CONV_PAD, c_ch), F32)],
                     (proj, proj, proj, proj, dc, dc, conv_w),
                     scratch=_ext_scratch(tm) + _ext_scratch(tm) + [pltpu.VMEM((CONV_PAD, CONV_COLS), F32)],
                     vmem_blocks=tm * CONV_COLS * 56)


ELEMENTWISE_COLS = 512


def _col_chunks(d):
    return [slice(c0, c0 + ELEMENTWISE_COLS) for c0 in range(0, d, ELEMENTWISE_COLS)]


def _pair_spec(tm, d, pair):
    return pl.BlockSpec((tm, 2 * d), lambda i: (i, pair))


def glu_bwd(dproj, dglu, proj, c_ch, tm=256):
    s = proj.shape[0]
    tm = min(tm, s)

    def body(_, dglu_ref, a_ref, g_ref, out_ref):
        for cols in _col_chunks(c_ch):
            dg = dglu_ref[:, cols]
            sg = jax.nn.sigmoid(g_ref[:, cols].astype(F32))
            out_ref[:, cols] = (dg * sg).astype(BF16)
            out_ref[:, c_ch + cols.start:c_ch + cols.stop] = (dg * a_ref[:, cols].astype(F32) * (sg * (1.0 - sg))).astype(BF16)

    return _row_call("glu_bwd", body, (s // tm,), [ANY, _rows(tm, c_ch), _rows(tm, c_ch), _rows(tm, c_ch, 1)],
                     _pair_spec(tm, c_ch, 0), jax.ShapeDtypeStruct(dproj.shape, BF16), (dproj, dglu, proj, proj),
                     aliases={0: 0}, vmem_blocks=tm * c_ch * 12)


_SQRT_HALF = 0.7071067811865476
_INV_SQRT_2PI = 0.3989422804014327


def _gelu_parts(x):
    cdf = 0.5 * (1.0 + lax.erf(x * _SQRT_HALF))
    return cdf, x * cdf


def _gelu_grad(x, cdf):
    return cdf + x * (_INV_SQRT_2PI * jnp.exp(-0.5 * x * x))


def _sgu_specs(tm, ch):
    grp = ch // SGU_GROUPS
    full3 = lambda shape: pl.BlockSpec(shape, lambda *_: (0, 0, 0))
    return grp, full3((SGU_GROUPS, CHUNK, CHUNK)), full3((SGU_GROUPS, CHUNK, grp))


def sgu_fwd(proj, ln_g, ln_b, w_mix, b_mix, ch, col0, tm=CHUNK):
    s = proj.shape[0]
    grp, wspec, bspec = _sgu_specs(tm, ch)

    def body(u_ref, v_ref, g_ref, b_ref, w_ref, bm_ref, p_ref, mix_ref):
        _, u = _gelu_parts(u_ref[...].astype(F32))
        _, v0 = _gelu_parts(v_ref[...].astype(F32))
        xhat, _ = _layer_norm_stats(v0)
        vn = (xhat * g_ref[...] + b_ref[...]).astype(BF16)
        for n in range(tm // CHUNK):
            for g in range(SGU_GROUPS):
                blk = vn[n * CHUNK:(n + 1) * CHUNK, g * grp:(g + 1) * grp]
                mix_ref[n * CHUNK:(n + 1) * CHUNK, g * grp:(g + 1) * grp] = (
                    jnp.dot(w_ref[g], blk, preferred_element_type=F32) + bm_ref[g])
        p_ref[...] = (u * mix_ref[...]).astype(BF16)

    return _row_call("sgu_fwd", body, (s // tm,),
                     [_rows(tm, ch, col0), _rows(tm, ch, col0 + 1), _vec(ch), _vec(ch), wspec, bspec], _rows(tm, ch),
                     jax.ShapeDtypeStruct((s, ch), BF16), (proj, proj, ln_g, ln_b, w_mix, b_mix),
                     scratch=[pltpu.VMEM((tm, ch), F32)], vmem_blocks=tm * ch * 30)


def sgu_bwd(dproj, proj, dp, ln_g, ln_b, w_mix, w_mix_t, b_mix, ch, col0, tm=CHUNK):
    s = proj.shape[0]
    assert tm == CHUNK and col0 % 2 == 0
    grp, wspec, bspec = _sgu_specs(tm, ch)
    groups = [slice(k * grp, (k + 1) * grp) for k in range(SGU_GROUPS)]

    def body(_, u_ref, v_ref, dp_ref, g_ref, b_ref, w_ref, wt_ref, bm_ref,
             out_ref, dw_ref, dbm_ref, dg_ref, db_ref, u_s, gu_s, gv_s, xh_s, dvn_s):
        first = pl.program_id(0) == 0
        row_sum = lambda x: jnp.sum(x, axis=1, keepdims=True)
        total = None
        for cols in groups:
            ub = u_ref[:, cols].astype(F32)
            vb = v_ref[:, cols].astype(F32)
            cdf_u, u = _gelu_parts(ub)
            cdf_v, v0 = _gelu_parts(vb)
            u_s[:, cols] = u
            gu_s[:, cols] = _gelu_grad(ub, cdf_u)
            gv_s[:, cols] = _gelu_grad(vb, cdf_v)
            xh_s[:, cols] = v0
            total = row_sum(v0) if total is None else total + row_sum(v0)
        mu = total * (1.0 / ch)
        total = None
        for cols in groups:
            xc = xh_s[:, cols] - mu
            xh_s[:, cols] = xc
            total = row_sum(xc * xc) if total is None else total + row_sum(xc * xc)
        rstd = lax.rsqrt(total * (1.0 / ch) + EPS)

        t1 = t2 = None
        for k, cols in enumerate(groups):
            g = g_ref[:, cols]
            xhat = xh_s[:, cols] * rstd
            xh_s[:, cols] = xhat
            vn = (xhat * g + b_ref[:, cols]).astype(BF16)
            dpk = dp_ref[:, cols]
            dmix = dpk * u_s[:, cols]
            dmix_bf = dmix.astype(BF16)
            mixed = jnp.dot(w_ref[k], vn, preferred_element_type=F32) + bm_ref[k]
            out_ref[:, cols] = (dpk * mixed * gu_s[:, cols]).astype(BF16)
            dvn = jnp.dot(wt_ref[k], dmix_bf, preferred_element_type=F32)
            dvn_s[:, cols] = dvn
            _accumulate(dw_ref.at[k], lax.dot_general(dmix_bf, vn, (((1,), (1,)), ((), ())), preferred_element_type=F32), first)
            _accumulate(dbm_ref.at[k], jnp.broadcast_to(row_sum(dmix), (CHUNK, CHUNK)), first)
            _accumulate(dg_ref.at[:, cols], jnp.sum(dvn * xhat, axis=0, keepdims=True), first)
            _accumulate(db_ref.at[:, cols], jnp.sum(dvn, axis=0, keepdims=True), first)
            dxh = dvn * g
            t1 = row_sum(dxh) if t1 is None else t1 + row_sum(dxh)
            t2 = row_sum(dxh * xhat) if t2 is None else t2 + row_sum(dxh * xhat)
        m1 = t1 * (1.0 / ch)
        m2 = t2 * (1.0 / ch)
        for cols in groups:
            dv0 = rstd * (dvn_s[:, cols] * g_ref[:, cols] - m1 - xh_s[:, cols] * m2)
            out_ref[:, ch + cols.start:ch + cols.stop] = (dv0 * gv_s[:, cols]).astype(BF16)

    vec = _vec(ch)
    acc3 = lambda: pl.BlockSpec((SGU_GROUPS, CHUNK, CHUNK), lambda i: (0, 0, 0))
    vshape = jax.ShapeDtypeStruct((1, ch), F32)
    mshape = jax.ShapeDtypeStruct((SGU_GROUPS, CHUNK, CHUNK), F32)
    return _row_call("sgu_bwd", body, (s // tm,),
                     [ANY, _rows(tm, ch, col0), _rows(tm, ch, col0 + 1), _rows(tm, ch), vec, vec, wspec, wspec, bspec],
                     [_pair_spec(tm, ch, col0 // 2), acc3(), acc3(), vec, vec],
                     [jax.ShapeDtypeStruct(dproj.shape, BF16), mshape, mshape, vshape, vshape],
                     (dproj, proj, proj, dp, ln_g, ln_b, w_mix, w_mix_t, b_mix),
                     scratch=[pltpu.VMEM((tm, ch), F32)] * 5, aliases={0: 0}, vmem_blocks=tm * ch * 40)


def branches_merge(s_act, p_act, wa_and_token, wb_and_token, proj, b_gate, col0, tm=256, tn=1024):
    (wa, tok_a), (wb, tok_b) = wa_and_token, wb_and_token
    deps = _deps([tok_a, tok_b])
    s, d = s_act.shape
    tm = min(tm, s)
    per = d // tn

    def body(s_ref, p_ref, wa_ref, wb_ref, l0_ref, l1_ref, bg_ref, *rest):
        ya_ref, yb_ref, m_ref = rest[len(deps):]
        ya = jnp.dot(s_ref[...], wa_ref[...], preferred_element_type=F32)
        yb = jnp.dot(p_ref[...], wb_ref[...], preferred_element_type=F32)
        ya_ref[...] = ya
        yb_ref[...] = yb
        g0 = jax.nn.sigmoid(l0_ref[...].astype(F32) + bg_ref[0:1, :])
        g1 = jax.nn.sigmoid(l1_ref[...].astype(F32) + bg_ref[1:2, :])
        m_ref[...] = (g0 * ya + g1 * yb).astype(BF16)

    act = pl.BlockSpec((tm, d), lambda j, i: (i, 0))
    wgt = pl.BlockSpec((d, tn), lambda j, i: (0, j))
    logits = lambda col: pl.BlockSpec((tm, tn), lambda j, i: (i, col * per + j))
    oblk = pl.BlockSpec((tm, tn), lambda j, i: (i, j))
    return pl.pallas_call(
        body, name="branches_merge", grid=(per, s // tm),
        in_specs=[act, act, wgt, wgt, logits(col0), logits(col0 + 1), pl.BlockSpec((2, tn), lambda j, i: (0, j))] + [ANY] * len(deps),
        out_specs=[oblk, oblk, oblk],
        out_shape=[_out_hbm((s, d), F32), _out_hbm((s, d), F32), _out_hbm((s, d), BF16)],
        compiler_params=_params(2 * tm * d * 2 + 2 * d * tn * 2 + 2 * tm * tn * 2 + tm * tn * 10, 2, 4 * tm * tn * 4),
    )(*[_in_hbm(a) for a in (s_act, p_act, wa, wb, proj, proj, b_gate)], *deps)


def merge_bwd(dm, y_a, y_b, proj, b_gate, col0, tm=256, deps=()):
    s, d = y_a.shape
    tm = min(tm, s)

    assert col0 % 2 == 0

    def body(dm_ref, ya_ref, yb_ref, l0_ref, l1_ref, bg_ref, dya_ref, dyb_ref, out_ref, dbg_ref):
        first = pl.program_id(0) == 0
        for cols in _col_chunks(d):
            dmv = dm_ref[:, cols]
            g0 = jax.nn.sigmoid(l0_ref[:, cols].astype(F32) + bg_ref[0:1, cols])
            g1 = jax.nn.sigmoid(l1_ref[:, cols].astype(F32) + bg_ref[1:2, cols])
            dya_ref[:, cols] = (dmv * g0).astype(BF16)
            dyb_ref[:, cols] = (dmv * g1).astype(BF16)
            dl0 = dmv * ya_ref[:, cols] * (g0 * (1.0 - g0))
            dl1 = dmv * yb_ref[:, cols] * (g1 * (1.0 - g1))
            _accumulate(dbg_ref.at[0:1, cols], jnp.sum(dl0, axis=0, keepdims=True), first)
            _accumulate(dbg_ref.at[1:2, cols], jnp.sum(dl1, axis=0, keepdims=True), first)
            out_ref[:, cols] = dl0.astype(BF16)
            out_ref[:, d + cols.start:d + cols.stop] = dl1.astype(BF16)

    bgspec = pl.BlockSpec((2, d), lambda i: (0, 0))
    return _row_call("merge_bwd", body, (s // tm,),
                     [_rows(tm, d), _rows(tm, d), _rows(tm, d), _rows(tm, d, col0), _rows(tm, d, col0 + 1), bgspec],
                     [_rows(tm, d), _rows(tm, d), _pair_spec(tm, d, col0 // 2), bgspec],
                     [jax.ShapeDtypeStruct((s, d), BF16), jax.ShapeDtypeStruct((s, d), BF16),
                      jax.ShapeDtypeStruct(proj.shape, BF16), jax.ShapeDtypeStruct((2, d), F32)],
                     (dm, y_a, y_b, proj, proj, b_gate), vmem_blocks=tm * d * 24, deps=deps)


def gate_up_swiglu(h, w_and_token, tm=256):
    w, token = w_and_token
    deps = _deps([token])
    s, d = h.shape
    nblk, _, nb = w.shape
    half = nblk // 2
    tm = min(tm, s)

    def body(h_ref, wg_ref, wu_ref, *rest):
        g_ref, u_ref, f_ref = rest[len(deps):]
        hv = h_ref[...]
        g = jnp.dot(hv, wg_ref[...], preferred_element_type=F32)
        u = jnp.dot(hv, wu_ref[...], preferred_element_type=F32)
        g_ref[...] = g.astype(BF16)
        u_ref[...] = u.astype(BF16)
        f_ref[...] = (jax.nn.silu(g) * u).astype(BF16)

    out = _out_hbm((s, half * nb), BF16)
    oblk = pl.BlockSpec((tm, nb), lambda j, i: (i, j))
    return pl.pallas_call(
        body, name="gate_up_swiglu", grid=(half, s // tm),
        in_specs=[pl.BlockSpec((tm, d), lambda j, i: (i, 0)), pl.BlockSpec((None, d, nb), lambda j, i: (j, 0, 0)),
                  pl.BlockSpec((None, d, nb), lambda j, i: (j + half, 0, 0))] + [ANY] * len(deps),
        out_specs=[oblk, oblk, oblk], out_shape=[out, out, out],
        compiler_params=_params(tm * d * 2 + 2 * d * nb * 2 + 3 * tm * nb * 2, 2, 4 * tm * nb * 4),
    )(_in_hbm(h), _in_hbm(w), _in_hbm(w), *deps)


def swiglu_bwd(g_act, u_act, df, tm=256, deps=()):
    s, half = g_act.shape
    w2 = 2 * half
    tm = min(tm, s)
    chunk = w2 // N_DEV

    def body(g_ref, u_ref, df_ref, out_ref):
        for c0 in range(0, half, chunk):
            cols = slice(c0, c0 + chunk)
            g = g_ref[:, cols].astype(F32)
            sg = jax.nn.sigmoid(g)
            dfv = df_ref[:, cols].astype(F32)
            out_ref[:, cols] = (dfv * u_ref[:, cols].astype(F32) * (sg * (1.0 + g * (1.0 - sg)))).astype(BF16)
            out_ref[:, half + c0:half + c0 + chunk] = (dfv * (g * sg)).astype(BF16)

    return _row_call("swiglu_bwd", body, (s // tm,), [_rows(tm, half), _rows(tm, half), _rows(tm, half)],
                     _rows(tm, w2), jax.ShapeDtypeStruct((s, w2), BF16), (g_act, u_act, df), vmem_blocks=tm * w2 * 5, deps=deps)


def _peers():
    x, y, c = lax.axis_index("x"), lax.axis_index("y"), lax.axis_index("c")
    me = 4 * x + 2 * y + c
    peers = []
    for k in range(1, N_DEV):
        px = 1 - x if k & 4 else x
        py = 1 - y if k & 2 else y
        pc = 1 - c if k & 1 else c
        peers.append(((px, py, pc), 4 * px + 2 * py + pc))
    return me, peers


def _exchange(name, arrays, scatter):
    n = len(arrays)

    def body(*refs):
        ins, outs = refs[:n], refs[n:2 * n]
        send_sems, recv_sems, local_sems = refs[2 * n:]
        me, peers = _peers()

        def remote(a, k):
            (pos, idx) = peers[k]
            src = ins[a].at[idx] if scatter else ins[a]
            return pltpu.make_async_remote_copy(src_ref=src, dst_ref=outs[a].at[me], send_sem=send_sems.at[a, k],
                                                recv_sem=recv_sems.at[a, k], device_id=pos, device_id_type=pl.DeviceIdType.MESH)

        def arrival(a, k):
            (pos, idx) = peers[k]
            src = ins[a].at[idx] if scatter else ins[a]
            return pltpu.make_async_remote_copy(src_ref=src, dst_ref=outs[a].at[idx], send_sem=send_sems.at[a, k],
                                                recv_sem=recv_sems.at[a, k], device_id=pos, device_id_type=pl.DeviceIdType.MESH)

        local = [pltpu.make_async_copy(ins[a].at[me] if scatter else ins[a], outs[a].at[me], local_sems.at[a]) for a in range(n)]
        sends = [remote(a, k) for k in range(N_DEV - 1) for a in range(n)]
        for cp in sends:
            cp.start()
        for cp in local:
            cp.start()
        for k in range(N_DEV - 1):
            for a in range(n):
                arrival(a, k).wait_recv()
        for cp in sends:
            cp.wait_send()
        for cp in local:
            cp.wait()

    out_shape = [jax.ShapeDtypeStruct(a.shape if scatter else (N_DEV,) + a.shape, a.dtype) for a in arrays]
    return pl.pallas_call(
        body, name=name, in_specs=[ANY] * n, out_specs=[ANY] * n, out_shape=out_shape,
        scratch_shapes=[pltpu.SemaphoreType.DMA((n, N_DEV - 1)), pltpu.SemaphoreType.DMA((n, N_DEV - 1)),
                        pltpu.SemaphoreType.DMA((n,))],
    )(*arrays)


HBM_SPEC = pl.BlockSpec(memory_space=pltpu.HBM)
SEM_SPEC = pl.BlockSpec(memory_space=pltpu.SEMAPHORE)
DATAFLOW_EFFECT = pltpu.SideEffectType.DATAFLOW_SIDE_EFFECTING
ALL_PEERS = (1, 2, 3, 4, 5, 6, 7)
SIBLING = 1
SAME_CORE_PEERS = (2, 4, 6)


def fill_own_slot(name, me, src, block, dtype):
    _, r, c = src.shape
    tr = _row_tile(r, 256)

    def body(me_ref, src_ref, out_ref):
        out_ref[...] = src_ref[...].astype(dtype)

    if block is None:
        src_index = lambda i, me_ref: (me_ref[0], i, 0)
    else:
        src_index = lambda i, me_ref: (block, i, 0)
    grid_spec = pltpu.PrefetchScalarGridSpec(
        num_scalar_prefetch=1, grid=(r // tr,), in_specs=[pl.BlockSpec((None, tr, c), src_index)],
        out_specs=pl.BlockSpec((None, tr, c), lambda i, me_ref: (me_ref[0], i, 0)))
    return pl.pallas_call(body, name=name, grid_spec=grid_spec, out_shape=_out_hbm((N_DEV, r, c), dtype),
                          compiler_params=_params(tr * c * (src.dtype.itemsize + jnp.dtype(dtype).itemsize), 1))(me, _in_hbm(src))


def _split_copy(src, land, send_sem, recv_sem, k, peers, me, arriving, forward):
    pos, idx = peers[k - 1]
    if forward:
        pos = peers[SIBLING - 1][0]
        slot = peers[(k | SIBLING) - 1][1] if arriving else idx
        src_ref, dst_ref = land.at[slot], land.at[slot]
    else:
        src_ref = land.at[me] if src is None else src.at[idx]
        dst_ref = land.at[idx if arriving else me]
    return pltpu.make_async_remote_copy(src_ref=src_ref, dst_ref=dst_ref, send_sem=send_sem, recv_sem=recv_sem,
                                        device_id=pos, device_id_type=pl.DeviceIdType.MESH)


def exchange_start(name, lands, peer_ks, srcs=None, after=None, forward=False):
    n = len(lands)
    ns = n if srcs is not None else 0
    extra = _deps([after])
    bufs = (list(srcs) if srcs is not None else []) + list(lands)

    def body(*refs):
        src, land = refs[:ns], refs[ns:ns + n]
        outs = refs[ns + n + len(extra):]
        send_sems, recv_sems, token = outs[:n], outs[n:2 * n], outs[2 * n + ns + n]
        me, peers = _peers()
        for a in range(n):
            for j, k in enumerate(peer_ks):
                _split_copy(src[a] if ns else None, land[a], send_sems[a].at[j], recv_sems[a].at[j], k, peers, me,
                            False, forward).start()
        token[...] = jnp.zeros_like(token)

    sems = [pltpu.SemaphoreType.DMA((len(peer_ks),))] * (2 * n)
    res = pl.pallas_call(
        body, name=name, in_specs=[HBM_SPEC] * len(bufs) + [ANY] * len(extra),
        out_specs=[SEM_SPEC] * (2 * n) + [HBM_SPEC] * len(bufs) + [pl.BlockSpec(memory_space=pltpu.VMEM)],
        out_shape=sems + [pltpu.HBM(a.shape, a.dtype) for a in bufs] + [jax.ShapeDtypeStruct((8, 128), F32)],
        input_output_aliases={i: 2 * n + i for i in range(len(bufs))},
        compiler_params=pltpu.CompilerParams(has_side_effects=DATAFLOW_EFFECT),
    )(*[_in_hbm(a) for a in bufs], *extra)
    handles = [(res[a], res[n + a], res[2 * n + a] if ns else None, res[2 * n + ns + a]) for a in range(n)]
    return handles, res[2 * n + ns + n]


def exchange_wait(name, handle, peer_ks, after, forward=False, with_source=False):
    send_sem, recv_sem, src, land = handle
    bufs = ([src] if src is not None else []) + [land]
    nb = len(bufs)

    def body(*refs):
        src_ref = refs[0] if nb == 2 else None
        land_ref, send_ref, recv_ref = refs[nb - 1], refs[nb], refs[nb + 1]
        me, peers = _peers()
        for j, k in enumerate(peer_ks):
            cp = _split_copy(src_ref, land_ref, send_ref.at[j], recv_ref.at[j], k, peers, me, True, forward)
            cp.wait_send()
            cp.wait_recv()

    outs = pl.pallas_call(
        body, name=name, in_specs=[HBM_SPEC] * nb + [SEM_SPEC, SEM_SPEC, ANY], out_specs=[HBM_SPEC] * nb,
        out_shape=[pltpu.HBM(a.shape, a.dtype) for a in bufs],
        input_output_aliases={i: i for i in range(nb)}, compiler_params=pltpu.CompilerParams(has_side_effects=DATAFLOW_EFFECT),
    )(*bufs, send_sem, recv_sem, after)
    return tuple(outs) if with_source else outs[nb - 1]


def _row_tile(r, cap):
    if r <= cap:
        return r
    return max(t for t in range(16, cap + 1, 16) if r % t == 0)


def sum_adamw(name, parts, w, m, v, tr, me=None, own=None):
    nl, r, c = w.shape
    tr = _row_tile(r, tr)
    c1 = 1.0 - ADAM_B1 ** ADAM_STEP
    c2 = 1.0 - ADAM_B2 ** ADAM_STEP
    nown = nl if own is not None else 0

    def body(*refs):
        me_ref = refs[0] if nown else None
        refs = refs[1:] if nown else refs
        part_refs, own_refs = refs[:nl], refs[nl:nl + nown]
        w_ref, m_ref, v_ref, g_out, d_out, m_out, v_out = refs[nl + nown:]
        layer = pl.program_id(0)
        for j in range(nl):
            @pl.when(layer == j)
            def _(j=j):
                g = None
                for p in range(N_DEV):
                    term = part_refs[j][p].astype(F32)
                    if nown:
                        term = jnp.where(me_ref[0] == p, own_refs[j][...].astype(F32), term)
                    g = term if g is None else g + term
                mn = ADAM_B1 * m_ref[...] + (1.0 - ADAM_B1) * g
                vn = ADAM_B2 * v_ref[...] + (1.0 - ADAM_B2) * (g * g)
                g_out[...] = g
                m_out[...] = mn
                v_out[...] = vn
                d_out[...] = -ADAM_LR * ((mn / c1) / (jnp.sqrt(vn / c2) + ADAM_EPS) + ADAM_WD * w_ref[...])

    def part_spec(j):
        return pl.BlockSpec((N_DEV, tr, c), lambda l, i, *_: (0, jnp.where(l == j, i, 0), 0))

    def own_spec(j):
        return pl.BlockSpec((None, tr, c), lambda l, i, me_ref: (me_ref[0], jnp.where(l == j, i, 0), 0))

    lspec = pl.BlockSpec((None, tr, c), lambda l, i, *_: (l, i, 0))
    out = jax.ShapeDtypeStruct((nl, r, c), F32)
    in_specs = [part_spec(j) for j in range(nl)] + [own_spec(j) for j in range(nown)] + [lspec] * 3
    vmem_blocks = (N_DEV + 1) * nl * tr * c * parts[0].dtype.itemsize + 7 * tr * c * 4
    if not nown:
        return _row_call(name, body, (nl, r // tr), in_specs, [lspec] * 4, [out] * 4, tuple(parts) + (w, m, v),
                         vmem_blocks=vmem_blocks)
    grid_spec = pltpu.PrefetchScalarGridSpec(num_scalar_prefetch=1, grid=(nl, r // tr), in_specs=in_specs, out_specs=[lspec] * 4)
    return pl.pallas_call(body, name=name, grid_spec=grid_spec, out_shape=[_out_hbm(out.shape, out.dtype)] * 4,
                          compiler_params=_params(vmem_blocks, 2))(me, *[_in_hbm(a) for a in tuple(parts) + tuple(own) + (w, m, v)])


REPLICATED = ("norm_mix_pre", "norm_mix_post", "norm_ffn_pre", "norm_ffn_post", "conv_b", "conv_ln_g", "conv_ln_b",
              "sgu_ln_g", "sgu_ln_b", "w_spatial", "b_spatial")
MATRICES = ("w_in", "w_a_out", "w_b_out", "w_o", "w_gate_up", "w_down")


def local_step(x, target, rep, weight, emit, b_gate, conv_w, start_token=None):
    s, d = x.shape
    causal = jnp.tril(jnp.ones((CHUNK, CHUNK), dtype=bool))
    row = lambda name, l: rep[name][l].reshape(1, -1)

    saved = []
    h = rms_fwd(x, row("norm_mix_pre", 0), deps=[start_token])
    for l in range(DEPTH):
        w_mix = jnp.where(causal[None], rep["w_spatial"][l], 0.0).astype(BF16)
        b_mix = jnp.broadcast_to(rep["b_spatial"][l][:, :, None], (SGU_GROUPS, CHUNK, d // SGU_GROUPS))
        proj = mm_nn("proj", h, weight(l, "w_in", h), BF16, 512)
        c = glu_conv_fwd(proj, conv_w[l], row("conv_b", l), d)
        s_act = ln_silu_fwd(c, row("conv_ln_g", l), row("conv_ln_b", l))
        p_act = sgu_fwd(proj, row("sgu_ln_g", l), row("sgu_ln_b", l), w_mix, b_mix, d, 2)
        y_a, y_b, merged = branches_merge(s_act, p_act, weight(l, "w_a_out", s_act), weight(l, "w_b_out", p_act),
                                          proj, b_gate[l], 4)
        o = mm_nn("branch_out", merged, weight(l, "w_o", merged), F32, 512)
        x_mid, h2 = norm_res(x, o, row("norm_mix_post", l), row("norm_ffn_pre", l))
        g_act, u_act, f = gate_up_swiglu(h2, weight(l, "w_gate_up", h2))
        o2 = mm_nn("down", f, weight(l, "w_down", f), F32, 512, tn=1024)
        saved.append(dict(x_in=x, h=h, proj=proj, c=c, s_act=s_act, p_act=p_act, y_a=y_a, y_b=y_b, merged=merged, o=o,
                          x_mid=x_mid, h2=h2, g_act=g_act, u_act=u_act, f=f, o2=o2, w_mix=w_mix, b_mix=b_mix))
        if l + 1 < DEPTH:
            x, h = norm_res(x_mid, o2, row("norm_ffn_post", l), row("norm_mix_pre", l + 1))

    top = saved[-1]
    loss_vec, dx, do2, dg_ffn_post = final_norm_loss(top["x_mid"], top["o2"], row("norm_ffn_post", DEPTH - 1), target)
    loss = (0.5 / d) * jnp.sum(loss_vec)

    grads = [None] * DEPTH
    for l in reversed(range(DEPTH)):
        sv = saved[l]
        wl = {name: weight(l, name, None)[0] for name in MATRICES}
        g = {"norm_ffn_post": dg_ffn_post}
        df = mm_nt("d_down_in", do2, wl["w_down"], BF16, 512, tn=wl["w_down"].shape[0] // 4)
        tok = emit(l, "w_down", mm_tn("d_down_w", sv["f"], do2, BF16, 512, sv["f"].shape[1] // 4))
        dgu = swiglu_bwd(sv["g_act"], sv["u_act"], df, deps=[tok])
        dh2 = mm_nt("d_gate_up_in", dgu, wl["w_gate_up"], F32, 1024)
        tok = emit(l, "w_gate_up", mm_tn("d_gate_up_w", sv["h2"], dgu, BF16, 2048, d // 2, nb=wl["w_gate_up"].shape[2]))
        dx, do, g["norm_ffn_pre"], g["norm_mix_post"] = norm_bwd_in_out(
            dx, dh2, sv["x_mid"], row("norm_ffn_pre", l), sv["o"], row("norm_mix_post", l), deps=[tok])
        dm = mm_nt("d_square_in", do, wl["w_o"], F32, 512)
        tok = emit(l, "w_o", mm_tn("d_square_w", sv["merged"], do, BF16, 512, d // 2))
        dy_a, dy_b, dproj, g["b_gate"] = merge_bwd(dm, sv["y_a"], sv["y_b"], sv["proj"], b_gate[l], 4, deps=[tok])
        ds = mm_nt("d_square_in", dy_a, wl["w_a_out"], F32, 512)
        tok = emit(l, "w_a_out", mm_tn("d_square_w", sv["s_act"], dy_a, BF16, 512, d // 2))
        dp = mm_nt("d_square_in", dy_b, wl["w_b_out"], F32, 512, deps=[tok])
        tok = emit(l, "w_b_out", mm_tn("d_square_w", sv["p_act"], dy_b, BF16, 512, d // 2))
        dc, g["conv_ln_g"], g["conv_ln_b"], g["conv_b"] = ln_silu_bwd(
            sv["c"], ds, row("conv_ln_g", l), row("conv_ln_b", l), deps=[tok])
        dglu, g["conv_w"] = conv_bwd(sv["proj"], dc, conv_w[l], d)
        dproj = glu_bwd(dproj, dglu, sv["proj"], d)
        dproj, dw_mix, db_mix, g["sgu_ln_g"], g["sgu_ln_b"] = sgu_bwd(
            dproj, sv["proj"], dp, row("sgu_ln_g", l), row("sgu_ln_b", l), sv["w_mix"],
            jnp.swapaxes(sv["w_mix"], 1, 2), sv["b_mix"], d, 2)
        g["w_spatial"] = jnp.where(causal[None], dw_mix, 0.0)
        g["b_spatial"] = db_mix[:, :, 0]
        tok = emit(l, "w_in", mm_tn("d_in_w", sv["h"], dproj, BF16, 2048, d // 2, nb=wl["w_in"].shape[2]))
        dh = mm_nt("d_in_in", dproj, wl["w_in"], F32, 1024, deps=[tok])
        if l > 0:
            below = saved[l - 1]
            dx, do2, g["norm_mix_pre"], dg_ffn_post = norm_bwd_in_out(
                dx, dh, sv["x_in"], row("norm_mix_pre", l), below["o2"], row("norm_ffn_post", l - 1))
        else:
            dx, g["norm_mix_pre"] = norm_bwd_in(dx, dh, sv["x_in"], row("norm_mix_pre", l))
        grads[l] = g
    return loss, dx, grads


def _pack_rows(arrays):
    return jnp.concatenate([a.reshape(-1, 128) for a in arrays], axis=0)


def _unpack_rows(packed, shapes):
    out, r0 = [], 0
    for shp in shapes:
        nr = math.prod(shp) // 128
        out.append(packed[r0:r0 + nr].reshape(shp))
        r0 += nr
    return out


def kernel(x, norm_mix_pre, norm_mix_post, norm_ffn_pre, norm_ffn_post, w_in, b_gate, conv_w, conv_b, conv_ln_g, conv_ln_b, w_a_out, sgu_ln_g, sgu_ln_b, w_spatial, b_spatial, w_b_out, w_o, w_gate_up, w_down, loss_target, m_norm_mix_pre, m_norm_mix_post, m_norm_ffn_pre, m_norm_ffn_post, m_w_in, m_b_gate, m_conv_w, m_conv_b, m_conv_ln_g, m_conv_ln_b, m_w_a_out, m_sgu_ln_g, m_sgu_ln_b, m_w_spatial, m_b_spatial, m_w_b_out, m_w_o, m_w_gate_up, m_w_down, v_norm_mix_pre, v_norm_mix_post, v_norm_ffn_pre, v_norm_ffn_post, v_w_in, v_b_gate, v_conv_w, v_conv_b, v_conv_ln_g, v_conv_ln_b, v_w_a_out, v_sgu_ln_g, v_sgu_ln_b, v_w_spatial, v_b_spatial, v_w_b_out, v_w_o, v_w_gate_up, v_w_down):
    names = ("norm_mix_pre", "norm_mix_post", "norm_ffn_pre", "norm_ffn_post", "w_in", "b_gate", "conv_w", "conv_b",
             "conv_ln_g", "conv_ln_b", "w_a_out", "sgu_ln_g", "sgu_ln_b", "w_spatial", "b_spatial", "w_b_out", "w_o",
             "w_gate_up", "w_down")
    w = dict(zip(names, (norm_mix_pre, norm_mix_post, norm_ffn_pre, norm_ffn_post, w_in, b_gate, conv_w, conv_b,
                         conv_ln_g, conv_ln_b, w_a_out, sgu_ln_g, sgu_ln_b, w_spatial, b_spatial, w_b_out, w_o,
                         w_gate_up, w_down)))
    m = dict(zip(names, (m_norm_mix_pre, m_norm_mix_post, m_norm_ffn_pre, m_norm_ffn_post, m_w_in, m_b_gate, m_conv_w,
                         m_conv_b, m_conv_ln_g, m_conv_ln_b, m_w_a_out, m_sgu_ln_g, m_sgu_ln_b, m_w_spatial,
                         m_b_spatial, m_w_b_out, m_w_o, m_w_gate_up, m_w_down)))
    v = dict(zip(names, (v_norm_mix_pre, v_norm_mix_post, v_norm_ffn_pre, v_norm_ffn_post, v_w_in, v_b_gate, v_conv_w,
                         v_conv_b, v_conv_ln_g, v_conv_ln_b, v_w_a_out, v_sgu_ln_g, v_sgu_ln_b, v_w_spatial,
                         v_b_spatial, v_w_b_out, v_w_o, v_w_gate_up, v_w_down)))
    d = x.shape[-1]
    shard_cols = d // N_DEV

    def small_pack(bg, cw):
        rows = jnp.concatenate([bg, cw], axis=1).reshape(DEPTH * (2 + CONV_WIDTH), shard_cols)
        return jnp.pad(rows, ((0, (-rows.shape[0]) % 8), (0, 0)))

    small_w, small_m, small_v = (small_pack(t["b_gate"], t["conv_w"]) for t in (w, m, v))

    small_full, = _exchange("gather_small", [small_w], scatter=False)
    small_full = small_full[:, :DEPTH * (2 + CONV_WIDTH)].reshape(N_DEV, DEPTH, 2 + CONV_WIDTH, shard_cols)
    small_full = jnp.transpose(small_full, (1, 2, 0, 3)).reshape(DEPTH, 2 + CONV_WIDTH, d)
    b_gate_full = small_full[:, :2]
    conv_w_full = jnp.pad(small_full[:, 2:], ((0, 0), (0, CONV_PAD - CONV_WIDTH), (0, 0)))

    me = (4 * lax.axis_index("x") + 2 * lax.axis_index("y") + lax.axis_index("c")).astype(jnp.int32).reshape(1)
    first_level = (SIBLING,) + SAME_CORE_PEERS
    gathers, token = {}, small_full
    for l in range(DEPTH):
        lands = [fill_own_slot(f"cast_{name}", me, w[name], l, BF16) for name in MATRICES]
        handles, token = exchange_start(f"gather_start_{l}", lands, first_level, after=token)
        for name, handle in zip(MATRICES, handles):
            gathers[l, name] = handle
    use_order = [(l, name) for l in range(DEPTH) for name in MATRICES]
    forwards, gathered = {}, {}

    def start_forward(i, after):
        if i >= len(use_order) or use_order[i] in forwards:
            return None
        l, name = use_order[i]
        land = exchange_wait(f"gather_wait_{name}_{l}", gathers[l, name], first_level, after)
        (forwards[l, name],), tok = exchange_start(f"forward_start_{name}_{l}", [land], SAME_CORE_PEERS, forward=True)
        return tok

    def weight(l, name, after):
        if (l, name) not in gathered:
            i = use_order.index((l, name))
            start_forward(i, after)
            tok = start_forward(i + 1, after)
            full = exchange_wait(f"forward_wait_{name}_{l}", forwards[l, name], SAME_CORE_PEERS, after, forward=True)
            gathered[l, name] = full if name in ("w_in", "w_gate_up") else full.reshape(-1, d)
            return gathered[l, name], tok
        return gathered[l, name], None

    scatters = {}

    def emit(l, name, g):
        chunks = g if g.ndim == 3 else g.reshape(N_DEV, -1, d)
        land = lax.empty(chunks.shape, chunks.dtype)
        (scatters[l, name],), tok = exchange_start(f"scatter_start_{name}_{l}", [land], ALL_PEERS, srcs=[chunks])
        return tok

    rep = {name: w[name] for name in REPLICATED}
    loss, grad_x, grads = local_step(x[0], loss_target[0], rep, weight, emit, b_gate_full, conv_w_full, token)
    loss = lax.psum(loss, MESH_AXES)

    small_g = jnp.stack([jnp.concatenate([grads[l]["b_gate"], grads[l]["conv_w"][:CONV_WIDTH]], axis=0) for l in range(DEPTH)])
    small_g = jnp.transpose(small_g.reshape(DEPTH * (2 + CONV_WIDTH), N_DEV, shard_cols), (1, 0, 2))
    small_g = jnp.pad(small_g, ((0, 0), (0, small_w.shape[0] - small_g.shape[1]), (0, 0)))
    rep_shapes = [w[name].shape for name in REPLICATED]
    rep_g = _pack_rows([jnp.stack([grads[l][name].reshape(w[name].shape[1:]) for l in range(DEPTH)]) for name in REPLICATED])
    small_land = fill_own_slot("own_small", me, small_g, None, F32)
    (small_handle,), tok_small = exchange_start("scatter_start_small", [small_land], ALL_PEERS, srcs=[small_g])
    rep_land = fill_own_slot("own_replicated", me, rep_g[None], 0, F32)
    (rep_handle,), tok_rep = exchange_start("gather_start_replicated", [rep_land], ALL_PEERS, after=tok_small)

    out = {}
    after = tok_rep
    for name in ("w_down", "w_gate_up", "w_o", "w_a_out", "w_b_out", "w_in"):
        done = [exchange_wait(f"scatter_wait_{name}_{l}", scatters[l, name], ALL_PEERS, after, with_source=True)
                for l in range(DEPTH)]
        out[name] = sum_adamw("adamw_" + name, [land for _, land in done], w[name], m[name], v[name], 128,
                              me=me, own=[chunks for chunks, _ in done])
        after = out[name][0]
    small_parts = exchange_wait("scatter_wait_small", small_handle, ALL_PEERS, after)
    rep_parts = exchange_wait("gather_wait_replicated", rep_handle, ALL_PEERS, after)
    small_res = sum_adamw("adamw_small", [small_parts], small_w[None], small_m[None], small_v[None], small_w.shape[0])
    n_small = DEPTH * (2 + CONV_WIDTH)
    small_res = [r[0, :n_small].reshape(DEPTH, 2 + CONV_WIDTH, shard_cols) for r in small_res]
    out["b_gate"] = [r[:, :2] for r in small_res]
    out["conv_w"] = [r[:, 2:] for r in small_res]
    rep_res = sum_adamw("adamw_replicated", [rep_parts], *(_pack_rows([t[name] for name in REPLICATED])[None] for t in (w, m, v)), 672)
    rep_res = [_unpack_rows(r[0], rep_shapes) for r in rep_res]
    for i, name in enumerate(REPLICATED):
        out[name] = [r[i] for r in rep_res]

    return (loss, grad_x[None], *[out[name][0] for name in names], *[out[name][1] for name in names],
            *[out[name][2] for name in names], *[out[name][3] for name in names])
```

```python
import functools
import math

import jax
import jax.numpy as jnp
from jax import lax
from jax.experimental import pallas as pl
from jax.experimental.pallas import tpu as pltpu

F32 = jnp.float32
BF16 = jnp.bfloat16

DEPTH = 4
N_DEV = 8
EPS = 1e-6
CONV_WIDTH = 31
CONV_PAD = 32
CHUNK = 128
SGU_GROUPS = 8

ADAM_LR = 0.001
ADAM_B1 = 0.9
ADAM_B2 = 0.999
ADAM_EPS = 1e-08
ADAM_WD = 0.01
ADAM_STEP = 10

VMEM_BYTES_V7X = 64 * 1024 * 1024
VMEM_COMPILER_SLACK = 12 * 1024 * 1024
MESH_AXES = ("x", "y", "c")
ANY = pl.BlockSpec(memory_space=pl.ANY)


def _nbytes(shape, dtype):
    return math.prod(shape) * jnp.dtype(dtype).itemsize


def _params(block_bytes, ngrid, single_bytes=0):
    limit = min(2 * block_bytes + single_bytes + VMEM_COMPILER_SLACK, VMEM_BYTES_V7X - 4 * 1024 * 1024)
    return pltpu.CompilerParams(dimension_semantics=("arbitrary",) * ngrid, vmem_limit_bytes=int(limit))


def _in_hbm(a):
    return pltpu.with_memory_space_constraint(a, pltpu.HBM)


def _out_hbm(shape, dtype):
    return pltpu.HBM(tuple(shape), dtype)


def _deps(deps):
    return [t for t in deps if t is not None]


def _mm_body(dims, nk, kaxis, ndeps):
    def body(a_ref, b_ref, *rest):
        o_ref, *acc = rest[ndeps:]

        def prod():
            return lax.dot_general(a_ref[...], b_ref[...], (dims, ((), ())), preferred_element_type=F32)

        if nk == 1:
            o_ref[...] = prod().astype(o_ref.dtype)
            return
        acc_ref, = acc
        k = pl.program_id(kaxis)

        @pl.when(k == 0)
        def _():
            acc_ref[...] = prod()

        @pl.when(k > 0)
        def _():
            acc_ref[...] += prod()

        @pl.when(k == nk - 1)
        def _():
            o_ref[...] = acc_ref[...].astype(o_ref.dtype)

    return body


def _mm_call(name, a, b, dims, grid, a_spec, b_spec, o_spec, out_shape, out_dtype, nk, kaxis, acc_shape, deps=()):
    deps = _deps(deps)
    blocks = (_nbytes([d for d in a_spec.block_shape if d], a.dtype) + _nbytes([d for d in b_spec.block_shape if d], b.dtype)
              + _nbytes([d for d in o_spec.block_shape if d], out_dtype))
    scratch = [pltpu.VMEM(acc_shape, F32)] if nk > 1 else []
    acc_bytes = _nbytes(acc_shape, F32) * (2 if nk > 1 else 1)
    return pl.pallas_call(
        _mm_body(dims, nk, kaxis, len(deps)), name=name, grid=grid, in_specs=[a_spec, b_spec] + [ANY] * len(deps),
        out_specs=o_spec, out_shape=_out_hbm(out_shape, out_dtype), scratch_shapes=scratch,
        compiler_params=_params(blocks, len(grid), acc_bytes),
    )(_in_hbm(a), _in_hbm(b), *deps)


def mm_nn(name, a, b_and_token, out_dtype, tm, tn=None, tk=None):
    b, token = b_and_token
    m, k = a.shape
    tm = min(tm, m)
    if b.ndim == 3:
        nblk, _, nb = b.shape
        return _mm_call(name, a, b, ((1,), (0,)), (nblk, m // tm),
                        pl.BlockSpec((tm, k), lambda j, i: (i, 0)), pl.BlockSpec((None, k, nb), lambda j, i: (j, 0, 0)),
                        pl.BlockSpec((tm, nb), lambda j, i: (i, j)), (m, nblk * nb), out_dtype, 1, 0, (tm, nb), [token])
    n = b.shape[1]
    tn = tn or n
    tk = tk or k
    nk = k // tk
    return _mm_call(name, a, b, ((1,), (0,)), (n // tn, m // tm, nk),
                    pl.BlockSpec((tm, tk), lambda j, i, kk: (i, kk)), pl.BlockSpec((tk, tn), lambda j, i, kk: (kk, j)),
                    pl.BlockSpec((tm, tn), lambda j, i, kk: (i, j)), (m, n), out_dtype, nk, 2, (tm, tn), [token])


def mm_nt(name, a, b, out_dtype, tm, tn=None, deps=()):
    m = a.shape[0]
    tm = min(tm, m)
    if b.ndim == 3:
        kblk, n, kb = b.shape
        return _mm_call(name, a, b, ((1,), (1,)), (m // tm, kblk),
                        pl.BlockSpec((tm, kb), lambda i, kk: (i, kk)), pl.BlockSpec((None, n, kb), lambda i, kk: (kk, 0, 0)),
                        pl.BlockSpec((tm, n), lambda i, kk: (i, 0)), (m, n), out_dtype, kblk, 1, (tm, n), deps)
    n, kc = b.shape
    tn = tn or n
    return _mm_call(name, a, b, ((1,), (1,)), (n // tn, m // tm),
                    pl.BlockSpec((tm, kc), lambda j, i: (i, 0)), pl.BlockSpec((tn, kc), lambda j, i: (j, 0)),
                    pl.BlockSpec((tm, tn), lambda j, i: (i, j)), (m, n), out_dtype, 1, 0, (tm, tn), deps)


def mm_tn(name, a, b, out_dtype, tm, tr, nb=None):
    m, k = a.shape
    n = b.shape[1]
    tm = min(tm, m)
    nm = m // tm
    if nb is not None:
        return _mm_call(name, a, b, ((0,), (0,)), (n // nb, k // tr, nm),
                        pl.BlockSpec((tm, tr), lambda j, r, mm: (mm, r)), pl.BlockSpec((tm, nb), lambda j, r, mm: (mm, j)),
                        pl.BlockSpec((None, tr, nb), lambda j, r, mm: (j, r, 0)), (n // nb, k, nb), out_dtype, nm, 2, (tr, nb))
    return _mm_call(name, a, b, ((0,), (0,)), (k // tr, nm),
                    pl.BlockSpec((tm, tr), lambda r, mm: (mm, r)), pl.BlockSpec((tm, n), lambda r, mm: (mm, 0)),
                    pl.BlockSpec((tr, n), lambda r, mm: (r, 0)), (k, n), out_dtype, nm, 1, (tr, n))


def _row_call(name, body, grid, in_specs, out_specs, out_shape, arrays, scratch=(), aliases=None, vmem_blocks=0, deps=()):
    deps = _deps(deps)
    nin = len(arrays)

    def with_deps(*refs):
        body(*refs[:nin], *refs[nin + len(deps):])

    single = not isinstance(out_shape, (list, tuple))
    outs = [_out_hbm(o.shape, o.dtype) for o in ([out_shape] if single else out_shape)]
    return pl.pallas_call(
        with_deps, name=name, grid=grid, in_specs=list(in_specs) + [ANY] * len(deps), out_specs=out_specs,
        out_shape=outs[0] if single else outs, scratch_shapes=list(scratch), input_output_aliases=aliases or {},
        compiler_params=_params(vmem_blocks, len(grid)),
    )(*[_in_hbm(a) for a in arrays], *deps)


def _rows(tm, d, col=0):
    return pl.BlockSpec((tm, d), lambda i, *_: (i, col))


def _vec(d):
    return pl.BlockSpec((1, d), lambda *_: (0, 0))


def _rstd(x):
    return lax.rsqrt(jnp.mean(x * x, axis=-1, keepdims=True) + EPS)


def _rms_bwd(dy, x, g):
    r = _rstd(x)
    n = x * r
    w = dy * g
    dx = r * (w - n * jnp.mean(w * n, axis=-1, keepdims=True))
    return dx, jnp.sum(dy * n, axis=0, keepdims=True)


def _accumulate(ref, value, first):
    @pl.when(first)
    def _():
        ref[...] = value

    @pl.when(jnp.logical_not(first))
    def _():
        ref[...] += value


def rms_fwd(x, g, tm=256, deps=()):
    s, d = x.shape
    tm = min(tm, s)

    def body(x_ref, g_ref, h_ref):
        xv = x_ref[...]
        h_ref[...] = (xv * _rstd(xv) * g_ref[...]).astype(BF16)

    return _row_call("rms_fwd", body, (s // tm,), [_rows(tm, d), _vec(d)], _rows(tm, d),
                     jax.ShapeDtypeStruct((s, d), BF16), (x, g), vmem_blocks=tm * d * 6, deps=deps)


def norm_res(x_in, o, g_post, g_next, tm=256):
    s, d = x_in.shape
    tm = min(tm, s)

    def body(x_ref, o_ref, gp_ref, gn_ref, xo_ref, h_ref):
        ov = o_ref[...]
        xo = x_ref[...] + (ov * _rstd(ov) * gp_ref[...])
        xo_ref[...] = xo
        h_ref[...] = (xo * _rstd(xo) * gn_ref[...]).astype(BF16)

    return _row_call("norm_res", body, (s // tm,), [_rows(tm, d), _rows(tm, d), _vec(d), _vec(d)],
                     [_rows(tm, d), _rows(tm, d)],
                     [jax.ShapeDtypeStruct((s, d), F32), jax.ShapeDtypeStruct((s, d), BF16)],
                     (x_in, o, g_post, g_next), vmem_blocks=tm * d * 14)


def final_norm_loss(x_in, o, g_post, target, tm=256):
    s, d = x_in.shape
    tm = min(tm, s)

    def body(x_ref, o_ref, gp_ref, t_ref, loss_ref, dy_ref, do_ref, dg_ref):
        first = pl.program_id(0) == 0
        ov = o_ref[...]
        g = gp_ref[...]
        diff = x_ref[...] + (ov * _rstd(ov) * g) - t_ref[...]
        _accumulate(loss_ref, jnp.sum(diff * diff, axis=0, keepdims=True), first)
        dy = diff * (1.0 / d)
        dy_ref[...] = dy
        do, dg = _rms_bwd(dy, ov, g)
        do_ref[...] = do.astype(BF16)
        _accumulate(dg_ref, dg, first)

    return _row_call("final_norm_loss", body, (s // tm,), [_rows(tm, d), _rows(tm, d), _vec(d), _rows(tm, d)],
                     [_vec(d), _rows(tm, d), _rows(tm, d), _vec(d)],
                     [jax.ShapeDtypeStruct((1, d), F32), jax.ShapeDtypeStruct((s, d), F32),
                      jax.ShapeDtypeStruct((s, d), BF16), jax.ShapeDtypeStruct((1, d), F32)],
                     (x_in, o, g_post, target), vmem_blocks=tm * d * 18)


def norm_bwd_in_out(dx_out, dh, x_in, g_pre, o_below, g_post_below, tm=256, deps=()):
    s, d = x_in.shape
    tm = min(tm, s)

    def body(dxo_ref, dh_ref, x_ref, g_ref, o_ref, gb_ref, dxi_ref, do_ref, dg_ref, dgb_ref):
        first = pl.program_id(0) == 0
        dx, dg = _rms_bwd(dh_ref[...].astype(F32), x_ref[...], g_ref[...])
        dxi = dxo_ref[...] + dx
        dxi_ref[...] = dxi
        _accumulate(dg_ref, dg, first)
        do, dgb = _rms_bwd(dxi, o_ref[...], gb_ref[...])
        do_ref[...] = do.astype(BF16)
        _accumulate(dgb_ref, dgb, first)

    return _row_call("norm_bwd_in_out", body, (s // tm,),
                     [_rows(tm, d), _rows(tm, d), _rows(tm, d), _vec(d), _rows(tm, d), _vec(d)],
                     [_rows(tm, d), _rows(tm, d), _vec(d), _vec(d)],
                     [jax.ShapeDtypeStruct((s, d), F32), jax.ShapeDtypeStruct((s, d), BF16),
                      jax.ShapeDtypeStruct((1, d), F32), jax.ShapeDtypeStruct((1, d), F32)],
                     (dx_out, dh, x_in, g_pre, o_below, g_post_below), vmem_blocks=tm * d * 22, deps=deps)


def norm_bwd_in(dx_out, dh, x_in, g_pre, tm=256):
    s, d = x_in.shape
    tm = min(tm, s)

    def body(dxo_ref, dh_ref, x_ref, g_ref, dxi_ref, dg_ref):
        dx, dg = _rms_bwd(dh_ref[...].astype(F32), x_ref[...], g_ref[...])
        dxi_ref[...] = dxo_ref[...] + dx
        _accumulate(dg_ref, dg, pl.program_id(0) == 0)

    return _row_call("norm_bwd_in", body, (s // tm,), [_rows(tm, d), _rows(tm, d), _rows(tm, d), _vec(d)],
                     [_rows(tm, d), _vec(d)],
                     [jax.ShapeDtypeStruct((s, d), F32), jax.ShapeDtypeStruct((1, d), F32)],
                     (dx_out, dh, x_in, g_pre), vmem_blocks=tm * d * 16)


CONV_COLS = 256
CONV_ROWS = 32


SUBLANES = 8


def _shifted_copies(ext_ref, sh_ref):
    n = sh_ref.shape[1]
    for r in range(SUBLANES):
        sh_ref[r] = ext_ref[r:r + n, :]


def _window(sh_ref, off):
    return sh_ref[off % SUBLANES, off - off % SUBLANES:off - off % SUBLANES + CONV_ROWS, :]


def _depthwise(sh_ref, w, n_out, in_off, flip, emit):
    for r0 in range(0, n_out, CONV_ROWS):
        acc = None
        for k in range(CONV_WIDTH):
            term = w[k:k + 1, :] * _window(sh_ref, r0 + in_off + (CONV_WIDTH - 1 - k if flip else k))
            acc = term if acc is None else acc + term
        emit(r0, acc)


def _ext_scratch(tm):
    return [pltpu.VMEM((tm + CONV_PAD + SUBLANES, CONV_COLS), F32), pltpu.VMEM((SUBLANES, tm + CONV_PAD, CONV_COLS), F32)]


def glu_conv_fwd(proj, conv_w, conv_b, c_ch, tm=512):
    s = proj.shape[0]
    tm = min(tm, s)
    ncb = c_ch // CONV_COLS
    hb = tm // CONV_PAD

    def body(a_ref, g_ref, ah_ref, gh_ref, w_ref, b_ref, c_ref, ext_ref, sh_ref):
        i = pl.program_id(1)
        halo = ah_ref[...].astype(F32) * jax.nn.sigmoid(gh_ref[...].astype(F32))
        ext_ref[0:CONV_PAD, :] = jnp.where(i > 0, halo, 0.0)
        ext_ref[CONV_PAD:CONV_PAD + tm, :] = a_ref[...].astype(F32) * jax.nn.sigmoid(g_ref[...].astype(F32))
        ext_ref[CONV_PAD + tm:, :] = jnp.zeros((SUBLANES, CONV_COLS), F32)
        _shifted_copies(ext_ref, sh_ref)
        w = w_ref[...]
        bias = b_ref[...]

        def emit(r0, acc):
            c_ref[r0:r0 + CONV_ROWS, :] = acc + bias

        _depthwise(sh_ref, w, tm, CONV_PAD - (CONV_WIDTH - 1), False, emit)

    main = lambda col0: pl.BlockSpec((tm, CONV_COLS), lambda c, i: (i, col0 + c))
    halo = lambda col0: pl.BlockSpec((CONV_PAD, CONV_COLS), lambda c, i: (jnp.maximum(i * hb - 1, 0), col0 + c))
    return _row_call("glu_conv_fwd", body, (ncb, s // tm),
                     [main(0), main(ncb), halo(0), halo(ncb),
                      pl.BlockSpec((CONV_PAD, CONV_COLS), lambda c, i: (0, c)), pl.BlockSpec((1, CONV_COLS), lambda c, i: (0, c))],
                     pl.BlockSpec((tm, CONV_COLS), lambda c, i: (i, c)), jax.ShapeDtypeStruct((s, c_ch), F32),
                     (proj, proj, proj, proj, conv_w, conv_b),
                     scratch=_ext_scratch(tm), vmem_blocks=tm * CONV_COLS * 32)


def _layer_norm_stats(x):
    mu = jnp.mean(x, axis=-1, keepdims=True)
    xc = x - mu
    rstd = lax.rsqrt(jnp.mean(xc * xc, axis=-1, keepdims=True) + EPS)
    return xc * rstd, rstd


def _layer_norm_bwd(dy, xhat, rstd, g):
    dxh = dy * g
    return rstd * (dxh - jnp.mean(dxh, axis=-1, keepdims=True) - xhat * jnp.mean(dxh * xhat, axis=-1, keepdims=True))


def ln_silu_fwd(c, ln_g, ln_b, tm=256):
    s, d = c.shape
    tm = min(tm, s)

    def body(c_ref, g_ref, b_ref, s_ref):
        xhat, _ = _layer_norm_stats(c_ref[...])
        s_ref[...] = jax.nn.silu(xhat * g_ref[...] + b_ref[...]).astype(BF16)

    return _row_call("ln_silu_fwd", body, (s // tm,), [_rows(tm, d), _vec(d), _vec(d)], _rows(tm, d),
                     jax.ShapeDtypeStruct((s, d), BF16), (c, ln_g, ln_b), vmem_blocks=tm * d * 10)


def ln_silu_bwd(c, ds, ln_g, ln_b, tm=256, deps=()):
    s, d = c.shape
    tm = min(tm, s)

    def body(c_ref, ds_ref, g_ref, b_ref, dc_ref, dg_ref, db_ref, dcb_ref):
        first = pl.program_id(0) == 0
        g = g_ref[...]
        xhat, rstd = _layer_norm_stats(c_ref[...])
        y = xhat * g + b_ref[...]
        sg = jax.nn.sigmoid(y)
        dln = ds_ref[...].astype(F32) * (sg * (1.0 + y * (1.0 - sg)))
        _accumulate(dg_ref, jnp.sum(dln * xhat, axis=0, keepdims=True), first)
        _accumulate(db_ref, jnp.sum(dln, axis=0, keepdims=True), first)
        dc = _layer_norm_bwd(dln, xhat, rstd, g)
        dc_ref[...] = dc
        _accumulate(dcb_ref, jnp.sum(dc, axis=0, keepdims=True), first)

    vec = jax.ShapeDtypeStruct((1, d), F32)
    return _row_call("ln_silu_bwd", body, (s // tm,), [_rows(tm, d), _rows(tm, d), _vec(d), _vec(d)],
                     [_rows(tm, d), _vec(d), _vec(d), _vec(d)], [jax.ShapeDtypeStruct((s, d), F32), vec, vec, vec],
                     (c, ds, ln_g, ln_b), vmem_blocks=tm * d * 20, deps=deps)


def conv_bwd(proj, dc, conv_w, c_ch, tm=512):
    s = proj.shape[0]
    tm = min(tm, s)
    ncb = c_ch // CONV_COLS
    hb = tm // CONV_PAD
    last_halo = s // CONV_PAD - 1
    n_i = s // tm

    def body(a_ref, g_ref, ah_ref, gh_ref, dc_ref, dcn_ref, w_ref, dglu_ref, dw_ref,
             ext_ref, sh_ref, dce_ref, dsh_ref, dwacc_ref):
        i = pl.program_id(1)
        zeros = jnp.zeros((SUBLANES, CONV_COLS), F32)
        halo = ah_ref[...].astype(F32) * jax.nn.sigmoid(gh_ref[...].astype(F32))
        ext_ref[0:CONV_PAD, :] = jnp.where(i > 0, halo, 0.0)
        ext_ref[CONV_PAD:CONV_PAD + tm, :] = a_ref[...].astype(F32) * jax.nn.sigmoid(g_ref[...].astype(F32))
        ext_ref[CONV_PAD + tm:, :] = zeros
        _shifted_copies(ext_ref, sh_ref)
        dce_ref[0:tm, :] = dc_ref[...]
        dce_ref[tm:tm + CONV_PAD, :] = jnp.where(i < n_i - 1, dcn_ref[...], 0.0)
        dce_ref[tm + CONV_PAD:, :] = zeros
        _shifted_copies(dce_ref, dsh_ref)
        w = w_ref[...]

        def emit(r0, acc):
            dglu_ref[r0:r0 + CONV_ROWS, :] = acc

        _depthwise(dsh_ref, w, tm, 0, True, emit)

        for k in range(CONV_WIDTH):
            acc = None
            for r0 in range(0, tm, CONV_ROWS):
                term = dce_ref[r0:r0 + CONV_ROWS, :] * _window(sh_ref, r0 + CONV_PAD - (CONV_WIDTH - 1) + k)
                acc = term if acc is None else acc + term
            dwacc_ref[k:k + 1, :] = jnp.sum(acc, axis=0, keepdims=True)
        dwacc_ref[CONV_WIDTH:, :] = jnp.zeros((CONV_PAD - CONV_WIDTH, CONV_COLS), F32)
        _accumulate(dw_ref, dwacc_ref[...], i == 0)

    main = lambda col0: pl.BlockSpec((tm, CONV_COLS), lambda c, i: (i, col0 + c))
    halo = lambda col0: pl.BlockSpec((CONV_PAD, CONV_COLS), lambda c, i: (jnp.maximum(i * hb - 1, 0), col0 + c))
    nxt = pl.BlockSpec((CONV_PAD, CONV_COLS), lambda c, i: (jnp.minimum((i + 1) * hb, last_halo), c))
    wspec = pl.BlockSpec((CONV_PAD, CONV_COLS), lambda c, i: (0, c))
    return _row_call("conv_bwd", body, (ncb, n_i),
                     [main(0), main(ncb), halo(0), halo(ncb), main(0), nxt, wspec],
                     [main(0), wspec],
                     [jax.ShapeDtypeStruct((s, c_ch), F32), jax.ShapeDtypeStruct((CONV_PAD, c_ch), F32)],
                     (proj, proj, proj, proj, dc, dc, conv_w),
                     scratch=_ext_scratch(tm) + _ext_scratch(tm) + [pltpu.VMEM((CONV_PAD, CONV_COLS), F32)],
                     vmem_blocks=tm * CONV_COLS * 56)


ELEMENTWISE_COLS = 512


def _col_chunks(d):
    return [slice(c0, c0 + ELEMENTWISE_COLS) for c0 in range(0, d, ELEMENTWISE_COLS)]


def _pair_spec(tm, d, pair):
    return pl.BlockSpec((tm, 2 * d), lambda i: (i, pair))


def glu_bwd(dproj, dglu, proj, c_ch, tm=256):
    s = proj.shape[0]
    tm = min(tm, s)

    def body(_, dglu_ref, a_ref, g_ref, out_ref):
        for cols in _col_chunks(c_ch):
            dg = dglu_ref[:, cols]
            sg = jax.nn.sigmoid(g_ref[:, cols].astype(F32))
            out_ref[:, cols] = (dg * sg).astype(BF16)
            out_ref[:, c_ch + cols.start:c_ch + cols.stop] = (dg * a_ref[:, cols].astype(F32) * (sg * (1.0 - sg))).astype(BF16)

    return _row_call("glu_bwd", body, (s // tm,), [ANY, _rows(tm, c_ch), _rows(tm, c_ch), _rows(tm, c_ch, 1)],
                     _pair_spec(tm, c_ch, 0), jax.ShapeDtypeStruct(dproj.shape, BF16), (dproj, dglu, proj, proj),
                     aliases={0: 0}, vmem_blocks=tm * c_ch * 12)


_SQRT_HALF = 0.7071067811865476
_INV_SQRT_2PI = 0.3989422804014327


def _gelu_parts(x):
    cdf = 0.5 * (1.0 + lax.erf(x * _SQRT_HALF))
    return cdf, x * cdf


def _gelu_grad(x, cdf):
    return cdf + x * (_INV_SQRT_2PI * jnp.exp(-0.5 * x * x))


def _sgu_specs(tm, ch):
    grp = ch // SGU_GROUPS
    full3 = lambda shape: pl.BlockSpec(shape, lambda *_: (0, 0, 0))
    return grp, full3((SGU_GROUPS, CHUNK, CHUNK)), full3((SGU_GROUPS, CHUNK, grp))


def sgu_fwd(proj, ln_g, ln_b, w_mix, b_mix, ch, col0, tm=CHUNK):
    s = proj.shape[0]
    grp, wspec, bspec = _sgu_specs(tm, ch)

    def body(u_ref, v_ref, g_ref, b_ref, w_ref, bm_ref, p_ref, mix_ref):
        _, u = _gelu_parts(u_ref[...].astype(F32))
        _, v0 = _gelu_parts(v_ref[...].astype(F32))
        xhat, _ = _layer_norm_stats(v0)
        vn = (xhat * g_ref[...] + b_ref[...]).astype(BF16)
        for n in range(tm // CHUNK):
            for g in range(SGU_GROUPS):
                blk = vn[n * CHUNK:(n + 1) * CHUNK, g * grp:(g + 1) * grp]
                mix_ref[n * CHUNK:(n + 1) * CHUNK, g * grp:(g + 1) * grp] = (
                    jnp.dot(w_ref[g], blk, preferred_element_type=F32) + bm_ref[g])
        p_ref[...] = (u * mix_ref[...]).astype(BF16)

    return _row_call("sgu_fwd", body, (s // tm,),
                     [_rows(tm, ch, col0), _rows(tm, ch, col0 + 1), _vec(ch), _vec(ch), wspec, bspec], _rows(tm, ch),
                     jax.ShapeDtypeStruct((s, ch), BF16), (proj, proj, ln_g, ln_b, w_mix, b_mix),
                     scratch=[pltpu.VMEM((tm, ch), F32)], vmem_blocks=tm * ch * 30)


def sgu_bwd(dproj, proj, dp, ln_g, ln_b, w_mix, w_mix_t, b_mix, ch, col0, tm=CHUNK):
    s = proj.shape[0]
    assert tm == CHUNK and col0 % 2 == 0
    grp, wspec, bspec = _sgu_specs(tm, ch)
    groups = [slice(k * grp, (k + 1) * grp) for k in range(SGU_GROUPS)]

    def body(_, u_ref, v_ref, dp_ref, g_ref, b_ref, w_ref, wt_ref, bm_ref,
             out_ref, dw_ref, dbm_ref, dg_ref, db_ref, u_s, gu_s, gv_s, xh_s, dvn_s):
        first = pl.program_id(0) == 0
        row_sum = lambda x: jnp.sum(x, axis=1, keepdims=True)
        total = None
        for cols in groups:
            ub = u_ref[:, cols].astype(F32)
            vb = v_ref[:, cols].astype(F32)
            cdf_u, u = _gelu_parts(ub)
            cdf_v, v0 = _gelu_parts(vb)
            u_s[:, cols] = u
            gu_s[:, cols] = _gelu_grad(ub, cdf_u)
            gv_s[:, cols] = _gelu_grad(vb, cdf_v)
            xh_s[:, cols] = v0
            total = row_sum(v0) if total is None else total + row_sum(v0)
        mu = total * (1.0 / ch)
        total = None
        for cols in groups:
            xc = xh_s[:, cols] - mu
            xh_s[:, cols] = xc
            total = row_sum(xc * xc) if total is None else total + row_sum(xc * xc)
        rstd = lax.rsqrt(total * (1.0 / ch) + EPS)

        t1 = t2 = None
        for k, cols in enumerate(groups):
            g = g_ref[:, cols]
            xhat = xh_s[:, cols] * rstd
            xh_s[:, cols] = xhat
            vn = (xhat * g + b_ref[:, cols]).astype(BF16)
            dpk = dp_ref[:, cols].astype(F32)
            dmix = dpk * u_s[:, cols]
            dmix_bf = dmix.astype(BF16)
            mixed = jnp.dot(w_ref[k], vn, preferred_element_type=F32) + bm_ref[k]
            out_ref[:, cols] = (dpk * mixed * gu_s[:, cols]).astype(BF16)
            dvn = jnp.dot(wt_ref[k], dmix_bf, preferred_element_type=F32)
            dvn_s[:, cols] = dvn
            _accumulate(dw_ref.at[k], lax.dot_general(dmix_bf, vn, (((1,), (1,)), ((), ())), preferred_element_type=F32), first)
            _accumulate(dbm_ref.at[k], jnp.broadcast_to(row_sum(dmix), (CHUNK, CHUNK)), first)
            _accumulate(dg_ref.at[:, cols], jnp.sum(dvn * xhat, axis=0, keepdims=True), first)
            _accumulate(db_ref.at[:, cols], jnp.sum(dvn, axis=0, keepdims=True), first)
            dxh = dvn * g
            t1 = row_sum(dxh) if t1 is None else t1 + row_sum(dxh)
            t2 = row_sum(dxh * xhat) if t2 is None else t2 + row_sum(dxh * xhat)
        m1 = t1 * (1.0 / ch)
        m2 = t2 * (1.0 / ch)
        for cols in groups:
            dv0 = rstd * (dvn_s[:, cols] * g_ref[:, cols] - m1 - xh_s[:, cols] * m2)
            out_ref[:, ch + cols.start:ch + cols.stop] = (dv0 * gv_s[:, cols]).astype(BF16)

    vec = _vec(ch)
    acc3 = lambda: pl.BlockSpec((SGU_GROUPS, CHUNK, CHUNK), lambda i: (0, 0, 0))
    vshape = jax.ShapeDtypeStruct((1, ch), F32)
    mshape = jax.ShapeDtypeStruct((SGU_GROUPS, CHUNK, CHUNK), F32)
    return _row_call("sgu_bwd", body, (s // tm,),
                     [ANY, _rows(tm, ch, col0), _rows(tm, ch, col0 + 1), _rows(tm, ch), vec, vec, wspec, wspec, bspec],
                     [_pair_spec(tm, ch, col0 // 2), acc3(), acc3(), vec, vec],
                     [jax.ShapeDtypeStruct(dproj.shape, BF16), mshape, mshape, vshape, vshape],
                     (dproj, proj, proj, dp, ln_g, ln_b, w_mix, w_mix_t, b_mix),
                     scratch=[pltpu.VMEM((tm, ch), F32)] * 5, aliases={0: 0}, vmem_blocks=tm * ch * 40)


def branches_merge(s_act, p_act, wa_and_token, wb_and_token, proj, b_gate, col0, tm=256, tn=1024):
    (wa, tok_a), (wb, tok_b) = wa_and_token, wb_and_token
    deps = _deps([tok_a, tok_b])
    s, d = s_act.shape
    tm = min(tm, s)
    per = d // tn

    def body(s_ref, p_ref, wa_ref, wb_ref, l0_ref, l1_ref, bg_ref, *rest):
        ya_ref, yb_ref, m_ref = rest[len(deps):]
        ya = jnp.dot(s_ref[...], wa_ref[...], preferred_element_type=F32)
        yb = jnp.dot(p_ref[...], wb_ref[...], preferred_element_type=F32)
        ya_ref[...] = ya.astype(BF16)
        yb_ref[...] = yb.astype(BF16)
        g0 = jax.nn.sigmoid(l0_ref[...].astype(F32) + bg_ref[0:1, :])
        g1 = jax.nn.sigmoid(l1_ref[...].astype(F32) + bg_ref[1:2, :])
        m_ref[...] = (g0 * ya + g1 * yb).astype(BF16)

    act = pl.BlockSpec((tm, d), lambda j, i: (i, 0))
    wgt = pl.BlockSpec((d, tn), lambda j, i: (0, j))
    logits = lambda col: pl.BlockSpec((tm, tn), lambda j, i: (i, col * per + j))
    oblk = pl.BlockSpec((tm, tn), lambda j, i: (i, j))
    return pl.pallas_call(
        body, name="branches_merge", grid=(per, s // tm),
        in_specs=[act, act, wgt, wgt, logits(col0), logits(col0 + 1), pl.BlockSpec((2, tn), lambda j, i: (0, j))] + [ANY] * len(deps),
        out_specs=[oblk, oblk, oblk],
        out_shape=[_out_hbm((s, d), BF16), _out_hbm((s, d), BF16), _out_hbm((s, d), BF16)],
        compiler_params=_params(2 * tm * d * 2 + 2 * d * tn * 2 + 2 * tm * tn * 2 + tm * tn * 10, 2, 4 * tm * tn * 4),
    )(*[_in_hbm(a) for a in (s_act, p_act, wa, wb, proj, proj, b_gate)], *deps)


def merge_bwd(dm, y_a, y_b, proj, b_gate, col0, tm=256, deps=()):
    s, d = y_a.shape
    tm = min(tm, s)

    assert col0 % 2 == 0

    def body(dm_ref, ya_ref, yb_ref, l0_ref, l1_ref, bg_ref, dya_ref, dyb_ref, out_ref, dbg_ref):
        first = pl.program_id(0) == 0
        for cols in _col_chunks(d):
            dmv = dm_ref[:, cols].astype(F32)
            g0 = jax.nn.sigmoid(l0_ref[:, cols].astype(F32) + bg_ref[0:1, cols])
            g1 = jax.nn.sigmoid(l1_ref[:, cols].astype(F32) + bg_ref[1:2, cols])
            dya_ref[:, cols] = (dmv * g0).astype(BF16)
            dyb_ref[:, cols] = (dmv * g1).astype(BF16)
            dl0 = dmv * ya_ref[:, cols].astype(F32) * (g0 * (1.0 - g0))
            dl1 = dmv * yb_ref[:, cols].astype(F32) * (g1 * (1.0 - g1))
            _accumulate(dbg_ref.at[0:1, cols], jnp.sum(dl0, axis=0, keepdims=True), first)
            _accumulate(dbg_ref.at[1:2, cols], jnp.sum(dl1, axis=0, keepdims=True), first)
            out_ref[:, cols] = dl0.astype(BF16)
            out_ref[:, d + cols.start:d + cols.stop] = dl1.astype(BF16)

    bgspec = pl.BlockSpec((2, d), lambda i: (0, 0))
    return _row_call("merge_bwd", body, (s // tm,),
                     [_rows(tm, d), _rows(tm, d), _rows(tm, d), _rows(tm, d, col0), _rows(tm, d, col0 + 1), bgspec],
                     [_rows(tm, d), _rows(tm, d), _pair_spec(tm, d, col0 // 2), bgspec],
                     [jax.ShapeDtypeStruct((s, d), BF16), jax.ShapeDtypeStruct((s, d), BF16),
                      jax.ShapeDtypeStruct(proj.shape, BF16), jax.ShapeDtypeStruct((2, d), F32)],
                     (dm, y_a, y_b, proj, proj, b_gate), vmem_blocks=tm * d * 24, deps=deps)


def gate_up_swiglu(h, w_and_token, tm=256):
    w, token = w_and_token
    deps = _deps([token])
    s, d = h.shape
    nblk, _, nb = w.shape
    half = nblk // 2
    tm = min(tm, s)

    def body(h_ref, wg_ref, wu_ref, *rest):
        g_ref, u_ref, f_ref = rest[len(deps):]
        hv = h_ref[...]
        g = jnp.dot(hv, wg_ref[...], preferred_element_type=F32)
        u = jnp.dot(hv, wu_ref[...], preferred_element_type=F32)
        g_ref[...] = g.astype(BF16)
        u_ref[...] = u.astype(BF16)
        f_ref[...] = (jax.nn.silu(g) * u).astype(BF16)

    out = _out_hbm((s, half * nb), BF16)
    oblk = pl.BlockSpec((tm, nb), lambda j, i: (i, j))
    return pl.pallas_call(
        body, name="gate_up_swiglu", grid=(half, s // tm),
        in_specs=[pl.BlockSpec((tm, d), lambda j, i: (i, 0)), pl.BlockSpec((None, d, nb), lambda j, i: (j, 0, 0)),
                  pl.BlockSpec((None, d, nb), lambda j, i: (j + half, 0, 0))] + [ANY] * len(deps),
        out_specs=[oblk, oblk, oblk], out_shape=[out, out, out],
        compiler_params=_params(tm * d * 2 + 2 * d * nb * 2 + 3 * tm * nb * 2, 2, 4 * tm * nb * 4),
    )(_in_hbm(h), _in_hbm(w), _in_hbm(w), *deps)


def swiglu_bwd(g_act, u_act, df, tm=256, deps=()):
    s, half = g_act.shape
    w2 = 2 * half
    tm = min(tm, s)
    chunk = w2 // N_DEV

    def body(g_ref, u_ref, df_ref, out_ref):
        for c0 in range(0, half, chunk):
            cols = slice(c0, c0 + chunk)
            g = g_ref[:, cols].astype(F32)
            sg = jax.nn.sigmoid(g)
            dfv = df_ref[:, cols].astype(F32)
            out_ref[:, cols] = (dfv * u_ref[:, cols].astype(F32) * (sg * (1.0 + g * (1.0 - sg)))).astype(BF16)
            out_ref[:, half + c0:half + c0 + chunk] = (dfv * (g * sg)).astype(BF16)

    return _row_call("swiglu_bwd", body, (s // tm,), [_rows(tm, half), _rows(tm, half), _rows(tm, half)],
                     _rows(tm, w2), jax.ShapeDtypeStruct((s, w2), BF16), (g_act, u_act, df), vmem_blocks=tm * w2 * 5, deps=deps)


def _peers():
    x, y, c = lax.axis_index("x"), lax.axis_index("y"), lax.axis_index("c")
    me = 4 * x + 2 * y + c
    peers = []
    for k in range(1, N_DEV):
        px = 1 - x if k & 4 else x
        py = 1 - y if k & 2 else y
        pc = 1 - c if k & 1 else c
        peers.append(((px, py, pc), 4 * px + 2 * py + pc))
    return me, peers


def _exchange(name, arrays, scatter):
    n = len(arrays)

    def body(*refs):
        ins, outs = refs[:n], refs[n:2 * n]
        send_sems, recv_sems, local_sems = refs[2 * n:]
        me, peers = _peers()

        def remote(a, k):
            (pos, idx) = peers[k]
            src = ins[a].at[idx] if scatter else ins[a]
            return pltpu.make_async_remote_copy(src_ref=src, dst_ref=outs[a].at[me], send_sem=send_sems.at[a, k],
                                                recv_sem=recv_sems.at[a, k], device_id=pos, device_id_type=pl.DeviceIdType.MESH)

        def arrival(a, k):
            (pos, idx) = peers[k]
            src = ins[a].at[idx] if scatter else ins[a]
            return pltpu.make_async_remote_copy(src_ref=src, dst_ref=outs[a].at[idx], send_sem=send_sems.at[a, k],
                                                recv_sem=recv_sems.at[a, k], device_id=pos, device_id_type=pl.DeviceIdType.MESH)

        local = [pltpu.make_async_copy(ins[a].at[me] if scatter else ins[a], outs[a].at[me], local_sems.at[a]) for a in range(n)]
        sends = [remote(a, k) for k in range(N_DEV - 1) for a in range(n)]
        for cp in sends:
            cp.start()
        for cp in local:
            cp.start()
        for k in range(N_DEV - 1):
            for a in range(n):
                arrival(a, k).wait_recv()
        for cp in sends:
            cp.wait_send()
        for cp in local:
            cp.wait()

    out_shape = [jax.ShapeDtypeStruct(a.shape if scatter else (N_DEV,) + a.shape, a.dtype) for a in arrays]
    return pl.pallas_call(
        body, name=name, in_specs=[ANY] * n, out_specs=[ANY] * n, out_shape=out_shape,
        scratch_shapes=[pltpu.SemaphoreType.DMA((n, N_DEV - 1)), pltpu.SemaphoreType.DMA((n, N_DEV - 1)),
                        pltpu.SemaphoreType.DMA((n,))],
    )(*arrays)


HBM_SPEC = pl.BlockSpec(memory_space=pltpu.HBM)
SEM_SPEC = pl.BlockSpec(memory_space=pltpu.SEMAPHORE)
DATAFLOW_EFFECT = pltpu.SideEffectType.DATAFLOW_SIDE_EFFECTING
ALL_PEERS = (1, 2, 3, 4, 5, 6, 7)
SIBLING = 1
SAME_CORE_PEERS = (2, 4, 6)


def fill_own_slot(name, me, src, block, dtype):
    _, r, c = src.shape
    tr = _row_tile(r, 256)

    def body(me_ref, src_ref, out_ref):
        out_ref[...] = src_ref[...].astype(dtype)

    if block is None:
        src_index = lambda i, me_ref: (me_ref[0], i, 0)
    else:
        src_index = lambda i, me_ref: (block, i, 0)
    grid_spec = pltpu.PrefetchScalarGridSpec(
        num_scalar_prefetch=1, grid=(r // tr,), in_specs=[pl.BlockSpec((None, tr, c), src_index)],
        out_specs=pl.BlockSpec((None, tr, c), lambda i, me_ref: (me_ref[0], i, 0)))
    return pl.pallas_call(body, name=name, grid_spec=grid_spec, out_shape=_out_hbm((N_DEV, r, c), dtype),
                          compiler_params=_params(tr * c * (src.dtype.itemsize + jnp.dtype(dtype).itemsize), 1))(me, _in_hbm(src))


def _split_copy(src, land, send_sem, recv_sem, k, peers, me, arriving, forward):
    pos, idx = peers[k - 1]
    if forward:
        pos = peers[SIBLING - 1][0]
        slot = peers[(k | SIBLING) - 1][1] if arriving else idx
        src_ref, dst_ref = land.at[slot], land.at[slot]
    else:
        src_ref = land.at[me] if src is None else src.at[idx]
        dst_ref = land.at[idx if arriving else me]
    return pltpu.make_async_remote_copy(src_ref=src_ref, dst_ref=dst_ref, send_sem=send_sem, recv_sem=recv_sem,
                                        device_id=pos, device_id_type=pl.DeviceIdType.MESH)


def exchange_start(name, lands, peer_ks, srcs=None, after=None, forward=False):
    n = len(lands)
    ns = n if srcs is not None else 0
    extra = _deps([after])
    bufs = (list(srcs) if srcs is not None else []) + list(lands)

    def body(*refs):
        src, land = refs[:ns], refs[ns:ns + n]
        outs = refs[ns + n + len(extra):]
        send_sems, recv_sems, token = outs[:n], outs[n:2 * n], outs[2 * n + ns + n]
        me, peers = _peers()
        for a in range(n):
            for j, k in enumerate(peer_ks):
                _split_copy(src[a] if ns else None, land[a], send_sems[a].at[j], recv_sems[a].at[j], k, peers, me,
                            False, forward).start()
        token[...] = jnp.zeros_like(token)

    sems = [pltpu.SemaphoreType.DMA((len(peer_ks),))] * (2 * n)
    res = pl.pallas_call(
        body, name=name, in_specs=[HBM_SPEC] * len(bufs) + [ANY] * len(extra),
        out_specs=[SEM_SPEC] * (2 * n) + [HBM_SPEC] * len(bufs) + [pl.BlockSpec(memory_space=pltpu.VMEM)],
        out_shape=sems + [pltpu.HBM(a.shape, a.dtype) for a in bufs] + [jax.ShapeDtypeStruct((8, 128), F32)],
        input_output_aliases={i: 2 * n + i for i in range(len(bufs))},
        compiler_params=pltpu.CompilerParams(has_side_effects=DATAFLOW_EFFECT),
    )(*[_in_hbm(a) for a in bufs], *extra)
    handles = [(res[a], res[n + a], res[2 * n + a] if ns else None, res[2 * n + ns + a]) for a in range(n)]
    return handles, res[2 * n + ns + n]


def exchange_wait(name, handle, peer_ks, after, forward=False, with_source=False):
    send_sem, recv_sem, src, land = handle
    bufs = ([src] if src is not None else []) + [land]
    nb = len(bufs)

    def body(*refs):
        src_ref = refs[0] if nb == 2 else None
        land_ref, send_ref, recv_ref = refs[nb - 1], refs[nb], refs[nb + 1]
        me, peers = _peers()
        for j, k in enumerate(peer_ks):
            cp = _split_copy(src_ref, land_ref, send_ref.at[j], recv_ref.at[j], k, peers, me, True, forward)
            cp.wait_send()
            cp.wait_recv()

    outs = pl.pallas_call(
        body, name=name, in_specs=[HBM_SPEC] * nb + [SEM_SPEC, SEM_SPEC, ANY], out_specs=[HBM_SPEC] * nb,
        out_shape=[pltpu.HBM(a.shape, a.dtype) for a in bufs],
        input_output_aliases={i: i for i in range(nb)}, compiler_params=pltpu.CompilerParams(has_side_effects=DATAFLOW_EFFECT),
    )(*bufs, send_sem, recv_sem, after)
    return tuple(outs) if with_source else outs[nb - 1]


def _row_tile(r, cap):
    if r <= cap:
        return r
    return max(t for t in range(16, cap + 1, 16) if r % t == 0)


def sum_adamw(name, parts, w, m, v, tr, me=None, own=None):
    nl, r, c = w.shape
    tr = _row_tile(r, tr)
    c1 = 1.0 - ADAM_B1 ** ADAM_STEP
    c2 = 1.0 - ADAM_B2 ** ADAM_STEP
    nown = nl if own is not None else 0

    def body(*refs):
        me_ref = refs[0] if nown else None
        refs = refs[1:] if nown else refs
        part_refs, own_refs = refs[:nl], refs[nl:nl + nown]
        w_ref, m_ref, v_ref, g_out, d_out, m_out, v_out = refs[nl + nown:]
        layer = pl.program_id(0)
        for j in range(nl):
            @pl.when(layer == j)
            def _(j=j):
                g = None
                for p in range(N_DEV):
                    term = part_refs[j][p].astype(F32)
                    if nown:
                        term = jnp.where(me_ref[0] == p, own_refs[j][...].astype(F32), term)
                    g = term if g is None else g + term
                mn = ADAM_B1 * m_ref[...] + (1.0 - ADAM_B1) * g
                vn = ADAM_B2 * v_ref[...] + (1.0 - ADAM_B2) * (g * g)
                g_out[...] = g
                m_out[...] = mn
                v_out[...] = vn
                d_out[...] = -ADAM_LR * ((mn / c1) / (jnp.sqrt(vn / c2) + ADAM_EPS) + ADAM_WD * w_ref[...])

    def part_spec(j):
        return pl.BlockSpec((N_DEV, tr, c), lambda l, i, *_: (0, jnp.where(l == j, i, 0), 0))

    def own_spec(j):
        return pl.BlockSpec((None, tr, c), lambda l, i, me_ref: (me_ref[0], jnp.where(l == j, i, 0), 0))

    lspec = pl.BlockSpec((None, tr, c), lambda l, i, *_: (l, i, 0))
    out = jax.ShapeDtypeStruct((nl, r, c), F32)
    in_specs = [part_spec(j) for j in range(nl)] + [own_spec(j) for j in range(nown)] + [lspec] * 3
    vmem_blocks = (N_DEV + 1) * nl * tr * c * parts[0].dtype.itemsize + 7 * tr * c * 4
    if not nown:
        return _row_call(name, body, (nl, r // tr), in_specs, [lspec] * 4, [out] * 4, tuple(parts) + (w, m, v),
                         vmem_blocks=vmem_blocks)
    grid_spec = pltpu.PrefetchScalarGridSpec(num_scalar_prefetch=1, grid=(nl, r // tr), in_specs=in_specs, out_specs=[lspec] * 4)
    return pl.pallas_call(body, name=name, grid_spec=grid_spec, out_shape=[_out_hbm(out.shape, out.dtype)] * 4,
                          compiler_params=_params(vmem_blocks, 2))(me, *[_in_hbm(a) for a in tuple(parts) + tuple(own) + (w, m, v)])


REPLICATED = ("norm_mix_pre", "norm_mix_post", "norm_ffn_pre", "norm_ffn_post", "conv_b", "conv_ln_g", "conv_ln_b",
              "sgu_ln_g", "sgu_ln_b", "w_spatial", "b_spatial")
MATRICES = ("w_in", "w_a_out", "w_b_out", "w_o", "w_gate_up", "w_down")


def local_step(x, target, rep, weight, emit, b_gate, conv_w, start_token=None):
    s, d = x.shape
    causal = jnp.tril(jnp.ones((CHUNK, CHUNK), dtype=bool))
    row = lambda name, l: rep[name][l].reshape(1, -1)

    saved = []
    h = rms_fwd(x, row("norm_mix_pre", 0), deps=[start_token])
    for l in range(DEPTH):
        w_mix = jnp.where(causal[None], rep["w_spatial"][l], 0.0).astype(BF16)
        b_mix = jnp.broadcast_to(rep["b_spatial"][l][:, :, None], (SGU_GROUPS, CHUNK, d // SGU_GROUPS))
        proj = mm_nn("proj", h, weight(l, "w_in", h), BF16, 512)
        c = glu_conv_fwd(proj, conv_w[l], row("conv_b", l), d)
        s_act = ln_silu_fwd(c, row("conv_ln_g", l), row("conv_ln_b", l))
        p_act = sgu_fwd(proj, row("sgu_ln_g", l), row("sgu_ln_b", l), w_mix, b_mix, d, 2)
        y_a, y_b, merged = branches_merge(s_act, p_act, weight(l, "w_a_out", s_act), weight(l, "w_b_out", p_act),
                                          proj, b_gate[l], 4)
        o = mm_nn("branch_out", merged, weight(l, "w_o", merged), F32, 512)
        x_mid, h2 = norm_res(x, o, row("norm_mix_post", l), row("norm_ffn_pre", l))
        g_act, u_act, f = gate_up_swiglu(h2, weight(l, "w_gate_up", h2))
        o2 = mm_nn("down", f, weight(l, "w_down", f), F32, 512, tn=1024)
        saved.append(dict(x_in=x, h=h, proj=proj, c=c, s_act=s_act, p_act=p_act, y_a=y_a, y_b=y_b, merged=merged, o=o,
                          x_mid=x_mid, h2=h2, g_act=g_act, u_act=u_act, f=f, o2=o2, w_mix=w_mix, b_mix=b_mix))
        if l + 1 < DEPTH:
            x, h = norm_res(x_mid, o2, row("norm_ffn_post", l), row("norm_mix_pre", l + 1))

    top = saved[-1]
    loss_vec, dx, do2, dg_ffn_post = final_norm_loss(top["x_mid"], top["o2"], row("norm_ffn_post", DEPTH - 1), target)
    loss = (0.5 / d) * jnp.sum(loss_vec)

    grads = [None] * DEPTH
    for l in reversed(range(DEPTH)):
        sv = saved[l]
        wl = {name: weight(l, name, None)[0] for name in MATRICES}
        g = {"norm_ffn_post": dg_ffn_post}
        df = mm_nt("d_down_in", do2, wl["w_down"], BF16, 512, tn=wl["w_down"].shape[0] // 4)
        tok = emit(l, "w_down", mm_tn("d_down_w", sv["f"], do2, BF16, 512, sv["f"].shape[1] // 4))
        dgu = swiglu_bwd(sv["g_act"], sv["u_act"], df, deps=[tok])
        dh2 = mm_nt("d_gate_up_in", dgu, wl["w_gate_up"], BF16, 1024)
        tok = emit(l, "w_gate_up", mm_tn("d_gate_up_w", sv["h2"], dgu, BF16, 2048, d // 2, nb=wl["w_gate_up"].shape[2]))
        dx, do, g["norm_ffn_pre"], g["norm_mix_post"] = norm_bwd_in_out(
            dx, dh2, sv["x_mid"], row("norm_ffn_pre", l), sv["o"], row("norm_mix_post", l), deps=[tok])
        dm = mm_nt("d_square_in", do, wl["w_o"], BF16, 512)
        tok = emit(l, "w_o", mm_tn("d_square_w", sv["merged"], do, BF16, 512, d // 2))
        dy_a, dy_b, dproj, g["b_gate"] = merge_bwd(dm, sv["y_a"], sv["y_b"], sv["proj"], b_gate[l], 4, deps=[tok])
        ds = mm_nt("d_square_in", dy_a, wl["w_a_out"], BF16, 512)
        tok = emit(l, "w_a_out", mm_tn("d_square_w", sv["s_act"], dy_a, BF16, 512, d // 2))
        dp = mm_nt("d_square_in", dy_b, wl["w_b_out"], BF16, 512, deps=[tok])
        tok = emit(l, "w_b_out", mm_tn("d_square_w", sv["p_act"], dy_b, BF16, 512, d // 2))
        dc, g["conv_ln_g"], g["conv_ln_b"], g["conv_b"] = ln_silu_bwd(
            sv["c"], ds, row("conv_ln_g", l), row("conv_ln_b", l), deps=[tok])
        dglu, g["conv_w"] = conv_bwd(sv["proj"], dc, conv_w[l], d)
        dproj = glu_bwd(dproj, dglu, sv["proj"], d)
        dproj, dw_mix, db_mix, g["sgu_ln_g"], g["sgu_ln_b"] = sgu_bwd(
            dproj, sv["proj"], dp, row("sgu_ln_g", l), row("sgu_ln_b", l), sv["w_mix"],
            jnp.swapaxes(sv["w_mix"], 1, 2), sv["b_mix"], d, 2)
        g["w_spatial"] = jnp.where(causal[None], dw_mix, 0.0)
        g["b_spatial"] = db_mix[:, :, 0]
        tok = emit(l, "w_in", mm_tn("d_in_w", sv["h"], dproj, BF16, 2048, d // 2, nb=wl["w_in"].shape[2]))
        dh = mm_nt("d_in_in", dproj, wl["w_in"], BF16, 1024, deps=[tok])
        if l > 0:
            below = saved[l - 1]
            dx, do2, g["norm_mix_pre"], dg_ffn_post = norm_bwd_in_out(
                dx, dh, sv["x_in"], row("norm_mix_pre", l), below["o2"], row("norm_ffn_post", l - 1))
        else:
            dx, g["norm_mix_pre"] = norm_bwd_in(dx, dh, sv["x_in"], row("norm_mix_pre", l))
        grads[l] = g
    return loss, dx, grads


def _pack_rows(arrays):
    return jnp.concatenate([a.reshape(-1, 128) for a in arrays], axis=0)


def _unpack_rows(packed, shapes):
    out, r0 = [], 0
    for shp in shapes:
        nr = math.prod(shp) // 128
        out.append(packed[r0:r0 + nr].reshape(shp))
        r0 += nr
    return out


def kernel(x, norm_mix_pre, norm_mix_post, norm_ffn_pre, norm_ffn_post, w_in, b_gate, conv_w, conv_b, conv_ln_g, conv_ln_b, w_a_out, sgu_ln_g, sgu_ln_b, w_spatial, b_spatial, w_b_out, w_o, w_gate_up, w_down, loss_target, m_norm_mix_pre, m_norm_mix_post, m_norm_ffn_pre, m_norm_ffn_post, m_w_in, m_b_gate, m_conv_w, m_conv_b, m_conv_ln_g, m_conv_ln_b, m_w_a_out, m_sgu_ln_g, m_sgu_ln_b, m_w_spatial, m_b_spatial, m_w_b_out, m_w_o, m_w_gate_up, m_w_down, v_norm_mix_pre, v_norm_mix_post, v_norm_ffn_pre, v_norm_ffn_post, v_w_in, v_b_gate, v_conv_w, v_conv_b, v_conv_ln_g, v_conv_ln_b, v_w_a_out, v_sgu_ln_g, v_sgu_ln_b, v_w_spatial, v_b_spatial, v_w_b_out, v_w_o, v_w_gate_up, v_w_down):
    names = ("norm_mix_pre", "norm_mix_post", "norm_ffn_pre", "norm_ffn_post", "w_in", "b_gate", "conv_w", "conv_b",
             "conv_ln_g", "conv_ln_b", "w_a_out", "sgu_ln_g", "sgu_ln_b", "w_spatial", "b_spatial", "w_b_out", "w_o",
             "w_gate_up", "w_down")
    w = dict(zip(names, (norm_mix_pre, norm_mix_post, norm_ffn_pre, norm_ffn_post, w_in, b_gate, conv_w, conv_b,
                         conv_ln_g, conv_ln_b, w_a_out, sgu_ln_g, sgu_ln_b, w_spatial, b_spatial, w_b_out, w_o,
                         w_gate_up, w_down)))
    m = dict(zip(names, (m_norm_mix_pre, m_norm_mix_post, m_norm_ffn_pre, m_norm_ffn_post, m_w_in, m_b_gate, m_conv_w,
                         m_conv_b, m_conv_ln_g, m_conv_ln_b, m_w_a_out, m_sgu_ln_g, m_sgu_ln_b, m_w_spatial,
                         m_b_spatial, m_w_b_out, m_w_o, m_w_gate_up, m_w_down)))
    v = dict(zip(names, (v_norm_mix_pre, v_norm_mix_post, v_norm_ffn_pre, v_norm_ffn_post, v_w_in, v_b_gate, v_conv_w,
                         v_conv_b, v_conv_ln_g, v_conv_ln_b, v_w_a_out, v_sgu_ln_g, v_sgu_ln_b, v_w_spatial,
                         v_b_spatial, v_w_b_out, v_w_o, v_w_gate_up, v_w_down)))
    d = x.shape[-1]
    shard_cols = d // N_DEV

    def small_pack(bg, cw):
        rows = jnp.concatenate([bg, cw], axis=1).reshape(DEPTH * (2 + CONV_WIDTH), shard_cols)
        return jnp.pad(rows, ((0, (-rows.shape[0]) % 8), (0, 0)))

    small_w, small_m, small_v = (small_pack(t["b_gate"], t["conv_w"]) for t in (w, m, v))

    small_full, = _exchange("gather_small", [small_w], scatter=False)
    small_full = small_full[:, :DEPTH * (2 + CONV_WIDTH)].reshape(N_DEV, DEPTH, 2 + CONV_WIDTH, shard_cols)
    small_full = jnp.transpose(small_full, (1, 2, 0, 3)).reshape(DEPTH, 2 + CONV_WIDTH, d)
    b_gate_full = small_full[:, :2]
    conv_w_full = jnp.pad(small_full[:, 2:], ((0, 0), (0, CONV_PAD - CONV_WIDTH), (0, 0)))

    me = (4 * lax.axis_index("x") + 2 * lax.axis_index("y") + lax.axis_index("c")).astype(jnp.int32).reshape(1)
    first_level = (SIBLING,) + SAME_CORE_PEERS
    gathers, token = {}, small_full
    for l in range(DEPTH):
        lands = [fill_own_slot(f"cast_{name}", me, w[name], l, BF16) for name in MATRICES]
        handles, token = exchange_start(f"gather_start_{l}", lands, first_level, after=token)
        for name, handle in zip(MATRICES, handles):
            gathers[l, name] = handle
    use_order = [(l, name) for l in range(DEPTH) for name in MATRICES]
    forwards, gathered = {}, {}

    def start_forward(i, after):
        if i >= len(use_order) or use_order[i] in forwards:
            return None
        l, name = use_order[i]
        land = exchange_wait(f"gather_wait_{name}_{l}", gathers[l, name], first_level, after)
        (forwards[l, name],), tok = exchange_start(f"forward_start_{name}_{l}", [land], SAME_CORE_PEERS, forward=True)
        return tok

    def weight(l, name, after):
        if (l, name) not in gathered:
            i = use_order.index((l, name))
            start_forward(i, after)
            tok = start_forward(i + 1, after)
            full = exchange_wait(f"forward_wait_{name}_{l}", forwards[l, name], SAME_CORE_PEERS, after, forward=True)
            gathered[l, name] = full if name in ("w_in", "w_gate_up") else full.reshape(-1, d)
            return gathered[l, name], tok
        return gathered[l, name], None

    scatters = {}

    def emit(l, name, g):
        chunks = g if g.ndim == 3 else g.reshape(N_DEV, -1, d)
        land = lax.empty(chunks.shape, chunks.dtype)
        (scatters[l, name],), tok = exchange_start(f"scatter_start_{name}_{l}", [land], ALL_PEERS, srcs=[chunks])
        return tok

    rep = {name: w[name] for name in REPLICATED}
    loss, grad_x, grads = local_step(x[0], loss_target[0], rep, weight, emit, b_gate_full, conv_w_full, token)
    loss = lax.psum(loss, MESH_AXES)

    small_g = jnp.stack([jnp.concatenate([grads[l]["b_gate"], grads[l]["conv_w"][:CONV_WIDTH]], axis=0) for l in range(DEPTH)])
    small_g = jnp.transpose(small_g.reshape(DEPTH * (2 + CONV_WIDTH), N_DEV, shard_cols), (1, 0, 2))
    small_g = jnp.pad(small_g, ((0, 0), (0, small_w.shape[0] - small_g.shape[1]), (0, 0)))
    rep_shapes = [w[name].shape for name in REPLICATED]
    rep_g = _pack_rows([jnp.stack([grads[l][name].reshape(w[name].shape[1:]) for l in range(DEPTH)]) for name in REPLICATED])
    small_land = fill_own_slot("own_small", me, small_g, None, F32)
    (small_handle,), tok_small = exchange_start("scatter_start_small", [small_land], ALL_PEERS, srcs=[small_g])
    rep_land = fill_own_slot("own_replicated", me, rep_g[None], 0, F32)
    (rep_handle,), tok_rep = exchange_start("gather_start_replicated", [rep_land], ALL_PEERS, after=tok_small)

    out = {}
    after = tok_rep
    for name in ("w_down", "w_gate_up", "w_o", "w_a_out", "w_b_out", "w_in"):
        done = [exchange_wait(f"scatter_wait_{name}_{l}", scatters[l, name], ALL_PEERS, after, with_source=True)
                for l in range(DEPTH)]
        out[name] = sum_adamw("adamw_" + name, [land for _, land in done], w[name], m[name], v[name], 128,
                              me=me, own=[chunks for chunks, _ in done])
        after = out[name][0]
    small_parts = exchange_wait("scatter_wait_small", small_handle, ALL_PEERS, after)
    rep_parts = exchange_wait("gather_wait_replicated", rep_handle, ALL_PEERS, after)
    small_res = sum_adamw("adamw_small", [small_parts], small_w[None], small_m[None], small_v[None], small_w.shape[0])
    n_small = DEPTH * (2 + CONV_WIDTH)
    small_res = [r[0, :n_small].reshape(DEPTH, 2 + CONV_WIDTH, shard_cols) for r in small_res]
    out["b_gate"] = [r[:, :2] for r in small_res]
    out["conv_w"] = [r[:, 2:] for r in small_res]
    rep_res = sum_adamw("adamw_replicated", [rep_parts], *(_pack_rows([t[name] for name in REPLICATED])[None] for t in (w, m, v)), 672)
    rep_res = [_unpack_rows(r[0], rep_shapes) for r in rep_res]
    for i, name in enumerate(REPLICATED):
        out[name] = [r[i] for r in rep_res]

    return (loss, grad_x[None], *[out[name][0] for name in names], *[out[name][1] for name in names],
            *[out[name][2] for name in names], *[out[name][3] for name in names])
```

```python
import functools
import math

import jax
import jax.numpy as jnp
from jax import lax
from jax.experimental import pallas as pl
from jax.experimental.pallas import tpu as pltpu

F32 = jnp.float32
BF16 = jnp.bfloat16

DEPTH = 4
N_DEV = 8
EPS = 1e-6
CONV_WIDTH = 31
CONV_PAD = 32
CHUNK = 128
SGU_GROUPS = 8

ADAM_LR = 0.001
ADAM_B1 = 0.9
ADAM_B2 = 0.999
ADAM_EPS = 1e-08
ADAM_WD = 0.01
ADAM_STEP = 10

VMEM_BYTES_V7X = 64 * 1024 * 1024
VMEM_COMPILER_SLACK = 12 * 1024 * 1024
MESH_AXES = ("x", "y", "c")
ANY = pl.BlockSpec(memory_space=pl.ANY)


def _nbytes(shape, dtype):
    return math.prod(shape) * jnp.dtype(dtype).itemsize


def _params(block_bytes, ngrid, single_bytes=0):
    limit = min(2 * block_bytes + single_bytes + VMEM_COMPILER_SLACK, VMEM_BYTES_V7X - 4 * 1024 * 1024)
    return pltpu.CompilerParams(dimension_semantics=("arbitrary",) * ngrid, vmem_limit_bytes=int(limit))


def _in_hbm(a):
    return pltpu.with_memory_space_constraint(a, pltpu.HBM)


def _out_hbm(shape, dtype):
    return pltpu.HBM(tuple(shape), dtype)


def _deps(deps):
    return [t for t in deps if t is not None]


def _mm_body(dims, nk, kaxis, ndeps):
    def body(a_ref, b_ref, *rest):
        o_ref, *acc = rest[ndeps:]

        def prod():
            return lax.dot_general(a_ref[...], b_ref[...], (dims, ((), ())), preferred_element_type=F32)

        if nk == 1:
            o_ref[...] = prod().astype(o_ref.dtype)
            return
        acc_ref, = acc
        k = pl.program_id(kaxis)

        @pl.when(k == 0)
        def _():
            acc_ref[...] = prod()

        @pl.when(k > 0)
        def _():
            acc_ref[...] += prod()

        @pl.when(k == nk - 1)
        def _():
            o_ref[...] = acc_ref[...].astype(o_ref.dtype)

    return body


def _mm_call(name, a, b, dims, grid, a_spec, b_spec, o_spec, out_shape, out_dtype, nk, kaxis, acc_shape, deps=()):
    deps = _deps(deps)
    blocks = (_nbytes([d for d in a_spec.block_shape if d], a.dtype) + _nbytes([d for d in b_spec.block_shape if d], b.dtype)
              + _nbytes([d for d in o_spec.block_shape if d], out_dtype))
    scratch = [pltpu.VMEM(acc_shape, F32)] if nk > 1 else []
    acc_bytes = _nbytes(acc_shape, F32) * (2 if nk > 1 else 1)
    return pl.pallas_call(
        _mm_body(dims, nk, kaxis, len(deps)), name=name, grid=grid, in_specs=[a_spec, b_spec] + [ANY] * len(deps),
        out_specs=o_spec, out_shape=_out_hbm(out_shape, out_dtype), scratch_shapes=scratch,
        compiler_params=_params(blocks, len(grid), acc_bytes),
    )(_in_hbm(a), _in_hbm(b), *deps)


def mm_nn(name, a, b_and_token, out_dtype, tm, tn=None, tk=None):
    b, token = b_and_token
    m, k = a.shape
    tm = min(tm, m)
    if b.ndim == 3:
        nblk, _, nb = b.shape
        return _mm_call(name, a, b, ((1,), (0,)), (nblk, m // tm),
                        pl.BlockSpec((tm, k), lambda j, i: (i, 0)), pl.BlockSpec((None, k, nb), lambda j, i: (j, 0, 0)),
                        pl.BlockSpec((tm, nb), lambda j, i: (i, j)), (m, nblk * nb), out_dtype, 1, 0, (tm, nb), [token])
    n = b.shape[1]
    tn = tn or n
    tk = tk or k
    nk = k // tk
    return _mm_call(name, a, b, ((1,), (0,)), (n // tn, m // tm, nk),
                    pl.BlockSpec((tm, tk), lambda j, i, kk: (i, kk)), pl.BlockSpec((tk, tn), lambda j, i, kk: (kk, j)),
                    pl.BlockSpec((tm, tn), lambda j, i, kk: (i, j)), (m, n), out_dtype, nk, 2, (tm, tn), [token])


def mm_nt(name, a, b, out_dtype, tm, tn=None, deps=()):
    m = a.shape[0]
    tm = min(tm, m)
    if b.ndim == 3:
        kblk, n, kb = b.shape
        return _mm_call(name, a, b, ((1,), (1,)), (m // tm, kblk),
                        pl.BlockSpec((tm, kb), lambda i, kk: (i, kk)), pl.BlockSpec((None, n, kb), lambda i, kk: (kk, 0, 0)),
                        pl.BlockSpec((tm, n), lambda i, kk: (i, 0)), (m, n), out_dtype, kblk, 1, (tm, n), deps)
    n, kc = b.shape
    tn = tn or n
    return _mm_call(name, a, b, ((1,), (1,)), (n // tn, m // tm),
                    pl.BlockSpec((tm, kc), lambda j, i: (i, 0)), pl.BlockSpec((tn, kc), lambda j, i: (j, 0)),
                    pl.BlockSpec((tm, tn), lambda j, i: (i, j)), (m, n), out_dtype, 1, 0, (tm, tn), deps)


def mm_tn(name, a, b, out_dtype, tm, tr, nb=None):
    m, k = a.shape
    n = b.shape[1]
    tm = min(tm, m)
    nm = m // tm
    if nb is not None:
        return _mm_call(name, a, b, ((0,), (0,)), (n // nb, k // tr, nm),
                        pl.BlockSpec((tm, tr), lambda j, r, mm: (mm, r)), pl.BlockSpec((tm, nb), lambda j, r, mm: (mm, j)),
                        pl.BlockSpec((None, tr, nb), lambda j, r, mm: (j, r, 0)), (n // nb, k, nb), out_dtype, nm, 2, (tr, nb))
    return _mm_call(name, a, b, ((0,), (0,)), (k // tr, nm),
                    pl.BlockSpec((tm, tr), lambda r, mm: (mm, r)), pl.BlockSpec((tm, n), lambda r, mm: (mm, 0)),
                    pl.BlockSpec((tr, n), lambda r, mm: (r, 0)), (k, n), out_dtype, nm, 1, (tr, n))


def _row_call(name, body, grid, in_specs, out_specs, out_shape, arrays, scratch=(), aliases=None, vmem_blocks=0, deps=()):
    deps = _deps(deps)
    nin = len(arrays)

    def with_deps(*refs):
        body(*refs[:nin], *refs[nin + len(deps):])

    single = not isinstance(out_shape, (list, tuple))
    outs = [_out_hbm(o.shape, o.dtype) for o in ([out_shape] if single else out_shape)]
    return pl.pallas_call(
        with_deps, name=name, grid=grid, in_specs=list(in_specs) + [ANY] * len(deps), out_specs=out_specs,
        out_shape=outs[0] if single else outs, scratch_shapes=list(scratch), input_output_aliases=aliases or {},
        compiler_params=_params(vmem_blocks, len(grid)),
    )(*[_in_hbm(a) for a in arrays], *deps)


def _rows(tm, d, col=0):
    return pl.BlockSpec((tm, d), lambda i, *_: (i, col))


def _vec(d):
    return pl.BlockSpec((1, d), lambda *_: (0, 0))


def _rstd(x):
    return lax.rsqrt(jnp.mean(x * x, axis=-1, keepdims=True) + EPS)


def _rms_bwd(dy, x, g):
    r = _rstd(x)
    n = x * r
    w = dy * g
    dx = r * (w - n * jnp.mean(w * n, axis=-1, keepdims=True))
    return dx, jnp.sum(dy * n, axis=0, keepdims=True)


def _accumulate(ref, value, first):
    @pl.when(first)
    def _():
        ref[...] = value

    @pl.when(jnp.logical_not(first))
    def _():
        ref[...] += value


def rms_fwd(x, g, tm=256, deps=()):
    s, d = x.shape
    tm = min(tm, s)

    def body(x_ref, g_ref, h_ref):
        xv = x_ref[...]
        h_ref[...] = (xv * _rstd(xv) * g_ref[...]).astype(BF16)

    return _row_call("rms_fwd", body, (s // tm,), [_rows(tm, d), _vec(d)], _rows(tm, d),
                     jax.ShapeDtypeStruct((s, d), BF16), (x, g), vmem_blocks=tm * d * 6, deps=deps)


def norm_res(x_in, o, g_post, g_next, tm=256):
    s, d = x_in.shape
    tm = min(tm, s)

    def body(x_ref, o_ref, gp_ref, gn_ref, xo_ref, h_ref):
        ov = o_ref[...]
        xo = x_ref[...] + (ov * _rstd(ov) * gp_ref[...])
        xo_ref[...] = xo
        h_ref[...] = (xo * _rstd(xo) * gn_ref[...]).astype(BF16)

    return _row_call("norm_res", body, (s // tm,), [_rows(tm, d), _rows(tm, d), _vec(d), _vec(d)],
                     [_rows(tm, d), _rows(tm, d)],
                     [jax.ShapeDtypeStruct((s, d), F32), jax.ShapeDtypeStruct((s, d), BF16)],
                     (x_in, o, g_post, g_next), vmem_blocks=tm * d * 14)


def final_norm_loss(x_in, o, g_post, target, tm=256):
    s, d = x_in.shape
    tm = min(tm, s)

    def body(x_ref, o_ref, gp_ref, t_ref, loss_ref, dy_ref, do_ref, dg_ref):
        first = pl.program_id(0) == 0
        ov = o_ref[...]
        g = gp_ref[...]
        diff = x_ref[...] + (ov * _rstd(ov) * g) - t_ref[...]
        _accumulate(loss_ref, jnp.sum(diff * diff, axis=0, keepdims=True), first)
        dy = diff * (1.0 / d)
        dy_ref[...] = dy
        do, dg = _rms_bwd(dy, ov, g)
        do_ref[...] = do.astype(BF16)
        _accumulate(dg_ref, dg, first)

    return _row_call("final_norm_loss", body, (s // tm,), [_rows(tm, d), _rows(tm, d), _vec(d), _rows(tm, d)],
                     [_vec(d), _rows(tm, d), _rows(tm, d), _vec(d)],
                     [jax.ShapeDtypeStruct((1, d), F32), jax.ShapeDtypeStruct((s, d), F32),
                      jax.ShapeDtypeStruct((s, d), BF16), jax.ShapeDtypeStruct((1, d), F32)],
                     (x_in, o, g_post, target), vmem_blocks=tm * d * 18)


def norm_bwd_in_out(dx_out, dh, x_in, g_pre, o_below, g_post_below, tm=256, deps=()):
    s, d = x_in.shape
    tm = min(tm, s)

    def body(dxo_ref, dh_ref, x_ref, g_ref, o_ref, gb_ref, dxi_ref, do_ref, dg_ref, dgb_ref):
        first = pl.program_id(0) == 0
        dx, dg = _rms_bwd(dh_ref[...].astype(F32), x_ref[...], g_ref[...])
        dxi = dxo_ref[...] + dx
        dxi_ref[...] = dxi
        _accumulate(dg_ref, dg, first)
        do, dgb = _rms_bwd(dxi, o_ref[...], gb_ref[...])
        do_ref[...] = do.astype(BF16)
        _accumulate(dgb_ref, dgb, first)

    return _row_call("norm_bwd_in_out", body, (s // tm,),
                     [_rows(tm, d), _rows(tm, d), _rows(tm, d), _vec(d), _rows(tm, d), _vec(d)],
                     [_rows(tm, d), _rows(tm, d), _vec(d), _vec(d)],
                     [jax.ShapeDtypeStruct((s, d), F32), jax.ShapeDtypeStruct((s, d), BF16),
                      jax.ShapeDtypeStruct((1, d), F32), jax.ShapeDtypeStruct((1, d), F32)],
                     (dx_out, dh, x_in, g_pre, o_below, g_post_below), vmem_blocks=tm * d * 22, deps=deps)


def norm_bwd_in(dx_out, dh, x_in, g_pre, tm=256):
    s, d = x_in.shape
    tm = min(tm, s)

    def body(dxo_ref, dh_ref, x_ref, g_ref, dxi_ref, dg_ref):
        dx, dg = _rms_bwd(dh_ref[...].astype(F32), x_ref[...], g_ref[...])
        dxi_ref[...] = dxo_ref[...] + dx
        _accumulate(dg_ref, dg, pl.program_id(0) == 0)

    return _row_call("norm_bwd_in", body, (s // tm,), [_rows(tm, d), _rows(tm, d), _rows(tm, d), _vec(d)],
                     [_rows(tm, d), _vec(d)],
                     [jax.ShapeDtypeStruct((s, d), F32), jax.ShapeDtypeStruct((1, d), F32)],
                     (dx_out, dh, x_in, g_pre), vmem_blocks=tm * d * 16)


CONV_COLS = 256
CONV_ROWS = 32


SUBLANES = 8


def _shifted_copies(ext_ref, sh_ref):
    n = sh_ref.shape[1]
    for r in range(SUBLANES):
        sh_ref[r] = ext_ref[r:r + n, :]


def _window(sh_ref, off):
    return sh_ref[off % SUBLANES, off - off % SUBLANES:off - off % SUBLANES + CONV_ROWS, :]


def _depthwise(sh_ref, w, n_out, in_off, flip, emit):
    for r0 in range(0, n_out, CONV_ROWS):
        acc = None
        for k in range(CONV_WIDTH):
            term = w[k:k + 1, :] * _window(sh_ref, r0 + in_off + (CONV_WIDTH - 1 - k if flip else k))
            acc = term if acc is None else acc + term
        emit(r0, acc)


def _ext_scratch(tm):
    return [pltpu.VMEM((tm + CONV_PAD + SUBLANES, CONV_COLS), F32), pltpu.VMEM((SUBLANES, tm + CONV_PAD, CONV_COLS), F32)]


def glu_conv_fwd(proj, conv_w, conv_b, c_ch, tm=512):
    s = proj.shape[0]
    tm = min(tm, s)
    ncb = c_ch // CONV_COLS
    hb = tm // CONV_PAD

    def body(a_ref, g_ref, ah_ref, gh_ref, w_ref, b_ref, c_ref, ext_ref, sh_ref):
        i = pl.program_id(1)
        halo = ah_ref[...].astype(F32) * jax.nn.sigmoid(gh_ref[...].astype(F32))
        ext_ref[0:CONV_PAD, :] = jnp.where(i > 0, halo, 0.0)
        ext_ref[CONV_PAD:CONV_PAD + tm, :] = a_ref[...].astype(F32) * jax.nn.sigmoid(g_ref[...].astype(F32))
        ext_ref[CONV_PAD + tm:, :] = jnp.zeros((SUBLANES, CONV_COLS), F32)
        _shifted_copies(ext_ref, sh_ref)
        w = w_ref[...]
        bias = b_ref[...]

        def emit(r0, acc):
            c_ref[r0:r0 + CONV_ROWS, :] = acc + bias

        _depthwise(sh_ref, w, tm, CONV_PAD - (CONV_WIDTH - 1), False, emit)

    main = lambda col0: pl.BlockSpec((tm, CONV_COLS), lambda c, i: (i, col0 + c))
    halo = lambda col0: pl.BlockSpec((CONV_PAD, CONV_COLS), lambda c, i: (jnp.maximum(i * hb - 1, 0), col0 + c))
    return _row_call("glu_conv_fwd", body, (ncb, s // tm),
                     [main(0), main(ncb), halo(0), halo(ncb),
                      pl.BlockSpec((CONV_PAD, CONV_COLS), lambda c, i: (0, c)), pl.BlockSpec((1, CONV_COLS), lambda c, i: (0, c))],
                     pl.BlockSpec((tm, CONV_COLS), lambda c, i: (i, c)), jax.ShapeDtypeStruct((s, c_ch), F32),
                     (proj, proj, proj, proj, conv_w, conv_b),
                     scratch=_ext_scratch(tm), vmem_blocks=tm * CONV_COLS * 32)


def _layer_norm_stats(x):
    mu = jnp.mean(x, axis=-1, keepdims=True)
    xc = x - mu
    rstd = lax.rsqrt(jnp.mean(xc * xc, axis=-1, keepdims=True) + EPS)
    return xc * rstd, rstd


def _layer_norm_bwd(dy, xhat, rstd, g):
    dxh = dy * g
    return rstd * (dxh - jnp.mean(dxh, axis=-1, keepdims=True) - xhat * jnp.mean(dxh * xhat, axis=-1, keepdims=True))


def ln_silu_fwd(c, ln_g, ln_b, tm=256):
    s, d = c.shape
    tm = min(tm, s)

    def body(c_ref, g_ref, b_ref, s_ref):
        xhat, _ = _layer_norm_stats(c_ref[...])
        s_ref[...] = jax.nn.silu(xhat * g_ref[...] + b_ref[...]).astype(BF16)

    return _row_call("ln_silu_fwd", body, (s // tm,), [_rows(tm, d), _vec(d), _vec(d)], _rows(tm, d),
                     jax.ShapeDtypeStruct((s, d), BF16), (c, ln_g, ln_b), vmem_blocks=tm * d * 10)


def ln_silu_bwd(c, ds, ln_g, ln_b, tm=256, deps=()):
    s, d = c.shape
    tm = min(tm, s)

    def body(c_ref, ds_ref, g_ref, b_ref, dc_ref, dg_ref, db_ref, dcb_ref):
        first = pl.program_id(0) == 0
        g = g_ref[...]
        xhat, rstd = _layer_norm_stats(c_ref[...])
        y = xhat * g + b_ref[...]
        sg = jax.nn.sigmoid(y)
        dln = ds_ref[...].astype(F32) * (sg * (1.0 + y * (1.0 - sg)))
        _accumulate(dg_ref, jnp.sum(dln * xhat, axis=0, keepdims=True), first)
        _accumulate(db_ref, jnp.sum(dln, axis=0, keepdims=True), first)
        dc = _layer_norm_bwd(dln, xhat, rstd, g)
        dc_ref[...] = dc.astype(BF16)
        _accumulate(dcb_ref, jnp.sum(dc, axis=0, keepdims=True), first)

    vec = jax.ShapeDtypeStruct((1, d), F32)
    return _row_call("ln_silu_bwd", body, (s // tm,), [_rows(tm, d), _rows(tm, d), _vec(d), _vec(d)],
                     [_rows(tm, d), _vec(d), _vec(d), _vec(d)], [jax.ShapeDtypeStruct((s, d), BF16), vec, vec, vec],
                     (c, ds, ln_g, ln_b), vmem_blocks=tm * d * 20, deps=deps)


def conv_bwd(proj, dc, conv_w, c_ch, tm=512):
    s = proj.shape[0]
    tm = min(tm, s)
    ncb = c_ch // CONV_COLS
    hb = tm // CONV_PAD
    last_halo = s // CONV_PAD - 1
    n_i = s // tm

    def body(a_ref, g_ref, ah_ref, gh_ref, dc_ref, dcn_ref, w_ref, dglu_ref, dw_ref,
             ext_ref, sh_ref, dce_ref, dsh_ref, dwacc_ref):
        i = pl.program_id(1)
        zeros = jnp.zeros((SUBLANES, CONV_COLS), F32)
        halo = ah_ref[...].astype(F32) * jax.nn.sigmoid(gh_ref[...].astype(F32))
        ext_ref[0:CONV_PAD, :] = jnp.where(i > 0, halo, 0.0)
        ext_ref[CONV_PAD:CONV_PAD + tm, :] = a_ref[...].astype(F32) * jax.nn.sigmoid(g_ref[...].astype(F32))
        ext_ref[CONV_PAD + tm:, :] = zeros
        _shifted_copies(ext_ref, sh_ref)
        dce_ref[0:tm, :] = dc_ref[...].astype(F32)
        dce_ref[tm:tm + CONV_PAD, :] = jnp.where(i < n_i - 1, dcn_ref[...].astype(F32), 0.0)
        dce_ref[tm + CONV_PAD:, :] = zeros
        _shifted_copies(dce_ref, dsh_ref)
        w = w_ref[...]

        def emit(r0, acc):
            dglu_ref[r0:r0 + CONV_ROWS, :] = acc.astype(BF16)

        _depthwise(dsh_ref, w, tm, 0, True, emit)

        for k in range(CONV_WIDTH):
            acc = None
            for r0 in range(0, tm, CONV_ROWS):
                term = dce_ref[r0:r0 + CONV_ROWS, :] * _window(sh_ref, r0 + CONV_PAD - (CONV_WIDTH - 1) + k)
                acc = term if acc is None else acc + term
            dwacc_ref[k:k + 1, :] = jnp.sum(acc, axis=0, keepdims=True)
        dwacc_ref[CONV_WIDTH:, :] = jnp.zeros((CONV_PAD - CONV_WIDTH, CONV_COLS), F32)
        _accumulate(dw_ref, dwacc_ref[...], i == 0)

    main = lambda col0: pl.BlockSpec((tm, CONV_COLS), lambda c, i: (i, col0 + c))
    halo = lambda col0: pl.BlockSpec((CONV_PAD, CONV_COLS), lambda c, i: (jnp.maximum(i * hb - 1, 0), col0 + c))
    nxt = pl.BlockSpec((CONV_PAD, CONV_COLS), lambda c, i: (jnp.minimum((i + 1) * hb, last_halo), c))
    wspec = pl.BlockSpec((CONV_PAD, CONV_COLS), lambda c, i: (0, c))
    return _row_call("conv_bwd", body, (ncb, n_i),
                     [main(0), main(ncb), halo(0), halo(ncb), main(0), nxt, wspec],
                     [main(0), wspec],
                     [jax.ShapeDtypeStruct((s, c_ch), BF16), jax.ShapeDtypeStruct((CONV_PAD, c_ch), F32)],
                     (proj, proj, proj, proj, dc, dc, conv_w),
                     scratch=_ext_scratch(tm) + _ext_scratch(tm) + [pltpu.VMEM((CONV_PAD, CONV_COLS), F32)],
                     vmem_blocks=tm * CONV_COLS * 56)


ELEMENTWISE_COLS = 512


def _col_chunks(d):
    return [slice(c0, c0 + ELEMENTWISE_COLS) for c0 in range(0, d, ELEMENTWISE_COLS)]


def _pair_spec(tm, d, pair):
    return pl.BlockSpec((tm, 2 * d), lambda i: (i, pair))


def glu_bwd(dproj, dglu, proj, c_ch, tm=256):
    s = proj.shape[0]
    tm = min(tm, s)

    def body(_, dglu_ref, a_ref, g_ref, out_ref):
        for cols in _col_chunks(c_ch):
            dg = dglu_ref[:, cols].astype(F32)
            sg = jax.nn.sigmoid(g_ref[:, cols].astype(F32))
            out_ref[:, cols] = (dg * sg).astype(BF16)
            out_ref[:, c_ch + cols.start:c_ch + cols.stop] = (dg * a_ref[:, cols].astype(F32) * (sg * (1.0 - sg))).astype(BF16)

    return _row_call("glu_bwd", body, (s // tm,), [ANY, _rows(tm, c_ch), _rows(tm, c_ch), _rows(tm, c_ch, 1)],
                     _pair_spec(tm, c_ch, 0), jax.ShapeDtypeStruct(dproj.shape, BF16), (dproj, dglu, proj, proj),
                     aliases={0: 0}, vmem_blocks=tm * c_ch * 12)


_SQRT_HALF = 0.7071067811865476
_INV_SQRT_2PI = 0.3989422804014327


def _gelu_parts(x):
    cdf = 0.5 * (1.0 + lax.erf(x * _SQRT_HALF))
    return cdf, x * cdf


def _gelu_grad(x, cdf):
    return cdf + x * (_INV_SQRT_2PI * jnp.exp(-0.5 * x * x))


def _sgu_specs(tm, ch):
    grp = ch // SGU_GROUPS
    full3 = lambda shape: pl.BlockSpec(shape, lambda *_: (0, 0, 0))
    return grp, full3((SGU_GROUPS, CHUNK, CHUNK)), full3((SGU_GROUPS, CHUNK, grp))


def sgu_fwd(proj, ln_g, ln_b, w_mix, b_mix, ch, col0, tm=CHUNK):
    s = proj.shape[0]
    grp, wspec, bspec = _sgu_specs(tm, ch)

    def body(u_ref, v_ref, g_ref, b_ref, w_ref, bm_ref, p_ref, mix_ref):
        _, u = _gelu_parts(u_ref[...].astype(F32))
        _, v0 = _gelu_parts(v_ref[...].astype(F32))
        xhat, _ = _layer_norm_stats(v0)
        vn = (xhat * g_ref[...] + b_ref[...]).astype(BF16)
        for n in range(tm // CHUNK):
            for g in range(SGU_GROUPS):
                blk = vn[n * CHUNK:(n + 1) * CHUNK, g * grp:(g + 1) * grp]
                mix_ref[n * CHUNK:(n + 1) * CHUNK, g * grp:(g + 1) * grp] = (
                    jnp.dot(w_ref[g], blk, preferred_element_type=F32) + bm_ref[g])
        p_ref[...] = (u * mix_ref[...]).astype(BF16)

    return _row_call("sgu_fwd", body, (s // tm,),
                     [_rows(tm, ch, col0), _rows(tm, ch, col0 + 1), _vec(ch), _vec(ch), wspec, bspec], _rows(tm, ch),
                     jax.ShapeDtypeStruct((s, ch), BF16), (proj, proj, ln_g, ln_b, w_mix, b_mix),
                     scratch=[pltpu.VMEM((tm, ch), F32)], vmem_blocks=tm * ch * 30)


def sgu_bwd(dproj, proj, dp, ln_g, ln_b, w_mix, w_mix_t, b_mix, ch, col0, tm=CHUNK):
    s = proj.shape[0]
    assert tm == CHUNK and col0 % 2 == 0
    grp, wspec, bspec = _sgu_specs(tm, ch)
    groups = [slice(k * grp, (k + 1) * grp) for k in range(SGU_GROUPS)]

    def body(_, u_ref, v_ref, dp_ref, g_ref, b_ref, w_ref, wt_ref, bm_ref,
             out_ref, dw_ref, dbm_ref, dg_ref, db_ref, u_s, gu_s, gv_s, xh_s, dvn_s):
        first = pl.program_id(0) == 0
        row_sum = lambda x: jnp.sum(x, axis=1, keepdims=True)
        total = None
        for cols in groups:
            ub = u_ref[:, cols].astype(F32)
            vb = v_ref[:, cols].astype(F32)
            cdf_u, u = _gelu_parts(ub)
            cdf_v, v0 = _gelu_parts(vb)
            u_s[:, cols] = u
            gu_s[:, cols] = _gelu_grad(ub, cdf_u)
            gv_s[:, cols] = _gelu_grad(vb, cdf_v)
            xh_s[:, cols] = v0
            total = row_sum(v0) if total is None else total + row_sum(v0)
        mu = total * (1.0 / ch)
        total = None
        for cols in groups:
            xc = xh_s[:, cols] - mu
            xh_s[:, cols] = xc
            total = row_sum(xc * xc) if total is None else total + row_sum(xc * xc)
        rstd = lax.rsqrt(total * (1.0 / ch) + EPS)

        t1 = t2 = None
        for k, cols in enumerate(groups):
            g = g_ref[:, cols]
            xhat = xh_s[:, cols] * rstd
            xh_s[:, cols] = xhat
            vn = (xhat * g + b_ref[:, cols]).astype(BF16)
            dpk = dp_ref[:, cols].astype(F32)
            dmix = dpk * u_s[:, cols]
            dmix_bf = dmix.astype(BF16)
            mixed = jnp.dot(w_ref[k], vn, preferred_element_type=F32) + bm_ref[k]
            out_ref[:, cols] = (dpk * mixed * gu_s[:, cols]).astype(BF16)
            dvn = jnp.dot(wt_ref[k], dmix_bf, preferred_element_type=F32)
            dvn_s[:, cols] = dvn
            _accumulate(dw_ref.at[k], lax.dot_general(dmix_bf, vn, (((1,), (1,)), ((), ())), preferred_element_type=F32), first)
            _accumulate(dbm_ref.at[k], jnp.broadcast_to(row_sum(dmix), (CHUNK, CHUNK)), first)
            _accumulate(dg_ref.at[:, cols], jnp.sum(dvn * xhat, axis=0, keepdims=True), first)
            _accumulate(db_ref.at[:, cols], jnp.sum(dvn, axis=0, keepdims=True), first)
            dxh = dvn * g
            t1 = row_sum(dxh) if t1 is None else t1 + row_sum(dxh)
            t2 = row_sum(dxh * xhat) if t2 is None else t2 + row_sum(dxh * xhat)
        m1 = t1 * (1.0 / ch)
        m2 = t2 * (1.0 / ch)
        for cols in groups:
            dv0 = rstd * (dvn_s[:, cols] * g_ref[:, cols] - m1 - xh_s[:, cols] * m2)
            out_ref[:, ch + cols.start:ch + cols.stop] = (dv0 * gv_s[:, cols]).astype(BF16)

    vec = _vec(ch)
    acc3 = lambda: pl.BlockSpec((SGU_GROUPS, CHUNK, CHUNK), lambda i: (0, 0, 0))
    vshape = jax.ShapeDtypeStruct((1, ch), F32)
    mshape = jax.ShapeDtypeStruct((SGU_GROUPS, CHUNK, CHUNK), F32)
    return _row_call("sgu_bwd", body, (s // tm,),
                     [ANY, _rows(tm, ch, col0), _rows(tm, ch, col0 + 1), _rows(tm, ch), vec, vec, wspec, wspec, bspec],
                     [_pair_spec(tm, ch, col0 // 2), acc3(), acc3(), vec, vec],
                     [jax.ShapeDtypeStruct(dproj.shape, BF16), mshape, mshape, vshape, vshape],
                     (dproj, proj, proj, dp, ln_g, ln_b, w_mix, w_mix_t, b_mix),
                     scratch=[pltpu.VMEM((tm, ch), F32)] * 5, aliases={0: 0}, vmem_blocks=tm * ch * 40)


def branches_merge(s_act, p_act, wa_and_token, wb_and_token, proj, b_gate, col0, tm=256, tn=1024):
    (wa, tok_a), (wb, tok_b) = wa_and_token, wb_and_token
    deps = _deps([tok_a, tok_b])
    s, d = s_act.shape
    tm = min(tm, s)
    per = d // tn

    def body(s_ref, p_ref, wa_ref, wb_ref, l0_ref, l1_ref, bg_ref, *rest):
        ya_ref, yb_ref, m_ref = rest[len(deps):]
        ya = jnp.dot(s_ref[...], wa_ref[...], preferred_element_type=F32)
        yb = jnp.dot(p_ref[...], wb_ref[...], preferred_element_type=F32)
        ya_ref[...] = ya.astype(BF16)
        yb_ref[...] = yb.astype(BF16)
        g0 = jax.nn.sigmoid(l0_ref[...].astype(F32) + bg_ref[0:1, :])
        g1 = jax.nn.sigmoid(l1_ref[...].astype(F32) + bg_ref[1:2, :])
        m_ref[...] = (g0 * ya + g1 * yb).astype(BF16)

    act = pl.BlockSpec((tm, d), lambda j, i: (i, 0))
    wgt = pl.BlockSpec((d, tn), lambda j, i: (0, j))
    logits = lambda col: pl.BlockSpec((tm, tn), lambda j, i: (i, col * per + j))
    oblk = pl.BlockSpec((tm, tn), lambda j, i: (i, j))
    return pl.pallas_call(
        body, name="branches_merge", grid=(per, s // tm),
        in_specs=[act, act, wgt, wgt, logits(col0), logits(col0 + 1), pl.BlockSpec((2, tn), lambda j, i: (0, j))] + [ANY] * len(deps),
        out_specs=[oblk, oblk, oblk],
        out_shape=[_out_hbm((s, d), BF16), _out_hbm((s, d), BF16), _out_hbm((s, d), BF16)],
        compiler_params=_params(2 * tm * d * 2 + 2 * d * tn * 2 + 2 * tm * tn * 2 + tm * tn * 10, 2, 4 * tm * tn * 4),
    )(*[_in_hbm(a) for a in (s_act, p_act, wa, wb, proj, proj, b_gate)], *deps)


def merge_bwd(dm, y_a, y_b, proj, b_gate, col0, tm=256, deps=()):
    s, d = y_a.shape
    tm = min(tm, s)

    assert col0 % 2 == 0

    def body(dm_ref, ya_ref, yb_ref, l0_ref, l1_ref, bg_ref, dya_ref, dyb_ref, out_ref, dbg_ref):
        first = pl.program_id(0) == 0
        for cols in _col_chunks(d):
            dmv = dm_ref[:, cols].astype(F32)
            g0 = jax.nn.sigmoid(l0_ref[:, cols].astype(F32) + bg_ref[0:1, cols])
            g1 = jax.nn.sigmoid(l1_ref[:, cols].astype(F32) + bg_ref[1:2, cols])
            dya_ref[:, cols] = (dmv * g0).astype(BF16)
            dyb_ref[:, cols] = (dmv * g1).astype(BF16)
            dl0 = dmv * ya_ref[:, cols].astype(F32) * (g0 * (1.0 - g0))
            dl1 = dmv * yb_ref[:, cols].astype(F32) * (g1 * (1.0 - g1))
            _accumulate(dbg_ref.at[0:1, cols], jnp.sum(dl0, axis=0, keepdims=True), first)
            _accumulate(dbg_ref.at[1:2, cols], jnp.sum(dl1, axis=0, keepdims=True), first)
            out_ref[:, cols] = dl0.astype(BF16)
            out_ref[:, d + cols.start:d + cols.stop] = dl1.astype(BF16)

    bgspec = pl.BlockSpec((2, d), lambda i: (0, 0))
    return _row_call("merge_bwd", body, (s // tm,),
                     [_rows(tm, d), _rows(tm, d), _rows(tm, d), _rows(tm, d, col0), _rows(tm, d, col0 + 1), bgspec],
                     [_rows(tm, d), _rows(tm, d), _pair_spec(tm, d, col0 // 2), bgspec],
                     [jax.ShapeDtypeStruct((s, d), BF16), jax.ShapeDtypeStruct((s, d), BF16),
                      jax.ShapeDtypeStruct(proj.shape, BF16), jax.ShapeDtypeStruct((2, d), F32)],
                     (dm, y_a, y_b, proj, proj, b_gate), vmem_blocks=tm * d * 24, deps=deps)


def gate_up_swiglu(h, w_and_token, tm=256):
    w, token = w_and_token
    deps = _deps([token])
    s, d = h.shape
    nblk, _, nb = w.shape
    half = nblk // 2
    tm = min(tm, s)

    def body(h_ref, wg_ref, wu_ref, *rest):
        g_ref, u_ref, f_ref = rest[len(deps):]
        hv = h_ref[...]
        g = jnp.dot(hv, wg_ref[...], preferred_element_type=F32)
        u = jnp.dot(hv, wu_ref[...], preferred_element_type=F32)
        g_ref[...] = g.astype(BF16)
        u_ref[...] = u.astype(BF16)
        f_ref[...] = (jax.nn.silu(g) * u).astype(BF16)

    out = _out_hbm((s, half * nb), BF16)
    oblk = pl.BlockSpec((tm, nb), lambda j, i: (i, j))
    return pl.pallas_call(
        body, name="gate_up_swiglu", grid=(half, s // tm),
        in_specs=[pl.BlockSpec((tm, d), lambda j, i: (i, 0)), pl.BlockSpec((None, d, nb), lambda j, i: (j, 0, 0)),
                  pl.BlockSpec((None, d, nb), lambda j, i: (j + half, 0, 0))] + [ANY] * len(deps),
        out_specs=[oblk, oblk, oblk], out_shape=[out, out, out],
        compiler_params=_params(tm * d * 2 + 2 * d * nb * 2 + 3 * tm * nb * 2, 2, 4 * tm * nb * 4),
    )(_in_hbm(h), _in_hbm(w), _in_hbm(w), *deps)


def swiglu_bwd(g_act, u_act, df, tm=256, deps=()):
    s, half = g_act.shape
    w2 = 2 * half
    tm = min(tm, s)
    chunk = w2 // N_DEV

    def body(g_ref, u_ref, df_ref, out_ref):
        for c0 in range(0, half, chunk):
            cols = slice(c0, c0 + chunk)
            g = g_ref[:, cols].astype(F32)
            sg = jax.nn.sigmoid(g)
            dfv = df_ref[:, cols].astype(F32)
            out_ref[:, cols] = (dfv * u_ref[:, cols].astype(F32) * (sg * (1.0 + g * (1.0 - sg)))).astype(BF16)
            out_ref[:, half + c0:half + c0 + chunk] = (dfv * (g * sg)).astype(BF16)

    return _row_call("swiglu_bwd", body, (s // tm,), [_rows(tm, half), _rows(tm, half), _rows(tm, half)],
                     _rows(tm, w2), jax.ShapeDtypeStruct((s, w2), BF16), (g_act, u_act, df), vmem_blocks=tm * w2 * 5, deps=deps)


def _peers():
    x, y, c = lax.axis_index("x"), lax.axis_index("y"), lax.axis_index("c")
    me = 4 * x + 2 * y + c
    peers = []
    for k in range(1, N_DEV):
        px = 1 - x if k & 4 else x
        py = 1 - y if k & 2 else y
        pc = 1 - c if k & 1 else c
        peers.append(((px, py, pc), 4 * px + 2 * py + pc))
    return me, peers


def _exchange(name, arrays, scatter):
    n = len(arrays)

    def body(*refs):
        ins, outs = refs[:n], refs[n:2 * n]
        send_sems, recv_sems, local_sems = refs[2 * n:]
        me, peers = _peers()

        def remote(a, k):
            (pos, idx) = peers[k]
            src = ins[a].at[idx] if scatter else ins[a]
            return pltpu.make_async_remote_copy(src_ref=src, dst_ref=outs[a].at[me], send_sem=send_sems.at[a, k],
                                                recv_sem=recv_sems.at[a, k], device_id=pos, device_id_type=pl.DeviceIdType.MESH)

        def arrival(a, k):
            (pos, idx) = peers[k]
            src = ins[a].at[idx] if scatter else ins[a]
            return pltpu.make_async_remote_copy(src_ref=src, dst_ref=outs[a].at[idx], send_sem=send_sems.at[a, k],
                                                recv_sem=recv_sems.at[a, k], device_id=pos, device_id_type=pl.DeviceIdType.MESH)

        local = [pltpu.make_async_copy(ins[a].at[me] if scatter else ins[a], outs[a].at[me], local_sems.at[a]) for a in range(n)]
        sends = [remote(a, k) for k in range(N_DEV - 1) for a in range(n)]
        for cp in sends:
            cp.start()
        for cp in local:
            cp.start()
        for k in range(N_DEV - 1):
            for a in range(n):
                arrival(a, k).wait_recv()
        for cp in sends:
            cp.wait_send()
        for cp in local:
            cp.wait()

    out_shape = [jax.ShapeDtypeStruct(a.shape if scatter else (N_DEV,) + a.shape, a.dtype) for a in arrays]
    return pl.pallas_call(
        body, name=name, in_specs=[ANY] * n, out_specs=[ANY] * n, out_shape=out_shape,
        scratch_shapes=[pltpu.SemaphoreType.DMA((n, N_DEV - 1)), pltpu.SemaphoreType.DMA((n, N_DEV - 1)),
                        pltpu.SemaphoreType.DMA((n,))],
    )(*arrays)


HBM_SPEC = pl.BlockSpec(memory_space=pltpu.HBM)
SEM_SPEC = pl.BlockSpec(memory_space=pltpu.SEMAPHORE)
DATAFLOW_EFFECT = pltpu.SideEffectType.DATAFLOW_SIDE_EFFECTING
ALL_PEERS = (1, 2, 3, 4, 5, 6, 7)
SIBLING = 1
SAME_CORE_PEERS = (2, 4, 6)


def fill_own_slot(name, me, src, block, dtype):
    _, r, c = src.shape
    tr = _row_tile(r, 256)

    def body(me_ref, src_ref, out_ref):
        out_ref[...] = src_ref[...].astype(dtype)

    if block is None:
        src_index = lambda i, me_ref: (me_ref[0], i, 0)
    else:
        src_index = lambda i, me_ref: (block, i, 0)
    grid_spec = pltpu.PrefetchScalarGridSpec(
        num_scalar_prefetch=1, grid=(r // tr,), in_specs=[pl.BlockSpec((None, tr, c), src_index)],
        out_specs=pl.BlockSpec((None, tr, c), lambda i, me_ref: (me_ref[0], i, 0)))
    return pl.pallas_call(body, name=name, grid_spec=grid_spec, out_shape=_out_hbm((N_DEV, r, c), dtype),
                          compiler_params=_params(tr * c * (src.dtype.itemsize + jnp.dtype(dtype).itemsize), 1))(me, _in_hbm(src))


def _split_copy(src, land, send_sem, recv_sem, k, peers, me, arriving, forward):
    pos, idx = peers[k - 1]
    if forward:
        pos = peers[SIBLING - 1][0]
        slot = peers[(k | SIBLING) - 1][1] if arriving else idx
        src_ref, dst_ref = land.at[slot], land.at[slot]
    else:
        src_ref = land.at[me] if src is None else src.at[idx]
        dst_ref = land.at[idx if arriving else me]
    return pltpu.make_async_remote_copy(src_ref=src_ref, dst_ref=dst_ref, send_sem=send_sem, recv_sem=recv_sem,
                                        device_id=pos, device_id_type=pl.DeviceIdType.MESH)


def exchange_start(name, lands, peer_ks, srcs=None, after=None, forward=False):
    n = len(lands)
    ns = n if srcs is not None else 0
    extra = _deps([after])
    bufs = (list(srcs) if srcs is not None else []) + list(lands)

    def body(*refs):
        src, land = refs[:ns], refs[ns:ns + n]
        outs = refs[ns + n + len(extra):]
        send_sems, recv_sems, token = outs[:n], outs[n:2 * n], outs[2 * n + ns + n]
        me, peers = _peers()
        for a in range(n):
            for j, k in enumerate(peer_ks):
                _split_copy(src[a] if ns else None, land[a], send_sems[a].at[j], recv_sems[a].at[j], k, peers, me,
                            False, forward).start()
        token[...] = jnp.zeros_like(token)

    sems = [pltpu.SemaphoreType.DMA((len(peer_ks),))] * (2 * n)
    res = pl.pallas_call(
        body, name=name, in_specs=[HBM_SPEC] * len(bufs) + [ANY] * len(extra),
        out_specs=[SEM_SPEC] * (2 * n) + [HBM_SPEC] * len(bufs) + [pl.BlockSpec(memory_space=pltpu.VMEM)],
        out_shape=sems + [pltpu.HBM(a.shape, a.dtype) for a in bufs] + [jax.ShapeDtypeStruct((8, 128), F32)],
        input_output_aliases={i: 2 * n + i for i in range(len(bufs))},
        compiler_params=pltpu.CompilerParams(has_side_effects=DATAFLOW_EFFECT),
    )(*[_in_hbm(a) for a in bufs], *extra)
    handles = [(res[a], res[n + a], res[2 * n + a] if ns else None, res[2 * n + ns + a]) for a in range(n)]
    return handles, res[2 * n + ns + n]


def exchange_wait(name, handle, peer_ks, after, forward=False, with_source=False):
    send_sem, recv_sem, src, land = handle
    bufs = ([src] if src is not None else []) + [land]
    nb = len(bufs)

    def body(*refs):
        src_ref = refs[0] if nb == 2 else None
        land_ref, send_ref, recv_ref = refs[nb - 1], refs[nb], refs[nb + 1]
        me, peers = _peers()
        for j, k in enumerate(peer_ks):
            cp = _split_copy(src_ref, land_ref, send_ref.at[j], recv_ref.at[j], k, peers, me, True, forward)
            cp.wait_send()
            cp.wait_recv()

    outs = pl.pallas_call(
        body, name=name, in_specs=[HBM_SPEC] * nb + [SEM_SPEC, SEM_SPEC, ANY], out_specs=[HBM_SPEC] * nb,
        out_shape=[pltpu.HBM(a.shape, a.dtype) for a in bufs],
        input_output_aliases={i: i for i in range(nb)}, compiler_params=pltpu.CompilerParams(has_side_effects=DATAFLOW_EFFECT),
    )(*bufs, send_sem, recv_sem, after)
    return tuple(outs) if with_source else outs[nb - 1]


def _row_tile(r, cap):
    if r <= cap:
        return r
    return max(t for t in range(16, cap + 1, 16) if r % t == 0)


def sum_adamw(name, parts, w, m, v, tr, me=None, own=None):
    nl, r, c = w.shape
    tr = _row_tile(r, tr)
    c1 = 1.0 - ADAM_B1 ** ADAM_STEP
    c2 = 1.0 - ADAM_B2 ** ADAM_STEP
    nown = nl if own is not None else 0

    def body(*refs):
        me_ref = refs[0] if nown else None
        refs = refs[1:] if nown else refs
        part_refs, own_refs = refs[:nl], refs[nl:nl + nown]
        w_ref, m_ref, v_ref, g_out, d_out, m_out, v_out = refs[nl + nown:]
        layer = pl.program_id(0)
        for j in range(nl):
            @pl.when(layer == j)
            def _(j=j):
                g = None
                for p in range(N_DEV):
                    term = part_refs[j][p].astype(F32)
                    if nown:
                        term = jnp.where(me_ref[0] == p, own_refs[j][...].astype(F32), term)
                    g = term if g is None else g + term
                mn = ADAM_B1 * m_ref[...] + (1.0 - ADAM_B1) * g
                vn = ADAM_B2 * v_ref[...] + (1.0 - ADAM_B2) * (g * g)
                g_out[...] = g
                m_out[...] = mn
                v_out[...] = vn
                d_out[...] = -ADAM_LR * ((mn / c1) / (jnp.sqrt(vn / c2) + ADAM_EPS) + ADAM_WD * w_ref[...])

    def part_spec(j):
        return pl.BlockSpec((N_DEV, tr, c), lambda l, i, *_: (0, jnp.where(l == j, i, 0), 0))

    def own_spec(j):
        return pl.BlockSpec((None, tr, c), lambda l, i, me_ref: (me_ref[0], jnp.where(l == j, i, 0), 0))

    lspec = pl.BlockSpec((None, tr, c), lambda l, i, *_: (l, i, 0))
    out = jax.ShapeDtypeStruct((nl, r, c), F32)
    in_specs = [part_spec(j) for j in range(nl)] + [own_spec(j) for j in range(nown)] + [lspec] * 3
    vmem_blocks = (N_DEV + 1) * nl * tr * c * parts[0].dtype.itemsize + 7 * tr * c * 4
    if not nown:
        return _row_call(name, body, (nl, r // tr), in_specs, [lspec] * 4, [out] * 4, tuple(parts) + (w, m, v),
                         vmem_blocks=vmem_blocks)
    grid_spec = pltpu.PrefetchScalarGridSpec(num_scalar_prefetch=1, grid=(nl, r // tr), in_specs=in_specs, out_specs=[lspec] * 4)
    return pl.pallas_call(body, name=name, grid_spec=grid_spec, out_shape=[_out_hbm(out.shape, out.dtype)] * 4,
                          compiler_params=_params(vmem_blocks, 2))(me, *[_in_hbm(a) for a in tuple(parts) + tuple(own) + (w, m, v)])


REPLICATED = ("norm_mix_pre", "norm_mix_post", "norm_ffn_pre", "norm_ffn_post", "conv_b", "conv_ln_g", "conv_ln_b",
              "sgu_ln_g", "sgu_ln_b", "w_spatial", "b_spatial")
MATRICES = ("w_in", "w_a_out", "w_b_out", "w_o", "w_gate_up", "w_down")


def local_step(x, target, rep, weight, emit, b_gate, conv_w, start_token=None):
    s, d = x.shape
    causal = jnp.tril(jnp.ones((CHUNK, CHUNK), dtype=bool))
    row = lambda name, l: rep[name][l].reshape(1, -1)

    saved = []
    h = rms_fwd(x, row("norm_mix_pre", 0), deps=[start_token])
    for l in range(DEPTH):
        w_mix = jnp.where(causal[None], rep["w_spatial"][l], 0.0).astype(BF16)
        b_mix = jnp.broadcast_to(rep["b_spatial"][l][:, :, None], (SGU_GROUPS, CHUNK, d // SGU_GROUPS))
        proj = mm_nn("proj", h, weight(l, "w_in", h), BF16, 512)
        c = glu_conv_fwd(proj, conv_w[l], row("conv_b", l), d)
        s_act = ln_silu_fwd(c, row("conv_ln_g", l), row("conv_ln_b", l))
        p_act = sgu_fwd(proj, row("sgu_ln_g", l), row("sgu_ln_b", l), w_mix, b_mix, d, 2)
        y_a, y_b, merged = branches_merge(s_act, p_act, weight(l, "w_a_out", s_act), weight(l, "w_b_out", p_act),
                                          proj, b_gate[l], 4)
        o = mm_nn("branch_out", merged, weight(l, "w_o", merged), F32, 512)
        x_mid, h2 = norm_res(x, o, row("norm_mix_post", l), row("norm_ffn_pre", l))
        g_act, u_act, f = gate_up_swiglu(h2, weight(l, "w_gate_up", h2))
        o2 = mm_nn("down", f, weight(l, "w_down", f), F32, 512, tn=1024)
        saved.append(dict(x_in=x, h=h, proj=proj, c=c, s_act=s_act, p_act=p_act, y_a=y_a, y_b=y_b, merged=merged, o=o,
                          x_mid=x_mid, h2=h2, g_act=g_act, u_act=u_act, f=f, o2=o2, w_mix=w_mix, b_mix=b_mix))
        if l + 1 < DEPTH:
            x, h = norm_res(x_mid, o2, row("norm_ffn_post", l), row("norm_mix_pre", l + 1))

    top = saved[-1]
    loss_vec, dx, do2, dg_ffn_post = final_norm_loss(top["x_mid"], top["o2"], row("norm_ffn_post", DEPTH - 1), target)
    loss = (0.5 / d) * jnp.sum(loss_vec)

    grads = [None] * DEPTH
    for l in reversed(range(DEPTH)):
        sv = saved[l]
        wl = {name: weight(l, name, None)[0] for name in MATRICES}
        g = {"norm_ffn_post": dg_ffn_post}
        df = mm_nt("d_down_in", do2, wl["w_down"], BF16, 512, tn=wl["w_down"].shape[0] // 4)
        tok = emit(l, "w_down", mm_tn("d_down_w", sv["f"], do2, BF16, 512, sv["f"].shape[1] // 4))
        dgu = swiglu_bwd(sv["g_act"], sv["u_act"], df, deps=[tok])
        dh2 = mm_nt("d_gate_up_in", dgu, wl["w_gate_up"], BF16, 1024)
        tok = emit(l, "w_gate_up", mm_tn("d_gate_up_w", sv["h2"], dgu, BF16, 2048, d // 2, nb=wl["w_gate_up"].shape[2]))
        dx, do, g["norm_ffn_pre"], g["norm_mix_post"] = norm_bwd_in_out(
            dx, dh2, sv["x_mid"], row("norm_ffn_pre", l), sv["o"], row("norm_mix_post", l), deps=[tok])
        dm = mm_nt("d_square_in", do, wl["w_o"], BF16, 512)
        tok = emit(l, "w_o", mm_tn("d_square_w", sv["merged"], do, BF16, 512, d // 2))
        dy_a, dy_b, dproj, g["b_gate"] = merge_bwd(dm, sv["y_a"], sv["y_b"], sv["proj"], b_gate[l], 4, deps=[tok])
        ds = mm_nt("d_square_in", dy_a, wl["w_a_out"], BF16, 512)
        tok = emit(l, "w_a_out", mm_tn("d_square_w", sv["s_act"], dy_a, BF16, 512, d // 2))
        dp = mm_nt("d_square_in", dy_b, wl["w_b_out"], BF16, 512, deps=[tok])
        tok = emit(l, "w_b_out", mm_tn("d_square_w", sv["p_act"], dy_b, BF16, 512, d // 2))
        dc, g["conv_ln_g"], g["conv_ln_b"], g["conv_b"] = ln_silu_bwd(
            sv["c"], ds, row("conv_ln_g", l), row("conv_ln_b", l), deps=[tok])
        dglu, g["conv_w"] = conv_bwd(sv["proj"], dc, conv_w[l], d)
        dproj = glu_bwd(dproj, dglu, sv["proj"], d)
        dproj, dw_mix, db_mix, g["sgu_ln_g"], g["sgu_ln_b"] = sgu_bwd(
            dproj, sv["proj"], dp, row("sgu_ln_g", l), row("sgu_ln_b", l), sv["w_mix"],
            jnp.swapaxes(sv["w_mix"], 1, 2), sv["b_mix"], d, 2)
        g["w_spatial"] = jnp.where(causal[None], dw_mix, 0.0)
        g["b_spatial"] = db_mix[:, :, 0]
        tok = emit(l, "w_in", mm_tn("d_in_w", sv["h"], dproj, BF16, 2048, d // 2, nb=wl["w_in"].shape[2]))
        dh = mm_nt("d_in_in", dproj, wl["w_in"], BF16, 1024, deps=[tok])
        if l > 0:
            below = saved[l - 1]
            dx, do2, g["norm_mix_pre"], dg_ffn_post = norm_bwd_in_out(
                dx, dh, sv["x_in"], row("norm_mix_pre", l), below["o2"], row("norm_ffn_post", l - 1))
        else:
            dx, g["norm_mix_pre"] = norm_bwd_in(dx, dh, sv["x_in"], row("norm_mix_pre", l))
        grads[l] = g
    return loss, dx, grads


def _pack_rows(arrays):
    return jnp.concatenate([a.reshape(-1, 128) for a in arrays], axis=0)


def _unpack_rows(packed, shapes):
    out, r0 = [], 0
    for shp in shapes:
        nr = math.prod(shp) // 128
        out.append(packed[r0:r0 + nr].reshape(shp))
        r0 += nr
    return out


def kernel(x, norm_mix_pre, norm_mix_post, norm_ffn_pre, norm_ffn_post, w_in, b_gate, conv_w, conv_b, conv_ln_g, conv_ln_b, w_a_out, sgu_ln_g, sgu_ln_b, w_spatial, b_spatial, w_b_out, w_o, w_gate_up, w_down, loss_target, m_norm_mix_pre, m_norm_mix_post, m_norm_ffn_pre, m_norm_ffn_post, m_w_in, m_b_gate, m_conv_w, m_conv_b, m_conv_ln_g, m_conv_ln_b, m_w_a_out, m_sgu_ln_g, m_sgu_ln_b, m_w_spatial, m_b_spatial, m_w_b_out, m_w_o, m_w_gate_up, m_w_down, v_norm_mix_pre, v_norm_mix_post, v_norm_ffn_pre, v_norm_ffn_post, v_w_in, v_b_gate, v_conv_w, v_conv_b, v_conv_ln_g, v_conv_ln_b, v_w_a_out, v_sgu_ln_g, v_sgu_ln_b, v_w_spatial, v_b_spatial, v_w_b_out, v_w_o, v_w_gate_up, v_w_down):
    names = ("norm_mix_pre", "norm_mix_post", "norm_ffn_pre", "norm_ffn_post", "w_in", "b_gate", "conv_w", "conv_b",
             "conv_ln_g", "conv_ln_b", "w_a_out", "sgu_ln_g", "sgu_ln_b", "w_spatial", "b_spatial", "w_b_out", "w_o",
             "w_gate_up", "w_down")
    w = dict(zip(names, (norm_mix_pre, norm_mix_post, norm_ffn_pre, norm_ffn_post, w_in, b_gate, conv_w, conv_b,
                         conv_ln_g, conv_ln_b, w_a_out, sgu_ln_g, sgu_ln_b, w_spatial, b_spatial, w_b_out, w_o,
                         w_gate_up, w_down)))
    m = dict(zip(names, (m_norm_mix_pre, m_norm_mix_post, m_norm_ffn_pre, m_norm_ffn_post, m_w_in, m_b_gate, m_conv_w,
                         m_conv_b, m_conv_ln_g, m_conv_ln_b, m_w_a_out, m_sgu_ln_g, m_sgu_ln_b, m_w_spatial,
                         m_b_spatial, m_w_b_out, m_w_o, m_w_gate_up, m_w_down)))
    v = dict(zip(names, (v_norm_mix_pre, v_norm_mix_post, v_norm_ffn_pre, v_norm_ffn_post, v_w_in, v_b_gate, v_conv_w,
                         v_conv_b, v_conv_ln_g, v_conv_ln_b, v_w_a_out, v_sgu_ln_g, v_sgu_ln_b, v_w_spatial,
                         v_b_spatial, v_w_b_out, v_w_o, v_w_gate_up, v_w_down)))
    d = x.shape[-1]
    shard_cols = d // N_DEV

    def small_pack(bg, cw):
        rows = jnp.concatenate([bg, cw], axis=1).reshape(DEPTH * (2 + CONV_WIDTH), shard_cols)
        return jnp.pad(rows, ((0, (-rows.shape[0]) % 8), (0, 0)))

    small_w, small_m, small_v = (small_pack(t["b_gate"], t["conv_w"]) for t in (w, m, v))

    small_full, = _exchange("gather_small", [small_w], scatter=False)
    small_full = small_full[:, :DEPTH * (2 + CONV_WIDTH)].reshape(N_DEV, DEPTH, 2 + CONV_WIDTH, shard_cols)
    small_full = jnp.transpose(small_full, (1, 2, 0, 3)).reshape(DEPTH, 2 + CONV_WIDTH, d)
    b_gate_full = small_full[:, :2]
    conv_w_full = jnp.pad(small_full[:, 2:], ((0, 0), (0, CONV_PAD - CONV_WIDTH), (0, 0)))

    me = (4 * lax.axis_index("x") + 2 * lax.axis_index("y") + lax.axis_index("c")).astype(jnp.int32).reshape(1)
    first_level = (SIBLING,) + SAME_CORE_PEERS
    gathers, token = {}, small_full
    for l in range(DEPTH):
        lands = [fill_own_slot(f"cast_{name}", me, w[name], l, BF16) for name in MATRICES]
        handles, token = exchange_start(f"gather_start_{l}", lands, first_level, after=token)
        for name, handle in zip(MATRICES, handles):
            gathers[l, name] = handle
    use_order = [(l, name) for l in range(DEPTH) for name in MATRICES]
    forwards, gathered = {}, {}

    def start_forward(i, after):
        if i >= len(use_order) or use_order[i] in forwards:
            return None
        l, name = use_order[i]
        land = exchange_wait(f"gather_wait_{name}_{l}", gathers[l, name], first_level, after)
        (forwards[l, name],), tok = exchange_start(f"forward_start_{name}_{l}", [land], SAME_CORE_PEERS, forward=True)
        return tok

    def weight(l, name, after):
        if (l, name) not in gathered:
            i = use_order.index((l, name))
            start_forward(i, after)
            tok = start_forward(i + 1, after)
            full = exchange_wait(f"forward_wait_{name}_{l}", forwards[l, name], SAME_CORE_PEERS, after, forward=True)
            gathered[l, name] = full if name in ("w_in", "w_gate_up") else full.reshape(-1, d)
            return gathered[l, name], tok
        return gathered[l, name], None

    scatters = {}

    def emit(l, name, g):
        chunks = g if g.ndim == 3 else g.reshape(N_DEV, -1, d)
        land = lax.empty(chunks.shape, chunks.dtype)
        (scatters[l, name],), tok = exchange_start(f"scatter_start_{name}_{l}", [land], ALL_PEERS, srcs=[chunks])
        return tok

    rep = {name: w[name] for name in REPLICATED}
    loss, grad_x, grads = local_step(x[0], loss_target[0], rep, weight, emit, b_gate_full, conv_w_full, token)
    loss = lax.psum(loss, MESH_AXES)

    small_g = jnp.stack([jnp.concatenate([grads[l]["b_gate"], grads[l]["conv_w"][:CONV_WIDTH]], axis=0) for l in range(DEPTH)])
    small_g = jnp.transpose(small_g.reshape(DEPTH * (2 + CONV_WIDTH), N_DEV, shard_cols), (1, 0, 2))
    small_g = jnp.pad(small_g, ((0, 0), (0, small_w.shape[0] - small_g.shape[1]), (0, 0)))
    rep_shapes = [w[name].shape for name in REPLICATED]
    rep_g = _pack_rows([jnp.stack([grads[l][name].reshape(w[name].shape[1:]) for l in range(DEPTH)]) for name in REPLICATED])
    small_land = fill_own_slot("own_small", me, small_g, None, F32)
    (small_handle,), tok_small = exchange_start("scatter_start_small", [small_land], ALL_PEERS, srcs=[small_g])
    rep_land = fill_own_slot("own_replicated", me, rep_g[None], 0, F32)
    (rep_handle,), tok_rep = exchange_start("gather_start_replicated", [rep_land], ALL_PEERS, after=tok_small)

    out = {}
    after = tok_rep
    for name in ("w_down", "w_gate_up", "w_o", "w_a_out", "w_b_out", "w_in"):
        done = [exchange_wait(f"scatter_wait_{name}_{l}", scatters[l, name], ALL_PEERS, after, with_source=True)
                for l in range(DEPTH)]
        out[name] = sum_adamw("adamw_" + name, [land for _, land in done], w[name], m[name], v[name], 128,
                              me=me, own=[chunks for chunks, _ in done])
        after = out[name][0]
    small_parts = exchange_wait("scatter_wait_small", small_handle, ALL_PEERS, after)
    rep_parts = exchange_wait("gather_wait_replicated", rep_handle, ALL_PEERS, after)
    small_res = sum_adamw("adamw_small", [small_parts], small_w[None], small_m[None], small_v[None], small_w.shape[0])
    n_small = DEPTH * (2 + CONV_WIDTH)
    small_res = [r[0, :n_small].reshape(DEPTH, 2 + CONV_WIDTH, shard_cols) for r in small_res]
    out["b_gate"] = [r[:, :2] for r in small_res]
    out["conv_w"] = [r[:, 2:] for r in small_res]
    rep_res = sum_adamw("adamw_replicated", [rep_parts], *(_pack_rows([t[name] for name in REPLICATED])[None] for t in (w, m, v)), 672)
    rep_res = [_unpack_rows(r[0], rep_shapes) for r in rep_res]
    for i, name in enumerate(REPLICATED):
        out[name] = [r[i] for r in rep_res]

    return (loss, grad_x[None], *[out[name][0] for name in names], *[out[name][1] for name in names],
            *[out[name][2] for name in names], *[out[name][3] for name in names])
```

```python
import functools
import math

import jax
import jax.numpy as jnp
from jax import lax
from jax.experimental import pallas as pl
from jax.experimental.pallas import tpu as pltpu

F32 = jnp.float32
BF16 = jnp.bfloat16

DEPTH = 4
N_DEV = 8
EPS = 1e-6
CONV_WIDTH = 31
CONV_PAD = 32
CHUNK = 128
SGU_GROUPS = 8

ADAM_LR = 0.001
ADAM_B1 = 0.9
ADAM_B2 = 0.999
ADAM_EPS = 1e-08
ADAM_WD = 0.01
ADAM_STEP = 10

VMEM_BYTES_V7X = 64 * 1024 * 1024
VMEM_COMPILER_SLACK = 12 * 1024 * 1024
MESH_AXES = ("x", "y", "c")
ANY = pl.BlockSpec(memory_space=pl.ANY)


def _nbytes(shape, dtype):
    return math.prod(shape) * jnp.dtype(dtype).itemsize


def _params(block_bytes, ngrid, single_bytes=0):
    limit = min(2 * block_bytes + single_bytes + VMEM_COMPILER_SLACK, VMEM_BYTES_V7X - 4 * 1024 * 1024)
    return pltpu.CompilerParams(dimension_semantics=("arbitrary",) * ngrid, vmem_limit_bytes=int(limit))


def _in_hbm(a):
    return pltpu.with_memory_space_constraint(a, pltpu.HBM)


def _out_hbm(shape, dtype):
    return pltpu.HBM(tuple(shape), dtype)


def _deps(deps):
    return [t for t in deps if t is not None]


def _mm_body(dims, nk, kaxis, ndeps):
    def body(a_ref, b_ref, *rest):
        o_ref, *acc = rest[ndeps:]

        def prod():
            return lax.dot_general(a_ref[...], b_ref[...], (dims, ((), ())), preferred_element_type=F32)

        if nk == 1:
            o_ref[...] = prod().astype(o_ref.dtype)
            return
        acc_ref, = acc
        k = pl.program_id(kaxis)

        @pl.when(k == 0)
        def _():
            acc_ref[...] = prod()

        @pl.when(k > 0)
        def _():
            acc_ref[...] += prod()

        @pl.when(k == nk - 1)
        def _():
            o_ref[...] = acc_ref[...].astype(o_ref.dtype)

    return body


def _mm_call(name, a, b, dims, grid, a_spec, b_spec, o_spec, out_shape, out_dtype, nk, kaxis, acc_shape, deps=()):
    deps = _deps(deps)
    blocks = (_nbytes([d for d in a_spec.block_shape if d], a.dtype) + _nbytes([d for d in b_spec.block_shape if d], b.dtype)
              + _nbytes([d for d in o_spec.block_shape if d], out_dtype))
    scratch = [pltpu.VMEM(acc_shape, F32)] if nk > 1 else []
    acc_bytes = _nbytes(acc_shape, F32) * (2 if nk > 1 else 1)
    return pl.pallas_call(
        _mm_body(dims, nk, kaxis, len(deps)), name=name, grid=grid, in_specs=[a_spec, b_spec] + [ANY] * len(deps),
        out_specs=o_spec, out_shape=_out_hbm(out_shape, out_dtype), scratch_shapes=scratch,
        compiler_params=_params(blocks, len(grid), acc_bytes),
    )(_in_hbm(a), _in_hbm(b), *deps)


def mm_nn(name, a, b_and_token, out_dtype, tm, tn=None, tk=None):
    b, token = b_and_token
    m, k = a.shape
    tm = min(tm, m)
    if b.ndim == 3:
        nblk, _, nb = b.shape
        return _mm_call(name, a, b, ((1,), (0,)), (nblk, m // tm),
                        pl.BlockSpec((tm, k), lambda j, i: (i, 0)), pl.BlockSpec((None, k, nb), lambda j, i: (j, 0, 0)),
                        pl.BlockSpec((tm, nb), lambda j, i: (i, j)), (m, nblk * nb), out_dtype, 1, 0, (tm, nb), [token])
    n = b.shape[1]
    tn = tn or n
    tk = tk or k
    nk = k // tk
    return _mm_call(name, a, b, ((1,), (0,)), (n // tn, m // tm, nk),
                    pl.BlockSpec((tm, tk), lambda j, i, kk: (i, kk)), pl.BlockSpec((tk, tn), lambda j, i, kk: (kk, j)),
                    pl.BlockSpec((tm, tn), lambda j, i, kk: (i, j)), (m, n), out_dtype, nk, 2, (tm, tn), [token])


def mm_nt(name, a, b, out_dtype, tm, tn=None, deps=()):
    m = a.shape[0]
    tm = min(tm, m)
    if b.ndim == 3:
        kblk, n, kb = b.shape
        return _mm_call(name, a, b, ((1,), (1,)), (m // tm, kblk),
                        pl.BlockSpec((tm, kb), lambda i, kk: (i, kk)), pl.BlockSpec((None, n, kb), lambda i, kk: (kk, 0, 0)),
                        pl.BlockSpec((tm, n), lambda i, kk: (i, 0)), (m, n), out_dtype, kblk, 1, (tm, n), deps)
    n, kc = b.shape
    tn = tn or n
    return _mm_call(name, a, b, ((1,), (1,)), (n // tn, m // tm),
                    pl.BlockSpec((tm, kc), lambda j, i: (i, 0)), pl.BlockSpec((tn, kc), lambda j, i: (j, 0)),
                    pl.BlockSpec((tm, tn), lambda j, i: (i, j)), (m, n), out_dtype, 1, 0, (tm, tn), deps)


def mm_tn(name, a, b, out_dtype, tm, tr, nb=None):
    m, k = a.shape
    n = b.shape[1]
    tm = min(tm, m)
    nm = m // tm
    if nb is not None:
        return _mm_call(name, a, b, ((0,), (0,)), (n // nb, k // tr, nm),
                        pl.BlockSpec((tm, tr), lambda j, r, mm: (mm, r)), pl.BlockSpec((tm, nb), lambda j, r, mm: (mm, j)),
                        pl.BlockSpec((None, tr, nb), lambda j, r, mm: (j, r, 0)), (n // nb, k, nb), out_dtype, nm, 2, (tr, nb))
    return _mm_call(name, a, b, ((0,), (0,)), (k // tr, nm),
                    pl.BlockSpec((tm, tr), lambda r, mm: (mm, r)), pl.BlockSpec((tm, n), lambda r, mm: (mm, 0)),
                    pl.BlockSpec((tr, n), lambda r, mm: (r, 0)), (k, n), out_dtype, nm, 1, (tr, n))


def _row_call(name, body, grid, in_specs, out_specs, out_shape, arrays, scratch=(), aliases=None, vmem_blocks=0, deps=()):
    deps = _deps(deps)
    nin = len(arrays)

    def with_deps(*refs):
        body(*refs[:nin], *refs[nin + len(deps):])

    single = not isinstance(out_shape, (list, tuple))
    outs = [_out_hbm(o.shape, o.dtype) for o in ([out_shape] if single else out_shape)]
    return pl.pallas_call(
        with_deps, name=name, grid=grid, in_specs=list(in_specs) + [ANY] * len(deps), out_specs=out_specs,
        out_shape=outs[0] if single else outs, scratch_shapes=list(scratch), input_output_aliases=aliases or {},
        compiler_params=_params(vmem_blocks, len(grid)),
    )(*[_in_hbm(a) for a in arrays], *deps)


def _rows(tm, d, col=0):
    return pl.BlockSpec((tm, d), lambda i, *_: (i, col))


def _vec(d):
    return pl.BlockSpec((1, d), lambda *_: (0, 0))


def _rstd(x):
    return lax.rsqrt(jnp.mean(x * x, axis=-1, keepdims=True) + EPS)


def _rms_bwd(dy, x, g):
    r = _rstd(x)
    n = x * r
    w = dy * g
    dx = r * (w - n * jnp.mean(w * n, axis=-1, keepdims=True))
    return dx, jnp.sum(dy * n, axis=0, keepdims=True)


def _accumulate(ref, value, first):
    @pl.when(first)
    def _():
        ref[...] = value

    @pl.when(jnp.logical_not(first))
    def _():
        ref[...] += value


def rms_fwd(x, g, tm=256, deps=()):
    s, d = x.shape
    tm = min(tm, s)

    def body(x_ref, g_ref, h_ref):
        xv = x_ref[...]
        h_ref[...] = (xv * _rstd(xv) * g_ref[...]).astype(BF16)

    return _row_call("rms_fwd", body, (s // tm,), [_rows(tm, d), _vec(d)], _rows(tm, d),
                     jax.ShapeDtypeStruct((s, d), BF16), (x, g), vmem_blocks=tm * d * 6, deps=deps)


def norm_res(x_in, o, g_post, g_next, tm=256):
    s, d = x_in.shape
    tm = min(tm, s)

    def body(x_ref, o_ref, gp_ref, gn_ref, xo_ref, h_ref):
        ov = o_ref[...]
        xo = x_ref[...] + (ov * _rstd(ov) * gp_ref[...])
        xo_ref[...] = xo
        h_ref[...] = (xo * _rstd(xo) * gn_ref[...]).astype(BF16)

    return _row_call("norm_res", body, (s // tm,), [_rows(tm, d), _rows(tm, d), _vec(d), _vec(d)],
                     [_rows(tm, d), _rows(tm, d)],
                     [jax.ShapeDtypeStruct((s, d), F32), jax.ShapeDtypeStruct((s, d), BF16)],
                     (x_in, o, g_post, g_next), vmem_blocks=tm * d * 14)


def final_norm_loss(x_in, o, g_post, target, tm=256):
    s, d = x_in.shape
    tm = min(tm, s)

    def body(x_ref, o_ref, gp_ref, t_ref, loss_ref, dy_ref, do_ref, dg_ref):
        first = pl.program_id(0) == 0
        ov = o_ref[...]
        g = gp_ref[...]
        diff = x_ref[...] + (ov * _rstd(ov) * g) - t_ref[...]
        _accumulate(loss_ref, jnp.sum(diff * diff, axis=0, keepdims=True), first)
        dy = diff * (1.0 / d)
        dy_ref[...] = dy
        do, dg = _rms_bwd(dy, ov, g)
        do_ref[...] = do.astype(BF16)
        _accumulate(dg_ref, dg, first)

    return _row_call("final_norm_loss", body, (s // tm,), [_rows(tm, d), _rows(tm, d), _vec(d), _rows(tm, d)],
                     [_vec(d), _rows(tm, d), _rows(tm, d), _vec(d)],
                     [jax.ShapeDtypeStruct((1, d), F32), jax.ShapeDtypeStruct((s, d), F32),
                      jax.ShapeDtypeStruct((s, d), BF16), jax.ShapeDtypeStruct((1, d), F32)],
                     (x_in, o, g_post, target), vmem_blocks=tm * d * 18)


def norm_bwd_in_out(dx_out, dh, x_in, g_pre, o_below, g_post_below, tm=256, deps=()):
    s, d = x_in.shape
    tm = min(tm, s)

    def body(dxo_ref, dh_ref, x_ref, g_ref, o_ref, gb_ref, dxi_ref, do_ref, dg_ref, dgb_ref):
        first = pl.program_id(0) == 0
        chunks = [slice(c0, c0 + 256) for c0 in range(0, d, 256)]
        row_sum = lambda v: jnp.sum(v, axis=1, keepdims=True)
        sxx = swx = None
        for cols in chunks:
            xv = x_ref[:, cols]
            wv = dh_ref[:, cols].astype(F32) * g_ref[:, cols]
            sxx = row_sum(xv * xv) if sxx is None else sxx + row_sum(xv * xv)
            swx = row_sum(wv * xv) if swx is None else swx + row_sum(wv * xv)
        r = lax.rsqrt(sxx * (1.0 / d) + EPS)
        mwn = swx * r * (1.0 / d)
        soo = swo = None
        for cols in chunks:
            xv = x_ref[:, cols]
            dhv = dh_ref[:, cols].astype(F32)
            n = xv * r
            dxi = dxo_ref[:, cols] + r * (dhv * g_ref[:, cols] - n * mwn)
            dxi_ref[:, cols] = dxi
            _accumulate(dg_ref.at[:, cols], jnp.sum(dhv * n, axis=0, keepdims=True), first)
            ov = o_ref[:, cols]
            soo = row_sum(ov * ov) if soo is None else soo + row_sum(ov * ov)
            swo = row_sum(dxi * gb_ref[:, cols] * ov) if swo is None else swo + row_sum(dxi * gb_ref[:, cols] * ov)
        ro = lax.rsqrt(soo * (1.0 / d) + EPS)
        mwo = swo * ro * (1.0 / d)
        for cols in chunks:
            dxi = dxi_ref[:, cols]
            no = o_ref[:, cols] * ro
            do_ref[:, cols] = (ro * (dxi * gb_ref[:, cols] - no * mwo)).astype(BF16)
            _accumulate(dgb_ref.at[:, cols], jnp.sum(dxi * no, axis=0, keepdims=True), first)

    return _row_call("norm_bwd_in_out", body, (s // tm,),
                     [_rows(tm, d), _rows(tm, d), _rows(tm, d), _vec(d), _rows(tm, d), _vec(d)],
                     [_rows(tm, d), _rows(tm, d), _vec(d), _vec(d)],
                     [jax.ShapeDtypeStruct((s, d), F32), jax.ShapeDtypeStruct((s, d), BF16),
                      jax.ShapeDtypeStruct((1, d), F32), jax.ShapeDtypeStruct((1, d), F32)],
                     (dx_out, dh, x_in, g_pre, o_below, g_post_below), vmem_blocks=tm * d * 22, deps=deps)


def norm_bwd_in(dx_out, dh, x_in, g_pre, tm=256):
    s, d = x_in.shape
    tm = min(tm, s)

    def body(dxo_ref, dh_ref, x_ref, g_ref, dxi_ref, dg_ref):
        dx, dg = _rms_bwd(dh_ref[...].astype(F32), x_ref[...], g_ref[...])
        dxi_ref[...] = dxo_ref[...] + dx
        _accumulate(dg_ref, dg, pl.program_id(0) == 0)

    return _row_call("norm_bwd_in", body, (s // tm,), [_rows(tm, d), _rows(tm, d), _rows(tm, d), _vec(d)],
                     [_rows(tm, d), _vec(d)],
                     [jax.ShapeDtypeStruct((s, d), F32), jax.ShapeDtypeStruct((1, d), F32)],
                     (dx_out, dh, x_in, g_pre), vmem_blocks=tm * d * 16)


CONV_COLS = 256
CONV_ROWS = 32


SUBLANES = 8


def _shifted_copies(ext_ref, sh_ref):
    n = sh_ref.shape[1]
    for r in range(SUBLANES):
        sh_ref[r] = ext_ref[r:r + n, :]


def _window(sh_ref, off):
    return sh_ref[off % SUBLANES, off - off % SUBLANES:off - off % SUBLANES + CONV_ROWS, :]


def _depthwise(sh_ref, w, n_out, in_off, flip, emit):
    for r0 in range(0, n_out, CONV_ROWS):
        acc = None
        for k in range(CONV_WIDTH):
            term = w[k:k + 1, :] * _window(sh_ref, r0 + in_off + (CONV_WIDTH - 1 - k if flip else k))
            acc = term if acc is None else acc + term
        emit(r0, acc)


def _ext_scratch(tm):
    return [pltpu.VMEM((tm + CONV_PAD + SUBLANES, CONV_COLS), F32), pltpu.VMEM((SUBLANES, tm + CONV_PAD, CONV_COLS), F32)]


def glu_conv_fwd(proj, conv_w, conv_b, c_ch, tm=512):
    s = proj.shape[0]
    tm = min(tm, s)
    ncb = c_ch // CONV_COLS
    hb = tm // CONV_PAD

    def body(a_ref, g_ref, ah_ref, gh_ref, w_ref, b_ref, c_ref, ext_ref, sh_ref):
        i = pl.program_id(1)
        halo = ah_ref[...].astype(F32) * jax.nn.sigmoid(gh_ref[...].astype(F32))
        ext_ref[0:CONV_PAD, :] = jnp.where(i > 0, halo, 0.0)
        ext_ref[CONV_PAD:CONV_PAD + tm, :] = a_ref[...].astype(F32) * jax.nn.sigmoid(g_ref[...].astype(F32))
        ext_ref[CONV_PAD + tm:, :] = jnp.zeros((SUBLANES, CONV_COLS), F32)
        _shifted_copies(ext_ref, sh_ref)
        w = w_ref[...]
        bias = b_ref[...]

        def emit(r0, acc):
            c_ref[r0:r0 + CONV_ROWS, :] = acc + bias

        _depthwise(sh_ref, w, tm, CONV_PAD - (CONV_WIDTH - 1), False, emit)

    main = lambda col0: pl.BlockSpec((tm, CONV_COLS), lambda c, i: (i, col0 + c))
    halo = lambda col0: pl.BlockSpec((CONV_PAD, CONV_COLS), lambda c, i: (jnp.maximum(i * hb - 1, 0), col0 + c))
    return _row_call("glu_conv_fwd", body, (ncb, s // tm),
                     [main(0), main(ncb), halo(0), halo(ncb),
                      pl.BlockSpec((CONV_PAD, CONV_COLS), lambda c, i: (0, c)), pl.BlockSpec((1, CONV_COLS), lambda c, i: (0, c))],
                     pl.BlockSpec((tm, CONV_COLS), lambda c, i: (i, c)), jax.ShapeDtypeStruct((s, c_ch), F32),
                     (proj, proj, proj, proj, conv_w, conv_b),
                     scratch=_ext_scratch(tm), vmem_blocks=tm * CONV_COLS * 32)


def _layer_norm_stats(x):
    mu = jnp.mean(x, axis=-1, keepdims=True)
    xc = x - mu
    rstd = lax.rsqrt(jnp.mean(xc * xc, axis=-1, keepdims=True) + EPS)
    return xc * rstd, rstd


def _layer_norm_bwd(dy, xhat, rstd, g):
    dxh = dy * g
    return rstd * (dxh - jnp.mean(dxh, axis=-1, keepdims=True) - xhat * jnp.mean(dxh * xhat, axis=-1, keepdims=True))


def ln_silu_fwd(c, ln_g, ln_b, tm=256):
    s, d = c.shape
    tm = min(tm, s)

    def body(c_ref, g_ref, b_ref, s_ref):
        xhat, _ = _layer_norm_stats(c_ref[...])
        s_ref[...] = jax.nn.silu(xhat * g_ref[...] + b_ref[...]).astype(BF16)

    return _row_call("ln_silu_fwd", body, (s // tm,), [_rows(tm, d), _vec(d), _vec(d)], _rows(tm, d),
                     jax.ShapeDtypeStruct((s, d), BF16), (c, ln_g, ln_b), vmem_blocks=tm * d * 10)


def ln_silu_bwd(c, ds, ln_g, ln_b, tm=256, deps=()):
    s, d = c.shape
    tm = min(tm, s)

    def body(c_ref, ds_ref, g_ref, b_ref, dc_ref, dg_ref, db_ref, dcb_ref):
        first = pl.program_id(0) == 0
        g = g_ref[...]
        xhat, rstd = _layer_norm_stats(c_ref[...])
        y = xhat * g + b_ref[...]
        sg = jax.nn.sigmoid(y)
        dln = ds_ref[...].astype(F32) * (sg * (1.0 + y * (1.0 - sg)))
        _accumulate(dg_ref, jnp.sum(dln * xhat, axis=0, keepdims=True), first)
        _accumulate(db_ref, jnp.sum(dln, axis=0, keepdims=True), first)
        dc = _layer_norm_bwd(dln, xhat, rstd, g)
        dc_ref[...] = dc.astype(BF16)
        _accumulate(dcb_ref, jnp.sum(dc, axis=0, keepdims=True), first)

    vec = jax.ShapeDtypeStruct((1, d), F32)
    return _row_call("ln_silu_bwd", body, (s // tm,), [_rows(tm, d), _rows(tm, d), _vec(d), _vec(d)],
                     [_rows(tm, d), _vec(d), _vec(d), _vec(d)], [jax.ShapeDtypeStruct((s, d), BF16), vec, vec, vec],
                     (c, ds, ln_g, ln_b), vmem_blocks=tm * d * 20, deps=deps)


def conv_bwd(proj, dc, conv_w, c_ch, tm=512):
    s = proj.shape[0]
    tm = min(tm, s)
    ncb = c_ch // CONV_COLS
    hb = tm // CONV_PAD
    last_halo = s // CONV_PAD - 1
    n_i = s // tm

    def body(a_ref, g_ref, ah_ref, gh_ref, dc_ref, dcn_ref, w_ref, dglu_ref, dw_ref,
             ext_ref, sh_ref, dce_ref, dsh_ref, dwacc_ref):
        i = pl.program_id(1)
        zeros = jnp.zeros((SUBLANES, CONV_COLS), F32)
        halo = ah_ref[...].astype(F32) * jax.nn.sigmoid(gh_ref[...].astype(F32))
        ext_ref[0:CONV_PAD, :] = jnp.where(i > 0, halo, 0.0)
        ext_ref[CONV_PAD:CONV_PAD + tm, :] = a_ref[...].astype(F32) * jax.nn.sigmoid(g_ref[...].astype(F32))
        ext_ref[CONV_PAD + tm:, :] = zeros
        _shifted_copies(ext_ref, sh_ref)
        dce_ref[0:tm, :] = dc_ref[...].astype(F32)
        dce_ref[tm:tm + CONV_PAD, :] = jnp.where(i < n_i - 1, dcn_ref[...].astype(F32), 0.0)
        dce_ref[tm + CONV_PAD:, :] = zeros
        _shifted_copies(dce_ref, dsh_ref)
        w = w_ref[...]

        def emit(r0, acc):
            dglu_ref[r0:r0 + CONV_ROWS, :] = acc.astype(BF16)

        _depthwise(dsh_ref, w, tm, 0, True, emit)

        for k in range(CONV_WIDTH):
            acc = None
            for r0 in range(0, tm, CONV_ROWS):
                term = dce_ref[r0:r0 + CONV_ROWS, :] * _window(sh_ref, r0 + CONV_PAD - (CONV_WIDTH - 1) + k)
                acc = term if acc is None else acc + term
            dwacc_ref[k:k + 1, :] = jnp.sum(acc, axis=0, keepdims=True)
        dwacc_ref[CONV_WIDTH:, :] = jnp.zeros((CONV_PAD - CONV_WIDTH, CONV_COLS), F32)
        _accumulate(dw_ref, dwacc_ref[...], i == 0)

    main = lambda col0: pl.BlockSpec((tm, CONV_COLS), lambda c, i: (i, col0 + c))
    halo = lambda col0: pl.BlockSpec((CONV_PAD, CONV_COLS), lambda c, i: (jnp.maximum(i * hb - 1, 0), col0 + c))
    nxt = pl.BlockSpec((CONV_PAD, CONV_COLS), lambda c, i: (jnp.minimum((i + 1) * hb, last_halo), c))
    wspec = pl.BlockSpec((CONV_PAD, CONV_COLS), lambda c, i: (0, c))
    return _row_call("conv_bwd", body, (ncb, n_i),
                     [main(0), main(ncb), halo(0), halo(ncb), main(0), nxt, wspec],
                     [main(0), wspec],
                     [jax.ShapeDtypeStruct((s, c_ch), BF16), jax.ShapeDtypeStruct((CONV_PAD, c_ch), F32)],
                     (proj, proj, proj, proj, dc, dc, conv_w),
                     scratch=_ext_scratch(tm) + _ext_scratch(tm) + [pltpu.VMEM((CONV_PAD, CONV_COLS), F32)],
                     vmem_blocks=tm * CONV_COLS * 56)


ELEMENTWISE_COLS = 512


def _col_chunks(d):
    return [slice(c0, c0 + ELEMENTWISE_COLS) for c0 in range(0, d, ELEMENTWISE_COLS)]


def _pair_spec(tm, d, pair):
    return pl.BlockSpec((tm, 2 * d), lambda i: (i, pair))


def glu_bwd(dproj, dglu, proj, c_ch, tm=256):
    s = proj.shape[0]
    tm = min(tm, s)

    def body(_, dglu_ref, a_ref, g_ref, out_ref):
        for cols in _col_chunks(c_ch):
            dg = dglu_ref[:, cols].astype(F32)
            sg = jax.nn.sigmoid(g_ref[:, cols].astype(F32))
            out_ref[:, cols] = (dg * sg).astype(BF16)
            out_ref[:, c_ch + cols.start:c_ch + cols.stop] = (dg * a_ref[:, cols].astype(F32) * (sg * (1.0 - sg))).astype(BF16)

    return _row_call("glu_bwd", body, (s // tm,), [ANY, _rows(tm, c_ch), _rows(tm, c_ch), _rows(tm, c_ch, 1)],
                     _pair_spec(tm, c_ch, 0), jax.ShapeDtypeStruct(dproj.shape, BF16), (dproj, dglu, proj, proj),
                     aliases={0: 0}, vmem_blocks=tm * c_ch * 12)


_SQRT_HALF = 0.7071067811865476
_INV_SQRT_2PI = 0.3989422804014327


def _gelu_parts(x):
    cdf = 0.5 * (1.0 + lax.erf(x * _SQRT_HALF))
    return cdf, x * cdf


def _gelu_grad(x, cdf):
    return cdf + x * (_INV_SQRT_2PI * jnp.exp(-0.5 * x * x))


def _sgu_specs(tm, ch):
    grp = ch // SGU_GROUPS
    full3 = lambda shape: pl.BlockSpec(shape, lambda *_: (0, 0, 0))
    return grp, full3((SGU_GROUPS, CHUNK, CHUNK)), full3((SGU_GROUPS, CHUNK, grp))


def sgu_fwd(proj, ln_g, ln_b, w_mix, b_mix, ch, col0, tm=CHUNK):
    s = proj.shape[0]
    grp, wspec, bspec = _sgu_specs(tm, ch)

    def body(u_ref, v_ref, g_ref, b_ref, w_ref, bm_ref, p_ref, mix_ref):
        _, u = _gelu_parts(u_ref[...].astype(F32))
        _, v0 = _gelu_parts(v_ref[...].astype(F32))
        xhat, _ = _layer_norm_stats(v0)
        vn = (xhat * g_ref[...] + b_ref[...]).astype(BF16)
        for n in range(tm // CHUNK):
            for g in range(SGU_GROUPS):
                blk = vn[n * CHUNK:(n + 1) * CHUNK, g * grp:(g + 1) * grp]
                mix_ref[n * CHUNK:(n + 1) * CHUNK, g * grp:(g + 1) * grp] = (
                    jnp.dot(w_ref[g], blk, preferred_element_type=F32) + bm_ref[g])
        p_ref[...] = (u * mix_ref[...]).astype(BF16)

    return _row_call("sgu_fwd", body, (s // tm,),
                     [_rows(tm, ch, col0), _rows(tm, ch, col0 + 1), _vec(ch), _vec(ch), wspec, bspec], _rows(tm, ch),
                     jax.ShapeDtypeStruct((s, ch), BF16), (proj, proj, ln_g, ln_b, w_mix, b_mix),
                     scratch=[pltpu.VMEM((tm, ch), F32)], vmem_blocks=tm * ch * 30)


def sgu_bwd(dproj, proj, dp, ln_g, ln_b, w_mix, w_mix_t, b_mix, ch, col0, tm=CHUNK):
    s = proj.shape[0]
    assert tm == CHUNK and col0 % 2 == 0
    grp, wspec, bspec = _sgu_specs(tm, ch)
    groups = [slice(k * grp, (k + 1) * grp) for k in range(SGU_GROUPS)]

    def body(_, u_ref, v_ref, dp_ref, g_ref, b_ref, w_ref, wt_ref, bm_ref,
             out_ref, dw_ref, dbm_ref, dg_ref, db_ref, u_s, gu_s, gv_s, xh_s, dvn_s):
        first = pl.program_id(0) == 0
        row_sum = lambda x: jnp.sum(x, axis=1, keepdims=True)
        total = None
        for cols in groups:
            ub = u_ref[:, cols].astype(F32)
            vb = v_ref[:, cols].astype(F32)
            cdf_u, u = _gelu_parts(ub)
            cdf_v, v0 = _gelu_parts(vb)
            u_s[:, cols] = u
            gu_s[:, cols] = _gelu_grad(ub, cdf_u)
            gv_s[:, cols] = _gelu_grad(vb, cdf_v)
            xh_s[:, cols] = v0
            total = row_sum(v0) if total is None else total + row_sum(v0)
        mu = total * (1.0 / ch)
        total = None
        for cols in groups:
            xc = xh_s[:, cols] - mu
            xh_s[:, cols] = xc
            total = row_sum(xc * xc) if total is None else total + row_sum(xc * xc)
        rstd = lax.rsqrt(total * (1.0 / ch) + EPS)

        t1 = t2 = None
        for k, cols in enumerate(groups):
            g = g_ref[:, cols]
            xhat = xh_s[:, cols] * rstd
            xh_s[:, cols] = xhat
            vn = (xhat * g + b_ref[:, cols]).astype(BF16)
            dpk = dp_ref[:, cols].astype(F32)
            dmix = dpk * u_s[:, cols]
            dmix_bf = dmix.astype(BF16)
            mixed = jnp.dot(w_ref[k], vn, preferred_element_type=F32) + bm_ref[k]
            out_ref[:, cols] = (dpk * mixed * gu_s[:, cols]).astype(BF16)
            dvn = jnp.dot(wt_ref[k], dmix_bf, preferred_element_type=F32)
            dvn_s[:, cols] = dvn
            _accumulate(dw_ref.at[k], lax.dot_general(dmix_bf, vn, (((1,), (1,)), ((), ())), preferred_element_type=F32), first)
            _accumulate(dbm_ref.at[k], jnp.broadcast_to(row_sum(dmix), (CHUNK, CHUNK)), first)
            _accumulate(dg_ref.at[:, cols], jnp.sum(dvn * xhat, axis=0, keepdims=True), first)
            _accumulate(db_ref.at[:, cols], jnp.sum(dvn, axis=0, keepdims=True), first)
            dxh = dvn * g
            t1 = row_sum(dxh) if t1 is None else t1 + row_sum(dxh)
            t2 = row_sum(dxh * xhat) if t2 is None else t2 + row_sum(dxh * xhat)
        m1 = t1 * (1.0 / ch)
        m2 = t2 * (1.0 / ch)
        for cols in groups:
            dv0 = rstd * (dvn_s[:, cols] * g_ref[:, cols] - m1 - xh_s[:, cols] * m2)
            out_ref[:, ch + cols.start:ch + cols.stop] = (dv0 * gv_s[:, cols]).astype(BF16)

    vec = _vec(ch)
    acc3 = lambda: pl.BlockSpec((SGU_GROUPS, CHUNK, CHUNK), lambda i: (0, 0, 0))
    vshape = jax.ShapeDtypeStruct((1, ch), F32)
    mshape = jax.ShapeDtypeStruct((SGU_GROUPS, CHUNK, CHUNK), F32)
    return _row_call("sgu_bwd", body, (s // tm,),
                     [ANY, _rows(tm, ch, col0), _rows(tm, ch, col0 + 1), _rows(tm, ch), vec, vec, wspec, wspec, bspec],
                     [_pair_spec(tm, ch, col0 // 2), acc3(), acc3(), vec, vec],
                     [jax.ShapeDtypeStruct(dproj.shape, BF16), mshape, mshape, vshape, vshape],
                     (dproj, proj, proj, dp, ln_g, ln_b, w_mix, w_mix_t, b_mix),
                     scratch=[pltpu.VMEM((tm, ch), F32)] * 5, aliases={0: 0}, vmem_blocks=tm * ch * 40)


def branches_merge(s_act, p_act, wa_and_token, wb_and_token, proj, b_gate, col0, tm=256, tn=1024):
    (wa, tok_a), (wb, tok_b) = wa_and_token, wb_and_token
    deps = _deps([tok_a, tok_b])
    s, d = s_act.shape
    tm = min(tm, s)
    per = d // tn

    def body(s_ref, p_ref, wa_ref, wb_ref, l0_ref, l1_ref, bg_ref, *rest):
        ya_ref, yb_ref, m_ref = rest[len(deps):]
        ya = jnp.dot(s_ref[...], wa_ref[...], preferred_element_type=F32)
        yb = jnp.dot(p_ref[...], wb_ref[...], preferred_element_type=F32)
        ya_ref[...] = ya.astype(BF16)
        yb_ref[...] = yb.astype(BF16)
        g0 = jax.nn.sigmoid(l0_ref[...].astype(F32) + bg_ref[0:1, :])
        g1 = jax.nn.sigmoid(l1_ref[...].astype(F32) + bg_ref[1:2, :])
        m_ref[...] = (g0 * ya + g1 * yb).astype(BF16)

    act = pl.BlockSpec((tm, d), lambda j, i: (i, 0))
    wgt = pl.BlockSpec((d, tn), lambda j, i: (0, j))
    logits = lambda col: pl.BlockSpec((tm, tn), lambda j, i: (i, col * per + j))
    oblk = pl.BlockSpec((tm, tn), lambda j, i: (i, j))
    return pl.pallas_call(
        body, name="branches_merge", grid=(per, s // tm),
        in_specs=[act, act, wgt, wgt, logits(col0), logits(col0 + 1), pl.BlockSpec((2, tn), lambda j, i: (0, j))] + [ANY] * len(deps),
        out_specs=[oblk, oblk, oblk],
        out_shape=[_out_hbm((s, d), BF16), _out_hbm((s, d), BF16), _out_hbm((s, d), BF16)],
        compiler_params=_params(2 * tm * d * 2 + 2 * d * tn * 2 + 2 * tm * tn * 2 + tm * tn * 10, 2, 4 * tm * tn * 4),
    )(*[_in_hbm(a) for a in (s_act, p_act, wa, wb, proj, proj, b_gate)], *deps)


def merge_bwd(dm, y_a, y_b, proj, b_gate, col0, tm=256, deps=()):
    s, d = y_a.shape
    tm = min(tm, s)

    assert col0 % 2 == 0

    def body(dm_ref, ya_ref, yb_ref, l0_ref, l1_ref, bg_ref, dya_ref, dyb_ref, out_ref, dbg_ref):
        first = pl.program_id(0) == 0
        for cols in _col_chunks(d):
            dmv = dm_ref[:, cols].astype(F32)
            g0 = jax.nn.sigmoid(l0_ref[:, cols].astype(F32) + bg_ref[0:1, cols])
            g1 = jax.nn.sigmoid(l1_ref[:, cols].astype(F32) + bg_ref[1:2, cols])
            dya_ref[:, cols] = (dmv * g0).astype(BF16)
            dyb_ref[:, cols] = (dmv * g1).astype(BF16)
            dl0 = dmv * ya_ref[:, cols].astype(F32) * (g0 * (1.0 - g0))
            dl1 = dmv * yb_ref[:, cols].astype(F32) * (g1 * (1.0 - g1))
            _accumulate(dbg_ref.at[0:1, cols], jnp.sum(dl0, axis=0, keepdims=True), first)
            _accumulate(dbg_ref.at[1:2, cols], jnp.sum(dl1, axis=0, keepdims=True), first)
            out_ref[:, cols] = dl0.astype(BF16)
            out_ref[:, d + cols.start:d + cols.stop] = dl1.astype(BF16)

    bgspec = pl.BlockSpec((2, d), lambda i: (0, 0))
    return _row_call("merge_bwd", body, (s // tm,),
                     [_rows(tm, d), _rows(tm, d), _rows(tm, d), _rows(tm, d, col0), _rows(tm, d, col0 + 1), bgspec],
                     [_rows(tm, d), _rows(tm, d), _pair_spec(tm, d, col0 // 2), bgspec],
                     [jax.ShapeDtypeStruct((s, d), BF16), jax.ShapeDtypeStruct((s, d), BF16),
                      jax.ShapeDtypeStruct(proj.shape, BF16), jax.ShapeDtypeStruct((2, d), F32)],
                     (dm, y_a, y_b, proj, proj, b_gate), vmem_blocks=tm * d * 24, deps=deps)


def gate_up_swiglu(h, w_and_token, tm=256):
    w, token = w_and_token
    deps = _deps([token])
    s, d = h.shape
    nblk, _, nb = w.shape
    half = nblk // 2
    tm = min(tm, s)

    def body(h_ref, wg_ref, wu_ref, *rest):
        g_ref, u_ref, f_ref = rest[len(deps):]
        hv = h_ref[...]
        g = jnp.dot(hv, wg_ref[...], preferred_element_type=F32)
        u = jnp.dot(hv, wu_ref[...], preferred_element_type=F32)
        g_ref[...] = g.astype(BF16)
        u_ref[...] = u.astype(BF16)
        f_ref[...] = (jax.nn.silu(g) * u).astype(BF16)

    out = _out_hbm((s, half * nb), BF16)
    oblk = pl.BlockSpec((tm, nb), lambda j, i: (i, j))
    return pl.pallas_call(
        body, name="gate_up_swiglu", grid=(half, s // tm),
        in_specs=[pl.BlockSpec((tm, d), lambda j, i: (i, 0)), pl.BlockSpec((None, d, nb), lambda j, i: (j, 0, 0)),
                  pl.BlockSpec((None, d, nb), lambda j, i: (j + half, 0, 0))] + [ANY] * len(deps),
        out_specs=[oblk, oblk, oblk], out_shape=[out, out, out],
        compiler_params=_params(tm * d * 2 + 2 * d * nb * 2 + 3 * tm * nb * 2, 2, 4 * tm * nb * 4),
    )(_in_hbm(h), _in_hbm(w), _in_hbm(w), *deps)


def swiglu_bwd(g_act, u_act, df, tm=256, deps=()):
    s, half = g_act.shape
    w2 = 2 * half
    tm = min(tm, s)
    chunk = w2 // N_DEV

    def body(g_ref, u_ref, df_ref, out_ref):
        for c0 in range(0, half, chunk):
            cols = slice(c0, c0 + chunk)
            g = g_ref[:, cols].astype(F32)
            sg = jax.nn.sigmoid(g)
            dfv = df_ref[:, cols].astype(F32)
            out_ref[:, cols] = (dfv * u_ref[:, cols].astype(F32) * (sg * (1.0 + g * (1.0 - sg)))).astype(BF16)
            out_ref[:, half + c0:half + c0 + chunk] = (dfv * (g * sg)).astype(BF16)

    return _row_call("swiglu_bwd", body, (s // tm,), [_rows(tm, half), _rows(tm, half), _rows(tm, half)],
                     _rows(tm, w2), jax.ShapeDtypeStruct((s, w2), BF16), (g_act, u_act, df), vmem_blocks=tm * w2 * 5, deps=deps)


def _peers():
    x, y, c = lax.axis_index("x"), lax.axis_index("y"), lax.axis_index("c")
    me = 4 * x + 2 * y + c
    peers = []
    for k in range(1, N_DEV):
        px = 1 - x if k & 4 else x
        py = 1 - y if k & 2 else y
        pc = 1 - c if k & 1 else c
        peers.append(((px, py, pc), 4 * px + 2 * py + pc))
    return me, peers


def _exchange(name, arrays, scatter):
    n = len(arrays)

    def body(*refs):
        ins, outs = refs[:n], refs[n:2 * n]
        send_sems, recv_sems, local_sems = refs[2 * n:]
        me, peers = _peers()

        def remote(a, k):
            (pos, idx) = peers[k]
            src = ins[a].at[idx] if scatter else ins[a]
            return pltpu.make_async_remote_copy(src_ref=src, dst_ref=outs[a].at[me], send_sem=send_sems.at[a, k],
                                                recv_sem=recv_sems.at[a, k], device_id=pos, device_id_type=pl.DeviceIdType.MESH)

        def arrival(a, k):
            (pos, idx) = peers[k]
            src = ins[a].at[idx] if scatter else ins[a]
            return pltpu.make_async_remote_copy(src_ref=src, dst_ref=outs[a].at[idx], send_sem=send_sems.at[a, k],
                                                recv_sem=recv_sems.at[a, k], device_id=pos, device_id_type=pl.DeviceIdType.MESH)

        local = [pltpu.make_async_copy(ins[a].at[me] if scatter else ins[a], outs[a].at[me], local_sems.at[a]) for a in range(n)]
        sends = [remote(a, k) for k in range(N_DEV - 1) for a in range(n)]
        for cp in sends:
            cp.start()
        for cp in local:
            cp.start()
        for k in range(N_DEV - 1):
            for a in range(n):
                arrival(a, k).wait_recv()
        for cp in sends:
            cp.wait_send()
        for cp in local:
            cp.wait()

    out_shape = [jax.ShapeDtypeStruct(a.shape if scatter else (N_DEV,) + a.shape, a.dtype) for a in arrays]
    return pl.pallas_call(
        body, name=name, in_specs=[ANY] * n, out_specs=[ANY] * n, out_shape=out_shape,
        scratch_shapes=[pltpu.SemaphoreType.DMA((n, N_DEV - 1)), pltpu.SemaphoreType.DMA((n, N_DEV - 1)),
                        pltpu.SemaphoreType.DMA((n,))],
    )(*arrays)


HBM_SPEC = pl.BlockSpec(memory_space=pltpu.HBM)
SEM_SPEC = pl.BlockSpec(memory_space=pltpu.SEMAPHORE)
DATAFLOW_EFFECT = pltpu.SideEffectType.DATAFLOW_SIDE_EFFECTING
ALL_PEERS = (1, 2, 3, 4, 5, 6, 7)
SIBLING = 1
SAME_CORE_PEERS = (2, 4, 6)


def fill_own_slot(name, me, src, block, dtype):
    _, r, c = src.shape
    tr = _row_tile(r, 256)

    def body(me_ref, src_ref, out_ref):
        out_ref[...] = src_ref[...].astype(dtype)

    if block is None:
        src_index = lambda i, me_ref: (me_ref[0], i, 0)
    else:
        src_index = lambda i, me_ref: (block, i, 0)
    grid_spec = pltpu.PrefetchScalarGridSpec(
        num_scalar_prefetch=1, grid=(r // tr,), in_specs=[pl.BlockSpec((None, tr, c), src_index)],
        out_specs=pl.BlockSpec((None, tr, c), lambda i, me_ref: (me_ref[0], i, 0)))
    return pl.pallas_call(body, name=name, grid_spec=grid_spec, out_shape=_out_hbm((N_DEV, r, c), dtype),
                          compiler_params=_params(tr * c * (src.dtype.itemsize + jnp.dtype(dtype).itemsize), 1))(me, _in_hbm(src))


def _split_copy(src, land, send_sem, recv_sem, k, peers, me, arriving, forward):
    pos, idx = peers[k - 1]
    if forward:
        pos = peers[SIBLING - 1][0]
        slot = peers[(k | SIBLING) - 1][1] if arriving else idx
        src_ref, dst_ref = land.at[slot], land.at[slot]
    else:
        src_ref = land.at[me] if src is None else src.at[idx]
        dst_ref = land.at[idx if arriving else me]
    return pltpu.make_async_remote_copy(src_ref=src_ref, dst_ref=dst_ref, send_sem=send_sem, recv_sem=recv_sem,
                                        device_id=pos, device_id_type=pl.DeviceIdType.MESH)


def exchange_start(name, lands, peer_ks, srcs=None, after=None, forward=False):
    n = len(lands)
    ns = n if srcs is not None else 0
    extra = _deps([after])
    bufs = (list(srcs) if srcs is not None else []) + list(lands)

    def body(*refs):
        src, land = refs[:ns], refs[ns:ns + n]
        outs = refs[ns + n + len(extra):]
        send_sems, recv_sems, token = outs[:n], outs[n:2 * n], outs[2 * n + ns + n]
        me, peers = _peers()
        for a in range(n):
            for j, k in enumerate(peer_ks):
                _split_copy(src[a] if ns else None, land[a], send_sems[a].at[j], recv_sems[a].at[j], k, peers, me,
                            False, forward).start()
        token[...] = jnp.zeros_like(token)

    sems = [pltpu.SemaphoreType.DMA((len(peer_ks),))] * (2 * n)
    res = pl.pallas_call(
        body, name=name, in_specs=[HBM_SPEC] * len(bufs) + [ANY] * len(extra),
        out_specs=[SEM_SPEC] * (2 * n) + [HBM_SPEC] * len(bufs) + [pl.BlockSpec(memory_space=pltpu.VMEM)],
        out_shape=sems + [pltpu.HBM(a.shape, a.dtype) for a in bufs] + [jax.ShapeDtypeStruct((8, 128), F32)],
        input_output_aliases={i: 2 * n + i for i in range(len(bufs))},
        compiler_params=pltpu.CompilerParams(has_side_effects=DATAFLOW_EFFECT),
    )(*[_in_hbm(a) for a in bufs], *extra)
    handles = [(res[a], res[n + a], res[2 * n + a] if ns else None, res[2 * n + ns + a]) for a in range(n)]
    return handles, res[2 * n + ns + n]


def exchange_wait(name, handle, peer_ks, after, forward=False, with_source=False):
    send_sem, recv_sem, src, land = handle
    bufs = ([src] if src is not None else []) + [land]
    nb = len(bufs)

    def body(*refs):
        src_ref = refs[0] if nb == 2 else None
        land_ref, send_ref, recv_ref = refs[nb - 1], refs[nb], refs[nb + 1]
        me, peers = _peers()
        for j, k in enumerate(peer_ks):
            cp = _split_copy(src_ref, land_ref, send_ref.at[j], recv_ref.at[j], k, peers, me, True, forward)
            cp.wait_send()
            cp.wait_recv()

    outs = pl.pallas_call(
        body, name=name, in_specs=[HBM_SPEC] * nb + [SEM_SPEC, SEM_SPEC, ANY], out_specs=[HBM_SPEC] * nb,
        out_shape=[pltpu.HBM(a.shape, a.dtype) for a in bufs],
        input_output_aliases={i: i for i in range(nb)}, compiler_params=pltpu.CompilerParams(has_side_effects=DATAFLOW_EFFECT),
    )(*bufs, send_sem, recv_sem, after)
    return tuple(outs) if with_source else outs[nb - 1]


def _row_tile(r, cap):
    if r <= cap:
        return r
    return max(t for t in range(16, cap + 1, 16) if r % t == 0)


def sum_adamw(name, parts, w, m, v, tr, me=None, own=None):
    nl, r, c = w.shape
    tr = _row_tile(r, tr)
    c1 = 1.0 - ADAM_B1 ** ADAM_STEP
    c2 = 1.0 - ADAM_B2 ** ADAM_STEP
    nown = nl if own is not None else 0

    def body(*refs):
        me_ref = refs[0] if nown else None
        refs = refs[1:] if nown else refs
        part_refs, own_refs = refs[:nl], refs[nl:nl + nown]
        w_ref, m_ref, v_ref, g_out, d_out, m_out, v_out = refs[nl + nown:]
        layer = pl.program_id(0)
        for j in range(nl):
            @pl.when(layer == j)
            def _(j=j):
                g = None
                for p in range(N_DEV):
                    term = part_refs[j][p].astype(F32)
                    if nown:
                        term = jnp.where(me_ref[0] == p, own_refs[j][...].astype(F32), term)
                    g = term if g is None else g + term
                mn = ADAM_B1 * m_ref[...] + (1.0 - ADAM_B1) * g
                vn = ADAM_B2 * v_ref[...] + (1.0 - ADAM_B2) * (g * g)
                g_out[...] = g
                m_out[...] = mn
                v_out[...] = vn
                d_out[...] = -ADAM_LR * ((mn / c1) / (jnp.sqrt(vn / c2) + ADAM_EPS) + ADAM_WD * w_ref[...])

    def part_spec(j):
        return pl.BlockSpec((N_DEV, tr, c), lambda l, i, *_: (0, jnp.where(l == j, i, 0), 0))

    def own_spec(j):
        return pl.BlockSpec((None, tr, c), lambda l, i, me_ref: (me_ref[0], jnp.where(l == j, i, 0), 0))

    lspec = pl.BlockSpec((None, tr, c), lambda l, i, *_: (l, i, 0))
    out = jax.ShapeDtypeStruct((nl, r, c), F32)
    in_specs = [part_spec(j) for j in range(nl)] + [own_spec(j) for j in range(nown)] + [lspec] * 3
    vmem_blocks = (N_DEV + 1) * nl * tr * c * parts[0].dtype.itemsize + 7 * tr * c * 4
    if not nown:
        return _row_call(name, body, (nl, r // tr), in_specs, [lspec] * 4, [out] * 4, tuple(parts) + (w, m, v),
                         vmem_blocks=vmem_blocks)
    grid_spec = pltpu.PrefetchScalarGridSpec(num_scalar_prefetch=1, grid=(nl, r // tr), in_specs=in_specs, out_specs=[lspec] * 4)
    return pl.pallas_call(body, name=name, grid_spec=grid_spec, out_shape=[_out_hbm(out.shape, out.dtype)] * 4,
                          compiler_params=_params(vmem_blocks, 2))(me, *[_in_hbm(a) for a in tuple(parts) + tuple(own) + (w, m, v)])


REPLICATED = ("norm_mix_pre", "norm_mix_post", "norm_ffn_pre", "norm_ffn_post", "conv_b", "conv_ln_g", "conv_ln_b",
              "sgu_ln_g", "sgu_ln_b", "w_spatial", "b_spatial")
MATRICES = ("w_in", "w_a_out", "w_b_out", "w_o", "w_gate_up", "w_down")


def local_step(x, target, rep, weight, emit, b_gate, conv_w, start_token=None):
    s, d = x.shape
    causal = jnp.tril(jnp.ones((CHUNK, CHUNK), dtype=bool))
    row = lambda name, l: rep[name][l].reshape(1, -1)

    saved = []
    h = rms_fwd(x, row("norm_mix_pre", 0), deps=[start_token])
    for l in range(DEPTH):
        w_mix = jnp.where(causal[None], rep["w_spatial"][l], 0.0).astype(BF16)
        b_mix = jnp.broadcast_to(rep["b_spatial"][l][:, :, None], (SGU_GROUPS, CHUNK, d // SGU_GROUPS))
        proj = mm_nn("proj", h, weight(l, "w_in", h), BF16, 512)
        c = glu_conv_fwd(proj, conv_w[l], row("conv_b", l), d)
        s_act = ln_silu_fwd(c, row("conv_ln_g", l), row("conv_ln_b", l))
        p_act = sgu_fwd(proj, row("sgu_ln_g", l), row("sgu_ln_b", l), w_mix, b_mix, d, 2)
        y_a, y_b, merged = branches_merge(s_act, p_act, weight(l, "w_a_out", s_act), weight(l, "w_b_out", p_act),
                                          proj, b_gate[l], 4)
        o = mm_nn("branch_out", merged, weight(l, "w_o", merged), F32, 512)
        x_mid, h2 = norm_res(x, o, row("norm_mix_post", l), row("norm_ffn_pre", l))
        g_act, u_act, f = gate_up_swiglu(h2, weight(l, "w_gate_up", h2))
        o2 = mm_nn("down", f, weight(l, "w_down", f), F32, 512, tn=1024)
        saved.append(dict(x_in=x, h=h, proj=proj, c=c, s_act=s_act, p_act=p_act, y_a=y_a, y_b=y_b, merged=merged, o=o,
                          x_mid=x_mid, h2=h2, g_act=g_act, u_act=u_act, f=f, o2=o2, w_mix=w_mix, b_mix=b_mix))
        if l + 1 < DEPTH:
            x, h = norm_res(x_mid, o2, row("norm_ffn_post", l), row("norm_mix_pre", l + 1))

    top = saved[-1]
    loss_vec, dx, do2, dg_ffn_post = final_norm_loss(top["x_mid"], top["o2"], row("norm_ffn_post", DEPTH - 1), target)
    loss = (0.5 / d) * jnp.sum(loss_vec)

    grads = [None] * DEPTH
    for l in reversed(range(DEPTH)):
        sv = saved[l]
        wl = {name: weight(l, name, None)[0] for name in MATRICES}
        g = {"norm_ffn_post": dg_ffn_post}
        df = mm_nt("d_down_in", do2, wl["w_down"], BF16, 512, tn=wl["w_down"].shape[0] // 4)
        tok = emit(l, "w_down", mm_tn("d_down_w", sv["f"], do2, BF16, 512, sv["f"].shape[1] // 4))
        dgu = swiglu_bwd(sv["g_act"], sv["u_act"], df, deps=[tok])
        dh2 = mm_nt("d_gate_up_in", dgu, wl["w_gate_up"], BF16, 1024)
        tok = emit(l, "w_gate_up", mm_tn("d_gate_up_w", sv["h2"], dgu, BF16, 2048, d // 2, nb=wl["w_gate_up"].shape[2]))
        dx, do, g["norm_ffn_pre"], g["norm_mix_post"] = norm_bwd_in_out(
            dx, dh2, sv["x_mid"], row("norm_ffn_pre", l), sv["o"], row("norm_mix_post", l), deps=[tok])
        dm = mm_nt("d_square_in", do, wl["w_o"], BF16, 512)
        tok = emit(l, "w_o", mm_tn("d_square_w", sv["merged"], do, BF16, 512, d // 2))
        dy_a, dy_b, dproj, g["b_gate"] = merge_bwd(dm, sv["y_a"], sv["y_b"], sv["proj"], b_gate[l], 4, deps=[tok])
        ds = mm_nt("d_square_in", dy_a, wl["w_a_out"], BF16, 512)
        tok = emit(l, "w_a_out", mm_tn("d_square_w", sv["s_act"], dy_a, BF16, 512, d // 2))
        dp = mm_nt("d_square_in", dy_b, wl["w_b_out"], BF16, 512, deps=[tok])
        tok = emit(l, "w_b_out", mm_tn("d_square_w", sv["p_act"], dy_b, BF16, 512, d // 2))
        dc, g["conv_ln_g"], g["conv_ln_b"], g["conv_b"] = ln_silu_bwd(
            sv["c"], ds, row("conv_ln_g", l), row("conv_ln_b", l), deps=[tok])
        dglu, g["conv_w"] = conv_bwd(sv["proj"], dc, conv_w[l], d)
        dproj = glu_bwd(dproj, dglu, sv["proj"], d)
        dproj, dw_mix, db_mix, g["sgu_ln_g"], g["sgu_ln_b"] = sgu_bwd(
            dproj, sv["proj"], dp, row("sgu_ln_g", l), row("sgu_ln_b", l), sv["w_mix"],
            jnp.swapaxes(sv["w_mix"], 1, 2), sv["b_mix"], d, 2)
        g["w_spatial"] = jnp.where(causal[None], dw_mix, 0.0)
        g["b_spatial"] = db_mix[:, :, 0]
        tok = emit(l, "w_in", mm_tn("d_in_w", sv["h"], dproj, BF16, 2048, d // 2, nb=wl["w_in"].shape[2]))
        dh = mm_nt("d_in_in", dproj, wl["w_in"], BF16, 1024, deps=[tok])
        if l > 0:
            below = saved[l - 1]
            dx, do2, g["norm_mix_pre"], dg_ffn_post = norm_bwd_in_out(
                dx, dh, sv["x_in"], row("norm_mix_pre", l), below["o2"], row("norm_ffn_post", l - 1))
        else:
            dx, g["norm_mix_pre"] = norm_bwd_in(dx, dh, sv["x_in"], row("norm_mix_pre", l))
        grads[l] = g
    return loss, dx, grads


def _pack_rows(arrays):
    return jnp.concatenate([a.reshape(-1, 128) for a in arrays], axis=0)


def _unpack_rows(packed, shapes):
    out, r0 = [], 0
    for shp in shapes:
        nr = math.prod(shp) // 128
        out.append(packed[r0:r0 + nr].reshape(shp))
        r0 += nr
    return out


def kernel(x, norm_mix_pre, norm_mix_post, norm_ffn_pre, norm_ffn_post, w_in, b_gate, conv_w, conv_b, conv_ln_g, conv_ln_b, w_a_out, sgu_ln_g, sgu_ln_b, w_spatial, b_spatial, w_b_out, w_o, w_gate_up, w_down, loss_target, m_norm_mix_pre, m_norm_mix_post, m_norm_ffn_pre, m_norm_ffn_post, m_w_in, m_b_gate, m_conv_w, m_conv_b, m_conv_ln_g, m_conv_ln_b, m_w_a_out, m_sgu_ln_g, m_sgu_ln_b, m_w_spatial, m_b_spatial, m_w_b_out, m_w_o, m_w_gate_up, m_w_down, v_norm_mix_pre, v_norm_mix_post, v_norm_ffn_pre, v_norm_ffn_post, v_w_in, v_b_gate, v_conv_w, v_conv_b, v_conv_ln_g, v_conv_ln_b, v_w_a_out, v_sgu_ln_g, v_sgu_ln_b, v_w_spatial, v_b_spatial, v_w_b_out, v_w_o, v_w_gate_up, v_w_down):
    names = ("norm_mix_pre", "norm_mix_post", "norm_ffn_pre", "norm_ffn_post", "w_in", "b_gate", "conv_w", "conv_b",
             "conv_ln_g", "conv_ln_b", "w_a_out", "sgu_ln_g", "sgu_ln_b", "w_spatial", "b_spatial", "w_b_out", "w_o",
             "w_gate_up", "w_down")
    w = dict(zip(names, (norm_mix_pre, norm_mix_post, norm_ffn_pre, norm_ffn_post, w_in, b_gate, conv_w, conv_b,
                         conv_ln_g, conv_ln_b, w_a_out, sgu_ln_g, sgu_ln_b, w_spatial, b_spatial, w_b_out, w_o,
                         w_gate_up, w_down)))
    m = dict(zip(names, (m_norm_mix_pre, m_norm_mix_post, m_norm_ffn_pre, m_norm_ffn_post, m_w_in, m_b_gate, m_conv_w,
                         m_conv_b, m_conv_ln_g, m_conv_ln_b, m_w_a_out, m_sgu_ln_g, m_sgu_ln_b, m_w_spatial,
                         m_b_spatial, m_w_b_out, m_w_o, m_w_gate_up, m_w_down)))
    v = dict(zip(names, (v_norm_mix_pre, v_norm_mix_post, v_norm_ffn_pre, v_norm_ffn_post, v_w_in, v_b_gate, v_conv_w,
                         v_conv_b, v_conv_ln_g, v_conv_ln_b, v_w_a_out, v_sgu_ln_g, v_sgu_ln_b, v_w_spatial,
                         v_b_spatial, v_w_b_out, v_w_o, v_w_gate_up, v_w_down)))
    d = x.shape[-1]
    shard_cols = d // N_DEV

    def small_pack(bg, cw):
        rows = jnp.concatenate([bg, cw], axis=1).reshape(DEPTH * (2 + CONV_WIDTH), shard_cols)
        return jnp.pad(rows, ((0, (-rows.shape[0]) % 8), (0, 0)))

    small_w, small_m, small_v = (small_pack(t["b_gate"], t["conv_w"]) for t in (w, m, v))

    small_full, = _exchange("gather_small", [small_w], scatter=False)
    small_full = small_full[:, :DEPTH * (2 + CONV_WIDTH)].reshape(N_DEV, DEPTH, 2 + CONV_WIDTH, shard_cols)
    small_full = jnp.transpose(small_full, (1, 2, 0, 3)).reshape(DEPTH, 2 + CONV_WIDTH, d)
    b_gate_full = small_full[:, :2]
    conv_w_full = jnp.pad(small_full[:, 2:], ((0, 0), (0, CONV_PAD - CONV_WIDTH), (0, 0)))

    me = (4 * lax.axis_index("x") + 2 * lax.axis_index("y") + lax.axis_index("c")).astype(jnp.int32).reshape(1)
    first_level = (SIBLING,) + SAME_CORE_PEERS
    gathers, token = {}, small_full
    for l in range(DEPTH):
        lands = [fill_own_slot(f"cast_{name}", me, w[name], l, BF16) for name in MATRICES]
        handles, token = exchange_start(f"gather_start_{l}", lands, first_level, after=token)
        for name, handle in zip(MATRICES, handles):
            gathers[l, name] = handle
    use_order = [(l, name) for l in range(DEPTH) for name in MATRICES]
    forwards, gathered = {}, {}

    def start_forward(i, after):
        if i >= len(use_order) or use_order[i] in forwards:
            return None
        l, name = use_order[i]
        land = exchange_wait(f"gather_wait_{name}_{l}", gathers[l, name], first_level, after)
        (forwards[l, name],), tok = exchange_start(f"forward_start_{name}_{l}", [land], SAME_CORE_PEERS, forward=True)
        return tok

    def weight(l, name, after):
        if (l, name) not in gathered:
            i = use_order.index((l, name))
            start_forward(i, after)
            tok = start_forward(i + 1, after)
            full = exchange_wait(f"forward_wait_{name}_{l}", forwards[l, name], SAME_CORE_PEERS, after, forward=True)
            gathered[l, name] = full if name in ("w_in", "w_gate_up") else full.reshape(-1, d)
            return gathered[l, name], tok
        return gathered[l, name], None

    scatters = {}

    def emit(l, name, g):
        chunks = g if g.ndim == 3 else g.reshape(N_DEV, -1, d)
        land = lax.empty(chunks.shape, chunks.dtype)
        (scatters[l, name],), tok = exchange_start(f"scatter_start_{name}_{l}", [land], ALL_PEERS, srcs=[chunks])
        return tok

    rep = {name: w[name] for name in REPLICATED}
    loss, grad_x, grads = local_step(x[0], loss_target[0], rep, weight, emit, b_gate_full, conv_w_full, token)
    loss = lax.psum(loss, MESH_AXES)

    small_g = jnp.stack([jnp.concatenate([grads[l]["b_gate"], grads[l]["conv_w"][:CONV_WIDTH]], axis=0) for l in range(DEPTH)])
    small_g = jnp.transpose(small_g.reshape(DEPTH * (2 + CONV_WIDTH), N_DEV, shard_cols), (1, 0, 2))
    small_g = jnp.pad(small_g, ((0, 0), (0, small_w.shape[0] - small_g.shape[1]), (0, 0)))
    rep_shapes = [w[name].shape for name in REPLICATED]
    rep_g = _pack_rows([jnp.stack([grads[l][name].reshape(w[name].shape[1:]) for l in range(DEPTH)]) for name in REPLICATED])
    small_land = fill_own_slot("own_small", me, small_g, None, F32)
    (small_handle,), tok_small = exchange_start("scatter_start_small", [small_land], ALL_PEERS, srcs=[small_g])
    rep_land = fill_own_slot("own_replicated", me, rep_g[None], 0, F32)
    (rep_handle,), tok_rep = exchange_start("gather_start_replicated", [rep_land], ALL_PEERS, after=tok_small)

    out = {}
    after = tok_rep
    for name in ("w_down", "w_gate_up", "w_o", "w_a_out", "w_b_out", "w_in"):
        done = [exchange_wait(f"scatter_wait_{name}_{l}", scatters[l, name], ALL_PEERS, after, with_source=True)
                for l in range(DEPTH)]
        out[name] = sum_adamw("adamw_" + name, [land for _, land in done], w[name], m[name], v[name], 128,
                              me=me, own=[chunks for chunks, _ in done])
        after = out[name][0]
    small_parts = exchange_wait("scatter_wait_small", small_handle, ALL_PEERS, after)
    rep_parts = exchange_wait("gather_wait_replicated", rep_handle, ALL_PEERS, after)
    small_res = sum_adamw("adamw_small", [small_parts], small_w[None], small_m[None], small_v[None], small_w.shape[0])
    n_small = DEPTH * (2 + CONV_WIDTH)
    small_res = [r[0, :n_small].reshape(DEPTH, 2 + CONV_WIDTH, shard_cols) for r in small_res]
    out["b_gate"] = [r[:, :2] for r in small_res]
    out["conv_w"] = [r[:, 2:] for r in small_res]
    rep_res = sum_adamw("adamw_replicated", [rep_parts], *(_pack_rows([t[name] for name in REPLICATED])[None] for t in (w, m, v)), 672)
    rep_res = [_unpack_rows(r[0], rep_shapes) for r in rep_res]
    for i, name in enumerate(REPLICATED):
        out[name] = [r[i] for r in rep_res]

    return (loss, grad_x[None], *[out[name][0] for name in names], *[out[name][1] for name in names],
            *[out[name][2] for name in names], *[out[name][3] for name in names])
```
